```python
import math
import jax, jax.numpy as jnp
from jax import lax
import numpy as np

D_MODEL = 2048
BATCH = 2
SEQ = 4096
DEPTH = 1

CHUNK = 64
Q_BLOCK = 128
D_DIFF = D_MODEL // 2
D_MLSTM = D_MODEL - D_DIFF
DIFF_HEAD_DIM = 64
N_DIFF_HEADS = D_DIFF // (2 * DIFF_HEAD_DIM)
DIFF_V_DIM = 2 * DIFF_HEAD_DIM
MLSTM_HEAD_DIM = 128
N_MLSTM_HEADS = D_MLSTM // MLSTM_HEAD_DIM
CONV_WIDTH = 4
N_BUCKETS = 32
MAX_DISTANCE = 128
N_GROUPS = 4
EXPERTS_PER_GROUP = 8
TOP_K_INNER = 2
D_EXPERT = D_MODEL // 4
EPS = 1e-6
PROJ_SIZES = (D_DIFF, D_DIFF, D_DIFF,
              D_MLSTM, D_MLSTM, D_MLSTM, D_MLSTM,
              N_MLSTM_HEADS, N_MLSTM_HEADS)
PROJ_DIM = sum(PROJ_SIZES)
PROJ_SPLITS = tuple(int(v) for v in np.cumsum(PROJ_SIZES)[:-1])

kernel_name = "hymba_diffattn_mlstm_hmoe"


def rmsnorm(x, g):
    xf = x.astype(jnp.float32)
    y = xf * lax.rsqrt(jnp.mean(xf * xf, axis=-1, keepdims=True) + EPS)
    return (y * g.astype(jnp.float32)).astype(x.dtype)


def head_rmsnorm(x, g):
    H, dh = x.shape[-2], x.shape[-1]
    return rmsnorm(x, g.reshape(H, dh))


def t5_bucket(rel):
    half = N_BUCKETS // 2
    max_exact = half // 2
    ret = jnp.where(rel > 0, half, 0)
    n = jnp.abs(rel)
    nf = jnp.maximum(n, 1).astype(jnp.float32)
    large = max_exact + (jnp.log(nf / max_exact) / math.log(MAX_DISTANCE / max_exact)
                         * (half - max_exact)).astype(jnp.int32)
    large = jnp.minimum(large, half - 1)
    return ret + jnp.where(n < max_exact, n, large)


def causal_conv(u, w, b):
    C = u.shape[-1]
    y = lax.conv_general_dilated(u, w[:, None, :].astype(u.dtype), window_strides=(1,),
                                 padding=((CONV_WIDTH - 1, 0),),
                                 dimension_numbers=('NWC', 'WIO', 'NWC'),
                                 feature_group_count=C)
    return y + b.astype(u.dtype)


def diff_attention(q1, q2, k1, k2, v, lam, rel_bias):
    B, H, S, dk = q1.shape
    nqb = S // Q_BLOCK
    scale = dk ** -0.5
    key_pos = jnp.arange(S, dtype=jnp.int32)

    def block(args):
        qb1, qb2, qi = args
        q_pos = qi * Q_BLOCK + jnp.arange(Q_BLOCK, dtype=jnp.int32)
        bias = rel_bias[t5_bucket(key_pos[None, :] - q_pos[:, None])]
        bias = jnp.transpose(bias, (2, 0, 1)).astype(jnp.float32)
        mask = (key_pos[None, :] // CHUNK) <= (q_pos[:, None] // CHUNK)

        def probs(qb, k):
            s = jnp.einsum('bhqd,bhkd->bhqk', qb, k).astype(jnp.float32) * scale + bias
            return jax.nn.softmax(jnp.where(mask, s, -jnp.inf), axis=-1)

        p = probs(qb1, k1) - lam * probs(qb2, k2)
        return jnp.einsum('bhqk,bhkd->bhqd', p.astype(v.dtype), v)

    def to_blocks(q):
        return jnp.transpose(q.reshape(B, H, nqb, Q_BLOCK, dk), (2, 0, 1, 3, 4))

    out = lax.map(block, (to_blocks(q1), to_blocks(q2), jnp.arange(nqb, dtype=jnp.int32)))
    return jnp.transpose(out, (1, 2, 0, 3, 4)).reshape(B, H, S, v.shape[-1])


def mlstm_chunkwise(q, k, v, i_pre, f_pre):
    B, H, S, dh = q.shape
    nc, L = S // CHUNK, CHUNK
    q = q.reshape(B, H, nc, L, dh)
    k = k.reshape(B, H, nc, L, dh)
    v = v.reshape(B, H, nc, L, dh)
    log_i = i_pre.reshape(B, H, nc, L)
    log_f = jax.nn.log_sigmoid(f_pre).reshape(B, H, nc, L)
    b = jnp.cumsum(log_f, axis=-1)
    g = b[..., -1]
    w = g[..., None] - b + log_i

    def step(carry, xs):
        C, n, m = carry
        k_c, v_c, g_c, w_c = xs
        m_new = jnp.maximum(g_c + m, jnp.max(w_c, axis=-1))
        decay = jnp.exp(g_c + m - m_new)
        wt = jnp.exp(w_c - m_new[..., None])
        C_new = decay[..., None, None] * C + jnp.einsum('bhl,bhld,bhle->bhde', wt, v_c, k_c)
        n_new = decay[..., None] * n + jnp.einsum('bhl,bhle->bhe', wt, k_c)
        return (C_new, n_new, m_new), (C, n, m)

    init = (jnp.zeros((B, H, dh, dh), jnp.float32), jnp.zeros((B, H, dh), jnp.float32),
            jnp.zeros((B, H), jnp.float32))
    xs = (jnp.moveaxis(k, 2, 0), jnp.moveaxis(v, 2, 0), jnp.moveaxis(g, 2, 0), jnp.moveaxis(w, 2, 0))
    _, (C_prev, n_prev, m_prev) = lax.scan(step, init, xs)
    C_prev = jnp.moveaxis(C_prev, 0, 2)
    n_prev = jnp.moveaxis(n_prev, 0, 2)
    m_prev = jnp.moveaxis(m_prev, 0, 2)

    causal = jnp.tril(jnp.ones((L, L), dtype=bool))
    D = jnp.where(causal, b[..., :, None] - b[..., None, :] + log_i[..., None, :], -jnp.inf)
    m_inter = b + m_prev[..., None]
    m_t = jnp.maximum(jnp.max(D, axis=-1), m_inter)
    Wts = jnp.exp(D - m_t[..., None])
    inter = jnp.exp(m_inter - m_t)
    s_qk = jnp.einsum('bhctd,bhcsd->bhcts', q, k) * Wts
    num = (jnp.einsum('bhcts,bhcsd->bhctd', s_qk, v)
           + inter[..., None] * jnp.einsum('bhcde,bhcte->bhctd', C_prev, q))
    den = jnp.sum(s_qk, axis=-1) + inter * jnp.einsum('bhce,bhcte->bhct', n_prev, q)
    h = num / jnp.maximum(jnp.abs(den), jnp.exp(-m_t))[..., None]
    return h.reshape(B, H, S, dh)


def hier_moe(x, w_group, b_group, w_router, b_router, w_gate, w_up, w_down):
    gl = (x @ w_group).astype(jnp.float32) + b_group.astype(jnp.float32)
    gp = jax.nn.softmax(gl, axis=-1)
    _, gsel = lax.top_k(gl, 1)
    gmask = jax.nn.one_hot(gsel[:, 0], N_GROUPS, dtype=jnp.float32)
    gw = jnp.sum(gp * gmask, axis=-1)
    el = (jnp.einsum('td,gde->tge', x, w_router).astype(jnp.float32)
          + b_router.astype(jnp.float32))
    el_sel = jnp.einsum('tge,tg->te', el, gmask)
    top_v, top_i = lax.top_k(el_sel, TOP_K_INNER)
    top_w = jax.nn.softmax(top_v, axis=-1)
    ew = jnp.sum(jax.nn.one_hot(top_i, EXPERTS_PER_GROUP, dtype=jnp.float32) * top_w[..., None], axis=1)
    comb = (gw[:, None, None] * gmask[:, :, None] * ew[:, None, :]).astype(x.dtype)
    out = jnp.zeros_like(x)
    for gi in range(N_GROUPS):
        hid = (jax.nn.silu(jnp.einsum('td,edf->tef', x, w_gate[gi]))
               * jnp.einsum('td,edf->tef', x, w_up[gi]) * comb[:, gi, :, None])
        out = out + jnp.einsum('tef,efd->td', hid, w_down[gi])
    return out


def setup_inputs(seed: int = 0) -> dict:
    key = jax.random.key(seed)
    ks = jax.random.split(key, 24)
    f32 = jnp.float32

    def nrm(k, shape, scale):
        return jax.random.normal(k, shape, f32) * scale

    L, D = DEPTH, D_MODEL
    G, E, F = N_GROUPS, EXPERTS_PER_GROUP, D_EXPERT
    return {
        "x": nrm(ks[0], (BATCH, SEQ, D), 1.0),
        "rel_bias": nrm(ks[1], (N_BUCKETS, N_DIFF_HEADS), 0.1),
        "ln_mix_g": 1.0 + nrm(ks[2], (L, D), 0.02),
        "w_in": nrm(ks[3], (L, D, PROJ_DIM), D ** -0.5),
        "conv_w": nrm(ks[4], (L, CONV_WIDTH, 2 * D_MLSTM), CONV_WIDTH ** -0.5),
        "conv_b": nrm(ks[5], (L, 2 * D_MLSTM), 0.02),
        "b_i": nrm(ks[6], (L, N_MLSTM_HEADS), 0.1),
        "b_f": 3.0 + nrm(ks[7], (L, N_MLSTM_HEADS), 0.1),
        "lam_q1": nrm(ks[8], (L, DIFF_HEAD_DIM), 0.1),
        "lam_k1": nrm(ks[9], (L, DIFF_HEAD_DIM), 0.1),
        "lam_q2": nrm(ks[10], (L, DIFF_HEAD_DIM), 0.1),
        "lam_k2": nrm(ks[11], (L, DIFF_HEAD_DIM), 0.1),
        "diff_norm_g": 1.0 + nrm(ks[12], (L, D_DIFF), 0.02),
        "mlstm_norm_g": 1.0 + nrm(ks[13], (L, D_MLSTM), 0.02),
        "w_out": nrm(ks[14], (L, D_DIFF + D_MLSTM, D), (D_DIFF + D_MLSTM) ** -0.5),
        "ln_ffn_g": 1.0 + nrm(ks[15], (L, D), 0.02),
        "w_group": nrm(ks[16], (L, D, G), D ** -0.5),
        "b_group": nrm(ks[17], (L, G), 0.01),
        "w_router": nrm(ks[18], (L, G, D, E), D ** -0.5),
        "b_router": nrm(ks[19], (L, G, E), 0.01),
        "w_gate": nrm(ks[20], (L, G, E, D, F), D ** -0.5),
        "w_up": nrm(ks[21], (L, G, E, D, F), D ** -0.5),
        "w_down": nrm(ks[22], (L, G, E, F, D), F ** -0.5),
        "ln_f_g": 1.0 + nrm(ks[23], (D,), 0.02),
    }


def reference(x, rel_bias, ln_mix_g, w_in, conv_w, conv_b, b_i, b_f, lam_q1, lam_k1, lam_q2,
              lam_k2, diff_norm_g, mlstm_norm_g, w_out, ln_ffn_g, w_group, b_group, w_router,
              b_router, w_gate, w_up, w_down, ln_f_g):
    B, S, D = x.shape
    f32 = jnp.float32
    for l in range(DEPTH):
        lambda_init = 0.8 - 0.6 * math.exp(-0.3 * l)
        h = rmsnorm(x, ln_mix_g[l])
        proj = h @ w_in[l]
        q_d, k_d, v_d, q_m, k_m, v_m, o_m, i_m, f_m = jnp.split(proj, PROJ_SPLITS, axis=-1)

        q_d = jnp.transpose(q_d.reshape(B, S, N_DIFF_HEADS, 2, DIFF_HEAD_DIM), (0, 2, 3, 1, 4))
        k_d = jnp.transpose(k_d.reshape(B, S, N_DIFF_HEADS, 2, DIFF_HEAD_DIM), (0, 2, 3, 1, 4))
        v_d = jnp.transpose(v_d.reshape(B, S, N_DIFF_HEADS, DIFF_V_DIM), (0, 2, 1, 3))
        lam = (jnp.exp(jnp.sum(lam_q1[l].astype(f32) * lam_k1[l].astype(f32)))
               - jnp.exp(jnp.sum(lam_q2[l].astype(f32) * lam_k2[l].astype(f32))) + lambda_init)
        a = diff_attention(q_d[:, :, 0], q_d[:, :, 1], k_d[:, :, 0], k_d[:, :, 1], v_d, lam, rel_bias)
        a = head_rmsnorm(jnp.transpose(a, (0, 2, 1, 3)), diff_norm_g[l]) * (1.0 - lambda_init)
        a = a.reshape(B, S, D_DIFF)

        qk = jax.nn.silu(causal_conv(jnp.concatenate([q_m, k_m], axis=-1), conv_w[l], conv_b[l]))
        q_m, k_m = jnp.split(qk, 2, axis=-1)

        def heads(t):
            return jnp.transpose(t.reshape(B, S, N_MLSTM_HEADS, MLSTM_HEAD_DIM), (0, 2, 1, 3)).astype(f32)

        i_pre = jnp.transpose(i_m.astype(f32) + b_i[l].astype(f32), (0, 2, 1))
        f_pre = jnp.transpose(f_m.astype(f32) + b_f[l].astype(f32), (0, 2, 1))
        hm = mlstm_chunkwise(heads(q_m), heads(k_m) * MLSTM_HEAD_DIM ** -0.5, heads(v_m), i_pre, f_pre)
        hm = head_rmsnorm(jnp.transpose(hm, (0, 2, 1, 3)).astype(x.dtype), mlstm_norm_g[l])
        hm = (hm * jax.nn.sigmoid(o_m).reshape(B, S, N_MLSTM_HEADS, MLSTM_HEAD_DIM)).reshape(B, S, D_MLSTM)

        x = x + jnp.concatenate([a, hm], axis=-1) @ w_out[l]

        h = rmsnorm(x, ln_ffn_g[l]).reshape(B * S, D)
        x = x + hier_moe(h, w_group[l], b_group[l], w_router[l], b_router[l],
                         w_gate[l], w_up[l], w_down[l]).reshape(B, S, D)
    return rmsnorm(x, ln_f_g)
```

```python
import functools
import math

import jax
import jax.numpy as jnp
from jax import lax
from jax.experimental import pallas as pl
from jax.experimental.pallas import tpu as pltpu

F32 = jnp.float32
BF16 = jnp.bfloat16

EPS = 1e-6
CHUNK = 64
DIFF_HEAD_DIM = 64
N_DIFF_HEADS = 8
MLSTM_HEAD_DIM = 128
N_MLSTM_HEADS = 8
CONV_WIDTH = 4
N_BUCKETS = 32
MAX_DISTANCE = 128
N_GROUPS = 4
EXPERTS_PER_GROUP = 8
N_EXPERTS = N_GROUPS * EXPERTS_PER_GROUP
TOP_K_INNER = 2
LANES = 128
NEG_BIG = -1e30

VMEM_LIMIT = 56 * 1024 * 1024

PROJ_TM, PROJ_TN = 1024, 1024
ATT_TQ = 256
MLSTM_TS = 512
OUT_TM = 256
MOE_TM = 256
FIN_TM = 512


def _cparams(sem):
    return pltpu.CompilerParams(dimension_semantics=sem, vmem_limit_bytes=VMEM_LIMIT)


def _proj_kernel(x_ref, g_ref, w_ref, wg_ref, o_ref, og_ref, h_ref):
    @pl.when(pl.program_id(1) == 0)
    def _():
        x = x_ref[...]
        ms = jnp.mean(x * x, axis=-1, keepdims=True)
        h = (x * lax.rsqrt(ms + EPS) * g_ref[...]).astype(BF16)
        h_ref[...] = h
        og_ref[...] = jnp.dot(h, wg_ref[...], preferred_element_type=F32)

    o_ref[...] = jnp.dot(h_ref[...], w_ref[...], preferred_element_type=F32).astype(o_ref.dtype)


def _proj(x2, g, w_main, w_gates):
    T, D = x2.shape
    N = w_main.shape[1]
    NG = w_gates.shape[1]
    return pl.pallas_call(
        _proj_kernel,
        grid=(T // PROJ_TM, N // PROJ_TN),
        in_specs=[
            pl.BlockSpec((PROJ_TM, D), lambda m, n: (m, 0)),
            pl.BlockSpec((1, D), lambda m, n: (0, 0)),
            pl.BlockSpec((D, PROJ_TN), lambda m, n: (0, n)),
            pl.BlockSpec((D, NG), lambda m, n: (0, 0)),
        ],
        out_specs=[
            pl.BlockSpec((PROJ_TM, PROJ_TN), lambda m, n: (m, n)),
            pl.BlockSpec((PROJ_TM, NG), lambda m, n: (m, 0)),
        ],
        out_shape=[
            jax.ShapeDtypeStruct((T, N), BF16),
            jax.ShapeDtypeStruct((T, NG), F32),
        ],
        scratch_shapes=[pltpu.VMEM((PROJ_TM, D), BF16)],
        compiler_params=_cparams(("parallel", "arbitrary")),
        name="rms_in_proj",
    )(x2, g, w_main, w_gates)


def _t5_bucket(rel):
    half = N_BUCKETS // 2
    max_exact = half // 2
    ret = jnp.where(rel > 0, half, 0)
    n = jnp.abs(rel)
    nf = jnp.maximum(n, 1).astype(F32)
    large = max_exact + (jnp.log(nf / max_exact) / math.log(MAX_DISTANCE / max_exact)
                         * (half - max_exact)).astype(jnp.int32)
    large = jnp.minimum(large, half - 1)
    return ret + jnp.where(n < max_exact, n, large)


def _attn_kernel(scal_ref, q_ref, k_ref, v_ref, bias_ref, g_ref, o_ref, m_ref, l_ref, acc_ref,
                 *, lambda_init):
    h = pl.program_id(1)
    qi = pl.program_id(2)
    tq = ATT_TQ
    lam = scal_ref[0]
    cfar = scal_ref[1 + h]

    q = q_ref[0] * jnp.asarray(DIFF_HEAD_DIM ** -0.5, BF16)
    lane = lax.broadcasted_iota(jnp.int32, q.shape, 1)
    zero = jnp.zeros_like(q)
    qs = jnp.concatenate([jnp.where(lane < DIFF_HEAD_DIM, q, zero),
                          jnp.where(lane >= DIFF_HEAD_DIM, q, zero)], axis=0)

    m_ref[...] = jnp.full(m_ref.shape, NEG_BIG, F32)
    l_ref[...] = jnp.zeros(l_ref.shape, F32)
    acc_ref[...] = jnp.zeros(acc_ref.shape, F32)

    def tile(ki, bias):
        start = pl.multiple_of(ki * tq, tq)
        kt = k_ref[0, pl.ds(start, tq), :]
        vt = v_ref[0, pl.ds(start, tq), :]
        s = lax.dot_general(qs, kt, (((1,), (1,)), ((), ())), preferred_element_type=F32)
        if bias is not None:
            s = s + jnp.concatenate([bias, bias], axis=0)
        m_old = m_ref[...]
        m_new = jnp.maximum(m_old, jnp.max(s, axis=-1, keepdims=True))
        alpha = jnp.exp(m_old - m_new)
        p = jnp.exp(s - m_new)
        l_ref[...] = alpha * l_ref[...] + jnp.sum(p, axis=-1, keepdims=True)
        acc_ref[...] = alpha * acc_ref[...] + jnp.dot(p.astype(BF16), vt, preferred_element_type=F32)
        m_ref[...] = m_new

    def far_body(ki, c):
        tile(ki, None)
        return c

    lax.fori_loop(0, jnp.maximum(qi - 1, 0), far_body, 0)
    m_ref[...] = m_ref[...] + cfar

    @pl.when(qi >= 1)
    def _():
        tile(qi - 1, bias_ref[0, 0])

    tile(qi, bias_ref[0, 1])

    inv_l = 1.0 / l_ref[...]
    o = acc_ref[0:tq, :] * inv_l[0:tq] - lam * (acc_ref[tq:2 * tq, :] * inv_l[tq:2 * tq])
    ms = jnp.mean(o * o, axis=-1, keepdims=True)
    y = o * lax.rsqrt(ms + EPS) * g_ref[...] * (1.0 - lambda_init)
    o_ref[0] = y.astype(o_ref.dtype)


def _diff_attention(proj3, scal, bias_tiles, gnorm, lambda_init):
    B, S, _ = proj3.shape
    H = N_DIFF_HEADS
    tq = ATT_TQ
    kern = functools.partial(_attn_kernel, lambda_init=lambda_init)
    return pl.pallas_call(
        kern,
        grid=(B, H, S // tq),
        in_specs=[
            pl.BlockSpec(memory_space=pltpu.SMEM),
            pl.BlockSpec((1, tq, LANES), lambda b, h, i: (b, i, h)),
            pl.BlockSpec((1, S, LANES), lambda b, h, i: (b, 0, H + h)),
            pl.BlockSpec((1, S, LANES), lambda b, h, i: (b, 0, 2 * H + h)),
            pl.BlockSpec((1, 2, tq, tq), lambda b, h, i: (h, 0, 0, 0)),
            pl.BlockSpec((1, LANES), lambda b, h, i: (0, h)),
        ],
        out_specs=pl.BlockSpec((1, tq, LANES), lambda b, h, i: (b, i, h)),
        out_shape=jax.ShapeDtypeStruct((B, S, H * LANES), BF16),
        scratch_shapes=[
            pltpu.VMEM((2 * tq, 1), F32),
            pltpu.VMEM((2 * tq, 1), F32),
            pltpu.VMEM((2 * tq, LANES), F32),
        ],
        compiler_params=_cparams(("parallel", "parallel", "arbitrary")),
        name="diff_attention",
    )(scal, proj3, proj3, proj3, bias_tiles, gnorm)


def _log_sigmoid(x):
    return jnp.minimum(x, 0.0) - jnp.log(1.0 + jnp.exp(-jnp.abs(x)))


def _sigmoid(x):
    return 1.0 / (1.0 + jnp.exp(-x))


def _mlstm_kernel(q_ref, k_ref, v_ref, o_ref, gi_ref, gf_ref, cw_ref, cb_ref, bi_ref, bf_ref, gn_ref,
                  out_ref, qext_ref, kext_ref, ct_ref, n_ref, m_ref):
    sb = pl.program_id(1)
    L = CHUNK
    dh = MLSTM_HEAD_DIM
    H = N_MLSTM_HEADS
    ts = MLSTM_TS
    pad = 8

    @pl.when(sb == 0)
    def _():
        qext_ref[0:pad, :] = jnp.zeros((pad, H * dh), F32)
        kext_ref[0:pad, :] = jnp.zeros((pad, H * dh), F32)
        ct_ref[...] = jnp.zeros(ct_ref.shape, F32)
        n_ref[...] = jnp.zeros(n_ref.shape, F32)
        m_ref[...] = jnp.zeros(m_ref.shape, F32)

    qext_ref[pad:pad + ts, :] = q_ref[0].astype(F32)
    kext_ref[pad:pad + ts, :] = k_ref[0].astype(F32)

    row = lax.broadcasted_iota(jnp.int32, (L, L), 0)
    col = lax.broadcasted_iota(jnp.int32, (L, L), 1)
    tril = col <= row
    ltri = tril.astype(F32)

    def conv_silu(ext_ref, base, h, off):
        win = ext_ref[pl.ds(base, L + pad), h * dh:(h + 1) * dh]
        w = cw_ref[:, off + h * dh:off + (h + 1) * dh]
        y = cb_ref[:, off + h * dh:off + (h + 1) * dh]
        for j in range(CONV_WIDTH):
            lo = pad - (CONV_WIDTH - 1) + j
            y = y + w[j:j + 1, :] * win[lo:lo + L, :]
        return y * _sigmoid(y)

    def chunk_body(c, carry):
        base = pl.multiple_of(c * L, L)
        li = gi_ref[0, pl.ds(base, L), :] + bi_ref[...]
        logf = _log_sigmoid(gf_ref[0, pl.ds(base, L), :] + bf_ref[...])
        b = jnp.dot(ltri, logf, preferred_element_type=F32, precision=lax.Precision.HIGHEST)
        a = li - b
        g_row = b[L - 1:L, :]
        m_row = m_ref[...]
        m_new_row = g_row + jnp.maximum(m_row, jnp.max(a, axis=0, keepdims=True))
        a_t = a.T

        for h in range(H):
            qc = conv_silu(qext_ref, base, h, 0)
            kc = conv_silu(kext_ref, base, h, H * dh) * (dh ** -0.5)
            qb = qc.astype(BF16)
            kb = kc.astype(BF16)
            vb = v_ref[0, pl.ds(base, L), h * dh:(h + 1) * dh]

            a_row = a_t[h:h + 1, :]
            a_col = a[:, h:h + 1]
            b_col = b[:, h:h + 1]
            m_prev = m_row[:, h:h + 1]
            m_next = m_new_row[:, h:h + 1]
            g_h = g_row[:, h:h + 1]

            amat = jnp.where(tril, a_row, NEG_BIG)
            mcol = jnp.maximum(jnp.max(amat, axis=-1, keepdims=True), m_prev)
            wts = jnp.exp(amat - mcol)
            inter = jnp.exp(m_prev - mcol)

            s = lax.dot_general(qb, kb, (((1,), (1,)), ((), ())), preferred_element_type=F32)
            sqk = s * wts
            ct = ct_ref[h]
            nrow = n_ref[h:h + 1, :]
            num = (jnp.dot(sqk.astype(BF16), vb, preferred_element_type=F32)
                   + inter * jnp.dot(qb, ct.astype(BF16), preferred_element_type=F32))
            den = (jnp.sum(sqk, axis=-1, keepdims=True)
                   + inter * jnp.sum(qb.astype(F32) * nrow, axis=-1, keepdims=True))
            hv = num / jnp.maximum(jnp.abs(den), jnp.exp(-(b_col + mcol)))

            wt = jnp.exp(g_h + a_col - m_next)
            decay = jnp.exp(g_h + m_prev - m_next)
            wv = (wt * vb.astype(F32)).astype(BF16)
            ct_ref[h] = decay * ct + lax.dot_general(kb, wv, (((0,), (0,)), ((), ())),
                                                     preferred_element_type=F32)
            n_ref[h:h + 1, :] = decay * nrow + jnp.sum(wt * kb.astype(F32), axis=0, keepdims=True)

            ms = jnp.mean(hv * hv, axis=-1, keepdims=True)
            y = hv * lax.rsqrt(ms + EPS) * gn_ref[:, h * dh:(h + 1) * dh]
            og = o_ref[0, pl.ds(base, L), h * dh:(h + 1) * dh].astype(F32)
            out_ref[0, pl.ds(base, L), h * dh:(h + 1) * dh] = (y * _sigmoid(og)).astype(out_ref.dtype)

        m_ref[...] = m_new_row
        return carry

    lax.fori_loop(0, ts // L, chunk_body, 0)

    qext_ref[0:pad, :] = qext_ref[ts:ts + pad, :]
    kext_ref[0:pad, :] = kext_ref[ts:ts + pad, :]


def _mlstm(proj3, gates3, conv_w, conv_b, bi_row, bf_row, gnorm):
    B, S, _ = proj3.shape
    W = N_MLSTM_HEADS * MLSTM_HEAD_DIM
    ts = MLSTM_TS
    first = 3
    blk = lambda j: pl.BlockSpec((1, ts, W), lambda b, s: (b, s, j))
    full = lambda shape: pl.BlockSpec(shape, lambda b, s: (0,) * len(shape))
    return pl.pallas_call(
        _mlstm_kernel,
        grid=(B, S // ts),
        in_specs=[
            blk(first), blk(first + 1), blk(first + 2), blk(first + 3),
            pl.BlockSpec((1, ts, LANES), lambda b, s: (b, s, 0)),
            pl.BlockSpec((1, ts, LANES), lambda b, s: (b, s, 1)),
            full((CONV_WIDTH, 2 * W)), full((1, 2 * W)),
            full((1, LANES)), full((1, LANES)), full((1, W)),
        ],
        out_specs=pl.BlockSpec((1, ts, W), lambda b, s: (b, s, 0)),
        out_shape=jax.ShapeDtypeStruct((B, S, W), BF16),
        scratch_shapes=[
            pltpu.VMEM((ts + 8, W), F32),
            pltpu.VMEM((ts + 8, W), F32),
            pltpu.VMEM((N_MLSTM_HEADS, MLSTM_HEAD_DIM, MLSTM_HEAD_DIM), F32),
            pltpu.VMEM((N_MLSTM_HEADS, MLSTM_HEAD_DIM), F32),
            pltpu.VMEM((1, LANES), F32),
        ],
        compiler_params=_cparams(("parallel", "arbitrary")),
        name="mlstm",
    )(proj3, proj3, proj3, proj3, gates3, gates3, conv_w, conv_b, bi_row, bf_row, gnorm)


def _out_kernel(x_ref, a_ref, hm_ref, wa_ref, wm_ref, g_ref, wr_ref, x1_ref, h2_ref, lg_ref):
    y = (jnp.dot(a_ref[...], wa_ref[...], preferred_element_type=F32)
         + jnp.dot(hm_ref[...], wm_ref[...], preferred_element_type=F32))
    x1 = x_ref[...] + y
    x1_ref[...] = x1
    ms = jnp.mean(x1 * x1, axis=-1, keepdims=True)
    h2 = x1 * lax.rsqrt(ms + EPS) * g_ref[...]
    h2_ref[...] = h2.astype(h2_ref.dtype)
    lg_ref[...] = jnp.dot(h2, wr_ref[...], preferred_element_type=F32, precision=lax.Precision.HIGHEST)


def _out_proj(x2, a2, hm2, wa, wm, g, wr):
    T, D = x2.shape
    W = a2.shape[1]
    tm = OUT_TM
    const = lambda shape: pl.BlockSpec(shape, lambda m: (0, 0))
    return pl.pallas_call(
        _out_kernel,
        grid=(T // tm,),
        in_specs=[
            pl.BlockSpec((tm, D), lambda m: (m, 0)),
            pl.BlockSpec((tm, W), lambda m: (m, 0)),
            pl.BlockSpec((tm, W), lambda m: (m, 0)),
            const((W, D)), const((W, D)), const((1, D)), const((D, LANES)),
        ],
        out_specs=[
            pl.BlockSpec((tm, D), lambda m: (m, 0)),
            pl.BlockSpec((tm, D), lambda m: (m, 0)),
            pl.BlockSpec((tm, LANES), lambda m: (m, 0)),
        ],
        out_shape=[
            jax.ShapeDtypeStruct((T, D), F32),
            jax.ShapeDtypeStruct((T, D), BF16),
            jax.ShapeDtypeStruct((T, LANES), F32),
        ],
        compiler_params=_cparams(("parallel",)),
        name="out_proj_router",
    )(x2, a2, hm2, wa, wm, g, wr)


def _moe_kernel(te_ref, nv_ref, xs_ref, wg_ref, wu_ref, wd_ref, ys_ref):
    j = pl.program_id(0)

    @pl.when(j < nv_ref[0])
    def _():
        xs = xs_ref[...]
        gt = jnp.dot(xs, wg_ref[...].astype(BF16), preferred_element_type=F32)
        up = jnp.dot(xs, wu_ref[...].astype(BF16), preferred_element_type=F32)
        hid = (gt * _sigmoid(gt) * up).astype(BF16)
        ys_ref[...] = jnp.dot(hid, wd_ref[...].astype(BF16),
                              preferred_element_type=F32).astype(ys_ref.dtype)

    @pl.when(j >= nv_ref[0])
    def _():
        ys_ref[...] = jnp.zeros(ys_ref.shape, ys_ref.dtype)


def _moe(tile_expert, n_valid, xs, wg, wu, wd):
    R, D = xs.shape
    F = wg.shape[2]
    tm = MOE_TM
    grid_spec = pltpu.PrefetchScalarGridSpec(
        num_scalar_prefetch=2,
        grid=(R // tm,),
        in_specs=[
            pl.BlockSpec((tm, D), lambda j, te, nv: (j, 0)),
            pl.BlockSpec((None, D, F), lambda j, te, nv: (te[j], 0, 0)),
            pl.BlockSpec((None, D, F), lambda j, te, nv: (te[j], 0, 0)),
            pl.BlockSpec((None, F, D), lambda j, te, nv: (te[j], 0, 0)),
        ],
        out_specs=pl.BlockSpec((tm, D), lambda j, te, nv: (j, 0)),
    )
    return pl.pallas_call(
        _moe_kernel,
        grid_spec=grid_spec,
        out_shape=jax.ShapeDtypeStruct((R, D), BF16),
        compiler_params=_cparams(("arbitrary",)),
        name="moe_experts",
    )(tile_expert, n_valid, xs, wg, wu, wd)


def _final_kernel(x1_ref, y0_ref, y1_ref, cw_ref, g_ref, o_ref):
    cw = cw_ref[...]
    x = (x1_ref[...] + cw[:, 0:1] * y0_ref[...].astype(F32) + cw[:, 1:2] * y1_ref[...].astype(F32))
    ms = jnp.mean(x * x, axis=-1, keepdims=True)
    o_ref[...] = x * lax.rsqrt(ms + EPS) * g_ref[...]


def _final(x1, y0, y1, cw, g):
    T, D = x1.shape
    tm = FIN_TM
    row = lambda w: pl.BlockSpec((tm, w), lambda m: (m, 0))
    return pl.pallas_call(
        _final_kernel,
        grid=(T // tm,),
        in_specs=[row(D), row(D), row(D), row(LANES), pl.BlockSpec((1, D), lambda m: (0, 0))],
        out_specs=row(D),
        out_shape=jax.ShapeDtypeStruct((T, D), F32),
        compiler_params=_cparams(("parallel",)),
        name="combine_final_norm",
    )(x1, y0, y1, cw, g)


def _route(logits, b_group, b_router):
    T = logits.shape[0]
    G, E = N_GROUPS, EXPERTS_PER_GROUP
    gl = logits[:, :G] + b_group.astype(F32)
    gp = jax.nn.softmax(gl, axis=-1)
    gsel = jnp.argmax(gl, axis=-1)
    gw = jnp.take_along_axis(gp, gsel[:, None], axis=1)[:, 0]
    el = logits[:, G:G + G * E].reshape(T, G, E) + b_router.astype(F32)
    el_sel = jnp.take_along_axis(el, gsel[:, None, None], axis=1)[:, 0, :]
    top_v, top_i = lax.top_k(el_sel, TOP_K_INNER)
    top_w = jax.nn.softmax(top_v, axis=-1)
    eid = (gsel[:, None] * E + top_i).astype(jnp.int32)
    cw = gw[:, None] * top_w
    return eid, cw


def _dispatch_plan(eid, tm, n_tiles):
    T2 = eid.size
    flat_e = eid.reshape(-1)
    onehot = (flat_e[:, None] == jnp.arange(N_EXPERTS, dtype=jnp.int32)[None, :]).astype(jnp.int32)
    csum = jnp.cumsum(onehot, axis=0)
    rank = jnp.sum((csum - onehot) * onehot, axis=1)
    counts = csum[-1]
    tiles_per_e = (counts + tm - 1) // tm
    tile_end = jnp.cumsum(tiles_per_e)
    row_start = (tile_end - tiles_per_e) * tm
    pos = (row_start[flat_e] + rank).astype(jnp.int32)
    n_valid = tile_end[-1].astype(jnp.int32)
    tile_ids = jnp.arange(n_tiles, dtype=jnp.int32)
    tile_expert = jnp.searchsorted(tile_end, jnp.minimum(tile_ids, n_valid - 1), side="right")
    tile_expert = jnp.minimum(tile_expert, N_EXPERTS - 1).astype(jnp.int32)
    src_token = jnp.zeros((n_tiles * tm,), jnp.int32).at[pos].set(
        jnp.arange(T2, dtype=jnp.int32) // TOP_K_INNER)
    return pos.reshape(eid.shape), src_token, tile_expert, n_valid.reshape(1)


def kernel(x, rel_bias, ln_mix_g, w_in, conv_w, conv_b, b_i, b_f, lam_q1, lam_k1, lam_q2, lam_k2,
           diff_norm_g, mlstm_norm_g, w_out, ln_ffn_g, w_group, b_group, w_router, b_router,
           w_gate, w_up, w_down, ln_f_g):
    B, S, D = x.shape
    T = B * S
    depth = w_in.shape[0]
    assert depth == 1, "the final rmsnorm is fused into the single layer's combine kernel"
    Hm = N_MLSTM_HEADS
    n_main = w_in.shape[2] - 2 * Hm
    n_diff = N_DIFF_HEADS * 2 * DIFF_HEAD_DIM
    xf = x.reshape(T, D)

    for l in range(depth):
        lambda_init = 0.8 - 0.6 * math.exp(-0.3 * l)
        w_main = w_in[l, :, :n_main].astype(BF16)
        wgt = w_in[l, :, n_main:]
        w_gates = jnp.zeros((D, 2 * LANES), F32).at[:, :Hm].set(wgt[:, :Hm]).at[:, LANES:LANES + Hm].set(
            wgt[:, Hm:]).astype(BF16)
        bi_row = jnp.zeros((1, LANES), F32).at[0, :Hm].set(b_i[l].astype(F32))
        bf_row = jnp.zeros((1, LANES), F32).at[0, :Hm].set(b_f[l].astype(F32))
        lam = (jnp.exp(jnp.sum(lam_q1[l].astype(F32) * lam_k1[l].astype(F32)))
               - jnp.exp(jnp.sum(lam_q2[l].astype(F32) * lam_k2[l].astype(F32))) + lambda_init)
        tq = ATT_TQ
        assert tq >= MAX_DISTANCE and tq % CHUNK == 0
        ii = jnp.arange(tq, dtype=jnp.int32)
        rel0 = ii[None, :] - ii[:, None]
        rb = rel_bias.astype(F32)
        bias_prev = rb[_t5_bucket(rel0 - tq)]
        bias_diag = jnp.where(((ii[None, :] // CHUNK) <= (ii[:, None] // CHUNK))[:, :, None],
                              rb[_t5_bucket(rel0)], NEG_BIG)
        bias_tiles = jnp.transpose(jnp.stack([bias_prev, bias_diag], axis=0), (3, 0, 1, 2))
        cfar = rb[_t5_bucket(jnp.int32(-(tq + 1)))]
        scal = jnp.concatenate([lam.reshape(1), cfar]).astype(F32)

        proj, gates = _proj(xf, ln_mix_g[l].reshape(1, D).astype(F32), w_main, w_gates)
        proj3 = proj.reshape(B, S, n_main)
        a = _diff_attention(proj3, scal, bias_tiles, diff_norm_g[l].reshape(1, n_diff).astype(F32),
                            lambda_init)
        hm = _mlstm(proj3, gates.reshape(B, S, 2 * LANES), conv_w[l].astype(F32),
                    conv_b[l].reshape(1, -1).astype(F32), bi_row, bf_row,
                    mlstm_norm_g[l].reshape(1, -1).astype(F32))

        wo = w_out[l].astype(BF16)
        G, E = N_GROUPS, EXPERTS_PER_GROUP
        wr = jnp.zeros((D, LANES), F32).at[:, :G].set(w_group[l].astype(F32)).at[:, G:G + G * E].set(
            jnp.transpose(w_router[l].astype(F32), (1, 0, 2)).reshape(D, G * E))
        x1, h2, logits = _out_proj(xf, a.reshape(T, n_diff), hm.reshape(T, -1), wo[:n_diff], wo[n_diff:],
                                   ln_ffn_g[l].reshape(1, D).astype(F32), wr)

        eid, cw = _route(logits, b_group[l], b_router[l])
        n_tiles = (T * TOP_K_INNER) // MOE_TM + N_EXPERTS
        pos, src_token, tile_expert, n_valid = _dispatch_plan(eid, MOE_TM, n_tiles)
        xs = jnp.take(h2, src_token, axis=0)
        Fe = w_gate.shape[-1]
        ys = _moe(tile_expert, n_valid, xs, w_gate[l].reshape(N_EXPERTS, D, Fe),
                  w_up[l].reshape(N_EXPERTS, D, Fe), w_down[l].reshape(N_EXPERTS, Fe, D))
        y0 = jnp.take(ys, pos[:, 0], axis=0)
        y1 = jnp.take(ys, pos[:, 1], axis=0)
        cwp = jnp.zeros((T, LANES), F32).at[:, :TOP_K_INNER].set(cw)
        xf = _final(x1, y0, y1, cwp, ln_f_g.reshape(1, D).astype(F32))
    return xf.reshape(B, S, D)
```

```python
import functools
import math

import jax
import jax.numpy as jnp
from jax import lax
from jax.experimental import pallas as pl
from jax.experimental.pallas import tpu as pltpu

F32 = jnp.float32
BF16 = jnp.bfloat16

EPS = 1e-6
CHUNK = 64
DIFF_HEAD_DIM = 64
N_DIFF_HEADS = 8
MLSTM_HEAD_DIM = 128
N_MLSTM_HEADS = 8
CONV_WIDTH = 4
N_BUCKETS = 32
MAX_DISTANCE = 128
N_GROUPS = 4
EXPERTS_PER_GROUP = 8
N_EXPERTS = N_GROUPS * EXPERTS_PER_GROUP
TOP_K_INNER = 2
LANES = 128
NEG_BIG = -1e30

VMEM_LIMIT = 56 * 1024 * 1024

PROJ_TM, PROJ_TN = 1024, 1024
ATT_TQ = 512
MLSTM_TS = 512
OUT_TM = 256
MOE_TM = 256
FIN_TM = 512


def _cparams(sem):
    return pltpu.CompilerParams(dimension_semantics=sem, vmem_limit_bytes=VMEM_LIMIT)


def _proj_kernel(x_ref, g_ref, w_ref, wg_ref, o_ref, og_ref, h_ref):
    @pl.when(pl.program_id(1) == 0)
    def _():
        x = x_ref[...]
        ms = jnp.mean(x * x, axis=-1, keepdims=True)
        h = (x * lax.rsqrt(ms + EPS) * g_ref[...]).astype(BF16)
        h_ref[...] = h
        og_ref[...] = jnp.dot(h, wg_ref[...], preferred_element_type=F32)

    o_ref[...] = jnp.dot(h_ref[...], w_ref[...], preferred_element_type=F32).astype(o_ref.dtype)


def _proj(x2, g, w_main, w_gates):
    T, D = x2.shape
    N = w_main.shape[1]
    NG = w_gates.shape[1]
    return pl.pallas_call(
        _proj_kernel,
        grid=(T // PROJ_TM, N // PROJ_TN),
        in_specs=[
            pl.BlockSpec((PROJ_TM, D), lambda m, n: (m, 0)),
            pl.BlockSpec((1, D), lambda m, n: (0, 0)),
            pl.BlockSpec((D, PROJ_TN), lambda m, n: (0, n)),
            pl.BlockSpec((D, NG), lambda m, n: (0, 0)),
        ],
        out_specs=[
            pl.BlockSpec((PROJ_TM, PROJ_TN), lambda m, n: (m, n)),
            pl.BlockSpec((PROJ_TM, NG), lambda m, n: (m, 0)),
        ],
        out_shape=[
            jax.ShapeDtypeStruct((T, N), BF16),
            jax.ShapeDtypeStruct((T, NG), F32),
        ],
        scratch_shapes=[pltpu.VMEM((PROJ_TM, D), BF16)],
        compiler_params=_cparams(("parallel", "arbitrary")),
        name="rms_in_proj",
    )(x2, g, w_main, w_gates)


def _t5_bucket(rel):
    half = N_BUCKETS // 2
    max_exact = half // 2
    ret = jnp.where(rel > 0, half, 0)
    n = jnp.abs(rel)
    nf = jnp.maximum(n, 1).astype(F32)
    large = max_exact + (jnp.log(nf / max_exact) / math.log(MAX_DISTANCE / max_exact)
                         * (half - max_exact)).astype(jnp.int32)
    large = jnp.minimum(large, half - 1)
    return ret + jnp.where(n < max_exact, n, large)


def _attn_kernel(scal_ref, q_ref, k_ref, v_ref, bias_ref, g_ref, o_ref, m_ref, l_ref, acc_ref,
                 *, lambda_init):
    h = pl.program_id(1)
    qi = pl.program_id(2)
    tq = ATT_TQ
    lam = scal_ref[0]
    cfar = scal_ref[1 + h]

    q = q_ref[0] * jnp.asarray(DIFF_HEAD_DIM ** -0.5, BF16)
    lane = lax.broadcasted_iota(jnp.int32, q.shape, 1)
    zero = jnp.zeros_like(q)
    qs = jnp.concatenate([jnp.where(lane < DIFF_HEAD_DIM, q, zero),
                          jnp.where(lane >= DIFF_HEAD_DIM, q, zero)], axis=0)

    m_ref[...] = jnp.full(m_ref.shape, NEG_BIG, F32)
    l_ref[...] = jnp.zeros(l_ref.shape, F32)
    acc_ref[...] = jnp.zeros(acc_ref.shape, F32)

    def tile(ki, bias):
        start = pl.multiple_of(ki * tq, tq)
        kt = k_ref[0, pl.ds(start, tq), :]
        vt = v_ref[0, pl.ds(start, tq), :]
        s = lax.dot_general(kt, qs, (((1,), (1,)), ((), ())), preferred_element_type=F32)
        if bias is not None:
            s = s + jnp.concatenate([bias, bias], axis=1)
        m_old = m_ref[...]
        m_new = jnp.maximum(m_old, jnp.max(s, axis=0, keepdims=True))
        alpha = jnp.exp(m_old - m_new)
        p = jnp.exp(s - m_new)
        l_ref[...] = alpha * l_ref[...] + jnp.sum(p, axis=0, keepdims=True)
        pv = lax.dot_general(vt, p.astype(BF16), (((0,), (0,)), ((), ())), preferred_element_type=F32)
        acc_ref[...] = alpha * acc_ref[...] + pv
        m_ref[...] = m_new

    def far_body(ki, c):
        tile(ki, None)
        return c

    lax.fori_loop(0, jnp.maximum(qi - 1, 0), far_body, 0)
    m_ref[...] = m_ref[...] + cfar

    @pl.when(qi >= 1)
    def _():
        tile(qi - 1, bias_ref[0, 0])

    tile(qi, bias_ref[0, 1])

    acc = acc_ref[...] * (1.0 / l_ref[...])
    o_t = acc[:, 0:tq] - lam * acc[:, tq:2 * tq]
    ms = jnp.mean(o_t * o_t, axis=0, keepdims=True)
    y = (o_t * lax.rsqrt(ms + EPS)).T * (g_ref[...] * (1.0 - lambda_init))
    o_ref[0] = y.astype(o_ref.dtype)


def _diff_attention(proj3, scal, bias_tiles, gnorm, lambda_init):
    B, S, _ = proj3.shape
    H = N_DIFF_HEADS
    tq = ATT_TQ
    kern = functools.partial(_attn_kernel, lambda_init=lambda_init)
    return pl.pallas_call(
        kern,
        grid=(B, H, S // tq),
        in_specs=[
            pl.BlockSpec(memory_space=pltpu.SMEM),
            pl.BlockSpec((1, tq, LANES), lambda b, h, i: (b, i, h)),
            pl.BlockSpec((1, S, LANES), lambda b, h, i: (b, 0, H + h)),
            pl.BlockSpec((1, S, LANES), lambda b, h, i: (b, 0, 2 * H + h)),
            pl.BlockSpec((1, 2, tq, tq), lambda b, h, i: (h, 0, 0, 0)),
            pl.BlockSpec((1, LANES), lambda b, h, i: (0, h)),
        ],
        out_specs=pl.BlockSpec((1, tq, LANES), lambda b, h, i: (b, i, h)),
        out_shape=jax.ShapeDtypeStruct((B, S, H * LANES), BF16),
        scratch_shapes=[
            pltpu.VMEM((1, 2 * tq), F32),
            pltpu.VMEM((1, 2 * tq), F32),
            pltpu.VMEM((LANES, 2 * tq), F32),
        ],
        compiler_params=_cparams(("parallel", "parallel", "arbitrary")),
        name="diff_attention",
    )(scal, proj3, proj3, proj3, bias_tiles, gnorm)


def _log_sigmoid(x):
    return jnp.minimum(x, 0.0) - jnp.log(1.0 + jnp.exp(-jnp.abs(x)))


def _sigmoid(x):
    return 1.0 / (1.0 + jnp.exp(-x))


def _mlstm_kernel(q_ref, k_ref, v_ref, o_ref, gi_ref, gf_ref, cw_ref, cb_ref, bi_ref, bf_ref, gn_ref,
                  out_ref, qext_ref, kext_ref, ct_ref, n_ref, m_ref):
    sb = pl.program_id(1)
    L = CHUNK
    dh = MLSTM_HEAD_DIM
    H = N_MLSTM_HEADS
    ts = MLSTM_TS
    pad = 8

    @pl.when(sb == 0)
    def _():
        qext_ref[0:pad, :] = jnp.zeros((pad, H * dh), F32)
        kext_ref[0:pad, :] = jnp.zeros((pad, H * dh), F32)
        ct_ref[...] = jnp.zeros(ct_ref.shape, F32)
        n_ref[...] = jnp.zeros(n_ref.shape, F32)
        m_ref[...] = jnp.zeros(m_ref.shape, F32)

    qext_ref[pad:pad + ts, :] = q_ref[0].astype(F32)
    kext_ref[pad:pad + ts, :] = k_ref[0].astype(F32)

    row = lax.broadcasted_iota(jnp.int32, (L, L), 0)
    col = lax.broadcasted_iota(jnp.int32, (L, L), 1)
    tril = col <= row
    ltri = tril.astype(F32)

    def conv_silu(ext_ref, base, h, off):
        win = ext_ref[pl.ds(base, L + pad), h * dh:(h + 1) * dh]
        w = cw_ref[:, off + h * dh:off + (h + 1) * dh]
        y = cb_ref[:, off + h * dh:off + (h + 1) * dh]
        for j in range(CONV_WIDTH):
            lo = pad - (CONV_WIDTH - 1) + j
            y = y + w[j:j + 1, :] * win[lo:lo + L, :]
        return y * _sigmoid(y)

    def chunk_body(c, carry):
        base = pl.multiple_of(c * L, L)
        li = gi_ref[0, pl.ds(base, L), :] + bi_ref[...]
        logf = _log_sigmoid(gf_ref[0, pl.ds(base, L), :] + bf_ref[...])
        b = jnp.dot(ltri, logf, preferred_element_type=F32, precision=lax.Precision.HIGHEST)
        a = li - b
        g_row = b[L - 1:L, :]
        m_row = m_ref[...]
        m_new_row = g_row + jnp.maximum(m_row, jnp.max(a, axis=0, keepdims=True))
        a_t = a.T

        for h in range(H):
            qc = conv_silu(qext_ref, base, h, 0)
            kc = conv_silu(kext_ref, base, h, H * dh) * (dh ** -0.5)
            qb = qc.astype(BF16)
            kb = kc.astype(BF16)
            vb = v_ref[0, pl.ds(base, L), h * dh:(h + 1) * dh]

            a_row = a_t[h:h + 1, :]
            a_col = a[:, h:h + 1]
            b_col = b[:, h:h + 1]
            m_prev = m_row[:, h:h + 1]
            m_next = m_new_row[:, h:h + 1]
            g_h = g_row[:, h:h + 1]

            amat = jnp.where(tril, a_row, NEG_BIG)
            mcol = jnp.maximum(jnp.max(amat, axis=-1, keepdims=True), m_prev)
            wts = jnp.exp(amat - mcol)
            inter = jnp.exp(m_prev - mcol)

            s = lax.dot_general(qb, kb, (((1,), (1,)), ((), ())), preferred_element_type=F32)
            sqk = s * wts
            ct = ct_ref[h]
            nrow = n_ref[h:h + 1, :]
            num = (jnp.dot(sqk.astype(BF16), vb, preferred_element_type=F32)
                   + inter * jnp.dot(qb, ct.astype(BF16), preferred_element_type=F32))
            den = (jnp.sum(sqk, axis=-1, keepdims=True)
                   + inter * jnp.sum(qb.astype(F32) * nrow, axis=-1, keepdims=True))
            hv = num / jnp.maximum(jnp.abs(den), jnp.exp(-(b_col + mcol)))

            wt = jnp.exp(g_h + a_col - m_next)
            decay = jnp.exp(g_h + m_prev - m_next)
            wv = (wt * vb.astype(F32)).astype(BF16)
            ct_ref[h] = decay * ct + lax.dot_general(kb, wv, (((0,), (0,)), ((), ())),
                                                     preferred_element_type=F32)
            n_ref[h:h + 1, :] = decay * nrow + jnp.sum(wt * kb.astype(F32), axis=0, keepdims=True)

            ms = jnp.mean(hv * hv, axis=-1, keepdims=True)
            y = hv * lax.rsqrt(ms + EPS) * gn_ref[:, h * dh:(h + 1) * dh]
            og = o_ref[0, pl.ds(base, L), h * dh:(h + 1) * dh].astype(F32)
            out_ref[0, pl.ds(base, L), h * dh:(h + 1) * dh] = (y * _sigmoid(og)).astype(out_ref.dtype)

        m_ref[...] = m_new_row
        return carry

    lax.fori_loop(0, ts // L, chunk_body, 0)

    qext_ref[0:pad, :] = qext_ref[ts:ts + pad, :]
    kext_ref[0:pad, :] = kext_ref[ts:ts + pad, :]


def _mlstm(proj3, gates3, conv_w, conv_b, bi_row, bf_row, gnorm):
    B, S, _ = proj3.shape
    W = N_MLSTM_HEADS * MLSTM_HEAD_DIM
    ts = MLSTM_TS
    first = 3
    blk = lambda j: pl.BlockSpec((1, ts, W), lambda b, s: (b, s, j))
    full = lambda shape: pl.BlockSpec(shape, lambda b, s: (0,) * len(shape))
    return pl.pallas_call(
        _mlstm_kernel,
        grid=(B, S // ts),
        in_specs=[
            blk(first), blk(first + 1), blk(first + 2), blk(first + 3),
            pl.BlockSpec((1, ts, LANES), lambda b, s: (b, s, 0)),
            pl.BlockSpec((1, ts, LANES), lambda b, s: (b, s, 1)),
            full((CONV_WIDTH, 2 * W)), full((1, 2 * W)),
            full((1, LANES)), full((1, LANES)), full((1, W)),
        ],
        out_specs=pl.BlockSpec((1, ts, W), lambda b, s: (b, s, 0)),
        out_shape=jax.ShapeDtypeStruct((B, S, W), BF16),
        scratch_shapes=[
            pltpu.VMEM((ts + 8, W), F32),
            pltpu.VMEM((ts + 8, W), F32),
            pltpu.VMEM((N_MLSTM_HEADS, MLSTM_HEAD_DIM, MLSTM_HEAD_DIM), F32),
            pltpu.VMEM((N_MLSTM_HEADS, MLSTM_HEAD_DIM), F32),
            pltpu.VMEM((1, LANES), F32),
        ],
        compiler_params=_cparams(("parallel", "arbitrary")),
        name="mlstm",
    )(proj3, proj3, proj3, proj3, gates3, gates3, conv_w, conv_b, bi_row, bf_row, gnorm)


def _out_kernel(x_ref, a_ref, hm_ref, wa_ref, wm_ref, g_ref, wr_ref, x1_ref, h2_ref, lg_ref):
    y = (jnp.dot(a_ref[...], wa_ref[...], preferred_element_type=F32)
         + jnp.dot(hm_ref[...], wm_ref[...], preferred_element_type=F32))
    x1 = x_ref[...] + y
    x1_ref[...] = x1
    ms = jnp.mean(x1 * x1, axis=-1, keepdims=True)
    h2 = x1 * lax.rsqrt(ms + EPS) * g_ref[...]
    h2_ref[...] = h2.astype(h2_ref.dtype)
    lg_ref[...] = jnp.dot(h2, wr_ref[...], preferred_element_type=F32, precision=lax.Precision.HIGHEST)


def _out_proj(x2, a2, hm2, wa, wm, g, wr):
    T, D = x2.shape
    W = a2.shape[1]
    tm = OUT_TM
    const = lambda shape: pl.BlockSpec(shape, lambda m: (0, 0))
    return pl.pallas_call(
        _out_kernel,
        grid=(T // tm,),
        in_specs=[
            pl.BlockSpec((tm, D), lambda m: (m, 0)),
            pl.BlockSpec((tm, W), lambda m: (m, 0)),
            pl.BlockSpec((tm, W), lambda m: (m, 0)),
            const((W, D)), const((W, D)), const((1, D)), const((D, LANES)),
        ],
        out_specs=[
            pl.BlockSpec((tm, D), lambda m: (m, 0)),
            pl.BlockSpec((tm, D), lambda m: (m, 0)),
            pl.BlockSpec((tm, LANES), lambda m: (m, 0)),
        ],
        out_shape=[
            jax.ShapeDtypeStruct((T, D), F32),
            jax.ShapeDtypeStruct((T, D), BF16),
            jax.ShapeDtypeStruct((T, LANES), F32),
        ],
        compiler_params=_cparams(("parallel",)),
        name="out_proj_router",
    )(x2, a2, hm2, wa, wm, g, wr)


def _moe_kernel(te_ref, nv_ref, xs_ref, wg_ref, wu_ref, wd_ref, ys_ref):
    j = pl.program_id(0)

    @pl.when(j < nv_ref[0])
    def _():
        xs = xs_ref[...]
        gt = jnp.dot(xs, wg_ref[...].astype(BF16), preferred_element_type=F32)
        up = jnp.dot(xs, wu_ref[...].astype(BF16), preferred_element_type=F32)
        hid = (gt * _sigmoid(gt) * up).astype(BF16)
        ys_ref[...] = jnp.dot(hid, wd_ref[...].astype(BF16),
                              preferred_element_type=F32).astype(ys_ref.dtype)

    @pl.when(j >= nv_ref[0])
    def _():
        ys_ref[...] = jnp.zeros(ys_ref.shape, ys_ref.dtype)


def _moe(tile_expert, n_valid, xs, wg, wu, wd):
    R, D = xs.shape
    F = wg.shape[2]
    tm = MOE_TM
    grid_spec = pltpu.PrefetchScalarGridSpec(
        num_scalar_prefetch=2,
        grid=(R // tm,),
        in_specs=[
            pl.BlockSpec((tm, D), lambda j, te, nv: (j, 0)),
            pl.BlockSpec((None, D, F), lambda j, te, nv: (te[j], 0, 0)),
            pl.BlockSpec((None, D, F), lambda j, te, nv: (te[j], 0, 0)),
            pl.BlockSpec((None, F, D), lambda j, te, nv: (te[j], 0, 0)),
        ],
        out_specs=pl.BlockSpec((tm, D), lambda j, te, nv: (j, 0)),
    )
    return pl.pallas_call(
        _moe_kernel,
        grid_spec=grid_spec,
        out_shape=jax.ShapeDtypeStruct((R, D), BF16),
        compiler_params=_cparams(("arbitrary",)),
        name="moe_experts",
    )(tile_expert, n_valid, xs, wg, wu, wd)


def _final_kernel(x1_ref, y0_ref, y1_ref, cw_ref, g_ref, o_ref):
    cw = cw_ref[...]
    x = (x1_ref[...] + cw[:, 0:1] * y0_ref[...].astype(F32) + cw[:, 1:2] * y1_ref[...].astype(F32))
    ms = jnp.mean(x * x, axis=-1, keepdims=True)
    o_ref[...] = x * lax.rsqrt(ms + EPS) * g_ref[...]


def _final(x1, y0, y1, cw, g):
    T, D = x1.shape
    tm = FIN_TM
    row = lambda w: pl.BlockSpec((tm, w), lambda m: (m, 0))
    return pl.pallas_call(
        _final_kernel,
        grid=(T // tm,),
        in_specs=[row(D), row(D), row(D), row(LANES), pl.BlockSpec((1, D), lambda m: (0, 0))],
        out_specs=row(D),
        out_shape=jax.ShapeDtypeStruct((T, D), F32),
        compiler_params=_cparams(("parallel",)),
        name="combine_final_norm",
    )(x1, y0, y1, cw, g)


def _route(logits, b_group, b_router):
    T = logits.shape[0]
    G, E = N_GROUPS, EXPERTS_PER_GROUP
    gl = logits[:, :G] + b_group.astype(F32)
    gp = jax.nn.softmax(gl, axis=-1)
    gsel = jnp.argmax(gl, axis=-1)
    gw = jnp.take_along_axis(gp, gsel[:, None], axis=1)[:, 0]
    el = logits[:, G:G + G * E].reshape(T, G, E) + b_router.astype(F32)
    el_sel = jnp.take_along_axis(el, gsel[:, None, None], axis=1)[:, 0, :]
    top_v, top_i = lax.top_k(el_sel, TOP_K_INNER)
    top_w = jax.nn.softmax(top_v, axis=-1)
    eid = (gsel[:, None] * E + top_i).astype(jnp.int32)
    cw = gw[:, None] * top_w
    return eid, cw


def _dispatch_plan(eid, tm, n_tiles):
    T2 = eid.size
    flat_e = eid.reshape(-1)
    onehot = (flat_e[:, None] == jnp.arange(N_EXPERTS, dtype=jnp.int32)[None, :]).astype(jnp.int32)
    csum = jnp.cumsum(onehot, axis=0)
    rank = jnp.sum((csum - onehot) * onehot, axis=1)
    counts = csum[-1]
    tiles_per_e = (counts + tm - 1) // tm
    tile_end = jnp.cumsum(tiles_per_e)
    row_start = (tile_end - tiles_per_e) * tm
    pos = (row_start[flat_e] + rank).astype(jnp.int32)
    n_valid = tile_end[-1].astype(jnp.int32)
    tile_ids = jnp.arange(n_tiles, dtype=jnp.int32)
    tile_expert = jnp.searchsorted(tile_end, jnp.minimum(tile_ids, n_valid - 1), side="right")
    tile_expert = jnp.minimum(tile_expert, N_EXPERTS - 1).astype(jnp.int32)
    src_token = jnp.zeros((n_tiles * tm,), jnp.int32).at[pos].set(
        jnp.arange(T2, dtype=jnp.int32) // TOP_K_INNER)
    return pos.reshape(eid.shape), src_token, tile_expert, n_valid.reshape(1)


def kernel(x, rel_bias, ln_mix_g, w_in, conv_w, conv_b, b_i, b_f, lam_q1, lam_k1, lam_q2, lam_k2,
           diff_norm_g, mlstm_norm_g, w_out, ln_ffn_g, w_group, b_group, w_router, b_router,
           w_gate, w_up, w_down, ln_f_g):
    B, S, D = x.shape
    T = B * S
    depth = w_in.shape[0]
    assert depth == 1, "the final rmsnorm is fused into the single layer's combine kernel"
    Hm = N_MLSTM_HEADS
    n_main = w_in.shape[2] - 2 * Hm
    n_diff = N_DIFF_HEADS * 2 * DIFF_HEAD_DIM
    xf = x.reshape(T, D)

    for l in range(depth):
        lambda_init = 0.8 - 0.6 * math.exp(-0.3 * l)
        w_main = w_in[l, :, :n_main].astype(BF16)
        wgt = w_in[l, :, n_main:]
        w_gates = jnp.zeros((D, 2 * LANES), F32).at[:, :Hm].set(wgt[:, :Hm]).at[:, LANES:LANES + Hm].set(
            wgt[:, Hm:]).astype(BF16)
        bi_row = jnp.zeros((1, LANES), F32).at[0, :Hm].set(b_i[l].astype(F32))
        bf_row = jnp.zeros((1, LANES), F32).at[0, :Hm].set(b_f[l].astype(F32))
        lam = (jnp.exp(jnp.sum(lam_q1[l].astype(F32) * lam_k1[l].astype(F32)))
               - jnp.exp(jnp.sum(lam_q2[l].astype(F32) * lam_k2[l].astype(F32))) + lambda_init)
        tq = ATT_TQ
        assert tq >= MAX_DISTANCE and tq % CHUNK == 0
        ii = jnp.arange(tq, dtype=jnp.int32)
        rel0 = ii[:, None] - ii[None, :]
        rb = rel_bias.astype(F32)

        def bias_of(bucket):
            out = jnp.zeros((N_DIFF_HEADS,) + bucket.shape, F32)
            for bkt in range(N_BUCKETS):
                out = jnp.where(bucket[None] == bkt, rb[bkt][:, None, None], out)
            return out

        bias_prev = bias_of(_t5_bucket(rel0 - tq))
        bias_diag = jnp.where(((ii[:, None] // CHUNK) <= (ii[None, :] // CHUNK))[None],
                              bias_of(_t5_bucket(rel0)), NEG_BIG)
        bias_tiles = jnp.stack([bias_prev, bias_diag], axis=1)
        cfar = rb[N_BUCKETS // 2 - 1]
        scal = jnp.concatenate([lam.reshape(1), cfar]).astype(F32)

        proj, gates = _proj(xf, ln_mix_g[l].reshape(1, D).astype(F32), w_main, w_gates)
        proj3 = proj.reshape(B, S, n_main)
        a = _diff_attention(proj3, scal, bias_tiles, diff_norm_g[l].reshape(1, n_diff).astype(F32),
                            lambda_init)
        hm = _mlstm(proj3, gates.reshape(B, S, 2 * LANES), conv_w[l].astype(F32),
                    conv_b[l].reshape(1, -1).astype(F32), bi_row, bf_row,
                    mlstm_norm_g[l].reshape(1, -1).astype(F32))

        wo = w_out[l].astype(BF16)
        G, E = N_GROUPS, EXPERTS_PER_GROUP
        wr = jnp.zeros((D, LANES), F32).at[:, :G].set(w_group[l].astype(F32)).at[:, G:G + G * E].set(
            jnp.transpose(w_router[l].astype(F32), (1, 0, 2)).reshape(D, G * E))
        x1, h2, logits = _out_proj(xf, a.reshape(T, n_diff), hm.reshape(T, -1), wo[:n_diff], wo[n_diff:],
                                   ln_ffn_g[l].reshape(1, D).astype(F32), wr)

        eid, cw = _route(logits, b_group[l], b_router[l])
        n_tiles = (T * TOP_K_INNER) // MOE_TM + N_EXPERTS
        pos, src_token, tile_expert, n_valid = _dispatch_plan(eid, MOE_TM, n_tiles)
        xs = jnp.take(h2, src_token, axis=0)
        Fe = w_gate.shape[-1]
        ys = _moe(tile_expert, n_valid, xs, w_gate[l].reshape(N_EXPERTS, D, Fe),
                  w_up[l].reshape(N_EXPERTS, D, Fe), w_down[l].reshape(N_EXPERTS, Fe, D))
        y0 = jnp.take(ys, pos[:, 0], axis=0)
        y1 = jnp.take(ys, pos[:, 1], axis=0)
        cwp = jnp.zeros((T, LANES), F32).at[:, :TOP_K_INNER].set(cw)
        xf = _final(x1, y0, y1, cwp, ln_f_g.reshape(1, D).astype(F32))
    return xf.reshape(B, S, D)
```

```python
import functools
import math

import jax
import jax.numpy as jnp
from jax import lax
from jax.experimental import pallas as pl
from jax.experimental.pallas import tpu as pltpu
from jax.experimental.pallas import tpu_sc as plsc

F32 = jnp.float32
BF16 = jnp.bfloat16

EPS = 1e-6
CHUNK = 64
DIFF_HEAD_DIM = 64
N_DIFF_HEADS = 8
MLSTM_HEAD_DIM = 128
N_MLSTM_HEADS = 8
CONV_WIDTH = 4
N_BUCKETS = 32
MAX_DISTANCE = 128
N_GROUPS = 4
EXPERTS_PER_GROUP = 8
N_EXPERTS = N_GROUPS * EXPERTS_PER_GROUP
TOP_K_INNER = 2
LANES = 128
NEG_BIG = -1e30

VMEM_LIMIT = 56 * 1024 * 1024

PROJ_TM, PROJ_TN = 1024, 1024
ATT_TQ = 512
MLSTM_TS = 512
OUT_TM = 256
MOE_TM = 256
FIN_TM = 512


def _cparams(sem):
    return pltpu.CompilerParams(dimension_semantics=sem, vmem_limit_bytes=VMEM_LIMIT)


_HI_MASK = 0xFFFF0000


def _pack_bf16_pairs(x):
    half = x.shape[-1] // 2
    xb = x.astype(BF16).astype(F32)
    lo = pltpu.bitcast(xb[:, :half], jnp.uint32)
    hi = pltpu.bitcast(xb[:, half:], jnp.uint32)
    return (hi & jnp.uint32(_HI_MASK)) | (lo >> 16)


def _unpack_bf16_pairs(w):
    lo = pltpu.bitcast(w << 16, F32)
    hi = pltpu.bitcast(w & jnp.uint32(_HI_MASK), F32)
    return lo, hi


def _proj_kernel(x_ref, g_ref, w_ref, wg_ref, o_ref, og_ref, h_ref):
    @pl.when(pl.program_id(1) == 0)
    def _():
        x = x_ref[...]
        ms = jnp.mean(x * x, axis=-1, keepdims=True)
        h = (x * lax.rsqrt(ms + EPS) * g_ref[...]).astype(BF16)
        h_ref[...] = h
        og_ref[...] = jnp.dot(h, wg_ref[...], preferred_element_type=F32)

    o_ref[...] = jnp.dot(h_ref[...], w_ref[...], preferred_element_type=F32).astype(o_ref.dtype)


def _proj(x2, g, w_main, w_gates):
    T, D = x2.shape
    N = w_main.shape[1]
    NG = w_gates.shape[1]
    return pl.pallas_call(
        _proj_kernel,
        grid=(T // PROJ_TM, N // PROJ_TN),
        in_specs=[
            pl.BlockSpec((PROJ_TM, D), lambda m, n: (m, 0)),
            pl.BlockSpec((1, D), lambda m, n: (0, 0)),
            pl.BlockSpec((D, PROJ_TN), lambda m, n: (0, n)),
            pl.BlockSpec((D, NG), lambda m, n: (0, 0)),
        ],
        out_specs=[
            pl.BlockSpec((PROJ_TM, PROJ_TN), lambda m, n: (m, n)),
            pl.BlockSpec((PROJ_TM, NG), lambda m, n: (m, 0)),
        ],
        out_shape=[
            jax.ShapeDtypeStruct((T, N), BF16),
            jax.ShapeDtypeStruct((T, NG), F32),
        ],
        scratch_shapes=[pltpu.VMEM((PROJ_TM, D), BF16)],
        compiler_params=_cparams(("parallel", "arbitrary")),
        name="rms_in_proj",
    )(x2, g, w_main, w_gates)


def _t5_bucket(rel):
    half = N_BUCKETS // 2
    max_exact = half // 2
    ret = jnp.where(rel > 0, half, 0)
    n = jnp.abs(rel)
    nf = jnp.maximum(n, 1).astype(F32)
    large = max_exact + (jnp.log(nf / max_exact) / math.log(MAX_DISTANCE / max_exact)
                         * (half - max_exact)).astype(jnp.int32)
    large = jnp.minimum(large, half - 1)
    return ret + jnp.where(n < max_exact, n, large)


def _attn_kernel(scal_ref, q_ref, k_ref, v_ref, bias_ref, g_ref, o_ref, m_ref, l_ref, acc_ref,
                 *, lambda_init):
    h = pl.program_id(1)
    qi = pl.program_id(2)
    tq = ATT_TQ
    lam = scal_ref[0]
    cfar = scal_ref[1 + h]

    q = q_ref[0] * jnp.asarray(DIFF_HEAD_DIM ** -0.5, BF16)
    lane = lax.broadcasted_iota(jnp.int32, q.shape, 1)
    zero = jnp.zeros_like(q)
    qs = jnp.concatenate([jnp.where(lane < DIFF_HEAD_DIM, q, zero),
                          jnp.where(lane >= DIFF_HEAD_DIM, q, zero)], axis=0)

    m_ref[...] = jnp.full(m_ref.shape, NEG_BIG, F32)
    l_ref[...] = jnp.zeros(l_ref.shape, F32)
    acc_ref[...] = jnp.zeros(acc_ref.shape, F32)

    def tile(ki, bias):
        start = pl.multiple_of(ki * tq, tq)
        kt = k_ref[0, pl.ds(start, tq), :]
        vt = v_ref[0, pl.ds(start, tq), :]
        s = lax.dot_general(kt, qs, (((1,), (1,)), ((), ())), preferred_element_type=F32)
        if bias is not None:
            s = s + jnp.concatenate([bias, bias], axis=1)
        m_old = m_ref[...]
        m_new = jnp.maximum(m_old, jnp.max(s, axis=0, keepdims=True))
        alpha = jnp.exp(m_old - m_new)
        p = jnp.exp(s - m_new)
        l_ref[...] = alpha * l_ref[...] + jnp.sum(p, axis=0, keepdims=True)
        pv = lax.dot_general(vt, p.astype(BF16), (((0,), (0,)), ((), ())), preferred_element_type=F32)
        acc_ref[...] = alpha * acc_ref[...] + pv
        m_ref[...] = m_new

    def far_body(ki, c):
        tile(ki, None)
        return c

    lax.fori_loop(0, jnp.maximum(qi - 1, 0), far_body, 0)
    m_ref[...] = m_ref[...] + cfar

    @pl.when(qi >= 1)
    def _():
        tile(qi - 1, bias_ref[0, 0])

    tile(qi, bias_ref[0, 1])

    acc = acc_ref[...] * (1.0 / l_ref[...])
    o_t = acc[:, 0:tq] - lam * acc[:, tq:2 * tq]
    ms = jnp.mean(o_t * o_t, axis=0, keepdims=True)
    y = (o_t * lax.rsqrt(ms + EPS)).T * (g_ref[...] * (1.0 - lambda_init))
    o_ref[0] = y.astype(o_ref.dtype)


def _diff_attention(proj3, scal, bias_tiles, gnorm, lambda_init):
    B, S, _ = proj3.shape
    H = N_DIFF_HEADS
    tq = ATT_TQ
    kern = functools.partial(_attn_kernel, lambda_init=lambda_init)
    return pl.pallas_call(
        kern,
        grid=(B, H, S // tq),
        in_specs=[
            pl.BlockSpec(memory_space=pltpu.SMEM),
            pl.BlockSpec((1, tq, LANES), lambda b, h, i: (b, i, h)),
            pl.BlockSpec((1, S, LANES), lambda b, h, i: (b, 0, H + h)),
            pl.BlockSpec((1, S, LANES), lambda b, h, i: (b, 0, 2 * H + h)),
            pl.BlockSpec((1, 2, tq, tq), lambda b, h, i: (h, 0, 0, 0)),
            pl.BlockSpec((1, LANES), lambda b, h, i: (0, h)),
        ],
        out_specs=pl.BlockSpec((1, tq, LANES), lambda b, h, i: (b, i, h)),
        out_shape=jax.ShapeDtypeStruct((B, S, H * LANES), BF16),
        scratch_shapes=[
            pltpu.VMEM((1, 2 * tq), F32),
            pltpu.VMEM((1, 2 * tq), F32),
            pltpu.VMEM((LANES, 2 * tq), F32),
        ],
        compiler_params=_cparams(("parallel", "parallel", "arbitrary")),
        name="diff_attention",
    )(scal, proj3, proj3, proj3, bias_tiles, gnorm)


def _log_sigmoid(x):
    return jnp.minimum(x, 0.0) - jnp.log(1.0 + jnp.exp(-jnp.abs(x)))


def _sigmoid(x):
    return 1.0 / (1.0 + jnp.exp(-x))


def _mlstm_kernel(q_ref, k_ref, v_ref, o_ref, gi_ref, gf_ref, cw_ref, cb_ref, bi_ref, bf_ref, gn_ref,
                  out_ref, qext_ref, kext_ref, ct_ref, n_ref, m_ref):
    sb = pl.program_id(1)
    L = CHUNK
    dh = MLSTM_HEAD_DIM
    H = N_MLSTM_HEADS
    ts = MLSTM_TS
    pad = 8

    @pl.when(sb == 0)
    def _():
        qext_ref[0:pad, :] = jnp.zeros((pad, H * dh), F32)
        kext_ref[0:pad, :] = jnp.zeros((pad, H * dh), F32)
        ct_ref[...] = jnp.zeros(ct_ref.shape, F32)
        n_ref[...] = jnp.zeros(n_ref.shape, F32)
        m_ref[...] = jnp.zeros(m_ref.shape, F32)

    qext_ref[pad:pad + ts, :] = q_ref[0].astype(F32)
    kext_ref[pad:pad + ts, :] = k_ref[0].astype(F32)

    row = lax.broadcasted_iota(jnp.int32, (L, L), 0)
    col = lax.broadcasted_iota(jnp.int32, (L, L), 1)
    tril = col <= row
    ltri = tril.astype(F32)

    def conv_silu(ext_ref, base, h, off):
        win = ext_ref[pl.ds(base, L + pad), h * dh:(h + 1) * dh]
        w = cw_ref[:, off + h * dh:off + (h + 1) * dh]
        y = cb_ref[:, off + h * dh:off + (h + 1) * dh]
        for j in range(CONV_WIDTH):
            lo = pad - (CONV_WIDTH - 1) + j
            y = y + w[j:j + 1, :] * win[lo:lo + L, :]
        return y * _sigmoid(y)

    def chunk_body(c, carry):
        base = pl.multiple_of(c * L, L)
        li = gi_ref[0, pl.ds(base, L), :] + bi_ref[...]
        logf = _log_sigmoid(gf_ref[0, pl.ds(base, L), :] + bf_ref[...])
        b = jnp.dot(ltri, logf, preferred_element_type=F32, precision=lax.Precision.HIGHEST)
        a = li - b
        g_row = b[L - 1:L, :]
        m_row = m_ref[...]
        m_new_row = g_row + jnp.maximum(m_row, jnp.max(a, axis=0, keepdims=True))
        a_t = a.T

        for h in range(H):
            qc = conv_silu(qext_ref, base, h, 0)
            kc = conv_silu(kext_ref, base, h, H * dh) * (dh ** -0.5)
            qb = qc.astype(BF16)
            kb = kc.astype(BF16)
            vb = v_ref[0, pl.ds(base, L), h * dh:(h + 1) * dh]

            a_row = a_t[h:h + 1, :]
            a_col = a[:, h:h + 1]
            b_col = b[:, h:h + 1]
            m_prev = m_row[:, h:h + 1]
            m_next = m_new_row[:, h:h + 1]
            g_h = g_row[:, h:h + 1]

            amat = jnp.where(tril, a_row, NEG_BIG)
            mcol = jnp.maximum(jnp.max(amat, axis=-1, keepdims=True), m_prev)
            wts = jnp.exp(amat - mcol)
            inter = jnp.exp(m_prev - mcol)

            s = lax.dot_general(qb, kb, (((1,), (1,)), ((), ())), preferred_element_type=F32)
            sqk = s * wts
            ct = ct_ref[h]
            nrow = n_ref[h:h + 1, :]
            num = (jnp.dot(sqk.astype(BF16), vb, preferred_element_type=F32)
                   + inter * jnp.dot(qb, ct.astype(BF16), preferred_element_type=F32))
            den = (jnp.sum(sqk, axis=-1, keepdims=True)
                   + inter * jnp.sum(qb.astype(F32) * nrow, axis=-1, keepdims=True))
            hv = num / jnp.maximum(jnp.abs(den), jnp.exp(-(b_col + mcol)))

            wt = jnp.exp(g_h + a_col - m_next)
            decay = jnp.exp(g_h + m_prev - m_next)
            wv = (wt * vb.astype(F32)).astype(BF16)
            ct_ref[h] = decay * ct + lax.dot_general(kb, wv, (((0,), (0,)), ((), ())),
                                                     preferred_element_type=F32)
            n_ref[h:h + 1, :] = decay * nrow + jnp.sum(wt * kb.astype(F32), axis=0, keepdims=True)

            ms = jnp.mean(hv * hv, axis=-1, keepdims=True)
            y = hv * lax.rsqrt(ms + EPS) * gn_ref[:, h * dh:(h + 1) * dh]
            og = o_ref[0, pl.ds(base, L), h * dh:(h + 1) * dh].astype(F32)
            out_ref[0, pl.ds(base, L), h * dh:(h + 1) * dh] = (y * _sigmoid(og)).astype(out_ref.dtype)

        m_ref[...] = m_new_row
        return carry

    lax.fori_loop(0, ts // L, chunk_body, 0)

    qext_ref[0:pad, :] = qext_ref[ts:ts + pad, :]
    kext_ref[0:pad, :] = kext_ref[ts:ts + pad, :]


def _mlstm(proj3, gates3, conv_w, conv_b, bi_row, bf_row, gnorm):
    B, S, _ = proj3.shape
    W = N_MLSTM_HEADS * MLSTM_HEAD_DIM
    ts = MLSTM_TS
    first = 3
    blk = lambda j: pl.BlockSpec((1, ts, W), lambda b, s: (b, s, j))
    full = lambda shape: pl.BlockSpec(shape, lambda b, s: (0,) * len(shape))
    return pl.pallas_call(
        _mlstm_kernel,
        grid=(B, S // ts),
        in_specs=[
            blk(first), blk(first + 1), blk(first + 2), blk(first + 3),
            pl.BlockSpec((1, ts, LANES), lambda b, s: (b, s, 0)),
            pl.BlockSpec((1, ts, LANES), lambda b, s: (b, s, 1)),
            full((CONV_WIDTH, 2 * W)), full((1, 2 * W)),
            full((1, LANES)), full((1, LANES)), full((1, W)),
        ],
        out_specs=pl.BlockSpec((1, ts, W), lambda b, s: (b, s, 0)),
        out_shape=jax.ShapeDtypeStruct((B, S, W), BF16),
        scratch_shapes=[
            pltpu.VMEM((ts + 8, W), F32),
            pltpu.VMEM((ts + 8, W), F32),
            pltpu.VMEM((N_MLSTM_HEADS, MLSTM_HEAD_DIM, MLSTM_HEAD_DIM), F32),
            pltpu.VMEM((N_MLSTM_HEADS, MLSTM_HEAD_DIM), F32),
            pltpu.VMEM((1, LANES), F32),
        ],
        compiler_params=_cparams(("parallel", "arbitrary")),
        name="mlstm",
    )(proj3, proj3, proj3, proj3, gates3, gates3, conv_w, conv_b, bi_row, bf_row, gnorm)


def _out_kernel(x_ref, a_ref, hm_ref, wa_ref, wm_ref, g_ref, wr_ref, x1_ref, h2_ref, lg_ref):
    y = (jnp.dot(a_ref[...], wa_ref[...], preferred_element_type=F32)
         + jnp.dot(hm_ref[...], wm_ref[...], preferred_element_type=F32))
    x1 = x_ref[...] + y
    x1_ref[...] = x1
    ms = jnp.mean(x1 * x1, axis=-1, keepdims=True)
    h2 = x1 * lax.rsqrt(ms + EPS) * g_ref[...]
    h2_ref[...] = _pack_bf16_pairs(h2)
    lg_ref[...] = jnp.dot(h2, wr_ref[...], preferred_element_type=F32, precision=lax.Precision.HIGHEST)


def _out_proj(x2, a2, hm2, wa, wm, g, wr):
    T, D = x2.shape
    W = a2.shape[1]
    tm = OUT_TM
    const = lambda shape: pl.BlockSpec(shape, lambda m: (0, 0))
    return pl.pallas_call(
        _out_kernel,
        grid=(T // tm,),
        in_specs=[
            pl.BlockSpec((tm, D), lambda m: (m, 0)),
            pl.BlockSpec((tm, W), lambda m: (m, 0)),
            pl.BlockSpec((tm, W), lambda m: (m, 0)),
            const((W, D)), const((W, D)), const((1, D)), const((D, LANES)),
        ],
        out_specs=[
            pl.BlockSpec((tm, D), lambda m: (m, 0)),
            pl.BlockSpec((tm, D // 2), lambda m: (m, 0)),
            pl.BlockSpec((tm, LANES), lambda m: (m, 0)),
        ],
        out_shape=[
            jax.ShapeDtypeStruct((T, D), F32),
            jax.ShapeDtypeStruct((T, D // 2), jnp.uint32),
            jax.ShapeDtypeStruct((T, LANES), F32),
        ],
        compiler_params=_cparams(("parallel",)),
        name="out_proj_router",
    )(x2, a2, hm2, wa, wm, g, wr)


def _moe_kernel(te_ref, nv_ref, xs_ref, wg_ref, wu_ref, wd_ref, ys_ref):
    j = pl.program_id(0)

    @pl.when(j < nv_ref[0])
    def _():
        lo, hi = _unpack_bf16_pairs(xs_ref[...])
        xs = jnp.concatenate([lo.astype(BF16), hi.astype(BF16)], axis=1)
        gt = jnp.dot(xs, wg_ref[...].astype(BF16), preferred_element_type=F32)
        up = jnp.dot(xs, wu_ref[...].astype(BF16), preferred_element_type=F32)
        hid = (gt * _sigmoid(gt) * up).astype(BF16)
        ys_ref[...] = _pack_bf16_pairs(jnp.dot(hid, wd_ref[...].astype(BF16), preferred_element_type=F32))

    @pl.when(j >= nv_ref[0])
    def _():
        ys_ref[...] = jnp.zeros(ys_ref.shape, ys_ref.dtype)


def _moe(tile_expert, n_valid, xs, wg, wu, wd):
    R, Dw = xs.shape
    D, F = wg.shape[1], wg.shape[2]
    tm = MOE_TM
    grid_spec = pltpu.PrefetchScalarGridSpec(
        num_scalar_prefetch=2,
        grid=(R // tm,),
        in_specs=[
            pl.BlockSpec((tm, Dw), lambda j, te, nv: (j, 0)),
            pl.BlockSpec((None, D, F), lambda j, te, nv: (te[j], 0, 0)),
            pl.BlockSpec((None, D, F), lambda j, te, nv: (te[j], 0, 0)),
            pl.BlockSpec((None, F, D), lambda j, te, nv: (te[j], 0, 0)),
        ],
        out_specs=pl.BlockSpec((tm, Dw), lambda j, te, nv: (j, 0)),
    )
    return pl.pallas_call(
        _moe_kernel,
        grid_spec=grid_spec,
        out_shape=jax.ShapeDtypeStruct((R, Dw), jnp.uint32),
        compiler_params=_cparams(("arbitrary",)),
        name="moe_experts",
    )(tile_expert, n_valid, xs, wg, wu, wd)


def _final_kernel(x1_ref, yw_ref, cw_ref, g_ref, o_ref):
    cw = cw_ref[...]
    half = yw_ref.shape[1] // 2
    lo0, hi0 = _unpack_bf16_pairs(yw_ref[:, :half])
    lo1, hi1 = _unpack_bf16_pairs(yw_ref[:, half:])
    w0, w1 = cw[:, 0:1], cw[:, 1:2]
    y = jnp.concatenate([w0 * lo0 + w1 * lo1, w0 * hi0 + w1 * hi1], axis=1)
    x = x1_ref[...] + y
    ms = jnp.mean(x * x, axis=-1, keepdims=True)
    o_ref[...] = x * lax.rsqrt(ms + EPS) * g_ref[...]


def _final(x1, yw, cw, g):
    T, D = x1.shape
    tm = FIN_TM
    row = lambda w: pl.BlockSpec((tm, w), lambda m: (m, 0))
    return pl.pallas_call(
        _final_kernel,
        grid=(T // tm,),
        in_specs=[row(D), row(D), row(LANES), pl.BlockSpec((1, D), lambda m: (0, 0))],
        out_specs=row(D),
        out_shape=jax.ShapeDtypeStruct((T, D), F32),
        compiler_params=_cparams(("parallel",)),
        name="combine_final_norm",
    )(x1, yw, cw, g)


SC_CORES, SC_SUBCORES = 2, 16
SC_CHUNK = 32


def _sc_gather_rows(table, idx):
    V, Dw = table.shape
    R = idx.shape[0]
    n_workers = SC_CORES * SC_SUBCORES
    ch = SC_CHUNK
    per_w = R // n_workers
    n_chunks = per_w // ch
    assert per_w * n_workers == R and n_chunks * ch == per_w and n_chunks % 2 == 0
    idx3 = idx.reshape(n_workers, n_chunks, ch)
    mesh = plsc.VectorSubcoreMesh(core_axis_name="c", subcore_axis_name="s")

    def body(table_hbm, idx_hbm, out_hbm, idx_v, rows_v, gsem, osem):
        wid = lax.axis_index("s") * SC_CORES + lax.axis_index("c")
        base = wid * per_w
        pltpu.sync_copy(idx_hbm.at[wid], idx_v)

        def gather(c, slot):
            return pltpu.make_async_copy(table_hbm.at[idx_v.at[c]], rows_v.at[slot], gsem.at[slot])

        def put(c, slot):
            return pltpu.make_async_copy(rows_v.at[slot], out_hbm.at[pl.ds(base + c * ch, ch)],
                                         osem.at[slot])

        gather(0, 0).start()

        @pl.loop(0, n_chunks, step=2)
        def _(c):
            @pl.when(c > 0)
            def _():
                put(c - 1, 1).wait()

            gather(c + 1, 1).start()
            gather(c, 0).wait()
            put(c, 0).start()
            put(c, 0).wait()

            @pl.when(c + 2 < n_chunks)
            def _():
                gather(c + 2, 0).start()

            gather(c + 1, 1).wait()
            put(c + 1, 1).start()

        put(n_chunks - 1, 1).wait()

    return pl.kernel(
        body,
        out_type=jax.ShapeDtypeStruct((R, Dw), table.dtype),
        mesh=mesh,
        scratch_types=[
            pltpu.VMEM((n_chunks, ch), jnp.int32),
            pltpu.VMEM((2, ch, Dw), table.dtype),
            pltpu.SemaphoreType.DMA((2,)),
            pltpu.SemaphoreType.DMA((2,)),
        ],
        name="sc_gather_rows",
    )(table, idx3)


def _route(logits, b_group, b_router):
    T = logits.shape[0]
    G, E = N_GROUPS, EXPERTS_PER_GROUP
    gl = logits[:, :G] + b_group.astype(F32)
    gp = jax.nn.softmax(gl, axis=-1)
    gsel = jnp.argmax(gl, axis=-1)
    gw = jnp.take_along_axis(gp, gsel[:, None], axis=1)[:, 0]
    el = logits[:, G:G + G * E].reshape(T, G, E) + b_router.astype(F32)
    el_sel = jnp.take_along_axis(el, gsel[:, None, None], axis=1)[:, 0, :]
    top_v, top_i = lax.top_k(el_sel, TOP_K_INNER)
    top_w = jax.nn.softmax(top_v, axis=-1)
    eid = (gsel[:, None] * E + top_i).astype(jnp.int32)
    cw = gw[:, None] * top_w
    return eid, cw


def _dispatch_plan(eid, tm, n_tiles):
    T2 = eid.size
    flat_e = eid.reshape(-1)
    onehot = (flat_e[:, None] == jnp.arange(N_EXPERTS, dtype=jnp.int32)[None, :]).astype(jnp.int32)
    csum = jnp.cumsum(onehot, axis=0)
    rank = jnp.sum((csum - onehot) * onehot, axis=1)
    counts = csum[-1]
    tiles_per_e = (counts + tm - 1) // tm
    tile_end = jnp.cumsum(tiles_per_e)
    row_start = (tile_end - tiles_per_e) * tm
    pos = (row_start[flat_e] + rank).astype(jnp.int32)
    n_valid = tile_end[-1].astype(jnp.int32)
    tile_ids = jnp.arange(n_tiles, dtype=jnp.int32)
    tile_expert = jnp.searchsorted(tile_end, jnp.minimum(tile_ids, n_valid - 1), side="right")
    tile_expert = jnp.minimum(tile_expert, N_EXPERTS - 1).astype(jnp.int32)
    src_token = jnp.zeros((n_tiles * tm,), jnp.int32).at[pos].set(
        jnp.arange(T2, dtype=jnp.int32) // TOP_K_INNER)
    return pos.reshape(eid.shape), src_token, tile_expert, n_valid.reshape(1)


def kernel(x, rel_bias, ln_mix_g, w_in, conv_w, conv_b, b_i, b_f, lam_q1, lam_k1, lam_q2, lam_k2,
           diff_norm_g, mlstm_norm_g, w_out, ln_ffn_g, w_group, b_group, w_router, b_router,
           w_gate, w_up, w_down, ln_f_g):
    B, S, D = x.shape
    T = B * S
    depth = w_in.shape[0]
    assert depth == 1, "the final rmsnorm is fused into the single layer's combine kernel"
    Hm = N_MLSTM_HEADS
    n_main = w_in.shape[2] - 2 * Hm
    n_diff = N_DIFF_HEADS * 2 * DIFF_HEAD_DIM
    xf = x.reshape(T, D)

    for l in range(depth):
        lambda_init = 0.8 - 0.6 * math.exp(-0.3 * l)
        w_main = w_in[l, :, :n_main].astype(BF16)
        wgt = w_in[l, :, n_main:]
        w_gates = jnp.zeros((D, 2 * LANES), F32).at[:, :Hm].set(wgt[:, :Hm]).at[:, LANES:LANES + Hm].set(
            wgt[:, Hm:]).astype(BF16)
        bi_row = jnp.zeros((1, LANES), F32).at[0, :Hm].set(b_i[l].astype(F32))
        bf_row = jnp.zeros((1, LANES), F32).at[0, :Hm].set(b_f[l].astype(F32))
        lam = (jnp.exp(jnp.sum(lam_q1[l].astype(F32) * lam_k1[l].astype(F32)))
               - jnp.exp(jnp.sum(lam_q2[l].astype(F32) * lam_k2[l].astype(F32))) + lambda_init)
        tq = ATT_TQ
        assert tq >= MAX_DISTANCE and tq % CHUNK == 0
        ii = jnp.arange(tq, dtype=jnp.int32)
        rel0 = ii[:, None] - ii[None, :]
        rb = rel_bias.astype(F32)

        def bias_of(bucket):
            out = jnp.zeros((N_DIFF_HEADS,) + bucket.shape, F32)
            for bkt in range(N_BUCKETS):
                out = jnp.where(bucket[None] == bkt, rb[bkt][:, None, None], out)
            return out

        bias_prev = bias_of(_t5_bucket(rel0 - tq))
        bias_diag = jnp.where(((ii[:, None] // CHUNK) <= (ii[None, :] // CHUNK))[None],
                              bias_of(_t5_bucket(rel0)), NEG_BIG)
        bias_tiles = jnp.stack([bias_prev, bias_diag], axis=1)
        cfar = rb[N_BUCKETS // 2 - 1]
        scal = jnp.concatenate([lam.reshape(1), cfar]).astype(F32)

        proj, gates = _proj(xf, ln_mix_g[l].reshape(1, D).astype(F32), w_main, w_gates)
        proj3 = proj.reshape(B, S, n_main)
        a = _diff_attention(proj3, scal, bias_tiles, diff_norm_g[l].reshape(1, n_diff).astype(F32),
                            lambda_init)
        hm = _mlstm(proj3, gates.reshape(B, S, 2 * LANES), conv_w[l].astype(F32),
                    conv_b[l].reshape(1, -1).astype(F32), bi_row, bf_row,
                    mlstm_norm_g[l].reshape(1, -1).astype(F32))

        wo = w_out[l].astype(BF16)
        G, E = N_GROUPS, EXPERTS_PER_GROUP
        wr = jnp.zeros((D, LANES), F32).at[:, :G].set(w_group[l].astype(F32)).at[:, G:G + G * E].set(
            jnp.transpose(w_router[l].astype(F32), (1, 0, 2)).reshape(D, G * E))
        x1, h2, logits = _out_proj(xf, a.reshape(T, n_diff), hm.reshape(T, -1), wo[:n_diff], wo[n_diff:],
                                   ln_ffn_g[l].reshape(1, D).astype(F32), wr)

        eid, cw = _route(logits, b_group[l], b_router[l])
        n_tiles = (T * TOP_K_INNER) // MOE_TM + N_EXPERTS
        pos, src_token, tile_expert, n_valid = _dispatch_plan(eid, MOE_TM, n_tiles)
        xs = _sc_gather_rows(h2, src_token)
        Fe = w_gate.shape[-1]
        ys = _moe(tile_expert, n_valid, xs, w_gate[l].reshape(N_EXPERTS, D, Fe),
                  w_up[l].reshape(N_EXPERTS, D, Fe), w_down[l].reshape(N_EXPERTS, Fe, D))
        yw = _sc_gather_rows(ys, pos.reshape(-1)).reshape(T, D)
        cwp = jnp.zeros((T, LANES), F32).at[:, :TOP_K_INNER].set(cw)
        xf = _final(x1, yw, cwp, ln_f_g.reshape(1, D).astype(F32))
    return xf.reshape(B, S, D)
```

```python
import functools
import math

import jax
import jax.numpy as jnp
from jax import lax
from jax.experimental import pallas as pl
from jax.experimental.pallas import tpu as pltpu
from jax.experimental.pallas import tpu_sc as plsc

F32 = jnp.float32
BF16 = jnp.bfloat16

EPS = 1e-6
CHUNK = 64
DIFF_HEAD_DIM = 64
N_DIFF_HEADS = 8
MLSTM_HEAD_DIM = 128
N_MLSTM_HEADS = 8
CONV_WIDTH = 4
N_BUCKETS = 32
MAX_DISTANCE = 128
N_GROUPS = 4
EXPERTS_PER_GROUP = 8
N_EXPERTS = N_GROUPS * EXPERTS_PER_GROUP
TOP_K_INNER = 2
LANES = 128
NEG_BIG = -1e30

VMEM_LIMIT = 56 * 1024 * 1024

PROJ_TM, PROJ_TN = 1024, 1024
ATT_TQ = 512
MLSTM_TS = 512
OUT_TM = 512
MOE_TM = 256
FIN_TM = 512


def _cparams(sem):
    return pltpu.CompilerParams(dimension_semantics=sem, vmem_limit_bytes=VMEM_LIMIT)


_HI_MASK = 0xFFFF0000


def _pack_bf16_pairs(x):
    half = x.shape[-1] // 2
    xb = x.astype(BF16).astype(F32)
    lo = pltpu.bitcast(xb[:, :half], jnp.uint32)
    hi = pltpu.bitcast(xb[:, half:], jnp.uint32)
    return (hi & jnp.uint32(_HI_MASK)) | (lo >> 16)


def _unpack_bf16_pairs(w):
    lo = pltpu.bitcast(w << 16, F32)
    hi = pltpu.bitcast(w & jnp.uint32(_HI_MASK), F32)
    return lo, hi


def _proj_kernel(x_ref, g_ref, w_ref, wg_ref, o_ref, og_ref, h_ref):
    @pl.when(pl.program_id(1) == 0)
    def _():
        x = x_ref[...]
        ms = jnp.mean(x * x, axis=-1, keepdims=True)
        h = (x * lax.rsqrt(ms + EPS) * g_ref[...]).astype(BF16)
        h_ref[...] = h
        og_ref[...] = jnp.dot(h, wg_ref[...], preferred_element_type=F32)

    o_ref[...] = jnp.dot(h_ref[...], w_ref[...], preferred_element_type=F32).astype(o_ref.dtype)


def _proj(x2, g, w_main, w_gates):
    T, D = x2.shape
    N = w_main.shape[1]
    NG = w_gates.shape[1]
    return pl.pallas_call(
        _proj_kernel,
        grid=(T // PROJ_TM, N // PROJ_TN),
        in_specs=[
            pl.BlockSpec((PROJ_TM, D), lambda m, n: (m, 0)),
            pl.BlockSpec((1, D), lambda m, n: (0, 0)),
            pl.BlockSpec((D, PROJ_TN), lambda m, n: (0, n)),
            pl.BlockSpec((D, NG), lambda m, n: (0, 0)),
        ],
        out_specs=[
            pl.BlockSpec((PROJ_TM, PROJ_TN), lambda m, n: (m, n)),
            pl.BlockSpec((PROJ_TM, NG), lambda m, n: (m, 0)),
        ],
        out_shape=[
            jax.ShapeDtypeStruct((T, N), BF16),
            jax.ShapeDtypeStruct((T, NG), F32),
        ],
        scratch_shapes=[pltpu.VMEM((PROJ_TM, D), BF16)],
        compiler_params=_cparams(("parallel", "arbitrary")),
        name="rms_in_proj",
    )(x2, g, w_main, w_gates)


def _t5_bucket(rel):
    half = N_BUCKETS // 2
    max_exact = half // 2
    ret = jnp.where(rel > 0, half, 0)
    n = jnp.abs(rel)
    nf = jnp.maximum(n, 1).astype(F32)
    large = max_exact + (jnp.log(nf / max_exact) / math.log(MAX_DISTANCE / max_exact)
                         * (half - max_exact)).astype(jnp.int32)
    large = jnp.minimum(large, half - 1)
    return ret + jnp.where(n < max_exact, n, large)


def _attn_kernel(scal_ref, q_ref, k_ref, v_ref, bias_ref, g_ref, o_ref, m_ref, l_ref, acc_ref,
                 *, lambda_init):
    h = pl.program_id(1)
    qi = pl.program_id(2)
    tq = ATT_TQ
    lam = scal_ref[0]
    cfar = scal_ref[1 + h]

    q = q_ref[0] * jnp.asarray(DIFF_HEAD_DIM ** -0.5, BF16)
    lane = lax.broadcasted_iota(jnp.int32, q.shape, 1)
    zero = jnp.zeros_like(q)
    qs = jnp.concatenate([jnp.where(lane < DIFF_HEAD_DIM, q, zero),
                          jnp.where(lane >= DIFF_HEAD_DIM, q, zero)], axis=0)

    m_ref[...] = jnp.full(m_ref.shape, NEG_BIG, F32)
    l_ref[...] = jnp.zeros(l_ref.shape, F32)
    acc_ref[...] = jnp.zeros(acc_ref.shape, F32)

    def tile(ki, bias):
        start = pl.multiple_of(ki * tq, tq)
        kt = k_ref[0, pl.ds(start, tq), :]
        vt = v_ref[0, pl.ds(start, tq), :]
        s = lax.dot_general(kt, qs, (((1,), (1,)), ((), ())), preferred_element_type=F32)
        if bias is not None:
            s = s + jnp.concatenate([bias, bias], axis=1)
        m_old = m_ref[...]
        m_new = jnp.maximum(m_old, jnp.max(s, axis=0, keepdims=True))
        alpha = jnp.exp(m_old - m_new)
        p = jnp.exp(s - m_new)
        l_ref[...] = alpha * l_ref[...] + jnp.sum(p, axis=0, keepdims=True)
        pv = lax.dot_general(vt, p.astype(BF16), (((0,), (0,)), ((), ())), preferred_element_type=F32)
        acc_ref[...] = alpha * acc_ref[...] + pv
        m_ref[...] = m_new

    def far_body(ki, c):
        tile(ki, None)
        return c

    lax.fori_loop(0, jnp.maximum(qi - 1, 0), far_body, 0)
    m_ref[...] = m_ref[...] + cfar

    @pl.when(qi >= 1)
    def _():
        tile(qi - 1, bias_ref[0, 0])

    tile(qi, bias_ref[0, 1])

    acc = acc_ref[...] * (1.0 / l_ref[...])
    o_t = acc[:, 0:tq] - lam * acc[:, tq:2 * tq]
    ms = jnp.mean(o_t * o_t, axis=0, keepdims=True)
    y = (o_t * lax.rsqrt(ms + EPS)).T * (g_ref[...] * (1.0 - lambda_init))
    o_ref[0] = y.astype(o_ref.dtype)


def _diff_attention(proj3, scal, bias_tiles, gnorm, lambda_init):
    B, S, _ = proj3.shape
    H = N_DIFF_HEADS
    tq = ATT_TQ
    kern = functools.partial(_attn_kernel, lambda_init=lambda_init)
    return pl.pallas_call(
        kern,
        grid=(B, H, S // tq),
        in_specs=[
            pl.BlockSpec(memory_space=pltpu.SMEM),
            pl.BlockSpec((1, tq, LANES), lambda b, h, i: (b, i, h)),
            pl.BlockSpec((1, S, LANES), lambda b, h, i: (b, 0, H + h)),
            pl.BlockSpec((1, S, LANES), lambda b, h, i: (b, 0, 2 * H + h)),
            pl.BlockSpec((1, 2, tq, tq), lambda b, h, i: (h, 0, 0, 0)),
            pl.BlockSpec((1, LANES), lambda b, h, i: (0, h)),
        ],
        out_specs=pl.BlockSpec((1, tq, LANES), lambda b, h, i: (b, i, h)),
        out_shape=jax.ShapeDtypeStruct((B, S, H * LANES), BF16),
        scratch_shapes=[
            pltpu.VMEM((1, 2 * tq), F32),
            pltpu.VMEM((1, 2 * tq), F32),
            pltpu.VMEM((LANES, 2 * tq), F32),
        ],
        compiler_params=_cparams(("parallel", "parallel", "arbitrary")),
        name="diff_attention",
    )(scal, proj3, proj3, proj3, bias_tiles, gnorm)


def _log_sigmoid(x):
    return jnp.minimum(x, 0.0) - jnp.log(1.0 + jnp.exp(-jnp.abs(x)))


def _sigmoid(x):
    return 1.0 / (1.0 + jnp.exp(-x))


def _mlstm_kernel(q_ref, k_ref, v_ref, o_ref, gi_ref, gf_ref, cw_ref, cb_ref, bi_ref, bf_ref, gn_ref,
                  out_ref, qext_ref, kext_ref, ct_ref, n_ref, m_ref):
    sb = pl.program_id(1)
    L = CHUNK
    dh = MLSTM_HEAD_DIM
    H = N_MLSTM_HEADS
    ts = MLSTM_TS
    pad = 8

    @pl.when(sb == 0)
    def _():
        qext_ref[0:pad, :] = jnp.zeros((pad, H * dh), F32)
        kext_ref[0:pad, :] = jnp.zeros((pad, H * dh), F32)
        ct_ref[...] = jnp.zeros(ct_ref.shape, F32)
        n_ref[...] = jnp.zeros(n_ref.shape, F32)
        m_ref[...] = jnp.zeros(m_ref.shape, F32)

    qext_ref[pad:pad + ts, :] = q_ref[0].astype(F32)
    kext_ref[pad:pad + ts, :] = k_ref[0].astype(F32)

    row = lax.broadcasted_iota(jnp.int32, (L, L), 0)
    col = lax.broadcasted_iota(jnp.int32, (L, L), 1)
    tril = col <= row
    ltri = tril.astype(F32)

    def conv_silu(ext_ref, base, h, off):
        win = ext_ref[pl.ds(base, L + pad), h * dh:(h + 1) * dh]
        w = cw_ref[:, off + h * dh:off + (h + 1) * dh]
        y = cb_ref[:, off + h * dh:off + (h + 1) * dh]
        for j in range(CONV_WIDTH):
            lo = pad - (CONV_WIDTH - 1) + j
            y = y + w[j:j + 1, :] * win[lo:lo + L, :]
        return y * _sigmoid(y)

    def chunk_body(c, carry):
        base = pl.multiple_of(c * L, L)
        li = gi_ref[0, pl.ds(base, L), :] + bi_ref[...]
        logf = _log_sigmoid(gf_ref[0, pl.ds(base, L), :] + bf_ref[...])
        b = jnp.dot(ltri, logf, preferred_element_type=F32, precision=lax.Precision.HIGHEST)
        a = li - b
        g_row = b[L - 1:L, :]
        m_row = m_ref[...]
        m_new_row = g_row + jnp.maximum(m_row, jnp.max(a, axis=0, keepdims=True))
        a_t = a.T

        for h in range(H):
            qc = conv_silu(qext_ref, base, h, 0)
            kc = conv_silu(kext_ref, base, h, H * dh) * (dh ** -0.5)
            qb = qc.astype(BF16)
            kb = kc.astype(BF16)
            vb = v_ref[0, pl.ds(base, L), h * dh:(h + 1) * dh]

            a_row = a_t[h:h + 1, :]
            a_col = a[:, h:h + 1]
            b_col = b[:, h:h + 1]
            m_prev = m_row[:, h:h + 1]
            m_next = m_new_row[:, h:h + 1]
            g_h = g_row[:, h:h + 1]

            amat = jnp.where(tril, a_row, NEG_BIG)
            mcol = jnp.maximum(jnp.max(amat, axis=-1, keepdims=True), m_prev)
            wts = jnp.exp(amat - mcol)
            inter = jnp.exp(m_prev - mcol)

            s = lax.dot_general(qb, kb, (((1,), (1,)), ((), ())), preferred_element_type=F32)
            sqk = s * wts
            ct = ct_ref[h]
            nrow = n_ref[h:h + 1, :]
            num = (jnp.dot(sqk.astype(BF16), vb, preferred_element_type=F32)
                   + inter * jnp.dot(qb, ct.astype(BF16), preferred_element_type=F32))
            den = (jnp.sum(sqk, axis=-1, keepdims=True)
                   + inter * jnp.sum(qb.astype(F32) * nrow, axis=-1, keepdims=True))
            hv = num / jnp.maximum(jnp.abs(den), jnp.exp(-(b_col + mcol)))

            wt = jnp.exp(g_h + a_col - m_next)
            decay = jnp.exp(g_h + m_prev - m_next)
            wv = (wt * vb.astype(F32)).astype(BF16)
            ct_ref[h] = decay * ct + lax.dot_general(kb, wv, (((0,), (0,)), ((), ())),
                                                     preferred_element_type=F32)
            n_ref[h:h + 1, :] = decay * nrow + jnp.sum(wt * kb.astype(F32), axis=0, keepdims=True)

            ms = jnp.mean(hv * hv, axis=-1, keepdims=True)
            y = hv * lax.rsqrt(ms + EPS) * gn_ref[:, h * dh:(h + 1) * dh]
            og = o_ref[0, pl.ds(base, L), h * dh:(h + 1) * dh].astype(F32)
            out_ref[0, pl.ds(base, L), h * dh:(h + 1) * dh] = (y * _sigmoid(og)).astype(out_ref.dtype)

        m_ref[...] = m_new_row
        return carry

    lax.fori_loop(0, ts // L, chunk_body, 0)

    qext_ref[0:pad, :] = qext_ref[ts:ts + pad, :]
    kext_ref[0:pad, :] = kext_ref[ts:ts + pad, :]


def _mlstm(proj3, gates3, conv_w, conv_b, bi_row, bf_row, gnorm):
    B, S, _ = proj3.shape
    W = N_MLSTM_HEADS * MLSTM_HEAD_DIM
    ts = MLSTM_TS
    first = 3
    blk = lambda j: pl.BlockSpec((1, ts, W), lambda b, s: (b, s, j))
    full = lambda shape: pl.BlockSpec(shape, lambda b, s: (0,) * len(shape))
    return pl.pallas_call(
        _mlstm_kernel,
        grid=(B, S // ts),
        in_specs=[
            blk(first), blk(first + 1), blk(first + 2), blk(first + 3),
            pl.BlockSpec((1, ts, LANES), lambda b, s: (b, s, 0)),
            pl.BlockSpec((1, ts, LANES), lambda b, s: (b, s, 1)),
            full((CONV_WIDTH, 2 * W)), full((1, 2 * W)),
            full((1, LANES)), full((1, LANES)), full((1, W)),
        ],
        out_specs=pl.BlockSpec((1, ts, W), lambda b, s: (b, s, 0)),
        out_shape=jax.ShapeDtypeStruct((B, S, W), BF16),
        scratch_shapes=[
            pltpu.VMEM((ts + 8, W), F32),
            pltpu.VMEM((ts + 8, W), F32),
            pltpu.VMEM((N_MLSTM_HEADS, MLSTM_HEAD_DIM, MLSTM_HEAD_DIM), F32),
            pltpu.VMEM((N_MLSTM_HEADS, MLSTM_HEAD_DIM), F32),
            pltpu.VMEM((1, LANES), F32),
        ],
        compiler_params=_cparams(("parallel", "arbitrary")),
        name="mlstm",
    )(proj3, proj3, proj3, proj3, gates3, gates3, conv_w, conv_b, bi_row, bf_row, gnorm)


def _out_kernel(x_ref, a_ref, hm_ref, wa_ref, wm_ref, g_ref, wr_ref, x1_ref, h2_ref, lg_ref):
    y = (jnp.dot(a_ref[...], wa_ref[...], preferred_element_type=F32)
         + jnp.dot(hm_ref[...], wm_ref[...], preferred_element_type=F32))
    x1 = x_ref[...] + y
    x1_ref[...] = x1
    ms = jnp.mean(x1 * x1, axis=-1, keepdims=True)
    h2 = x1 * lax.rsqrt(ms + EPS) * g_ref[...]
    h2_ref[...] = _pack_bf16_pairs(h2)
    lg_ref[...] = jnp.dot(h2.astype(BF16), wr_ref[...], preferred_element_type=F32)


def _out_proj(x2, a2, hm2, wa, wm, g, wr):
    T, D = x2.shape
    W = a2.shape[1]
    tm = OUT_TM
    const = lambda shape: pl.BlockSpec(shape, lambda m: (0, 0), pipeline_mode=pl.Buffered(1))
    return pl.pallas_call(
        _out_kernel,
        grid=(T // tm,),
        in_specs=[
            pl.BlockSpec((tm, D), lambda m: (m, 0)),
            pl.BlockSpec((tm, W), lambda m: (m, 0)),
            pl.BlockSpec((tm, W), lambda m: (m, 0)),
            const((W, D)), const((W, D)), const((1, D)), const((D, LANES)),
        ],
        out_specs=[
            pl.BlockSpec((tm, D), lambda m: (m, 0)),
            pl.BlockSpec((tm, D // 2), lambda m: (m, 0)),
            pl.BlockSpec((tm, LANES), lambda m: (m, 0)),
        ],
        out_shape=[
            jax.ShapeDtypeStruct((T, D), F32),
            jax.ShapeDtypeStruct((T, D // 2), jnp.uint32),
            jax.ShapeDtypeStruct((T, LANES), F32),
        ],
        compiler_params=_cparams(("parallel",)),
        name="out_proj_router",
    )(x2, a2, hm2, wa, wm, g, wr)


def _moe_kernel(te_ref, nv_ref, xs_ref, wg_ref, wu_ref, wd_ref, ys_ref):
    j = pl.program_id(0)

    @pl.when(j < nv_ref[0])
    def _():
        lo, hi = _unpack_bf16_pairs(xs_ref[...])
        xs = jnp.concatenate([lo.astype(BF16), hi.astype(BF16)], axis=1)
        gt = jnp.dot(xs, wg_ref[...].astype(BF16), preferred_element_type=F32)
        up = jnp.dot(xs, wu_ref[...].astype(BF16), preferred_element_type=F32)
        hid = (gt * _sigmoid(gt) * up).astype(BF16)
        ys_ref[...] = _pack_bf16_pairs(jnp.dot(hid, wd_ref[...].astype(BF16), preferred_element_type=F32))

    @pl.when(j >= nv_ref[0])
    def _():
        ys_ref[...] = jnp.zeros(ys_ref.shape, ys_ref.dtype)


def _moe(tile_expert, n_valid, xs, wg, wu, wd):
    R, Dw = xs.shape
    D, F = wg.shape[1], wg.shape[2]
    tm = MOE_TM
    grid_spec = pltpu.PrefetchScalarGridSpec(
        num_scalar_prefetch=2,
        grid=(R // tm,),
        in_specs=[
            pl.BlockSpec((tm, Dw), lambda j, te, nv: (j, 0)),
            pl.BlockSpec((None, D, F), lambda j, te, nv: (te[j], 0, 0)),
            pl.BlockSpec((None, D, F), lambda j, te, nv: (te[j], 0, 0)),
            pl.BlockSpec((None, F, D), lambda j, te, nv: (te[j], 0, 0)),
        ],
        out_specs=pl.BlockSpec((tm, Dw), lambda j, te, nv: (j, 0)),
    )
    return pl.pallas_call(
        _moe_kernel,
        grid_spec=grid_spec,
        out_shape=jax.ShapeDtypeStruct((R, Dw), jnp.uint32),
        compiler_params=_cparams(("arbitrary",)),
        name="moe_experts",
    )(tile_expert, n_valid, xs, wg, wu, wd)


def _final_kernel(x1_ref, y0_ref, y1_ref, cw_ref, g_ref, o_ref):
    cw = cw_ref[...]
    lo0, hi0 = _unpack_bf16_pairs(y0_ref[...])
    lo1, hi1 = _unpack_bf16_pairs(y1_ref[...])
    w0, w1 = cw[:, 0:1], cw[:, 1:2]
    y = jnp.concatenate([w0 * lo0 + w1 * lo1, w0 * hi0 + w1 * hi1], axis=1)
    x = x1_ref[...] + y
    ms = jnp.mean(x * x, axis=-1, keepdims=True)
    o_ref[...] = x * lax.rsqrt(ms + EPS) * g_ref[...]


def _final(x1, yw, cw, g):
    T, D = x1.shape
    tm = FIN_TM
    row = lambda w: pl.BlockSpec((tm, w), lambda m: (m, 0))
    slot1 = pl.BlockSpec((tm, D // 2), lambda m: (m + T // tm, 0))
    return pl.pallas_call(
        _final_kernel,
        grid=(T // tm,),
        in_specs=[row(D), row(D // 2), slot1, row(LANES), pl.BlockSpec((1, D), lambda m: (0, 0))],
        out_specs=row(D),
        out_shape=jax.ShapeDtypeStruct((T, D), F32),
        compiler_params=_cparams(("parallel",)),
        name="combine_final_norm",
    )(x1, yw, yw, cw, g)


SC_CORES, SC_SUBCORES = 2, 16
SC_CHUNK = 32


def _sc_gather_rows(table, idx):
    V, Dw = table.shape
    R = idx.shape[0]
    n_workers = SC_CORES * SC_SUBCORES
    ch = SC_CHUNK
    per_w = R // n_workers
    n_chunks = per_w // ch
    assert per_w * n_workers == R and n_chunks * ch == per_w and n_chunks % 2 == 0
    idx3 = idx.reshape(n_workers, n_chunks, ch)
    mesh = plsc.VectorSubcoreMesh(core_axis_name="c", subcore_axis_name="s")

    def body(table_hbm, idx_hbm, out_hbm, idx_v, rows_v, gsem, osem):
        wid = lax.axis_index("s") * SC_CORES + lax.axis_index("c")
        base = wid * per_w
        pltpu.sync_copy(idx_hbm.at[wid], idx_v)

        def gather(c, slot):
            return pltpu.make_async_copy(table_hbm.at[idx_v.at[c]], rows_v.at[slot], gsem.at[slot])

        def put(c, slot):
            return pltpu.make_async_copy(rows_v.at[slot], out_hbm.at[pl.ds(base + c * ch, ch)],
                                         osem.at[slot])

        gather(0, 0).start()

        @pl.loop(0, n_chunks, step=2)
        def _(c):
            @pl.when(c > 0)
            def _():
                put(c - 1, 1).wait()

            gather(c + 1, 1).start()
            gather(c, 0).wait()
            put(c, 0).start()
            put(c, 0).wait()

            @pl.when(c + 2 < n_chunks)
            def _():
                gather(c + 2, 0).start()

            gather(c + 1, 1).wait()
            put(c + 1, 1).start()

        put(n_chunks - 1, 1).wait()

    return pl.kernel(
        body,
        out_type=jax.ShapeDtypeStruct((R, Dw), table.dtype),
        mesh=mesh,
        scratch_types=[
            pltpu.VMEM((n_chunks, ch), jnp.int32),
            pltpu.VMEM((2, ch, Dw), table.dtype),
            pltpu.SemaphoreType.DMA((2,)),
            pltpu.SemaphoreType.DMA((2,)),
        ],
        name="sc_gather_rows",
    )(table, idx3)


def _route(logits, b_group, b_router):
    T = logits.shape[0]
    G, E = N_GROUPS, EXPERTS_PER_GROUP
    gl = logits[:, :G] + b_group.astype(F32)
    gp = jax.nn.softmax(gl, axis=-1)
    gsel = jnp.argmax(gl, axis=-1)
    gw = jnp.take_along_axis(gp, gsel[:, None], axis=1)[:, 0]
    el = logits[:, G:G + G * E].reshape(T, G, E) + b_router.astype(F32)
    el_sel = jnp.take_along_axis(el, gsel[:, None, None], axis=1)[:, 0, :]
    top_v, top_i = lax.top_k(el_sel, TOP_K_INNER)
    top_w = jax.nn.softmax(top_v, axis=-1)
    eid = (gsel[:, None] * E + top_i).astype(jnp.int32)
    cw = gw[:, None] * top_w
    return eid, cw


def _dispatch_plan(eid, tm, n_tiles):
    T2 = eid.size
    flat_e = eid.reshape(-1)
    onehot = (flat_e[:, None] == jnp.arange(N_EXPERTS, dtype=jnp.int32)[None, :]).astype(jnp.int32)
    csum = jnp.cumsum(onehot, axis=0)
    rank = jnp.sum((csum - onehot) * onehot, axis=1)
    counts = csum[-1]
    tiles_per_e = (counts + tm - 1) // tm
    tile_end = jnp.cumsum(tiles_per_e)
    row_start = (tile_end - tiles_per_e) * tm
    pos = (row_start[flat_e] + rank).astype(jnp.int32)
    n_valid = tile_end[-1].astype(jnp.int32)
    tile_ids = jnp.arange(n_tiles, dtype=jnp.int32)
    tile_expert = jnp.searchsorted(tile_end, jnp.minimum(tile_ids, n_valid - 1), side="right")
    tile_expert = jnp.minimum(tile_expert, N_EXPERTS - 1).astype(jnp.int32)
    n_rows = n_tiles * tm
    src_token = (jnp.arange(n_rows, dtype=jnp.int32) % (T2 // TOP_K_INNER)).at[pos].set(
        jnp.arange(T2, dtype=jnp.int32) // TOP_K_INNER)
    return pos.reshape(eid.shape), src_token, tile_expert, n_valid.reshape(1)


def kernel(x, rel_bias, ln_mix_g, w_in, conv_w, conv_b, b_i, b_f, lam_q1, lam_k1, lam_q2, lam_k2,
           diff_norm_g, mlstm_norm_g, w_out, ln_ffn_g, w_group, b_group, w_router, b_router,
           w_gate, w_up, w_down, ln_f_g):
    B, S, D = x.shape
    T = B * S
    depth = w_in.shape[0]
    assert depth == 1, "the final rmsnorm is fused into the single layer's combine kernel"
    Hm = N_MLSTM_HEADS
    n_main = w_in.shape[2] - 2 * Hm
    n_diff = N_DIFF_HEADS * 2 * DIFF_HEAD_DIM
    xf = x.reshape(T, D)

    for l in range(depth):
        lambda_init = 0.8 - 0.6 * math.exp(-0.3 * l)
        w_main = w_in[l, :, :n_main].astype(BF16)
        wgt = w_in[l, :, n_main:]
        w_gates = jnp.zeros((D, 2 * LANES), F32).at[:, :Hm].set(wgt[:, :Hm]).at[:, LANES:LANES + Hm].set(
            wgt[:, Hm:]).astype(BF16)
        bi_row = jnp.zeros((1, LANES), F32).at[0, :Hm].set(b_i[l].astype(F32))
        bf_row = jnp.zeros((1, LANES), F32).at[0, :Hm].set(b_f[l].astype(F32))
        lam = (jnp.exp(jnp.sum(lam_q1[l].astype(F32) * lam_k1[l].astype(F32)))
               - jnp.exp(jnp.sum(lam_q2[l].astype(F32) * lam_k2[l].astype(F32))) + lambda_init)
        tq = ATT_TQ
        assert tq >= MAX_DISTANCE and tq % CHUNK == 0
        ii = jnp.arange(tq, dtype=jnp.int32)
        rel0 = ii[:, None] - ii[None, :]
        rb = rel_bias.astype(F32)

        def bias_of(bucket):
            out = jnp.zeros((N_DIFF_HEADS,) + bucket.shape, F32)
            for bkt in range(N_BUCKETS):
                out = jnp.where(bucket[None] == bkt, rb[bkt][:, None, None], out)
            return out

        bias_prev = bias_of(_t5_bucket(rel0 - tq))
        bias_diag = jnp.where(((ii[:, None] // CHUNK) <= (ii[None, :] // CHUNK))[None],
                              bias_of(_t5_bucket(rel0)), NEG_BIG)
        bias_tiles = jnp.stack([bias_prev, bias_diag], axis=1)
        cfar = rb[N_BUCKETS // 2 - 1]
        scal = jnp.concatenate([lam.reshape(1), cfar]).astype(F32)

        proj, gates = _proj(xf, ln_mix_g[l].reshape(1, D).astype(F32), w_main, w_gates)
        proj3 = proj.reshape(B, S, n_main)
        a = _diff_attention(proj3, scal, bias_tiles, diff_norm_g[l].reshape(1, n_diff).astype(F32),
                            lambda_init)
        hm = _mlstm(proj3, gates.reshape(B, S, 2 * LANES), conv_w[l].astype(F32),
                    conv_b[l].reshape(1, -1).astype(F32), bi_row, bf_row,
                    mlstm_norm_g[l].reshape(1, -1).astype(F32))

        wo = w_out[l].astype(BF16)
        G, E = N_GROUPS, EXPERTS_PER_GROUP
        wr = jnp.zeros((D, LANES), F32).at[:, :G].set(w_group[l].astype(F32)).at[:, G:G + G * E].set(
            jnp.transpose(w_router[l].astype(F32), (1, 0, 2)).reshape(D, G * E)).astype(BF16)
        x1, h2, logits = _out_proj(xf, a.reshape(T, n_diff), hm.reshape(T, -1), wo[:n_diff], wo[n_diff:],
                                   ln_ffn_g[l].reshape(1, D).astype(F32), wr)

        eid, cw = _route(logits, b_group[l], b_router[l])
        n_tiles = (T * TOP_K_INNER) // MOE_TM + N_EXPERTS
        pos, src_token, tile_expert, n_valid = _dispatch_plan(eid, MOE_TM, n_tiles)
        xs = _sc_gather_rows(h2, src_token)
        Fe = w_gate.shape[-1]
        ys = _moe(tile_expert, n_valid, xs, w_gate[l].reshape(N_EXPERTS, D, Fe),
                  w_up[l].reshape(N_EXPERTS, D, Fe), w_down[l].reshape(N_EXPERTS, Fe, D))
        yw = _sc_gather_rows(ys, pos.T.reshape(-1))
        cwp = jnp.zeros((T, LANES), F32).at[:, :TOP_K_INNER].set(cw)
        xf = _final(x1, yw, cwp, ln_f_g.reshape(1, D).astype(F32))
    return xf.reshape(B, S, D)
```

```python
import functools
import math

import jax
import jax.numpy as jnp
from jax import lax
from jax.experimental import pallas as pl
from jax.experimental.pallas import tpu as pltpu
from jax.experimental.pallas import tpu_sc as plsc

F32 = jnp.float32
BF16 = jnp.bfloat16

EPS = 1e-6
CHUNK = 64
DIFF_HEAD_DIM = 64
N_DIFF_HEADS = 8
MLSTM_HEAD_DIM = 128
N_MLSTM_HEADS = 8
CONV_WIDTH = 4
N_BUCKETS = 32
MAX_DISTANCE = 128
N_GROUPS = 4
EXPERTS_PER_GROUP = 8
N_EXPERTS = N_GROUPS * EXPERTS_PER_GROUP
TOP_K_INNER = 2
LANES = 128
NEG_BIG = -1e30

VMEM_LIMIT = 56 * 1024 * 1024

PROJ_TM, PROJ_TN = 1024, 1024
ATT_TQ = 512
MLSTM_TS = 512
OUT_TM = 512
MOE_TM = 256
FIN_TM = 512


def _cparams(sem):
    return pltpu.CompilerParams(dimension_semantics=sem, vmem_limit_bytes=VMEM_LIMIT)


_HI_MASK = 0xFFFF0000


def _pack_bf16_pairs(x):
    half = x.shape[-1] // 2
    xb = x.astype(BF16).astype(F32)
    lo = pltpu.bitcast(xb[:, :half], jnp.uint32)
    hi = pltpu.bitcast(xb[:, half:], jnp.uint32)
    return (hi & jnp.uint32(_HI_MASK)) | (lo >> 16)


def _unpack_bf16_pairs(w):
    lo = pltpu.bitcast(w << 16, F32)
    hi = pltpu.bitcast(w & jnp.uint32(_HI_MASK), F32)
    return lo, hi


def _proj_kernel(x_ref, g_ref, w_ref, cs_ref, wg_ref, o_ref, og_ref, h_ref):
    @pl.when(pl.program_id(1) == 0)
    def _():
        x = x_ref[...]
        ms = jnp.mean(x * x, axis=-1, keepdims=True)
        h = (x * lax.rsqrt(ms + EPS) * g_ref[...]).astype(BF16)
        h_ref[...] = h
        og_ref[...] = jnp.dot(h, wg_ref[...], preferred_element_type=F32)

    o_ref[...] = (jnp.dot(h_ref[...], w_ref[...], preferred_element_type=F32) * cs_ref[...]).astype(o_ref.dtype)


def _proj(x2, g, w_main, col_scale, w_gates):
    T, D = x2.shape
    N = w_main.shape[1]
    NG = w_gates.shape[1]
    return pl.pallas_call(
        _proj_kernel,
        grid=(T // PROJ_TM, N // PROJ_TN),
        in_specs=[
            pl.BlockSpec((PROJ_TM, D), lambda m, n: (m, 0)),
            pl.BlockSpec((1, D), lambda m, n: (0, 0)),
            pl.BlockSpec((D, PROJ_TN), lambda m, n: (0, n)),
            pl.BlockSpec((1, PROJ_TN), lambda m, n: (0, n)),
            pl.BlockSpec((D, NG), lambda m, n: (0, 0)),
        ],
        out_specs=[
            pl.BlockSpec((PROJ_TM, PROJ_TN), lambda m, n: (m, n)),
            pl.BlockSpec((PROJ_TM, NG), lambda m, n: (m, 0)),
        ],
        out_shape=[
            jax.ShapeDtypeStruct((T, N), BF16),
            jax.ShapeDtypeStruct((T, NG), F32),
        ],
        scratch_shapes=[pltpu.VMEM((PROJ_TM, D), BF16)],
        compiler_params=_cparams(("parallel", "arbitrary")),
        name="rms_in_proj",
    )(x2, g, w_main, col_scale, w_gates)


def _t5_bucket(rel):
    half = N_BUCKETS // 2
    max_exact = half // 2
    ret = jnp.where(rel > 0, half, 0)
    n = jnp.abs(rel)
    nf = jnp.maximum(n, 1).astype(F32)
    large = max_exact + (jnp.log(nf / max_exact) / math.log(MAX_DISTANCE / max_exact)
                         * (half - max_exact)).astype(jnp.int32)
    large = jnp.minimum(large, half - 1)
    return ret + jnp.where(n < max_exact, n, large)


def _attn_kernel(scal_ref, q_ref, k_ref, v_ref, bias_ref, g_ref, o_ref, m_ref, l_ref, acc_ref,
                 *, lambda_init):
    h = pl.program_id(1)
    qi = pl.program_id(2)
    tq = ATT_TQ
    lam = scal_ref[0]
    cfar = scal_ref[1 + h]

    q = q_ref[0]
    lane = lax.broadcasted_iota(jnp.int32, q.shape, 1)
    zero = jnp.zeros_like(q)
    qs = jnp.concatenate([jnp.where(lane < DIFF_HEAD_DIM, q, zero),
                          jnp.where(lane >= DIFF_HEAD_DIM, q, zero)], axis=0)

    m_ref[...] = jnp.full(m_ref.shape, NEG_BIG, F32)
    l_ref[...] = jnp.zeros(l_ref.shape, F32)
    acc_ref[...] = jnp.zeros(acc_ref.shape, F32)

    def tile(ki, bias):
        start = pl.multiple_of(ki * tq, tq)
        kt = k_ref[0, pl.ds(start, tq), :]
        vt = v_ref[0, pl.ds(start, tq), :]
        s = lax.dot_general(kt, qs, (((1,), (1,)), ((), ())), preferred_element_type=F32)
        if bias is not None:
            s = s + jnp.concatenate([bias, bias], axis=1)
        m_old = m_ref[...]
        m_new = jnp.maximum(m_old, jnp.max(s, axis=0, keepdims=True))
        alpha = jnp.exp2(m_old - m_new)
        p = jnp.exp2(s - m_new)
        l_ref[...] = alpha * l_ref[...] + jnp.sum(p, axis=0, keepdims=True)
        pv = lax.dot_general(vt, p.astype(BF16), (((0,), (0,)), ((), ())), preferred_element_type=F32)
        acc_ref[...] = alpha * acc_ref[...] + pv
        m_ref[...] = m_new

    def far_body(ki, c):
        tile(ki, None)
        return c

    lax.fori_loop(0, jnp.maximum(qi - 1, 0), far_body, 0)
    m_ref[...] = m_ref[...] + cfar

    @pl.when(qi >= 1)
    def _():
        tile(qi - 1, bias_ref[0, 0])

    tile(qi, bias_ref[0, 1])

    acc = acc_ref[...] * (1.0 / l_ref[...])
    o_t = acc[:, 0:tq] - lam * acc[:, tq:2 * tq]
    ms = jnp.mean(o_t * o_t, axis=0, keepdims=True)
    y = (o_t * lax.rsqrt(ms + EPS)).T * (g_ref[...] * (1.0 - lambda_init))
    o_ref[0] = y.astype(o_ref.dtype)


def _diff_attention(proj3, scal, bias_tiles, gnorm, lambda_init):
    B, S, _ = proj3.shape
    H = N_DIFF_HEADS
    tq = ATT_TQ
    kern = functools.partial(_attn_kernel, lambda_init=lambda_init)
    return pl.pallas_call(
        kern,
        grid=(B, H, S // tq),
        in_specs=[
            pl.BlockSpec(memory_space=pltpu.SMEM),
            pl.BlockSpec((1, tq, LANES), lambda b, h, i: (b, i, h)),
            pl.BlockSpec((1, S, LANES), lambda b, h, i: (b, 0, H + h)),
            pl.BlockSpec((1, S, LANES), lambda b, h, i: (b, 0, 2 * H + h)),
            pl.BlockSpec((1, 2, tq, tq), lambda b, h, i: (h, 0, 0, 0)),
            pl.BlockSpec((1, LANES), lambda b, h, i: (0, h)),
        ],
        out_specs=pl.BlockSpec((1, tq, LANES), lambda b, h, i: (b, i, h)),
        out_shape=jax.ShapeDtypeStruct((B, S, H * LANES), BF16),
        scratch_shapes=[
            pltpu.VMEM((1, 2 * tq), F32),
            pltpu.VMEM((1, 2 * tq), F32),
            pltpu.VMEM((LANES, 2 * tq), F32),
        ],
        compiler_params=_cparams(("parallel", "parallel", "arbitrary")),
        name="diff_attention",
    )(scal, proj3, proj3, proj3, bias_tiles, gnorm)


def _log_sigmoid(x):
    return jnp.minimum(x, 0.0) - jnp.log(1.0 + jnp.exp(-jnp.abs(x)))


def _sigmoid(x):
    return 1.0 / (1.0 + jnp.exp(-x))


def _mlstm_kernel(q_ref, k_ref, v_ref, o_ref, gi_ref, gf_ref, cw_ref, cb_ref, bi_ref, bf_ref, gn_ref,
                  out_ref, qext_ref, kext_ref, ct_ref, n_ref, m_ref):
    sb = pl.program_id(1)
    L = CHUNK
    dh = MLSTM_HEAD_DIM
    H = N_MLSTM_HEADS
    ts = MLSTM_TS
    pad = 8

    @pl.when(sb == 0)
    def _():
        qext_ref[0:pad, :] = jnp.zeros((pad, H * dh), F32)
        kext_ref[0:pad, :] = jnp.zeros((pad, H * dh), F32)
        ct_ref[...] = jnp.zeros(ct_ref.shape, F32)
        n_ref[...] = jnp.zeros(n_ref.shape, F32)
        m_ref[...] = jnp.zeros(m_ref.shape, F32)

    qext_ref[pad:pad + ts, :] = q_ref[0].astype(F32)
    kext_ref[pad:pad + ts, :] = k_ref[0].astype(F32)

    row = lax.broadcasted_iota(jnp.int32, (L, L), 0)
    col = lax.broadcasted_iota(jnp.int32, (L, L), 1)
    tril = col <= row
    ltri = tril.astype(F32)

    def conv_silu(ext_ref, base, h, off):
        win = ext_ref[pl.ds(base, L + pad), h * dh:(h + 1) * dh]
        w = cw_ref[:, off + h * dh:off + (h + 1) * dh]
        y = cb_ref[:, off + h * dh:off + (h + 1) * dh]
        for j in range(CONV_WIDTH):
            lo = pad - (CONV_WIDTH - 1) + j
            y = y + w[j:j + 1, :] * win[lo:lo + L, :]
        return y * _sigmoid(y)

    def chunk_body(c, carry):
        base = pl.multiple_of(c * L, L)
        li = gi_ref[0, pl.ds(base, L), :] + bi_ref[...]
        logf = _log_sigmoid(gf_ref[0, pl.ds(base, L), :] + bf_ref[...])
        b = jnp.dot(ltri, logf, preferred_element_type=F32, precision=lax.Precision.HIGHEST)
        a = li - b
        g_row = b[L - 1:L, :]
        m_row = m_ref[...]
        m_new_row = g_row + jnp.maximum(m_row, jnp.max(a, axis=0, keepdims=True))
        a_t = a.T

        for h in range(H):
            qc = conv_silu(qext_ref, base, h, 0)
            kc = conv_silu(kext_ref, base, h, H * dh) * (dh ** -0.5)
            qb = qc.astype(BF16)
            kb = kc.astype(BF16)
            vb = v_ref[0, pl.ds(base, L), h * dh:(h + 1) * dh]

            a_row = a_t[h:h + 1, :]
            a_col = a[:, h:h + 1]
            b_col = b[:, h:h + 1]
            m_prev = m_row[:, h:h + 1]
            m_next = m_new_row[:, h:h + 1]
            g_h = g_row[:, h:h + 1]

            amat = jnp.where(tril, a_row, NEG_BIG)
            mcol = jnp.maximum(jnp.max(amat, axis=-1, keepdims=True), m_prev)
            wts = jnp.exp(amat - mcol)
            inter = jnp.exp(m_prev - mcol)

            s = lax.dot_general(qb, kb, (((1,), (1,)), ((), ())), preferred_element_type=F32)
            sqk = s * wts
            ct = ct_ref[h]
            nrow = n_ref[h:h + 1, :]
            num = (jnp.dot(sqk.astype(BF16), vb, preferred_element_type=F32)
                   + inter * jnp.dot(qb, ct.astype(BF16), preferred_element_type=F32))
            den = (jnp.sum(sqk, axis=-1, keepdims=True)
                   + inter * jnp.sum(qb.astype(F32) * nrow, axis=-1, keepdims=True))
            hv = num / jnp.maximum(jnp.abs(den), jnp.exp(-(b_col + mcol)))

            wt = jnp.exp(g_h + a_col - m_next)
            decay = jnp.exp(g_h + m_prev - m_next)
            wv = (wt * vb.astype(F32)).astype(BF16)
            ct_ref[h] = decay * ct + lax.dot_general(kb, wv, (((0,), (0,)), ((), ())),
                                                     preferred_element_type=F32)
            n_ref[h:h + 1, :] = decay * nrow + jnp.sum(wt * kb.astype(F32), axis=0, keepdims=True)

            ms = jnp.mean(hv * hv, axis=-1, keepdims=True)
            y = hv * lax.rsqrt(ms + EPS) * gn_ref[:, h * dh:(h + 1) * dh]
            og = o_ref[0, pl.ds(base, L), h * dh:(h + 1) * dh].astype(F32)
            out_ref[0, pl.ds(base, L), h * dh:(h + 1) * dh] = (y * _sigmoid(og)).astype(out_ref.dtype)

        m_ref[...] = m_new_row
        return carry

    lax.fori_loop(0, ts // L, chunk_body, 0)

    qext_ref[0:pad, :] = qext_ref[ts:ts + pad, :]
    kext_ref[0:pad, :] = kext_ref[ts:ts + pad, :]


def _mlstm(proj3, gates3, conv_w, conv_b, bi_row, bf_row, gnorm):
    B, S, _ = proj3.shape
    W = N_MLSTM_HEADS * MLSTM_HEAD_DIM
    ts = MLSTM_TS
    first = 3
    blk = lambda j: pl.BlockSpec((1, ts, W), lambda b, s: (b, s, j))
    full = lambda shape: pl.BlockSpec(shape, lambda b, s: (0,) * len(shape))
    return pl.pallas_call(
        _mlstm_kernel,
        grid=(B, S // ts),
        in_specs=[
            blk(first), blk(first + 1), blk(first + 2), blk(first + 3),
            pl.BlockSpec((1, ts, LANES), lambda b, s: (b, s, 0)),
            pl.BlockSpec((1, ts, LANES), lambda b, s: (b, s, 1)),
            full((CONV_WIDTH, 2 * W)), full((1, 2 * W)),
            full((1, LANES)), full((1, LANES)), full((1, W)),
        ],
        out_specs=pl.BlockSpec((1, ts, W), lambda b, s: (b, s, 0)),
        out_shape=jax.ShapeDtypeStruct((B, S, W), BF16),
        scratch_shapes=[
            pltpu.VMEM((ts + 8, W), F32),
            pltpu.VMEM((ts + 8, W), F32),
            pltpu.VMEM((N_MLSTM_HEADS, MLSTM_HEAD_DIM, MLSTM_HEAD_DIM), F32),
            pltpu.VMEM((N_MLSTM_HEADS, MLSTM_HEAD_DIM), F32),
            pltpu.VMEM((1, LANES), F32),
        ],
        compiler_params=_cparams(("parallel", "arbitrary")),
        name="mlstm",
    )(proj3, proj3, proj3, proj3, gates3, gates3, conv_w, conv_b, bi_row, bf_row, gnorm)


def _out_kernel(x_ref, a_ref, hm_ref, wa_ref, wm_ref, g_ref, wr_ref, x1_ref, h2_ref, lg_ref):
    y = (jnp.dot(a_ref[...], wa_ref[...], preferred_element_type=F32)
         + jnp.dot(hm_ref[...], wm_ref[...], preferred_element_type=F32))
    x1 = x_ref[...] + y
    x1_ref[...] = x1
    ms = jnp.mean(x1 * x1, axis=-1, keepdims=True)
    h2 = x1 * lax.rsqrt(ms + EPS) * g_ref[...]
    h2_ref[...] = _pack_bf16_pairs(h2)
    lg_ref[...] = jnp.dot(h2.astype(BF16), wr_ref[...], preferred_element_type=F32)


def _out_proj(x2, a2, hm2, wa, wm, g, wr):
    T, D = x2.shape
    W = a2.shape[1]
    tm = OUT_TM
    const = lambda shape: pl.BlockSpec(shape, lambda m: (0, 0), pipeline_mode=pl.Buffered(1))
    return pl.pallas_call(
        _out_kernel,
        grid=(T // tm,),
        in_specs=[
            pl.BlockSpec((tm, D), lambda m: (m, 0)),
            pl.BlockSpec((tm, W), lambda m: (m, 0)),
            pl.BlockSpec((tm, W), lambda m: (m, 0)),
            const((W, D)), const((W, D)), const((1, D)), const((D, LANES)),
        ],
        out_specs=[
            pl.BlockSpec((tm, D), lambda m: (m, 0)),
            pl.BlockSpec((tm, D // 2), lambda m: (m, 0)),
            pl.BlockSpec((tm, LANES), lambda m: (m, 0)),
        ],
        out_shape=[
            jax.ShapeDtypeStruct((T, D), F32),
            jax.ShapeDtypeStruct((T, D // 2), jnp.uint32),
            jax.ShapeDtypeStruct((T, LANES), F32),
        ],
        compiler_params=_cparams(("parallel",)),
        name="out_proj_router",
    )(x2, a2, hm2, wa, wm, g, wr)


def _moe_kernel(te_ref, nv_ref, first_ref, slot_ref, nxt_ref, xs_ref, wg_hbm, wu_hbm, wd_hbm, ys_ref,
                wg_buf, wu_buf, wd_buf, sem):
    j = pl.program_id(0)
    valid = j < nv_ref[0]

    def weight_copies(e, s):
        return (pltpu.make_async_copy(wg_hbm.at[e], wg_buf.at[s], sem.at[s, 0]),
                pltpu.make_async_copy(wu_hbm.at[e], wu_buf.at[s], sem.at[s, 1]),
                pltpu.make_async_copy(wd_hbm.at[e], wd_buf.at[s], sem.at[s, 2]))

    @pl.when(j == 0)
    def _():
        for c in weight_copies(te_ref[0], 0):
            c.start()

    @pl.when(jnp.logical_and(valid, first_ref[j] == 1))
    def _():
        for c in weight_copies(te_ref[j], slot_ref[j]):
            c.wait()

        @pl.when(nxt_ref[j] >= 0)
        def _():
            for c in weight_copies(nxt_ref[j], 1 - slot_ref[j]):
                c.start()

    @pl.when(valid)
    def _():
        s = slot_ref[j]
        lo, hi = _unpack_bf16_pairs(xs_ref[...])
        xs = jnp.concatenate([lo.astype(BF16), hi.astype(BF16)], axis=1)
        gt = jnp.dot(xs, wg_buf[s].astype(BF16), preferred_element_type=F32)
        up = jnp.dot(xs, wu_buf[s].astype(BF16), preferred_element_type=F32)
        hid = (gt * _sigmoid(gt) * up).astype(BF16)
        ys_ref[...] = _pack_bf16_pairs(jnp.dot(hid, wd_buf[s].astype(BF16), preferred_element_type=F32))

    @pl.when(jnp.logical_not(valid))
    def _():
        ys_ref[...] = jnp.zeros(ys_ref.shape, ys_ref.dtype)


def _moe(tile_expert, n_valid, tile_first, tile_slot, tile_next, xs, wg, wu, wd):
    R, Dw = xs.shape
    D, F = wg.shape[1], wg.shape[2]
    tm = MOE_TM
    hbm = pl.BlockSpec(memory_space=pl.ANY)
    grid_spec = pltpu.PrefetchScalarGridSpec(
        num_scalar_prefetch=5,
        grid=(R // tm,),
        in_specs=[pl.BlockSpec((tm, Dw), lambda j, *_: (j, 0)), hbm, hbm, hbm],
        out_specs=pl.BlockSpec((tm, Dw), lambda j, *_: (j, 0)),
        scratch_shapes=[
            pltpu.VMEM((2, D, F), wg.dtype),
            pltpu.VMEM((2, D, F), wu.dtype),
            pltpu.VMEM((2, F, D), wd.dtype),
            pltpu.SemaphoreType.DMA((2, 3)),
        ],
    )
    return pl.pallas_call(
        _moe_kernel,
        grid_spec=grid_spec,
        out_shape=jax.ShapeDtypeStruct((R, Dw), jnp.uint32),
        compiler_params=_cparams(("arbitrary",)),
        name="moe_experts",
    )(tile_expert, n_valid, tile_first, tile_slot, tile_next, xs, wg, wu, wd)


def _final_kernel(x1_ref, y0_ref, y1_ref, cw_ref, g_ref, o_ref):
    cw = cw_ref[...]
    lo0, hi0 = _unpack_bf16_pairs(y0_ref[...])
    lo1, hi1 = _unpack_bf16_pairs(y1_ref[...])
    w0, w1 = cw[:, 0:1], cw[:, 1:2]
    y = jnp.concatenate([w0 * lo0 + w1 * lo1, w0 * hi0 + w1 * hi1], axis=1)
    x = x1_ref[...] + y
    ms = jnp.mean(x * x, axis=-1, keepdims=True)
    o_ref[...] = x * lax.rsqrt(ms + EPS) * g_ref[...]


def _final(x1, yw, cw, g):
    T, D = x1.shape
    tm = FIN_TM
    row = lambda w: pl.BlockSpec((tm, w), lambda m: (m, 0))
    slot1 = pl.BlockSpec((tm, D // 2), lambda m: (m + T // tm, 0))
    return pl.pallas_call(
        _final_kernel,
        grid=(T // tm,),
        in_specs=[row(D), row(D // 2), slot1, row(LANES), pl.BlockSpec((1, D), lambda m: (0, 0))],
        out_specs=row(D),
        out_shape=jax.ShapeDtypeStruct((T, D), F32),
        compiler_params=_cparams(("parallel",)),
        name="combine_final_norm",
    )(x1, yw, yw, cw, g)


SC_CORES, SC_SUBCORES = 2, 16
SC_CHUNK = 32


def _sc_gather_rows(table, idx):
    V, Dw = table.shape
    R = idx.shape[0]
    n_workers = SC_CORES * SC_SUBCORES
    ch = SC_CHUNK
    per_w = R // n_workers
    n_chunks = per_w // ch
    assert per_w * n_workers == R and n_chunks * ch == per_w and n_chunks % 2 == 0
    idx3 = idx.reshape(n_workers, n_chunks, ch)
    mesh = plsc.VectorSubcoreMesh(core_axis_name="c", subcore_axis_name="s")

    def body(table_hbm, idx_hbm, out_hbm, idx_v, rows_v, gsem, osem):
        wid = lax.axis_index("s") * SC_CORES + lax.axis_index("c")
        base = wid * per_w
        pltpu.sync_copy(idx_hbm.at[wid], idx_v)

        def gather(c, slot):
            return pltpu.make_async_copy(table_hbm.at[idx_v.at[c]], rows_v.at[slot], gsem.at[slot])

        def put(c, slot):
            return pltpu.make_async_copy(rows_v.at[slot], out_hbm.at[pl.ds(base + c * ch, ch)],
                                         osem.at[slot])

        gather(0, 0).start()

        @pl.loop(0, n_chunks, step=2)
        def _(c):
            @pl.when(c > 0)
            def _():
                put(c - 1, 1).wait()

            gather(c + 1, 1).start()
            gather(c, 0).wait()
            put(c, 0).start()
            put(c, 0).wait()

            @pl.when(c + 2 < n_chunks)
            def _():
                gather(c + 2, 0).start()

            gather(c + 1, 1).wait()
            put(c + 1, 1).start()

        put(n_chunks - 1, 1).wait()

    return pl.kernel(
        body,
        out_type=jax.ShapeDtypeStruct((R, Dw), table.dtype),
        mesh=mesh,
        scratch_types=[
            pltpu.VMEM((n_chunks, ch), jnp.int32),
            pltpu.VMEM((2, ch, Dw), table.dtype),
            pltpu.SemaphoreType.DMA((2,)),
            pltpu.SemaphoreType.DMA((2,)),
        ],
        name="sc_gather_rows",
    )(table, idx3)


def _route(logits, b_group, b_router):
    T = logits.shape[0]
    G, E = N_GROUPS, EXPERTS_PER_GROUP
    gl = logits[:, :G] + b_group.astype(F32)
    gp = jax.nn.softmax(gl, axis=-1)
    gsel = jnp.argmax(gl, axis=-1)
    gw = jnp.take_along_axis(gp, gsel[:, None], axis=1)[:, 0]
    el = logits[:, G:G + G * E].reshape(T, G, E) + b_router.astype(F32)
    el_sel = jnp.take_along_axis(el, gsel[:, None, None], axis=1)[:, 0, :]
    top_v, top_i = lax.top_k(el_sel, TOP_K_INNER)
    top_w = jax.nn.softmax(top_v, axis=-1)
    eid = (gsel[:, None] * E + top_i).astype(jnp.int32)
    cw = gw[:, None] * top_w
    return eid, cw


def _dispatch_plan(eid, tm, n_tiles):
    T2 = eid.size
    flat_e = eid.reshape(-1)
    onehot = (flat_e[:, None] == jnp.arange(N_EXPERTS, dtype=jnp.int32)[None, :]).astype(jnp.int32)
    csum = jnp.cumsum(onehot, axis=0)
    rank = jnp.sum((csum - onehot) * onehot, axis=1)
    counts = csum[-1]
    tiles_per_e = (counts + tm - 1) // tm
    tile_end = jnp.cumsum(tiles_per_e)
    row_start = (tile_end - tiles_per_e) * tm
    pos = (row_start[flat_e] + rank).astype(jnp.int32)
    n_valid = tile_end[-1].astype(jnp.int32)
    tile_ids = jnp.arange(n_tiles, dtype=jnp.int32)
    tile_expert = jnp.searchsorted(tile_end, jnp.minimum(tile_ids, n_valid - 1), side="right")
    tile_expert = jnp.minimum(tile_expert, N_EXPERTS - 1).astype(jnp.int32)
    valid = tile_ids < n_valid
    prev_expert = jnp.concatenate([jnp.full((1,), -1, jnp.int32), tile_expert[:-1]])
    tile_first = jnp.logical_and(valid, tile_expert != prev_expert).astype(jnp.int32)
    tile_slot = ((jnp.cumsum(tile_first) - 1) % 2).astype(jnp.int32)
    next_start = tile_end[tile_expert]
    tile_next = jnp.where(next_start < n_valid, tile_expert[jnp.minimum(next_start, n_tiles - 1)],
                          -1).astype(jnp.int32)
    n_rows = n_tiles * tm
    src_token = (jnp.arange(n_rows, dtype=jnp.int32) % (T2 // TOP_K_INNER)).at[pos].set(
        jnp.arange(T2, dtype=jnp.int32) // TOP_K_INNER)
    tiles = (tile_expert, n_valid.reshape(1), tile_first, tile_slot, tile_next)
    return pos.reshape(eid.shape), src_token, tiles


def kernel(x, rel_bias, ln_mix_g, w_in, conv_w, conv_b, b_i, b_f, lam_q1, lam_k1, lam_q2, lam_k2,
           diff_norm_g, mlstm_norm_g, w_out, ln_ffn_g, w_group, b_group, w_router, b_router,
           w_gate, w_up, w_down, ln_f_g):
    B, S, D = x.shape
    T = B * S
    depth = w_in.shape[0]
    assert depth == 1, "the final rmsnorm is fused into the single layer's combine kernel"
    Hm = N_MLSTM_HEADS
    n_main = w_in.shape[2] - 2 * Hm
    n_diff = N_DIFF_HEADS * 2 * DIFF_HEAD_DIM
    xf = x.reshape(T, D)

    for l in range(depth):
        lambda_init = 0.8 - 0.6 * math.exp(-0.3 * l)
        w_main = w_in[l, :, :n_main].astype(BF16)
        wgt = w_in[l, :, n_main:]
        w_gates = jnp.zeros((D, 2 * LANES), F32).at[:, :Hm].set(wgt[:, :Hm]).at[:, LANES:LANES + Hm].set(
            wgt[:, Hm:]).astype(BF16)
        bi_row = jnp.zeros((1, LANES), F32).at[0, :Hm].set(b_i[l].astype(F32))
        bf_row = jnp.zeros((1, LANES), F32).at[0, :Hm].set(b_f[l].astype(F32))
        lam = (jnp.exp(jnp.sum(lam_q1[l].astype(F32) * lam_k1[l].astype(F32)))
               - jnp.exp(jnp.sum(lam_q2[l].astype(F32) * lam_k2[l].astype(F32))) + lambda_init)
        tq = ATT_TQ
        assert tq >= MAX_DISTANCE and tq % CHUNK == 0
        ii = jnp.arange(tq, dtype=jnp.int32)
        rel0 = ii[:, None] - ii[None, :]
        rb = rel_bias.astype(F32)

        def bias_of(bucket):
            out = jnp.zeros((N_DIFF_HEADS,) + bucket.shape, F32)
            for bkt in range(N_BUCKETS):
                out = jnp.where(bucket[None] == bkt, rb[bkt][:, None, None], out)
            return out

        log2e = math.log2(math.e)
        bias_prev = bias_of(_t5_bucket(rel0 - tq)) * log2e
        bias_diag = jnp.where(((ii[:, None] // CHUNK) <= (ii[None, :] // CHUNK))[None],
                              bias_of(_t5_bucket(rel0)) * log2e, NEG_BIG)
        bias_tiles = jnp.stack([bias_prev, bias_diag], axis=1)
        cfar = rb[N_BUCKETS // 2 - 1] * log2e
        scal = jnp.concatenate([lam.reshape(1), cfar]).astype(F32)
        col_scale = jnp.ones((1, n_main), F32).at[:, :n_diff].set(DIFF_HEAD_DIM ** -0.5 * log2e)

        proj, gates = _proj(xf, ln_mix_g[l].reshape(1, D).astype(F32), w_main, col_scale, w_gates)
        proj3 = proj.reshape(B, S, n_main)
        a = _diff_attention(proj3, scal, bias_tiles, diff_norm_g[l].reshape(1, n_diff).astype(F32),
                            lambda_init)
        hm = _mlstm(proj3, gates.reshape(B, S, 2 * LANES), conv_w[l].astype(F32),
                    conv_b[l].reshape(1, -1).astype(F32), bi_row, bf_row,
                    mlstm_norm_g[l].reshape(1, -1).astype(F32))

        wo = w_out[l].astype(BF16)
        G, E = N_GROUPS, EXPERTS_PER_GROUP
        wr = jnp.zeros((D, LANES), F32).at[:, :G].set(w_group[l].astype(F32)).at[:, G:G + G * E].set(
            jnp.transpose(w_router[l].astype(F32), (1, 0, 2)).reshape(D, G * E)).astype(BF16)
        x1, h2, logits = _out_proj(xf, a.reshape(T, n_diff), hm.reshape(T, -1), wo[:n_diff], wo[n_diff:],
                                   ln_ffn_g[l].reshape(1, D).astype(F32), wr)

        eid, cw = _route(logits, b_group[l], b_router[l])
        n_tiles = (T * TOP_K_INNER) // MOE_TM + N_EXPERTS
        pos, src_token, tiles = _dispatch_plan(eid, MOE_TM, n_tiles)
        xs = _sc_gather_rows(h2, src_token)
        Fe = w_gate.shape[-1]
        ys = _moe(*tiles, xs, w_gate[l].reshape(N_EXPERTS, D, Fe),
                  w_up[l].reshape(N_EXPERTS, D, Fe), w_down[l].reshape(N_EXPERTS, Fe, D))
        yw = _sc_gather_rows(ys, pos.T.reshape(-1))
        cwp = jnp.zeros((T, LANES), F32).at[:, :TOP_K_INNER].set(cw)
        xf = _final(x1, yw, cwp, ln_f_g.reshape(1, D).astype(F32))
    return xf.reshape(B, S, D)
```

```python
import functools
import math

import jax
import jax.numpy as jnp
from jax import lax
from jax.experimental import pallas as pl
from jax.experimental.pallas import tpu as pltpu
from jax.experimental.pallas import tpu_sc as plsc

F32 = jnp.float32
BF16 = jnp.bfloat16

EPS = 1e-6
CHUNK = 64
DIFF_HEAD_DIM = 64
N_DIFF_HEADS = 8
MLSTM_HEAD_DIM = 128
N_MLSTM_HEADS = 8
CONV_WIDTH = 4
N_BUCKETS = 32
MAX_DISTANCE = 128
N_GROUPS = 4
EXPERTS_PER_GROUP = 8
N_EXPERTS = N_GROUPS * EXPERTS_PER_GROUP
TOP_K_INNER = 2
LANES = 128
NEG_BIG = -1e30

VMEM_LIMIT = 56 * 1024 * 1024

PROJ_TM, PROJ_TN = 1024, 1024
ATT_TQ = 512
MLSTM_TS = 512
OUT_TM = 512
MOE_TM = 256
FIN_TM = 512


def _cparams(sem):
    return pltpu.CompilerParams(dimension_semantics=sem, vmem_limit_bytes=VMEM_LIMIT)


_HI_MASK = 0xFFFF0000


def _pack_bf16_pairs(x):
    half = x.shape[-1] // 2
    xb = x.astype(BF16).astype(F32)
    lo = pltpu.bitcast(xb[:, :half], jnp.uint32)
    hi = pltpu.bitcast(xb[:, half:], jnp.uint32)
    return (hi & jnp.uint32(_HI_MASK)) | (lo >> 16)


def _unpack_bf16_pairs(w):
    lo = pltpu.bitcast(w << 16, F32)
    hi = pltpu.bitcast(w & jnp.uint32(_HI_MASK), F32)
    return lo, hi


def _proj_kernel(x_ref, g_ref, w_ref, cs_ref, wg_ref, o_ref, og_ref, h_ref):
    @pl.when(pl.program_id(1) == 0)
    def _():
        x = x_ref[...]
        ms = jnp.mean(x * x, axis=-1, keepdims=True)
        h = (x * lax.rsqrt(ms + EPS) * g_ref[...]).astype(BF16)
        h_ref[...] = h
        og_ref[...] = jnp.dot(h, wg_ref[...], preferred_element_type=F32)

    o_ref[...] = (jnp.dot(h_ref[...], w_ref[...], preferred_element_type=F32) * cs_ref[...]).astype(o_ref.dtype)


def _proj(x2, g, w_main, col_scale, w_gates):
    T, D = x2.shape
    N = w_main.shape[1]
    NG = w_gates.shape[1]
    return pl.pallas_call(
        _proj_kernel,
        grid=(T // PROJ_TM, N // PROJ_TN),
        in_specs=[
            pl.BlockSpec((PROJ_TM, D), lambda m, n: (m, 0)),
            pl.BlockSpec((1, D), lambda m, n: (0, 0)),
            pl.BlockSpec((D, PROJ_TN), lambda m, n: (0, n)),
            pl.BlockSpec((1, PROJ_TN), lambda m, n: (0, n)),
            pl.BlockSpec((D, NG), lambda m, n: (0, 0)),
        ],
        out_specs=[
            pl.BlockSpec((PROJ_TM, PROJ_TN), lambda m, n: (m, n)),
            pl.BlockSpec((PROJ_TM, NG), lambda m, n: (m, 0)),
        ],
        out_shape=[
            jax.ShapeDtypeStruct((T, N), BF16),
            jax.ShapeDtypeStruct((T, NG), F32),
        ],
        scratch_shapes=[pltpu.VMEM((PROJ_TM, D), BF16)],
        compiler_params=_cparams(("parallel", "arbitrary")),
        name="rms_in_proj",
    )(x2, g, w_main, col_scale, w_gates)


def _t5_bucket(rel):
    half = N_BUCKETS // 2
    max_exact = half // 2
    ret = jnp.where(rel > 0, half, 0)
    n = jnp.abs(rel)
    nf = jnp.maximum(n, 1).astype(F32)
    large = max_exact + (jnp.log(nf / max_exact) / math.log(MAX_DISTANCE / max_exact)
                         * (half - max_exact)).astype(jnp.int32)
    large = jnp.minimum(large, half - 1)
    return ret + jnp.where(n < max_exact, n, large)


def _attn_kernel(scal_ref, q_ref, k_ref, v_ref, bias_ref, g_ref, o_ref, m_ref, l_ref, acc_ref,
                 s0_ref, s1_ref, ml0_ref, ml1_ref, *, lambda_init):
    h = pl.program_id(1)
    qi = pl.program_id(2)
    tq = ATT_TQ
    lam = scal_ref[0]
    cfar = scal_ref[1 + h]

    q = q_ref[0]
    lane = lax.broadcasted_iota(jnp.int32, q.shape, 1)
    zero = jnp.zeros_like(q)
    qs = jnp.concatenate([jnp.where(lane < DIFF_HEAD_DIM, q, zero),
                          jnp.where(lane >= DIFF_HEAD_DIM, q, zero)], axis=0)

    m_ref[...] = jnp.full(m_ref.shape, NEG_BIG, F32)
    l_ref[...] = jnp.zeros(l_ref.shape, F32)
    acc_ref[...] = jnp.zeros(acc_ref.shape, F32)

    bufs = ((s0_ref, ml0_ref), (s1_ref, ml1_ref))

    def score(ki, bias, slot):
        s_ref, ml_ref = bufs[slot]
        start = pl.multiple_of(ki * tq, tq)
        kt = k_ref[0, pl.ds(start, tq), :]
        s = lax.dot_general(kt, qs, (((1,), (1,)), ((), ())), preferred_element_type=F32)
        if bias is not None:
            s = s + jnp.concatenate([bias, bias], axis=1)
        s_ref[...] = s
        ml_ref[...] = jnp.max(s, axis=0, keepdims=True)

    def accumulate(ki, shift, slot):
        s_ref, ml_ref = bufs[slot]
        start = pl.multiple_of(ki * tq, tq)
        vt = v_ref[0, pl.ds(start, tq), :]
        m_old = m_ref[...]
        m_new = jnp.maximum(m_old, ml_ref[...] + shift)
        alpha = jnp.exp2(m_old - m_new)
        p = jnp.exp2(s_ref[...] - (m_new - shift))
        l_ref[...] = alpha * l_ref[...] + jnp.sum(p, axis=0, keepdims=True)
        pv = lax.dot_general(vt, p.astype(BF16), (((0,), (0,)), ((), ())), preferred_element_type=F32)
        acc_ref[...] = alpha * acc_ref[...] + pv
        m_ref[...] = m_new

    n_far = qi - 1
    score(qi, bias_ref[0, 1], 0)

    @pl.when(qi == 0)
    def _():
        accumulate(qi, 0.0, 0)

    @pl.when(qi >= 1)
    def _():
        accumulate(qi, 0.0, 0)
        score(qi - 1, bias_ref[0, 0], 1)

    @pl.when(qi == 1)
    def _():
        accumulate(qi - 1, 0.0, 1)

    @pl.when(qi >= 2)
    def _():
        accumulate(qi - 1, 0.0, 1)
        score(0, None, 0)
        trips = (n_far - 1) // 2

        def pair(j, c):
            accumulate(2 * j, cfar, 0)
            score(2 * j + 1, None, 1)
            accumulate(2 * j + 1, cfar, 1)
            score(2 * j + 2, None, 0)
            return c

        lax.fori_loop(0, trips, pair, 0)
        last = 2 * trips

        @pl.when(n_far - last == 2)
        def _():
            accumulate(last, cfar, 0)
            score(last + 1, None, 1)
            accumulate(last + 1, cfar, 1)

        @pl.when(n_far - last == 1)
        def _():
            accumulate(last, cfar, 0)

    acc = acc_ref[...] * (1.0 / l_ref[...])
    o_t = acc[:, 0:tq] - lam * acc[:, tq:2 * tq]
    ms = jnp.mean(o_t * o_t, axis=0, keepdims=True)
    y = (o_t * lax.rsqrt(ms + EPS)).T * (g_ref[...] * (1.0 - lambda_init))
    o_ref[0] = y.astype(o_ref.dtype)


def _diff_attention(proj3, scal, bias_tiles, gnorm, lambda_init):
    B, S, _ = proj3.shape
    H = N_DIFF_HEADS
    tq = ATT_TQ
    kern = functools.partial(_attn_kernel, lambda_init=lambda_init)
    return pl.pallas_call(
        kern,
        grid=(B, H, S // tq),
        in_specs=[
            pl.BlockSpec(memory_space=pltpu.SMEM),
            pl.BlockSpec((1, tq, LANES), lambda b, h, i: (b, i, h)),
            pl.BlockSpec((1, S, LANES), lambda b, h, i: (b, 0, H + h)),
            pl.BlockSpec((1, S, LANES), lambda b, h, i: (b, 0, 2 * H + h)),
            pl.BlockSpec((1, 2, tq, tq), lambda b, h, i: (h, 0, 0, 0)),
            pl.BlockSpec((1, LANES), lambda b, h, i: (0, h)),
        ],
        out_specs=pl.BlockSpec((1, tq, LANES), lambda b, h, i: (b, i, h)),
        out_shape=jax.ShapeDtypeStruct((B, S, H * LANES), BF16),
        scratch_shapes=[
            pltpu.VMEM((1, 2 * tq), F32),
            pltpu.VMEM((1, 2 * tq), F32),
            pltpu.VMEM((LANES, 2 * tq), F32),
            pltpu.VMEM((tq, 2 * tq), F32),
            pltpu.VMEM((tq, 2 * tq), F32),
            pltpu.VMEM((1, 2 * tq), F32),
            pltpu.VMEM((1, 2 * tq), F32),
        ],
        compiler_params=_cparams(("parallel", "parallel", "arbitrary")),
        name="diff_attention",
    )(scal, proj3, proj3, proj3, bias_tiles, gnorm)


def _log_sigmoid(x):
    return jnp.minimum(x, 0.0) - jnp.log(1.0 + jnp.exp(-jnp.abs(x)))


def _sigmoid(x):
    return 1.0 / (1.0 + jnp.exp(-x))


def _mlstm_kernel(q_ref, k_ref, v_ref, o_ref, gi_ref, gf_ref, cw_ref, cb_ref, bi_ref, bf_ref, gn_ref,
                  out_ref, qext_ref, kext_ref, ct_ref, n_ref, m_ref):
    sb = pl.program_id(1)
    L = CHUNK
    dh = MLSTM_HEAD_DIM
    H = N_MLSTM_HEADS
    ts = MLSTM_TS
    pad = 8

    @pl.when(sb == 0)
    def _():
        qext_ref[0:pad, :] = jnp.zeros((pad, H * dh), F32)
        kext_ref[0:pad, :] = jnp.zeros((pad, H * dh), F32)
        ct_ref[...] = jnp.zeros(ct_ref.shape, F32)
        n_ref[...] = jnp.zeros(n_ref.shape, F32)
        m_ref[...] = jnp.zeros(m_ref.shape, F32)

    qext_ref[pad:pad + ts, :] = q_ref[0].astype(F32)
    kext_ref[pad:pad + ts, :] = k_ref[0].astype(F32)

    row = lax.broadcasted_iota(jnp.int32, (L, L), 0)
    col = lax.broadcasted_iota(jnp.int32, (L, L), 1)
    tril = col <= row
    ltri = tril.astype(F32)

    def conv_silu(ext_ref, base, h, off):
        win = ext_ref[pl.ds(base, L + pad), h * dh:(h + 1) * dh]
        w = cw_ref[:, off + h * dh:off + (h + 1) * dh]
        y = cb_ref[:, off + h * dh:off + (h + 1) * dh]
        for j in range(CONV_WIDTH):
            lo = pad - (CONV_WIDTH - 1) + j
            y = y + w[j:j + 1, :] * win[lo:lo + L, :]
        return y * _sigmoid(y)

    def chunk_body(c, carry):
        base = pl.multiple_of(c * L, L)
        li = gi_ref[0, pl.ds(base, L), :] + bi_ref[...]
        logf = _log_sigmoid(gf_ref[0, pl.ds(base, L), :] + bf_ref[...])
        b = jnp.dot(ltri, logf, preferred_element_type=F32, precision=lax.Precision.HIGHEST)
        a = li - b
        g_row = b[L - 1:L, :]
        m_row = m_ref[...]
        m_new_row = g_row + jnp.maximum(m_row, jnp.max(a, axis=0, keepdims=True))
        a_t = a.T

        for h in range(H):
            qc = conv_silu(qext_ref, base, h, 0)
            kc = conv_silu(kext_ref, base, h, H * dh) * (dh ** -0.5)
            qb = qc.astype(BF16)
            kb = kc.astype(BF16)
            vb = v_ref[0, pl.ds(base, L), h * dh:(h + 1) * dh]

            a_row = a_t[h:h + 1, :]
            a_col = a[:, h:h + 1]
            b_col = b[:, h:h + 1]
            m_prev = m_row[:, h:h + 1]
            m_next = m_new_row[:, h:h + 1]
            g_h = g_row[:, h:h + 1]

            amat = jnp.where(tril, a_row, NEG_BIG)
            mcol = jnp.maximum(jnp.max(amat, axis=-1, keepdims=True), m_prev)
            wts = jnp.exp(amat - mcol)
            inter = jnp.exp(m_prev - mcol)

            s = lax.dot_general(qb, kb, (((1,), (1,)), ((), ())), preferred_element_type=F32)
            sqk = s * wts
            ct = ct_ref[h]
            nrow = n_ref[h:h + 1, :]
            num = (jnp.dot(sqk.astype(BF16), vb, preferred_element_type=F32)
                   + inter * jnp.dot(qb, ct.astype(BF16), preferred_element_type=F32))
            den = (jnp.sum(sqk, axis=-1, keepdims=True)
                   + inter * jnp.sum(qb.astype(F32) * nrow, axis=-1, keepdims=True))
            hv = num / jnp.maximum(jnp.abs(den), jnp.exp(-(b_col + mcol)))

            wt = jnp.exp(g_h + a_col - m_next)
            decay = jnp.exp(g_h + m_prev - m_next)
            wv = (wt * vb.astype(F32)).astype(BF16)
            ct_ref[h] = decay * ct + lax.dot_general(kb, wv, (((0,), (0,)), ((), ())),
                                                     preferred_element_type=F32)
            n_ref[h:h + 1, :] = decay * nrow + jnp.sum(wt * kb.astype(F32), axis=0, keepdims=True)

            ms = jnp.mean(hv * hv, axis=-1, keepdims=True)
            y = hv * lax.rsqrt(ms + EPS) * gn_ref[:, h * dh:(h + 1) * dh]
            og = o_ref[0, pl.ds(base, L), h * dh:(h + 1) * dh].astype(F32)
            out_ref[0, pl.ds(base, L), h * dh:(h + 1) * dh] = (y * _sigmoid(og)).astype(out_ref.dtype)

        m_ref[...] = m_new_row
        return carry

    lax.fori_loop(0, ts // L, chunk_body, 0)

    qext_ref[0:pad, :] = qext_ref[ts:ts + pad, :]
    kext_ref[0:pad, :] = kext_ref[ts:ts + pad, :]


def _mlstm(proj3, gates3, conv_w, conv_b, bi_row, bf_row, gnorm):
    B, S, _ = proj3.shape
    W = N_MLSTM_HEADS * MLSTM_HEAD_DIM
    ts = MLSTM_TS
    first = 3
    blk = lambda j: pl.BlockSpec((1, ts, W), lambda b, s: (b, s, j))
    full = lambda shape: pl.BlockSpec(shape, lambda b, s: (0,) * len(shape))
    return pl.pallas_call(
        _mlstm_kernel,
        grid=(B, S // ts),
        in_specs=[
            blk(first), blk(first + 1), blk(first + 2), blk(first + 3),
            pl.BlockSpec((1, ts, LANES), lambda b, s: (b, s, 0)),
            pl.BlockSpec((1, ts, LANES), lambda b, s: (b, s, 1)),
            full((CONV_WIDTH, 2 * W)), full((1, 2 * W)),
            full((1, LANES)), full((1, LANES)), full((1, W)),
        ],
        out_specs=pl.BlockSpec((1, ts, W), lambda b, s: (b, s, 0)),
        out_shape=jax.ShapeDtypeStruct((B, S, W), BF16),
        scratch_shapes=[
            pltpu.VMEM((ts + 8, W), F32),
            pltpu.VMEM((ts + 8, W), F32),
            pltpu.VMEM((N_MLSTM_HEADS, MLSTM_HEAD_DIM, MLSTM_HEAD_DIM), F32),
            pltpu.VMEM((N_MLSTM_HEADS, MLSTM_HEAD_DIM), F32),
            pltpu.VMEM((1, LANES), F32),
        ],
        compiler_params=_cparams(("parallel", "arbitrary")),
        name="mlstm",
    )(proj3, proj3, proj3, proj3, gates3, gates3, conv_w, conv_b, bi_row, bf_row, gnorm)


def _out_kernel(x_ref, a_ref, hm_ref, wa_ref, wm_ref, g_ref, wr_ref, x1_ref, h2_ref, lg_ref):
    y = (jnp.dot(a_ref[...], wa_ref[...], preferred_element_type=F32)
         + jnp.dot(hm_ref[...], wm_ref[...], preferred_element_type=F32))
    x1 = x_ref[...] + y
    x1_ref[...] = x1
    ms = jnp.mean(x1 * x1, axis=-1, keepdims=True)
    h2 = x1 * lax.rsqrt(ms + EPS) * g_ref[...]
    h2_ref[...] = _pack_bf16_pairs(h2)
    lg_ref[...] = jnp.dot(h2.astype(BF16), wr_ref[...], preferred_element_type=F32)


def _out_proj(x2, a2, hm2, wa, wm, g, wr):
    T, D = x2.shape
    W = a2.shape[1]
    tm = OUT_TM
    const = lambda shape: pl.BlockSpec(shape, lambda m: (0, 0), pipeline_mode=pl.Buffered(1))
    return pl.pallas_call(
        _out_kernel,
        grid=(T // tm,),
        in_specs=[
            pl.BlockSpec((tm, D), lambda m: (m, 0)),
            pl.BlockSpec((tm, W), lambda m: (m, 0)),
            pl.BlockSpec((tm, W), lambda m: (m, 0)),
            const((W, D)), const((W, D)), const((1, D)), const((D, LANES)),
        ],
        out_specs=[
            pl.BlockSpec((tm, D), lambda m: (m, 0)),
            pl.BlockSpec((tm, D // 2), lambda m: (m, 0)),
            pl.BlockSpec((tm, LANES), lambda m: (m, 0)),
        ],
        out_shape=[
            jax.ShapeDtypeStruct((T, D), F32),
            jax.ShapeDtypeStruct((T, D // 2), jnp.uint32),
            jax.ShapeDtypeStruct((T, LANES), F32),
        ],
        compiler_params=_cparams(("parallel",)),
        name="out_proj_router",
    )(x2, a2, hm2, wa, wm, g, wr)


def _moe_kernel(te_ref, nv_ref, first_ref, slot_ref, nxt_ref, xs_ref, wg_hbm, wu_hbm, wd_hbm, ys_ref,
                wg_buf, wu_buf, wd_buf, sem):
    j = pl.program_id(0)
    valid = j < nv_ref[0]

    half_f = wd_buf.shape[1] // 2

    def weight_copies(e, s):
        lo, hi = pl.ds(0, half_f), pl.ds(half_f, half_f)
        return ((pltpu.make_async_copy(wg_hbm.at[e], wg_buf.at[s], sem.at[s, 0]), 0),
                (pltpu.make_async_copy(wu_hbm.at[e], wu_buf.at[s], sem.at[s, 1]), 1),
                (pltpu.make_async_copy(wd_hbm.at[e, lo], wd_buf.at[s, lo], sem.at[s, 2]), 0),
                (pltpu.make_async_copy(wd_hbm.at[e, hi], wd_buf.at[s, hi], sem.at[s, 3]), 1))

    @pl.when(j == 0)
    def _():
        for c, prio in weight_copies(te_ref[0], 0):
            c.start(priority=prio)

    @pl.when(jnp.logical_and(valid, first_ref[j] == 1))
    def _():
        for c, _ in weight_copies(te_ref[j], slot_ref[j]):
            c.wait()

        @pl.when(nxt_ref[j] >= 0)
        def _():
            for c, prio in weight_copies(nxt_ref[j], 1 - slot_ref[j]):
                c.start(priority=prio)

    @pl.when(valid)
    def _():
        s = slot_ref[j]
        lo, hi = _unpack_bf16_pairs(xs_ref[...])
        xs = jnp.concatenate([lo.astype(BF16), hi.astype(BF16)], axis=1)
        gt = jnp.dot(xs, wg_buf[s].astype(BF16), preferred_element_type=F32)
        up = jnp.dot(xs, wu_buf[s].astype(BF16), preferred_element_type=F32)
        hid = (gt * _sigmoid(gt) * up).astype(BF16)
        ys_ref[...] = _pack_bf16_pairs(jnp.dot(hid, wd_buf[s].astype(BF16), preferred_element_type=F32))

    @pl.when(jnp.logical_not(valid))
    def _():
        ys_ref[...] = jnp.zeros(ys_ref.shape, ys_ref.dtype)


def _moe(tile_expert, n_valid, tile_first, tile_slot, tile_next, xs, wg, wu, wd):
    R, Dw = xs.shape
    D, F = wg.shape[1], wg.shape[2]
    tm = MOE_TM
    hbm = pl.BlockSpec(memory_space=pl.ANY)
    grid_spec = pltpu.PrefetchScalarGridSpec(
        num_scalar_prefetch=5,
        grid=(R // tm,),
        in_specs=[pl.BlockSpec((tm, Dw), lambda j, *_: (j, 0)), hbm, hbm, hbm],
        out_specs=pl.BlockSpec((tm, Dw), lambda j, *_: (j, 0)),
        scratch_shapes=[
            pltpu.VMEM((2, D, F), wg.dtype),
            pltpu.VMEM((2, D, F), wu.dtype),
            pltpu.VMEM((2, F, D), wd.dtype),
            pltpu.SemaphoreType.DMA((2, 4)),
        ],
    )
    return pl.pallas_call(
        _moe_kernel,
        grid_spec=grid_spec,
        out_shape=jax.ShapeDtypeStruct((R, Dw), jnp.uint32),
        compiler_params=_cparams(("arbitrary",)),
        name="moe_experts",
    )(tile_expert, n_valid, tile_first, tile_slot, tile_next, xs, wg, wu, wd)


def _final_kernel(x1_ref, y0_ref, y1_ref, cw_ref, g_ref, o_ref):
    cw = cw_ref[...]
    lo0, hi0 = _unpack_bf16_pairs(y0_ref[...])
    lo1, hi1 = _unpack_bf16_pairs(y1_ref[...])
    w0, w1 = cw[:, 0:1], cw[:, 1:2]
    y = jnp.concatenate([w0 * lo0 + w1 * lo1, w0 * hi0 + w1 * hi1], axis=1)
    x = x1_ref[...] + y
    ms = jnp.mean(x * x, axis=-1, keepdims=True)
    o_ref[...] = x * lax.rsqrt(ms + EPS) * g_ref[...]


def _final(x1, yw, cw, g):
    T, D = x1.shape
    tm = FIN_TM
    row = lambda w: pl.BlockSpec((tm, w), lambda m: (m, 0))
    slot1 = pl.BlockSpec((tm, D // 2), lambda m: (m + T // tm, 0))
    return pl.pallas_call(
        _final_kernel,
        grid=(T // tm,),
        in_specs=[row(D), row(D // 2), slot1, row(LANES), pl.BlockSpec((1, D), lambda m: (0, 0))],
        out_specs=row(D),
        out_shape=jax.ShapeDtypeStruct((T, D), F32),
        compiler_params=_cparams(("parallel",)),
        name="combine_final_norm",
    )(x1, yw, yw, cw, g)


SC_CORES, SC_SUBCORES = 2, 16
SC_CHUNK = 32


def _sc_gather_rows(table, idx):
    V, Dw = table.shape
    R = idx.shape[0]
    n_workers = SC_CORES * SC_SUBCORES
    ch = SC_CHUNK
    per_w = R // n_workers
    n_chunks = per_w // ch
    assert per_w * n_workers == R and n_chunks * ch == per_w and n_chunks % 2 == 0
    idx3 = idx.reshape(n_workers, n_chunks, ch)
    mesh = plsc.VectorSubcoreMesh(core_axis_name="c", subcore_axis_name="s")

    def body(table_hbm, idx_hbm, out_hbm, idx_v, rows_v, gsem, osem):
        wid = lax.axis_index("s") * SC_CORES + lax.axis_index("c")
        base = wid * per_w
        pltpu.sync_copy(idx_hbm.at[wid], idx_v)

        def gather(c, slot):
            return pltpu.make_async_copy(table_hbm.at[idx_v.at[c]], rows_v.at[slot], gsem.at[slot])

        def put(c, slot):
            return pltpu.make_async_copy(rows_v.at[slot], out_hbm.at[pl.ds(base + c * ch, ch)],
                                         osem.at[slot])

        gather(0, 0).start()

        @pl.loop(0, n_chunks, step=2)
        def _(c):
            @pl.when(c > 0)
            def _():
                put(c - 1, 1).wait()

            gather(c + 1, 1).start()
            gather(c, 0).wait()
            put(c, 0).start()
            put(c, 0).wait()

            @pl.when(c + 2 < n_chunks)
            def _():
                gather(c + 2, 0).start()

            gather(c + 1, 1).wait()
            put(c + 1, 1).start()

        put(n_chunks - 1, 1).wait()

    return pl.kernel(
        body,
        out_type=jax.ShapeDtypeStruct((R, Dw), table.dtype),
        mesh=mesh,
        scratch_types=[
            pltpu.VMEM((n_chunks, ch), jnp.int32),
            pltpu.VMEM((2, ch, Dw), table.dtype),
            pltpu.SemaphoreType.DMA((2,)),
            pltpu.SemaphoreType.DMA((2,)),
        ],
        name="sc_gather_rows",
    )(table, idx3)


def _route(logits, b_group, b_router):
    T = logits.shape[0]
    G, E = N_GROUPS, EXPERTS_PER_GROUP
    gl = logits[:, :G] + b_group.astype(F32)
    gp = jax.nn.softmax(gl, axis=-1)
    gsel = jnp.argmax(gl, axis=-1)
    gw = jnp.take_along_axis(gp, gsel[:, None], axis=1)[:, 0]
    el = logits[:, G:G + G * E].reshape(T, G, E) + b_router.astype(F32)
    el_sel = jnp.take_along_axis(el, gsel[:, None, None], axis=1)[:, 0, :]
    top_v, top_i = lax.top_k(el_sel, TOP_K_INNER)
    top_w = jax.nn.softmax(top_v, axis=-1)
    eid = (gsel[:, None] * E + top_i).astype(jnp.int32)
    cw = gw[:, None] * top_w
    return eid, cw


def _dispatch_plan(eid, tm, n_tiles):
    T2 = eid.size
    flat_e = eid.reshape(-1)
    onehot = (flat_e[:, None] == jnp.arange(N_EXPERTS, dtype=jnp.int32)[None, :]).astype(jnp.int32)
    csum = jnp.cumsum(onehot, axis=0)
    rank = jnp.sum((csum - onehot) * onehot, axis=1)
    counts = csum[-1]
    tiles_per_e = (counts + tm - 1) // tm
    tile_end = jnp.cumsum(tiles_per_e)
    row_start = (tile_end - tiles_per_e) * tm
    pos = (row_start[flat_e] + rank).astype(jnp.int32)
    n_valid = tile_end[-1].astype(jnp.int32)
    tile_ids = jnp.arange(n_tiles, dtype=jnp.int32)
    tile_expert = jnp.searchsorted(tile_end, jnp.minimum(tile_ids, n_valid - 1), side="right")
    tile_expert = jnp.minimum(tile_expert, N_EXPERTS - 1).astype(jnp.int32)
    valid = tile_ids < n_valid
    prev_expert = jnp.concatenate([jnp.full((1,), -1, jnp.int32), tile_expert[:-1]])
    tile_first = jnp.logical_and(valid, tile_expert != prev_expert).astype(jnp.int32)
    tile_slot = ((jnp.cumsum(tile_first) - 1) % 2).astype(jnp.int32)
    next_start = tile_end[tile_expert]
    tile_next = jnp.where(next_start < n_valid, tile_expert[jnp.minimum(next_start, n_tiles - 1)],
                          -1).astype(jnp.int32)
    n_rows = n_tiles * tm
    src_token = (jnp.arange(n_rows, dtype=jnp.int32) % (T2 // TOP_K_INNER)).at[pos].set(
        jnp.arange(T2, dtype=jnp.int32) // TOP_K_INNER)
    tiles = (tile_expert, n_valid.reshape(1), tile_first, tile_slot, tile_next)
    return pos.reshape(eid.shape), src_token, tiles


def kernel(x, rel_bias, ln_mix_g, w_in, conv_w, conv_b, b_i, b_f, lam_q1, lam_k1, lam_q2, lam_k2,
           diff_norm_g, mlstm_norm_g, w_out, ln_ffn_g, w_group, b_group, w_router, b_router,
           w_gate, w_up, w_down, ln_f_g):
    B, S, D = x.shape
    T = B * S
    depth = w_in.shape[0]
    assert depth == 1, "the final rmsnorm is fused into the single layer's combine kernel"
    Hm = N_MLSTM_HEADS
    n_main = w_in.shape[2] - 2 * Hm
    n_diff = N_DIFF_HEADS * 2 * DIFF_HEAD_DIM
    xf = x.reshape(T, D)

    for l in range(depth):
        lambda_init = 0.8 - 0.6 * math.exp(-0.3 * l)
        w_main = w_in[l, :, :n_main].astype(BF16)
        wgt = w_in[l, :, n_main:]
        w_gates = jnp.zeros((D, 2 * LANES), F32).at[:, :Hm].set(wgt[:, :Hm]).at[:, LANES:LANES + Hm].set(
            wgt[:, Hm:]).astype(BF16)
        bi_row = jnp.zeros((1, LANES), F32).at[0, :Hm].set(b_i[l].astype(F32))
        bf_row = jnp.zeros((1, LANES), F32).at[0, :Hm].set(b_f[l].astype(F32))
        lam = (jnp.exp(jnp.sum(lam_q1[l].astype(F32) * lam_k1[l].astype(F32)))
               - jnp.exp(jnp.sum(lam_q2[l].astype(F32) * lam_k2[l].astype(F32))) + lambda_init)
        tq = ATT_TQ
        assert tq >= MAX_DISTANCE and tq % CHUNK == 0
        ii = jnp.arange(tq, dtype=jnp.int32)
        rel0 = ii[:, None] - ii[None, :]
        rb = rel_bias.astype(F32)

        def bias_of(bucket):
            out = jnp.zeros((N_DIFF_HEADS,) + bucket.shape, F32)
            for bkt in range(N_BUCKETS):
                out = jnp.where(bucket[None] == bkt, rb[bkt][:, None, None], out)
            return out

        log2e = math.log2(math.e)
        bias_prev = bias_of(_t5_bucket(rel0 - tq)) * log2e
        bias_diag = jnp.where(((ii[:, None] // CHUNK) <= (ii[None, :] // CHUNK))[None],
                              bias_of(_t5_bucket(rel0)) * log2e, NEG_BIG)
        bias_tiles = jnp.stack([bias_prev, bias_diag], axis=1)
        cfar = rb[N_BUCKETS // 2 - 1] * log2e
        scal = jnp.concatenate([lam.reshape(1), cfar]).astype(F32)
        col_scale = jnp.ones((1, n_main), F32).at[:, :n_diff].set(DIFF_HEAD_DIM ** -0.5 * log2e)

        proj, gates = _proj(xf, ln_mix_g[l].reshape(1, D).astype(F32), w_main, col_scale, w_gates)
        proj3 = proj.reshape(B, S, n_main)
        a = _diff_attention(proj3, scal, bias_tiles, diff_norm_g[l].reshape(1, n_diff).astype(F32),
                            lambda_init)
        hm = _mlstm(proj3, gates.reshape(B, S, 2 * LANES), conv_w[l].astype(F32),
                    conv_b[l].reshape(1, -1).astype(F32), bi_row, bf_row,
                    mlstm_norm_g[l].reshape(1, -1).astype(F32))

        wo = w_out[l].astype(BF16)
        G, E = N_GROUPS, EXPERTS_PER_GROUP
        wr = jnp.zeros((D, LANES), F32).at[:, :G].set(w_group[l].astype(F32)).at[:, G:G + G * E].set(
            jnp.transpose(w_router[l].astype(F32), (1, 0, 2)).reshape(D, G * E)).astype(BF16)
        x1, h2, logits = _out_proj(xf, a.reshape(T, n_diff), hm.reshape(T, -1), wo[:n_diff], wo[n_diff:],
                                   ln_ffn_g[l].reshape(1, D).astype(F32), wr)

        eid, cw = _route(logits, b_group[l], b_router[l])
        n_tiles = (T * TOP_K_INNER) // MOE_TM + N_EXPERTS
        pos, src_token, tiles = _dispatch_plan(eid, MOE_TM, n_tiles)
        xs = _sc_gather_rows(h2, src_token)
        Fe = w_gate.shape[-1]
        ys = _moe(*tiles, xs, w_gate[l].reshape(N_EXPERTS, D, Fe),
                  w_up[l].reshape(N_EXPERTS, D, Fe), w_down[l].reshape(N_EXPERTS, Fe, D))
        yw = _sc_gather_rows(ys, pos.T.reshape(-1))
        cwp = jnp.zeros((T, LANES), F32).at[:, :TOP_K_INNER].set(cw)
        xf = _final(x1, yw, cwp, ln_f_g.reshape(1, D).astype(F32))
    return xf.reshape(B, S, D)
```

```python
import functools
import math

import jax
import jax.numpy as jnp
from jax import lax
from jax.experimental import pallas as pl
from jax.experimental.pallas import tpu as pltpu
from jax.experimental.pallas import tpu_sc as plsc

F32 = jnp.float32
BF16 = jnp.bfloat16

EPS = 1e-6
CHUNK = 64
DIFF_HEAD_DIM = 64
N_DIFF_HEADS = 8
MLSTM_HEAD_DIM = 128
N_MLSTM_HEADS = 8
CONV_WIDTH = 4
N_BUCKETS = 32
MAX_DISTANCE = 128
N_GROUPS = 4
EXPERTS_PER_GROUP = 8
N_EXPERTS = N_GROUPS * EXPERTS_PER_GROUP
TOP_K_INNER = 2
LANES = 128
NEG_BIG = -1e30

VMEM_LIMIT = 56 * 1024 * 1024

PROJ_TM, PROJ_TN = 1024, 1024
ATT_TQ = 512
MLSTM_TS = 512
OUT_TM = 512
MOE_TM = 256
FIN_TM = 512


def _cparams(sem):
    return pltpu.CompilerParams(dimension_semantics=sem, vmem_limit_bytes=VMEM_LIMIT)


_HI_MASK = 0xFFFF0000


def _pack_bf16_pairs(x):
    half = x.shape[-1] // 2
    xb = x.astype(BF16).astype(F32)
    lo = pltpu.bitcast(xb[:, :half], jnp.uint32)
    hi = pltpu.bitcast(xb[:, half:], jnp.uint32)
    return (hi & jnp.uint32(_HI_MASK)) | (lo >> 16)


def _unpack_bf16_pairs(w):
    lo = pltpu.bitcast(w << 16, F32)
    hi = pltpu.bitcast(w & jnp.uint32(_HI_MASK), F32)
    return lo, hi


def _proj_kernel(x_ref, g_ref, w_ref, cs_ref, wg_ref, o_ref, og_ref, h_ref):
    @pl.when(pl.program_id(1) == 0)
    def _():
        x = x_ref[...]
        ms = jnp.mean(x * x, axis=-1, keepdims=True)
        h = (x * lax.rsqrt(ms + EPS) * g_ref[...]).astype(BF16)
        h_ref[...] = h
        og_ref[...] = jnp.dot(h, wg_ref[...], preferred_element_type=F32)

    o_ref[...] = (jnp.dot(h_ref[...], w_ref[...], preferred_element_type=F32) * cs_ref[...]).astype(o_ref.dtype)


def _proj(x2, g, w_main, col_scale, w_gates):
    T, D = x2.shape
    N = w_main.shape[1]
    NG = w_gates.shape[1]
    return pl.pallas_call(
        _proj_kernel,
        grid=(T // PROJ_TM, N // PROJ_TN),
        in_specs=[
            pl.BlockSpec((PROJ_TM, D), lambda m, n: (m, 0)),
            pl.BlockSpec((1, D), lambda m, n: (0, 0)),
            pl.BlockSpec((D, PROJ_TN), lambda m, n: (0, n)),
            pl.BlockSpec((1, PROJ_TN), lambda m, n: (0, n)),
            pl.BlockSpec((D, NG), lambda m, n: (0, 0)),
        ],
        out_specs=[
            pl.BlockSpec((PROJ_TM, PROJ_TN), lambda m, n: (m, n)),
            pl.BlockSpec((PROJ_TM, NG), lambda m, n: (m, 0)),
        ],
        out_shape=[
            jax.ShapeDtypeStruct((T, N), BF16),
            jax.ShapeDtypeStruct((T, NG), F32),
        ],
        scratch_shapes=[pltpu.VMEM((PROJ_TM, D), BF16)],
        compiler_params=_cparams(("parallel", "arbitrary")),
        name="rms_in_proj",
    )(x2, g, w_main, col_scale, w_gates)


def _t5_bucket(rel):
    half = N_BUCKETS // 2
    max_exact = half // 2
    ret = jnp.where(rel > 0, half, 0)
    n = jnp.abs(rel)
    nf = jnp.maximum(n, 1).astype(F32)
    large = max_exact + (jnp.log(nf / max_exact) / math.log(MAX_DISTANCE / max_exact)
                         * (half - max_exact)).astype(jnp.int32)
    large = jnp.minimum(large, half - 1)
    return ret + jnp.where(n < max_exact, n, large)


def _attn_kernel(scal_ref, q_ref, k_ref, v_ref, bias_ref, g_ref, o_ref, m_ref, l_ref, acc_ref,
                 s0_ref, s1_ref, ml0_ref, ml1_ref, bt_ref, *, lambda_init):
    h = pl.program_id(1)
    qi = pl.program_id(2)
    tq = ATT_TQ
    lam = scal_ref[0]
    cfar = scal_ref[1 + h]

    q = q_ref[0]
    lane = lax.broadcasted_iota(jnp.int32, q.shape, 1)
    zero = jnp.zeros_like(q)
    qs = jnp.concatenate([jnp.where(lane < DIFF_HEAD_DIM, q, zero),
                          jnp.where(lane >= DIFF_HEAD_DIM, q, zero)], axis=0)

    m_ref[...] = jnp.full(m_ref.shape, NEG_BIG, F32)
    l_ref[...] = jnp.zeros(l_ref.shape, F32)
    acc_ref[...] = jnp.zeros(acc_ref.shape, F32)

    @pl.when(qi == 0)
    def _():
        kj = lax.broadcasted_iota(jnp.int32, (tq, tq), 0)
        qq = lax.broadcasted_iota(jnp.int32, (tq, tq), 1)
        allowed = (kj // CHUNK) <= (qq // CHUNK)
        for d in range(2):
            rows = jnp.broadcast_to(bias_ref[0, d], (tq, 2 * tq))
            tile = pltpu.roll(rows, tq + 1, 1, stride=1, stride_axis=0)[:, :tq]
            if d == 1:
                tile = jnp.where(allowed, tile, NEG_BIG)
            bt_ref[d] = tile

    bufs = ((s0_ref, ml0_ref), (s1_ref, ml1_ref))

    def score(ki, bias, slot):
        s_ref, ml_ref = bufs[slot]
        start = pl.multiple_of(ki * tq, tq)
        kt = k_ref[0, pl.ds(start, tq), :]
        s = lax.dot_general(kt, qs, (((1,), (1,)), ((), ())), preferred_element_type=F32)
        if bias is not None:
            s = s + jnp.concatenate([bias, bias], axis=1)
        s_ref[...] = s
        ml_ref[...] = jnp.max(s, axis=0, keepdims=True)

    def accumulate(ki, shift, slot):
        s_ref, ml_ref = bufs[slot]
        start = pl.multiple_of(ki * tq, tq)
        vt = v_ref[0, pl.ds(start, tq), :]
        m_old = m_ref[...]
        m_new = jnp.maximum(m_old, ml_ref[...] + shift)
        alpha = jnp.exp2(m_old - m_new)
        p = jnp.exp2(s_ref[...] - (m_new - shift))
        l_ref[...] = alpha * l_ref[...] + jnp.sum(p, axis=0, keepdims=True)
        pv = lax.dot_general(vt, p.astype(BF16), (((0,), (0,)), ((), ())), preferred_element_type=F32)
        acc_ref[...] = alpha * acc_ref[...] + pv
        m_ref[...] = m_new

    n_far = qi - 1
    score(qi, bt_ref[1], 0)

    @pl.when(qi == 0)
    def _():
        accumulate(qi, 0.0, 0)

    @pl.when(qi >= 1)
    def _():
        accumulate(qi, 0.0, 0)
        score(qi - 1, bt_ref[0], 1)

    @pl.when(qi == 1)
    def _():
        accumulate(qi - 1, 0.0, 1)

    @pl.when(qi >= 2)
    def _():
        accumulate(qi - 1, 0.0, 1)
        score(0, None, 0)
        trips = (n_far - 1) // 2

        def pair(j, c):
            accumulate(2 * j, cfar, 0)
            score(2 * j + 1, None, 1)
            accumulate(2 * j + 1, cfar, 1)
            score(2 * j + 2, None, 0)
            return c

        lax.fori_loop(0, trips, pair, 0)
        last = 2 * trips

        @pl.when(n_far - last == 2)
        def _():
            accumulate(last, cfar, 0)
            score(last + 1, None, 1)
            accumulate(last + 1, cfar, 1)

        @pl.when(n_far - last == 1)
        def _():
            accumulate(last, cfar, 0)

    acc = acc_ref[...] * (1.0 / l_ref[...])
    o_t = acc[:, 0:tq] - lam * acc[:, tq:2 * tq]
    ms = jnp.mean(o_t * o_t, axis=0, keepdims=True)
    y = (o_t * lax.rsqrt(ms + EPS)).T * (g_ref[...] * (1.0 - lambda_init))
    o_ref[0] = y.astype(o_ref.dtype)


def _diff_attention(proj3, scal, bias_vecs, gnorm, lambda_init):
    B, S, _ = proj3.shape
    H = N_DIFF_HEADS
    tq = ATT_TQ
    kern = functools.partial(_attn_kernel, lambda_init=lambda_init)
    return pl.pallas_call(
        kern,
        grid=(B, H, S // tq),
        in_specs=[
            pl.BlockSpec(memory_space=pltpu.SMEM),
            pl.BlockSpec((1, tq, LANES), lambda b, h, i: (b, i, h)),
            pl.BlockSpec((1, S, LANES), lambda b, h, i: (b, 0, H + h)),
            pl.BlockSpec((1, S, LANES), lambda b, h, i: (b, 0, 2 * H + h)),
            pl.BlockSpec((1, 2, 1, 2 * tq), lambda b, h, i: (h, 0, 0, 0)),
            pl.BlockSpec((1, LANES), lambda b, h, i: (0, h)),
        ],
        out_specs=pl.BlockSpec((1, tq, LANES), lambda b, h, i: (b, i, h)),
        out_shape=jax.ShapeDtypeStruct((B, S, H * LANES), BF16),
        scratch_shapes=[
            pltpu.VMEM((1, 2 * tq), F32),
            pltpu.VMEM((1, 2 * tq), F32),
            pltpu.VMEM((LANES, 2 * tq), F32),
            pltpu.VMEM((tq, 2 * tq), F32),
            pltpu.VMEM((tq, 2 * tq), F32),
            pltpu.VMEM((1, 2 * tq), F32),
            pltpu.VMEM((1, 2 * tq), F32),
            pltpu.VMEM((2, tq, tq), F32),
        ],
        compiler_params=_cparams(("parallel", "parallel", "arbitrary")),
        name="diff_attention",
    )(scal, proj3, proj3, proj3, bias_vecs, gnorm)


def _log_sigmoid(x):
    return jnp.minimum(x, 0.0) - jnp.log(1.0 + jnp.exp(-jnp.abs(x)))


def _sigmoid(x):
    return 1.0 / (1.0 + jnp.exp(-x))


def _mlstm_kernel(q_ref, k_ref, v_ref, o_ref, gi_ref, gf_ref, cw_ref, cb_ref, bi_ref, bf_ref, gn_ref,
                  out_ref, qext_ref, kext_ref, ct_ref, n_ref, m_ref):
    sb = pl.program_id(1)
    L = CHUNK
    dh = MLSTM_HEAD_DIM
    H = N_MLSTM_HEADS
    ts = MLSTM_TS
    pad = 8

    @pl.when(sb == 0)
    def _():
        qext_ref[0:pad, :] = jnp.zeros((pad, H * dh), F32)
        kext_ref[0:pad, :] = jnp.zeros((pad, H * dh), F32)
        ct_ref[...] = jnp.zeros(ct_ref.shape, F32)
        n_ref[...] = jnp.zeros(n_ref.shape, F32)
        m_ref[...] = jnp.zeros(m_ref.shape, F32)

    qext_ref[pad:pad + ts, :] = q_ref[0].astype(F32)
    kext_ref[pad:pad + ts, :] = k_ref[0].astype(F32)

    row = lax.broadcasted_iota(jnp.int32, (L, L), 0)
    col = lax.broadcasted_iota(jnp.int32, (L, L), 1)
    tril = col <= row
    ltri = tril.astype(F32)

    def conv_silu(ext_ref, base, h, off):
        win = ext_ref[pl.ds(base, L + pad), h * dh:(h + 1) * dh]
        w = cw_ref[:, off + h * dh:off + (h + 1) * dh]
        y = cb_ref[:, off + h * dh:off + (h + 1) * dh]
        for j in range(CONV_WIDTH):
            lo = pad - (CONV_WIDTH - 1) + j
            y = y + w[j:j + 1, :] * win[lo:lo + L, :]
        return y * _sigmoid(y)

    def chunk_body(c, carry):
        base = pl.multiple_of(c * L, L)
        li = gi_ref[0, pl.ds(base, L), :] + bi_ref[...]
        logf = _log_sigmoid(gf_ref[0, pl.ds(base, L), :] + bf_ref[...])
        b = jnp.dot(ltri, logf, preferred_element_type=F32, precision=lax.Precision.HIGHEST)
        a = li - b
        g_row = b[L - 1:L, :]
        m_row = m_ref[...]
        m_new_row = g_row + jnp.maximum(m_row, jnp.max(a, axis=0, keepdims=True))
        a_t = a.T

        for h in range(H):
            qc = conv_silu(qext_ref, base, h, 0)
            kc = conv_silu(kext_ref, base, h, H * dh) * (dh ** -0.5)
            qb = qc.astype(BF16)
            kb = kc.astype(BF16)
            vb = v_ref[0, pl.ds(base, L), h * dh:(h + 1) * dh]

            a_row = a_t[h:h + 1, :]
            a_col = a[:, h:h + 1]
            b_col = b[:, h:h + 1]
            m_prev = m_row[:, h:h + 1]
            m_next = m_new_row[:, h:h + 1]
            g_h = g_row[:, h:h + 1]

            amat = jnp.where(tril, a_row, NEG_BIG)
            mcol = jnp.maximum(jnp.max(amat, axis=-1, keepdims=True), m_prev)
            wts = jnp.exp(amat - mcol)
            inter = jnp.exp(m_prev - mcol)

            s = lax.dot_general(qb, kb, (((1,), (1,)), ((), ())), preferred_element_type=F32)
            sqk = s * wts
            ct = ct_ref[h]
            nrow = n_ref[h:h + 1, :]
            num = (jnp.dot(sqk.astype(BF16), vb, preferred_element_type=F32)
                   + inter * jnp.dot(qb, ct.astype(BF16), preferred_element_type=F32))
            den = (jnp.sum(sqk, axis=-1, keepdims=True)
                   + inter * jnp.sum(qc * nrow, axis=-1, keepdims=True))
            hv = num / jnp.maximum(jnp.abs(den), jnp.exp(-(b_col + mcol)))

            wt = jnp.exp(g_h + a_col - m_next)
            decay = jnp.exp(g_h + m_prev - m_next)
            wk = wt * kc
            ct_ref[h] = decay * ct + lax.dot_general(wk.astype(BF16), vb, (((0,), (0,)), ((), ())),
                                                     preferred_element_type=F32)
            n_ref[h:h + 1, :] = decay * nrow + jnp.sum(wk, axis=0, keepdims=True)

            ms = jnp.mean(hv * hv, axis=-1, keepdims=True)
            y = hv * lax.rsqrt(ms + EPS) * gn_ref[:, h * dh:(h + 1) * dh]
            og = o_ref[0, pl.ds(base, L), h * dh:(h + 1) * dh].astype(F32)
            out_ref[0, pl.ds(base, L), h * dh:(h + 1) * dh] = (y * _sigmoid(og)).astype(out_ref.dtype)

        m_ref[...] = m_new_row
        return carry

    lax.fori_loop(0, ts // L, chunk_body, 0)

    qext_ref[0:pad, :] = qext_ref[ts:ts + pad, :]
    kext_ref[0:pad, :] = kext_ref[ts:ts + pad, :]


def _mlstm(proj3, gates3, conv_w, conv_b, bi_row, bf_row, gnorm):
    B, S, _ = proj3.shape
    W = N_MLSTM_HEADS * MLSTM_HEAD_DIM
    ts = MLSTM_TS
    first = 3
    blk = lambda j: pl.BlockSpec((1, ts, W), lambda b, s: (b, s, j))
    full = lambda shape: pl.BlockSpec(shape, lambda b, s: (0,) * len(shape))
    return pl.pallas_call(
        _mlstm_kernel,
        grid=(B, S // ts),
        in_specs=[
            blk(first), blk(first + 1), blk(first + 2), blk(first + 3),
            pl.BlockSpec((1, ts, LANES), lambda b, s: (b, s, 0)),
            pl.BlockSpec((1, ts, LANES), lambda b, s: (b, s, 1)),
            full((CONV_WIDTH, 2 * W)), full((1, 2 * W)),
            full((1, LANES)), full((1, LANES)), full((1, W)),
        ],
        out_specs=pl.BlockSpec((1, ts, W), lambda b, s: (b, s, 0)),
        out_shape=jax.ShapeDtypeStruct((B, S, W), BF16),
        scratch_shapes=[
            pltpu.VMEM((ts + 8, W), F32),
            pltpu.VMEM((ts + 8, W), F32),
            pltpu.VMEM((N_MLSTM_HEADS, MLSTM_HEAD_DIM, MLSTM_HEAD_DIM), F32),
            pltpu.VMEM((N_MLSTM_HEADS, MLSTM_HEAD_DIM), F32),
            pltpu.VMEM((1, LANES), F32),
        ],
        compiler_params=_cparams(("parallel", "arbitrary")),
        name="mlstm",
    )(proj3, proj3, proj3, proj3, gates3, gates3, conv_w, conv_b, bi_row, bf_row, gnorm)


def _out_kernel(x_ref, a_ref, hm_ref, wa_ref, wm_ref, g_ref, wr_ref, x1_ref, h2_ref, lg_ref):
    y = (jnp.dot(a_ref[...], wa_ref[...], preferred_element_type=F32)
         + jnp.dot(hm_ref[...], wm_ref[...], preferred_element_type=F32))
    x1 = x_ref[...] + y
    x1_ref[...] = x1
    ms = jnp.mean(x1 * x1, axis=-1, keepdims=True)
    h2 = x1 * lax.rsqrt(ms + EPS) * g_ref[...]
    h2_ref[...] = _pack_bf16_pairs(h2)
    lg_ref[...] = jnp.dot(h2.astype(BF16), wr_ref[...], preferred_element_type=F32)


def _out_proj(x2, a2, hm2, wa, wm, g, wr):
    T, D = x2.shape
    W = a2.shape[1]
    tm = OUT_TM
    const = lambda shape: pl.BlockSpec(shape, lambda m: (0, 0), pipeline_mode=pl.Buffered(1))
    return pl.pallas_call(
        _out_kernel,
        grid=(T // tm,),
        in_specs=[
            pl.BlockSpec((tm, D), lambda m: (m, 0)),
            pl.BlockSpec((tm, W), lambda m: (m, 0)),
            pl.BlockSpec((tm, W), lambda m: (m, 0)),
            const((W, D)), const((W, D)), const((1, D)), const((D, LANES)),
        ],
        out_specs=[
            pl.BlockSpec((tm, D), lambda m: (m, 0)),
            pl.BlockSpec((tm, D // 2), lambda m: (m, 0)),
            pl.BlockSpec((tm, LANES), lambda m: (m, 0)),
        ],
        out_shape=[
            jax.ShapeDtypeStruct((T, D), F32),
            jax.ShapeDtypeStruct((T, D // 2), jnp.uint32),
            jax.ShapeDtypeStruct((T, LANES), F32),
        ],
        compiler_params=_cparams(("parallel",)),
        name="out_proj_router",
    )(x2, a2, hm2, wa, wm, g, wr)


def _moe_kernel(te_ref, nv_ref, first_ref, slot_ref, nxt_ref, xs_ref, wg_hbm, wu_hbm, wd_hbm, ys_ref,
                wg_buf, wu_buf, wd_buf, sem):
    j = pl.program_id(0)
    valid = j < nv_ref[0]

    half_f = wd_buf.shape[1] // 2

    def weight_copies(e, s):
        lo, hi = pl.ds(0, half_f), pl.ds(half_f, half_f)
        return ((pltpu.make_async_copy(wg_hbm.at[e], wg_buf.at[s], sem.at[s, 0]), 0),
                (pltpu.make_async_copy(wu_hbm.at[e], wu_buf.at[s], sem.at[s, 1]), 1),
                (pltpu.make_async_copy(wd_hbm.at[e, lo], wd_buf.at[s, lo], sem.at[s, 2]), 0),
                (pltpu.make_async_copy(wd_hbm.at[e, hi], wd_buf.at[s, hi], sem.at[s, 3]), 1))

    @pl.when(j == 0)
    def _():
        for c, prio in weight_copies(te_ref[0], 0):
            c.start(priority=prio)

    @pl.when(jnp.logical_and(valid, first_ref[j] == 1))
    def _():
        for c, _ in weight_copies(te_ref[j], slot_ref[j]):
            c.wait()

        @pl.when(nxt_ref[j] >= 0)
        def _():
            for c, prio in weight_copies(nxt_ref[j], 1 - slot_ref[j]):
                c.start(priority=prio)

    @pl.when(valid)
    def _():
        s = slot_ref[j]
        lo, hi = _unpack_bf16_pairs(xs_ref[...])
        xs = jnp.concatenate([lo.astype(BF16), hi.astype(BF16)], axis=1)
        gt = jnp.dot(xs, wg_buf[s].astype(BF16), preferred_element_type=F32)
        up = jnp.dot(xs, wu_buf[s].astype(BF16), preferred_element_type=F32)
        hid = (gt * _sigmoid(gt) * up).astype(BF16)
        ys_ref[...] = _pack_bf16_pairs(jnp.dot(hid, wd_buf[s].astype(BF16), preferred_element_type=F32))

    @pl.when(jnp.logical_not(valid))
    def _():
        ys_ref[...] = jnp.zeros(ys_ref.shape, ys_ref.dtype)


def _moe(tile_expert, n_valid, tile_first, tile_slot, tile_next, xs, wg, wu, wd):
    R, Dw = xs.shape
    D, F = wg.shape[1], wg.shape[2]
    tm = MOE_TM
    hbm = pl.BlockSpec(memory_space=pl.ANY)
    grid_spec = pltpu.PrefetchScalarGridSpec(
        num_scalar_prefetch=5,
        grid=(R // tm,),
        in_specs=[pl.BlockSpec((tm, Dw), lambda j, *_: (j, 0)), hbm, hbm, hbm],
        out_specs=pl.BlockSpec((tm, Dw), lambda j, *_: (j, 0)),
        scratch_shapes=[
            pltpu.VMEM((2, D, F), wg.dtype),
            pltpu.VMEM((2, D, F), wu.dtype),
            pltpu.VMEM((2, F, D), wd.dtype),
            pltpu.SemaphoreType.DMA((2, 4)),
        ],
    )
    return pl.pallas_call(
        _moe_kernel,
        grid_spec=grid_spec,
        out_shape=jax.ShapeDtypeStruct((R, Dw), jnp.uint32),
        compiler_params=_cparams(("arbitrary",)),
        name="moe_experts",
    )(tile_expert, n_valid, tile_first, tile_slot, tile_next, xs, wg, wu, wd)


def _final_kernel(x1_ref, y0_ref, y1_ref, cw_ref, g_ref, o_ref):
    cw = cw_ref[...]
    lo0, hi0 = _unpack_bf16_pairs(y0_ref[...])
    lo1, hi1 = _unpack_bf16_pairs(y1_ref[...])
    w0, w1 = cw[:, 0:1], cw[:, 1:2]
    y = jnp.concatenate([w0 * lo0 + w1 * lo1, w0 * hi0 + w1 * hi1], axis=1)
    x = x1_ref[...] + y
    ms = jnp.mean(x * x, axis=-1, keepdims=True)
    o_ref[...] = x * lax.rsqrt(ms + EPS) * g_ref[...]


def _final(x1, yw, cw, g):
    T, D = x1.shape
    tm = FIN_TM
    row = lambda w: pl.BlockSpec((tm, w), lambda m: (m, 0))
    slot1 = pl.BlockSpec((tm, D // 2), lambda m: (m + T // tm, 0))
    return pl.pallas_call(
        _final_kernel,
        grid=(T // tm,),
        in_specs=[row(D), row(D // 2), slot1, row(LANES), pl.BlockSpec((1, D), lambda m: (0, 0))],
        out_specs=row(D),
        out_shape=jax.ShapeDtypeStruct((T, D), F32),
        compiler_params=_cparams(("parallel",)),
        name="combine_final_norm",
    )(x1, yw, yw, cw, g)


SC_CORES, SC_SUBCORES = 2, 16
SC_CHUNK = 32


def _sc_gather_rows(table, idx):
    V, Dw = table.shape
    R = idx.shape[0]
    n_workers = SC_CORES * SC_SUBCORES
    ch = SC_CHUNK
    per_w = R // n_workers
    n_chunks = per_w // ch
    assert per_w * n_workers == R and n_chunks * ch == per_w and n_chunks % 2 == 0
    idx3 = idx.reshape(n_workers, n_chunks, ch)
    mesh = plsc.VectorSubcoreMesh(core_axis_name="c", subcore_axis_name="s")

    def body(table_hbm, idx_hbm, out_hbm, idx_v, rows_v, gsem, osem):
        wid = lax.axis_index("s") * SC_CORES + lax.axis_index("c")
        base = wid * per_w
        pltpu.sync_copy(idx_hbm.at[wid], idx_v)

        def gather(c, slot):
            return pltpu.make_async_copy(table_hbm.at[idx_v.at[c]], rows_v.at[slot], gsem.at[slot])

        def put(c, slot):
            return pltpu.make_async_copy(rows_v.at[slot], out_hbm.at[pl.ds(base + c * ch, ch)],
                                         osem.at[slot])

        gather(0, 0).start()

        @pl.loop(0, n_chunks, step=2)
        def _(c):
            @pl.when(c > 0)
            def _():
                put(c - 1, 1).wait()

            gather(c + 1, 1).start()
            gather(c, 0).wait()
            put(c, 0).start()
            put(c, 0).wait()

            @pl.when(c + 2 < n_chunks)
            def _():
                gather(c + 2, 0).start()

            gather(c + 1, 1).wait()
            put(c + 1, 1).start()

        put(n_chunks - 1, 1).wait()

    return pl.kernel(
        body,
        out_type=jax.ShapeDtypeStruct((R, Dw), table.dtype),
        mesh=mesh,
        scratch_types=[
            pltpu.VMEM((n_chunks, ch), jnp.int32),
            pltpu.VMEM((2, ch, Dw), table.dtype),
            pltpu.SemaphoreType.DMA((2,)),
            pltpu.SemaphoreType.DMA((2,)),
        ],
        name="sc_gather_rows",
    )(table, idx3)


def _route(logits, b_group, b_router):
    T = logits.shape[0]
    G, E = N_GROUPS, EXPERTS_PER_GROUP
    gl = logits[:, :G] + b_group.astype(F32)
    gp = jax.nn.softmax(gl, axis=-1)
    gsel = jnp.argmax(gl, axis=-1)
    gw = jnp.take_along_axis(gp, gsel[:, None], axis=1)[:, 0]
    el = logits[:, G:G + G * E].reshape(T, G, E) + b_router.astype(F32)
    el_sel = jnp.take_along_axis(el, gsel[:, None, None], axis=1)[:, 0, :]
    top_v, top_i = lax.top_k(el_sel, TOP_K_INNER)
    top_w = jax.nn.softmax(top_v, axis=-1)
    eid = (gsel[:, None] * E + top_i).astype(jnp.int32)
    cw = gw[:, None] * top_w
    return eid, cw


def _dispatch_plan(eid, tm, n_tiles):
    T2 = eid.size
    flat_e = eid.reshape(-1)
    onehot = (flat_e[:, None] == jnp.arange(N_EXPERTS, dtype=jnp.int32)[None, :]).astype(jnp.int32)
    csum = jnp.cumsum(onehot, axis=0)
    rank = jnp.sum((csum - onehot) * onehot, axis=1)
    counts = csum[-1]
    tiles_per_e = (counts + tm - 1) // tm
    tile_end = jnp.cumsum(tiles_per_e)
    row_start = (tile_end - tiles_per_e) * tm
    pos = (row_start[flat_e] + rank).astype(jnp.int32)
    n_valid = tile_end[-1].astype(jnp.int32)
    tile_ids = jnp.arange(n_tiles, dtype=jnp.int32)
    tile_expert = jnp.searchsorted(tile_end, jnp.minimum(tile_ids, n_valid - 1), side="right")
    tile_expert = jnp.minimum(tile_expert, N_EXPERTS - 1).astype(jnp.int32)
    valid = tile_ids < n_valid
    prev_expert = jnp.concatenate([jnp.full((1,), -1, jnp.int32), tile_expert[:-1]])
    tile_first = jnp.logical_and(valid, tile_expert != prev_expert).astype(jnp.int32)
    tile_slot = ((jnp.cumsum(tile_first) - 1) % 2).astype(jnp.int32)
    next_start = tile_end[tile_expert]
    tile_next = jnp.where(next_start < n_valid, tile_expert[jnp.minimum(next_start, n_tiles - 1)],
                          -1).astype(jnp.int32)
    n_rows = n_tiles * tm
    src_token = (jnp.arange(n_rows, dtype=jnp.int32) % (T2 // TOP_K_INNER)).at[pos].set(
        jnp.arange(T2, dtype=jnp.int32) // TOP_K_INNER)
    tiles = (tile_expert, n_valid.reshape(1), tile_first, tile_slot, tile_next)
    return pos.reshape(eid.shape), src_token, tiles


def kernel(x, rel_bias, ln_mix_g, w_in, conv_w, conv_b, b_i, b_f, lam_q1, lam_k1, lam_q2, lam_k2,
           diff_norm_g, mlstm_norm_g, w_out, ln_ffn_g, w_group, b_group, w_router, b_router,
           w_gate, w_up, w_down, ln_f_g):
    B, S, D = x.shape
    T = B * S
    depth = w_in.shape[0]
    assert depth == 1, "the final rmsnorm is fused into the single layer's combine kernel"
    Hm = N_MLSTM_HEADS
    n_main = w_in.shape[2] - 2 * Hm
    n_diff = N_DIFF_HEADS * 2 * DIFF_HEAD_DIM
    xf = x.reshape(T, D)

    for l in range(depth):
        lambda_init = 0.8 - 0.6 * math.exp(-0.3 * l)
        w_main = w_in[l, :, :n_main].astype(BF16)
        wgt = w_in[l, :, n_main:]
        w_gates = jnp.zeros((D, 2 * LANES), F32).at[:, :Hm].set(wgt[:, :Hm]).at[:, LANES:LANES + Hm].set(
            wgt[:, Hm:]).astype(BF16)
        bi_row = jnp.zeros((1, LANES), F32).at[0, :Hm].set(b_i[l].astype(F32))
        bf_row = jnp.zeros((1, LANES), F32).at[0, :Hm].set(b_f[l].astype(F32))
        lam = (jnp.exp(jnp.sum(lam_q1[l].astype(F32) * lam_k1[l].astype(F32)))
               - jnp.exp(jnp.sum(lam_q2[l].astype(F32) * lam_k2[l].astype(F32))) + lambda_init)
        tq = ATT_TQ
        assert tq >= MAX_DISTANCE and tq % CHUNK == 0
        rb = rel_bias.astype(F32)
        log2e = math.log2(math.e)
        xx = jnp.arange(2 * tq, dtype=jnp.int32)
        rel_vec = jnp.stack([-tq + tq - 1 - xx, tq - 1 - xx], axis=0)
        bias_vecs = jnp.take(rb, _t5_bucket(rel_vec), axis=0) * log2e
        bias_vecs = jnp.transpose(bias_vecs, (2, 0, 1))[:, :, None, :]
        cfar = rb[N_BUCKETS // 2 - 1] * log2e
        scal = jnp.concatenate([lam.reshape(1), cfar]).astype(F32)
        col_scale = jnp.ones((1, n_main), F32).at[:, :n_diff].set(DIFF_HEAD_DIM ** -0.5 * log2e)

        proj, gates = _proj(xf, ln_mix_g[l].reshape(1, D).astype(F32), w_main, col_scale, w_gates)
        proj3 = proj.reshape(B, S, n_main)
        a = _diff_attention(proj3, scal, bias_vecs, diff_norm_g[l].reshape(1, n_diff).astype(F32),
                            lambda_init)
        hm = _mlstm(proj3, gates.reshape(B, S, 2 * LANES), conv_w[l].astype(F32),
                    conv_b[l].reshape(1, -1).astype(F32), bi_row, bf_row,
                    mlstm_norm_g[l].reshape(1, -1).astype(F32))

        wo = w_out[l].astype(BF16)
        G, E = N_GROUPS, EXPERTS_PER_GROUP
        wr = jnp.zeros((D, LANES), F32).at[:, :G].set(w_group[l].astype(F32)).at[:, G:G + G * E].set(
            jnp.transpose(w_router[l].astype(F32), (1, 0, 2)).reshape(D, G * E)).astype(BF16)
        x1, h2, logits = _out_proj(xf, a.reshape(T, n_diff), hm.reshape(T, -1), wo[:n_diff], wo[n_diff:],
                                   ln_ffn_g[l].reshape(1, D).astype(F32), wr)

        eid, cw = _route(logits, b_group[l], b_router[l])
        n_tiles = (T * TOP_K_INNER) // MOE_TM + N_EXPERTS
        pos, src_token, tiles = _dispatch_plan(eid, MOE_TM, n_tiles)
        xs = _sc_gather_rows(h2, src_token)
        Fe = w_gate.shape[-1]
        ys = _moe(*tiles, xs, w_gate[l].reshape(N_EXPERTS, D, Fe),
                  w_up[l].reshape(N_EXPERTS, D, Fe), w_down[l].reshape(N_EXPERTS, Fe, D))
        yw = _sc_gather_rows(ys, pos.T.reshape(-1))
        cwp = jnp.zeros((T, LANES), F32).at[:, :TOP_K_INNER].set(cw)
        xf = _final(x1, yw, cwp, ln_f_g.reshape(1, D).astype(F32))
    return xf.reshape(B, S, D)
```

```python
import functools
import math

import jax
import jax.numpy as jnp
from jax import lax
from jax.experimental import pallas as pl
from jax.experimental.pallas import tpu as pltpu
from jax.experimental.pallas import tpu_sc as plsc

F32 = jnp.float32
BF16 = jnp.bfloat16

EPS = 1e-6
CHUNK = 64
DIFF_HEAD_DIM = 64
N_DIFF_HEADS = 8
MLSTM_HEAD_DIM = 128
N_MLSTM_HEADS = 8
CONV_WIDTH = 4
N_BUCKETS = 32
MAX_DISTANCE = 128
N_GROUPS = 4
EXPERTS_PER_GROUP = 8
N_EXPERTS = N_GROUPS * EXPERTS_PER_GROUP
TOP_K_INNER = 2
LANES = 128
NEG_BIG = -1e30

VMEM_LIMIT = 56 * 1024 * 1024

PROJ_TM, PROJ_TN = 1024, 1024
ATT_TQ = 512
MLSTM_TS = 512
OUT_TM = 512
MOE_TM = 256
FIN_TM = 512


def _cparams(sem):
    return pltpu.CompilerParams(dimension_semantics=sem, vmem_limit_bytes=VMEM_LIMIT)


_HI_MASK = 0xFFFF0000


def _pack_bf16_pairs(x):
    half = x.shape[-1] // 2
    xb = x.astype(BF16).astype(F32)
    lo = pltpu.bitcast(xb[:, :half], jnp.uint32)
    hi = pltpu.bitcast(xb[:, half:], jnp.uint32)
    return (hi & jnp.uint32(_HI_MASK)) | (lo >> 16)


def _unpack_bf16_pairs(w):
    lo = pltpu.bitcast(w << 16, F32)
    hi = pltpu.bitcast(w & jnp.uint32(_HI_MASK), F32)
    return lo, hi


def _proj_kernel(x_ref, g_ref, w_ref, cs_ref, wg_ref, o_ref, og_ref, h_ref):
    @pl.when(pl.program_id(1) == 0)
    def _():
        x = x_ref[...]
        ms = jnp.mean(x * x, axis=-1, keepdims=True)
        h = (x * lax.rsqrt(ms + EPS) * g_ref[...]).astype(BF16)
        h_ref[...] = h
        og_ref[...] = jnp.dot(h, wg_ref[...], preferred_element_type=F32)

    o_ref[...] = (jnp.dot(h_ref[...], w_ref[...], preferred_element_type=F32) * cs_ref[...]).astype(o_ref.dtype)


def _proj(x2, g, w_main, col_scale, w_gates):
    T, D = x2.shape
    N = w_main.shape[1]
    NG = w_gates.shape[1]
    return pl.pallas_call(
        _proj_kernel,
        grid=(T // PROJ_TM, N // PROJ_TN),
        in_specs=[
            pl.BlockSpec((PROJ_TM, D), lambda m, n: (m, 0)),
            pl.BlockSpec((1, D), lambda m, n: (0, 0)),
            pl.BlockSpec((D, PROJ_TN), lambda m, n: (0, n)),
            pl.BlockSpec((1, PROJ_TN), lambda m, n: (0, n)),
            pl.BlockSpec((D, NG), lambda m, n: (0, 0)),
        ],
        out_specs=[
            pl.BlockSpec((PROJ_TM, PROJ_TN), lambda m, n: (m, n)),
            pl.BlockSpec((PROJ_TM, NG), lambda m, n: (m, 0)),
        ],
        out_shape=[
            jax.ShapeDtypeStruct((T, N), BF16),
            jax.ShapeDtypeStruct((T, NG), F32),
        ],
        scratch_shapes=[pltpu.VMEM((PROJ_TM, D), BF16)],
        compiler_params=_cparams(("parallel", "arbitrary")),
        name="rms_in_proj",
    )(x2, g, w_main, col_scale, w_gates)


def _t5_bucket(rel):
    half = N_BUCKETS // 2
    max_exact = half // 2
    ret = jnp.where(rel > 0, half, 0)
    n = jnp.abs(rel)
    nf = jnp.maximum(n, 1).astype(F32)
    large = max_exact + (jnp.log(nf / max_exact) / math.log(MAX_DISTANCE / max_exact)
                         * (half - max_exact)).astype(jnp.int32)
    large = jnp.minimum(large, half - 1)
    return ret + jnp.where(n < max_exact, n, large)


def _attn_kernel(scal_ref, q_ref, k_ref, v_ref, bias_ref, g_ref, o_ref, m_ref, l_ref, acc_ref,
                 s0_ref, s1_ref, ml0_ref, ml1_ref, bt_ref, *, lambda_init):
    h = pl.program_id(1)
    qi = pl.program_id(2)
    tq = ATT_TQ
    lam = scal_ref[0]
    cfar = scal_ref[1 + h]

    q = q_ref[0]
    lane = lax.broadcasted_iota(jnp.int32, q.shape, 1)
    zero = jnp.zeros_like(q)
    qs = jnp.concatenate([jnp.where(lane < DIFF_HEAD_DIM, q, zero),
                          jnp.where(lane >= DIFF_HEAD_DIM, q, zero)], axis=0)

    m_ref[...] = jnp.full(m_ref.shape, NEG_BIG, F32)
    l_ref[...] = jnp.zeros(l_ref.shape, F32)
    acc_ref[...] = jnp.zeros(acc_ref.shape, F32)

    @pl.when(qi == 0)
    def _():
        kj = lax.broadcasted_iota(jnp.int32, (tq, tq), 0)
        qq = lax.broadcasted_iota(jnp.int32, (tq, tq), 1)
        allowed = (kj // CHUNK) <= (qq // CHUNK)
        for d in range(2):
            rows = jnp.broadcast_to(bias_ref[0, d], (tq, 2 * tq))
            tile = pltpu.roll(rows, tq + 1, 1, stride=1, stride_axis=0)[:, :tq]
            if d == 1:
                tile = jnp.where(allowed, tile, NEG_BIG)
            bt_ref[d] = tile

    bufs = ((s0_ref, ml0_ref), (s1_ref, ml1_ref))

    def score(ki, bias, slot):
        s_ref, ml_ref = bufs[slot]
        start = pl.multiple_of(ki * tq, tq)
        kt = k_ref[0, pl.ds(start, tq), :]
        s = lax.dot_general(kt, qs, (((1,), (1,)), ((), ())), preferred_element_type=F32)
        if bias is not None:
            s = s + jnp.concatenate([bias, bias], axis=1)
        s_ref[...] = s
        ml_ref[...] = jnp.max(s, axis=0, keepdims=True)

    def accumulate(ki, shift, slot):
        s_ref, ml_ref = bufs[slot]
        start = pl.multiple_of(ki * tq, tq)
        vt = v_ref[0, pl.ds(start, tq), :]
        m_old = m_ref[...]
        m_new = jnp.maximum(m_old, ml_ref[...] + shift)
        alpha = jnp.exp2(m_old - m_new)
        p = jnp.exp2(s_ref[...] - (m_new - shift))
        l_ref[...] = alpha * l_ref[...] + jnp.sum(p, axis=0, keepdims=True)
        pv = lax.dot_general(vt, p.astype(BF16), (((0,), (0,)), ((), ())), preferred_element_type=F32)
        acc_ref[...] = alpha * acc_ref[...] + pv
        m_ref[...] = m_new

    n_far = qi - 1
    score(qi, bt_ref[1], 0)

    @pl.when(qi == 0)
    def _():
        accumulate(qi, 0.0, 0)

    @pl.when(qi >= 1)
    def _():
        accumulate(qi, 0.0, 0)
        score(qi - 1, bt_ref[0], 1)

    @pl.when(qi == 1)
    def _():
        accumulate(qi - 1, 0.0, 1)

    @pl.when(qi >= 2)
    def _():
        accumulate(qi - 1, 0.0, 1)
        score(0, None, 0)
        trips = (n_far - 1) // 2

        def pair(j, c):
            accumulate(2 * j, cfar, 0)
            score(2 * j + 1, None, 1)
            accumulate(2 * j + 1, cfar, 1)
            score(2 * j + 2, None, 0)
            return c

        lax.fori_loop(0, trips, pair, 0)
        last = 2 * trips

        @pl.when(n_far - last == 2)
        def _():
            accumulate(last, cfar, 0)
            score(last + 1, None, 1)
            accumulate(last + 1, cfar, 1)

        @pl.when(n_far - last == 1)
        def _():
            accumulate(last, cfar, 0)

    acc = acc_ref[...] * (1.0 / l_ref[...])
    o_t = acc[:, 0:tq] - lam * acc[:, tq:2 * tq]
    ms = jnp.mean(o_t * o_t, axis=0, keepdims=True)
    y = (o_t * lax.rsqrt(ms + EPS)).T * (g_ref[...] * (1.0 - lambda_init))
    o_ref[0] = y.astype(o_ref.dtype)


def _diff_attention(proj3, scal, bias_vecs, gnorm, lambda_init):
    B, S, _ = proj3.shape
    H = N_DIFF_HEADS
    tq = ATT_TQ
    kern = functools.partial(_attn_kernel, lambda_init=lambda_init)
    return pl.pallas_call(
        kern,
        grid=(B, H, S // tq),
        in_specs=[
            pl.BlockSpec(memory_space=pltpu.SMEM),
            pl.BlockSpec((1, tq, LANES), lambda b, h, i: (b, i, h)),
            pl.BlockSpec((1, S, LANES), lambda b, h, i: (b, 0, H + h)),
            pl.BlockSpec((1, S, LANES), lambda b, h, i: (b, 0, 2 * H + h)),
            pl.BlockSpec((1, 2, 1, 2 * tq), lambda b, h, i: (h, 0, 0, 0)),
            pl.BlockSpec((1, LANES), lambda b, h, i: (0, h)),
        ],
        out_specs=pl.BlockSpec((1, tq, LANES), lambda b, h, i: (b, i, h)),
        out_shape=jax.ShapeDtypeStruct((B, S, H * LANES), BF16),
        scratch_shapes=[
            pltpu.VMEM((1, 2 * tq), F32),
            pltpu.VMEM((1, 2 * tq), F32),
            pltpu.VMEM((LANES, 2 * tq), F32),
            pltpu.VMEM((tq, 2 * tq), F32),
            pltpu.VMEM((tq, 2 * tq), F32),
            pltpu.VMEM((1, 2 * tq), F32),
            pltpu.VMEM((1, 2 * tq), F32),
            pltpu.VMEM((2, tq, tq), F32),
        ],
        compiler_params=_cparams(("parallel", "parallel", "arbitrary")),
        name="diff_attention",
    )(scal, proj3, proj3, proj3, bias_vecs, gnorm)


def _log_sigmoid(x):
    return jnp.minimum(x, 0.0) - jnp.log(1.0 + jnp.exp(-jnp.abs(x)))


def _sigmoid(x):
    return 1.0 / (1.0 + jnp.exp(-x))


def _mlstm_kernel(q_ref, k_ref, v_ref, o_ref, gi_ref, gf_ref, cw_ref, cb_ref, bi_ref, bf_ref, gn_ref,
                  out_ref, qext_ref, kext_ref, ct_ref, n_ref, m_ref):
    sb = pl.program_id(1)
    L = CHUNK
    dh = MLSTM_HEAD_DIM
    H = N_MLSTM_HEADS
    ts = MLSTM_TS
    pad = 8

    @pl.when(sb == 0)
    def _():
        qext_ref[0:pad, :] = jnp.zeros((pad, H * dh), F32)
        kext_ref[0:pad, :] = jnp.zeros((pad, H * dh), F32)
        ct_ref[...] = jnp.zeros(ct_ref.shape, F32)
        n_ref[...] = jnp.zeros(n_ref.shape, F32)
        m_ref[...] = jnp.zeros(m_ref.shape, F32)

    qext_ref[pad:pad + ts, :] = q_ref[0].astype(F32)
    kext_ref[pad:pad + ts, :] = k_ref[0].astype(F32)

    row = lax.broadcasted_iota(jnp.int32, (L, L), 0)
    col = lax.broadcasted_iota(jnp.int32, (L, L), 1)
    tril = col <= row
    ltri = tril.astype(F32)

    def conv_silu(ext_ref, base, h, off):
        win = ext_ref[pl.ds(base, L + pad), h * dh:(h + 1) * dh]
        w = cw_ref[:, off + h * dh:off + (h + 1) * dh]
        y = cb_ref[:, off + h * dh:off + (h + 1) * dh]
        for j in range(CONV_WIDTH):
            lo = pad - (CONV_WIDTH - 1) + j
            y = y + w[j:j + 1, :] * win[lo:lo + L, :]
        return y * _sigmoid(y)

    def chunk_body(c, carry):
        base = pl.multiple_of(c * L, L)
        li = gi_ref[0, pl.ds(base, L), :] + bi_ref[...]
        logf = _log_sigmoid(gf_ref[0, pl.ds(base, L), :] + bf_ref[...])
        b = jnp.dot(ltri, logf, preferred_element_type=F32, precision=lax.Precision.HIGHEST)
        a = li - b
        g_row = b[L - 1:L, :]
        m_row = m_ref[...]
        m_new_row = g_row + jnp.maximum(m_row, jnp.max(a, axis=0, keepdims=True))
        a_t = a.T

        for h in range(H):
            qc = conv_silu(qext_ref, base, h, 0)
            kc = conv_silu(kext_ref, base, h, H * dh) * (dh ** -0.5)
            qb = qc.astype(BF16)
            kb = kc.astype(BF16)
            vb = v_ref[0, pl.ds(base, L), h * dh:(h + 1) * dh]

            a_row = a_t[h:h + 1, :]
            a_col = a[:, h:h + 1]
            b_col = b[:, h:h + 1]
            m_prev = m_row[:, h:h + 1]
            m_next = m_new_row[:, h:h + 1]
            g_h = g_row[:, h:h + 1]

            amat = jnp.where(tril, a_row, NEG_BIG)
            mcol = jnp.maximum(jnp.max(amat, axis=-1, keepdims=True), m_prev)
            wts = jnp.exp(amat - mcol)
            inter = jnp.exp(m_prev - mcol)

            s = lax.dot_general(qb, kb, (((1,), (1,)), ((), ())), preferred_element_type=F32)
            sqk = s * wts
            ct = ct_ref[h]
            nrow = n_ref[h:h + 1, :]
            num = (jnp.dot(sqk.astype(BF16), vb, preferred_element_type=F32)
                   + inter * jnp.dot(qb, ct.astype(BF16), preferred_element_type=F32))
            den = (jnp.sum(sqk, axis=-1, keepdims=True)
                   + inter * jnp.sum(qb.astype(F32) * nrow, axis=-1, keepdims=True))
            hv = num / jnp.maximum(jnp.abs(den), jnp.exp(-(b_col + mcol)))

            wt = jnp.exp(g_h + a_col - m_next)
            decay = jnp.exp(g_h + m_prev - m_next)
            wv = (wt * vb.astype(F32)).astype(BF16)
            ct_ref[h] = decay * ct + lax.dot_general(kb, wv, (((0,), (0,)), ((), ())),
                                                     preferred_element_type=F32)
            n_ref[h:h + 1, :] = decay * nrow + jnp.sum(wt * kb.astype(F32), axis=0, keepdims=True)

            ms = jnp.mean(hv * hv, axis=-1, keepdims=True)
            y = hv * lax.rsqrt(ms + EPS) * gn_ref[:, h * dh:(h + 1) * dh]
            og = o_ref[0, pl.ds(base, L), h * dh:(h + 1) * dh].astype(F32)
            out_ref[0, pl.ds(base, L), h * dh:(h + 1) * dh] = (y * _sigmoid(og)).astype(out_ref.dtype)

        m_ref[...] = m_new_row
        return carry

    lax.fori_loop(0, ts // L, chunk_body, 0)

    qext_ref[0:pad, :] = qext_ref[ts:ts + pad, :]
    kext_ref[0:pad, :] = kext_ref[ts:ts + pad, :]


def _mlstm(proj3, gates3, conv_w, conv_b, bi_row, bf_row, gnorm):
    B, S, _ = proj3.shape
    W = N_MLSTM_HEADS * MLSTM_HEAD_DIM
    ts = MLSTM_TS
    first = 3
    blk = lambda j: pl.BlockSpec((1, ts, W), lambda b, s: (b, s, j))
    full = lambda shape: pl.BlockSpec(shape, lambda b, s: (0,) * len(shape))
    return pl.pallas_call(
        _mlstm_kernel,
        grid=(B, S // ts),
        in_specs=[
            blk(first), blk(first + 1), blk(first + 2), blk(first + 3),
            pl.BlockSpec((1, ts, LANES), lambda b, s: (b, s, 0)),
            pl.BlockSpec((1, ts, LANES), lambda b, s: (b, s, 1)),
            full((CONV_WIDTH, 2 * W)), full((1, 2 * W)),
            full((1, LANES)), full((1, LANES)), full((1, W)),
        ],
        out_specs=pl.BlockSpec((1, ts, W), lambda b, s: (b, s, 0)),
        out_shape=jax.ShapeDtypeStruct((B, S, W), BF16),
        scratch_shapes=[
            pltpu.VMEM((ts + 8, W), F32),
            pltpu.VMEM((ts + 8, W), F32),
            pltpu.VMEM((N_MLSTM_HEADS, MLSTM_HEAD_DIM, MLSTM_HEAD_DIM), F32),
            pltpu.VMEM((N_MLSTM_HEADS, MLSTM_HEAD_DIM), F32),
            pltpu.VMEM((1, LANES), F32),
        ],
        compiler_params=_cparams(("parallel", "arbitrary")),
        name="mlstm",
    )(proj3, proj3, proj3, proj3, gates3, gates3, conv_w, conv_b, bi_row, bf_row, gnorm)


def _out_kernel(x_ref, a_ref, hm_ref, wa_ref, wm_ref, g_ref, wr_ref, x1_ref, h2_ref, lg_ref):
    y = (jnp.dot(a_ref[...], wa_ref[...], preferred_element_type=F32)
         + jnp.dot(hm_ref[...], wm_ref[...], preferred_element_type=F32))
    x1 = x_ref[...] + y
    x1_ref[...] = x1
    ms = jnp.mean(x1 * x1, axis=-1, keepdims=True)
    h2 = x1 * lax.rsqrt(ms + EPS) * g_ref[...]
    h2_ref[...] = _pack_bf16_pairs(h2)
    lg_ref[...] = jnp.dot(h2.astype(BF16), wr_ref[...], preferred_element_type=F32)


def _out_proj(x2, a2, hm2, wa, wm, g, wr):
    T, D = x2.shape
    W = a2.shape[1]
    tm = OUT_TM
    const = lambda shape: pl.BlockSpec(shape, lambda m: (0, 0), pipeline_mode=pl.Buffered(1))
    return pl.pallas_call(
        _out_kernel,
        grid=(T // tm,),
        in_specs=[
            pl.BlockSpec((tm, D), lambda m: (m, 0)),
            pl.BlockSpec((tm, W), lambda m: (m, 0)),
            pl.BlockSpec((tm, W), lambda m: (m, 0)),
            const((W, D)), const((W, D)), const((1, D)), const((D, LANES)),
        ],
        out_specs=[
            pl.BlockSpec((tm, D), lambda m: (m, 0)),
            pl.BlockSpec((tm, D // 2), lambda m: (m, 0)),
            pl.BlockSpec((tm, LANES), lambda m: (m, 0)),
        ],
        out_shape=[
            jax.ShapeDtypeStruct((T, D), F32),
            jax.ShapeDtypeStruct((T, D // 2), jnp.uint32),
            jax.ShapeDtypeStruct((T, LANES), F32),
        ],
        compiler_params=_cparams(("parallel",)),
        name="out_proj_router",
    )(x2, a2, hm2, wa, wm, g, wr)


def _moe_kernel(te_ref, nv_ref, first_ref, slot_ref, nxt_ref, rows_ref, xs_ref, wg_hbm, wu_hbm, wd_hbm,
                ys_ref, wg_buf, wu_buf, wd_buf, sem):
    j = pl.program_id(0)
    valid = j < nv_ref[0]

    half_f = wd_buf.shape[1] // 2

    def weight_copies(e, s):
        lo, hi = pl.ds(0, half_f), pl.ds(half_f, half_f)
        return ((pltpu.make_async_copy(wg_hbm.at[e], wg_buf.at[s], sem.at[s, 0]), 0),
                (pltpu.make_async_copy(wu_hbm.at[e], wu_buf.at[s], sem.at[s, 1]), 1),
                (pltpu.make_async_copy(wd_hbm.at[e, lo], wd_buf.at[s, lo], sem.at[s, 2]), 0),
                (pltpu.make_async_copy(wd_hbm.at[e, hi], wd_buf.at[s, hi], sem.at[s, 3]), 1))

    @pl.when(j == 0)
    def _():
        for c, prio in weight_copies(te_ref[0], 0):
            c.start(priority=prio)

    @pl.when(jnp.logical_and(valid, first_ref[j] == 1))
    def _():
        for c, _ in weight_copies(te_ref[j], slot_ref[j]):
            c.wait()

        @pl.when(nxt_ref[j] >= 0)
        def _():
            for c, prio in weight_copies(nxt_ref[j], 1 - slot_ref[j]):
                c.start(priority=prio)

    @pl.when(valid)
    def _():
        s = slot_ref[j]
        row = lax.broadcasted_iota(jnp.int32, xs_ref.shape, 0)
        lo, hi = _unpack_bf16_pairs(jnp.where(row < rows_ref[j], xs_ref[...], jnp.uint32(0)))
        xs = jnp.concatenate([lo.astype(BF16), hi.astype(BF16)], axis=1)
        gt = jnp.dot(xs, wg_buf[s].astype(BF16), preferred_element_type=F32)
        up = jnp.dot(xs, wu_buf[s].astype(BF16), preferred_element_type=F32)
        hid = (gt * _sigmoid(gt) * up).astype(BF16)
        ys_ref[...] = _pack_bf16_pairs(jnp.dot(hid, wd_buf[s].astype(BF16), preferred_element_type=F32))

    @pl.when(jnp.logical_not(valid))
    def _():
        ys_ref[...] = jnp.zeros(ys_ref.shape, ys_ref.dtype)


def _moe(tile_expert, n_valid, tile_first, tile_slot, tile_next, tile_rows, xs, wg, wu, wd):
    R, Dw = xs.shape
    D, F = wg.shape[1], wg.shape[2]
    tm = MOE_TM
    hbm = pl.BlockSpec(memory_space=pl.ANY)
    grid_spec = pltpu.PrefetchScalarGridSpec(
        num_scalar_prefetch=6,
        grid=(R // tm,),
        in_specs=[pl.BlockSpec((tm, Dw), lambda j, *_: (j, 0)), hbm, hbm, hbm],
        out_specs=pl.BlockSpec((tm, Dw), lambda j, *_: (j, 0)),
        scratch_shapes=[
            pltpu.VMEM((2, D, F), wg.dtype),
            pltpu.VMEM((2, D, F), wu.dtype),
            pltpu.VMEM((2, F, D), wd.dtype),
            pltpu.SemaphoreType.DMA((2, 4)),
        ],
    )
    return pl.pallas_call(
        _moe_kernel,
        grid_spec=grid_spec,
        out_shape=jax.ShapeDtypeStruct((R, Dw), jnp.uint32),
        compiler_params=_cparams(("arbitrary",)),
        name="moe_experts",
    )(tile_expert, n_valid, tile_first, tile_slot, tile_next, tile_rows, xs, wg, wu, wd)


def _final_kernel(x1_ref, y0_ref, y1_ref, cw_ref, g_ref, o_ref):
    cw = cw_ref[...]
    lo0, hi0 = _unpack_bf16_pairs(y0_ref[...])
    lo1, hi1 = _unpack_bf16_pairs(y1_ref[...])
    w0, w1 = cw[:, 0:1], cw[:, 1:2]
    y = jnp.concatenate([w0 * lo0 + w1 * lo1, w0 * hi0 + w1 * hi1], axis=1)
    x = x1_ref[...] + y
    ms = jnp.mean(x * x, axis=-1, keepdims=True)
    o_ref[...] = x * lax.rsqrt(ms + EPS) * g_ref[...]


def _final(x1, yw, cw, g):
    T, D = x1.shape
    tm = FIN_TM
    row = lambda w: pl.BlockSpec((tm, w), lambda m: (m, 0))
    slot1 = pl.BlockSpec((tm, D // 2), lambda m: (m + T // tm, 0))
    return pl.pallas_call(
        _final_kernel,
        grid=(T // tm,),
        in_specs=[row(D), row(D // 2), slot1, row(LANES), pl.BlockSpec((1, D), lambda m: (0, 0))],
        out_specs=row(D),
        out_shape=jax.ShapeDtypeStruct((T, D), F32),
        compiler_params=_cparams(("parallel",)),
        name="combine_final_norm",
    )(x1, yw, yw, cw, g)


SC_CORES, SC_SUBCORES = 2, 16
SC_CHUNK = 32


def _sc_gather_rows(table, idx):
    V, Dw = table.shape
    R = idx.shape[0]
    n_workers = SC_CORES * SC_SUBCORES
    ch = SC_CHUNK
    per_w = R // n_workers
    n_chunks = per_w // ch
    assert per_w * n_workers == R and n_chunks * ch == per_w and n_chunks % 2 == 0
    idx3 = idx.reshape(n_workers, n_chunks, ch)
    mesh = plsc.VectorSubcoreMesh(core_axis_name="c", subcore_axis_name="s")

    def body(table_hbm, idx_hbm, out_hbm, idx_v, rows_v, gsem, osem):
        wid = lax.axis_index("s") * SC_CORES + lax.axis_index("c")
        base = wid * per_w
        pltpu.sync_copy(idx_hbm.at[wid], idx_v)

        def gather(c, slot):
            return pltpu.make_async_copy(table_hbm.at[idx_v.at[c]], rows_v.at[slot], gsem.at[slot])

        def put(c, slot):
            return pltpu.make_async_copy(rows_v.at[slot], out_hbm.at[pl.ds(base + c * ch, ch)],
                                         osem.at[slot])

        gather(0, 0).start()

        @pl.loop(0, n_chunks, step=2)
        def _(c):
            @pl.when(c > 0)
            def _():
                put(c - 1, 1).wait()

            gather(c + 1, 1).start()
            gather(c, 0).wait()
            put(c, 0).start()
            put(c, 0).wait()

            @pl.when(c + 2 < n_chunks)
            def _():
                gather(c + 2, 0).start()

            gather(c + 1, 1).wait()
            put(c + 1, 1).start()

        put(n_chunks - 1, 1).wait()

    return pl.kernel(
        body,
        out_type=jax.ShapeDtypeStruct((R, Dw), table.dtype),
        mesh=mesh,
        scratch_types=[
            pltpu.VMEM((n_chunks, ch), jnp.int32),
            pltpu.VMEM((2, ch, Dw), table.dtype),
            pltpu.SemaphoreType.DMA((2,)),
            pltpu.SemaphoreType.DMA((2,)),
        ],
        name="sc_gather_rows",
    )(table, idx3)


def _sc_scatter_rows(table, idx, n_rows_out):
    V, Dw = table.shape
    K = idx.shape[0]
    n_workers = SC_CORES * SC_SUBCORES
    ch = SC_CHUNK
    per_w = V // n_workers
    n_chunks = per_w // ch
    assert K == 2 and per_w * n_workers == V and n_chunks * ch == per_w and n_chunks % 2 == 0
    idx4 = jnp.transpose(idx.reshape(K, n_workers, n_chunks, ch), (1, 0, 2, 3))
    mesh = plsc.VectorSubcoreMesh(core_axis_name="c", subcore_axis_name="s")

    def body(table_hbm, idx_hbm, out_hbm, idx_v, rows_v, lsem, ssem):
        wid = lax.axis_index("s") * SC_CORES + lax.axis_index("c")
        base = wid * per_w
        pltpu.sync_copy(idx_hbm.at[wid], idx_v)

        def load(c, slot):
            return pltpu.make_async_copy(table_hbm.at[pl.ds(base + c * ch, ch)], rows_v.at[slot],
                                         lsem.at[slot])

        def scatter(c, slot, k):
            return pltpu.make_async_copy(rows_v.at[slot], out_hbm.at[idx_v.at[k, c]], ssem.at[slot, k])

        load(0, 0).start()

        @pl.loop(0, n_chunks, step=2)
        def _(c):
            @pl.when(c > 0)
            def _():
                scatter(c - 1, 1, 0).wait()
                scatter(c - 1, 1, 1).wait()

            load(c + 1, 1).start()
            load(c, 0).wait()
            scatter(c, 0, 0).start()
            scatter(c, 0, 1).start()
            scatter(c, 0, 0).wait()
            scatter(c, 0, 1).wait()

            @pl.when(c + 2 < n_chunks)
            def _():
                load(c + 2, 0).start()

            load(c + 1, 1).wait()
            scatter(c + 1, 1, 0).start()
            scatter(c + 1, 1, 1).start()

        scatter(n_chunks - 1, 1, 0).wait()
        scatter(n_chunks - 1, 1, 1).wait()

    return pl.kernel(
        body,
        out_type=jax.ShapeDtypeStruct((n_rows_out, Dw), table.dtype),
        mesh=mesh,
        scratch_types=[
            pltpu.VMEM((K, n_chunks, ch), jnp.int32),
            pltpu.VMEM((2, ch, Dw), table.dtype),
            pltpu.SemaphoreType.DMA((2,)),
            pltpu.SemaphoreType.DMA((2, K)),
        ],
        name="sc_scatter_rows",
    )(table, idx4)


def _route(logits, b_group, b_router):
    T = logits.shape[0]
    G, E = N_GROUPS, EXPERTS_PER_GROUP
    gl = logits[:, :G] + b_group.astype(F32)
    gp = jax.nn.softmax(gl, axis=-1)
    gsel = jnp.argmax(gl, axis=-1)
    gw = jnp.take_along_axis(gp, gsel[:, None], axis=1)[:, 0]
    el = logits[:, G:G + G * E].reshape(T, G, E) + b_router.astype(F32)
    el_sel = jnp.take_along_axis(el, gsel[:, None, None], axis=1)[:, 0, :]
    top_v, top_i = lax.top_k(el_sel, TOP_K_INNER)
    top_w = jax.nn.softmax(top_v, axis=-1)
    eid = (gsel[:, None] * E + top_i).astype(jnp.int32)
    cw = gw[:, None] * top_w
    return eid, cw


def _dispatch_plan(eid, tm, n_tiles):
    T2 = eid.size
    flat_e = eid.reshape(-1)
    onehot = (flat_e[:, None] == jnp.arange(N_EXPERTS, dtype=jnp.int32)[None, :]).astype(jnp.int32)
    csum = jnp.cumsum(onehot, axis=0)
    rank = jnp.sum((csum - onehot) * onehot, axis=1)
    counts = csum[-1]
    tiles_per_e = (counts + tm - 1) // tm
    tile_end = jnp.cumsum(tiles_per_e)
    row_start = (tile_end - tiles_per_e) * tm
    pos = (row_start[flat_e] + rank).astype(jnp.int32)
    n_valid = tile_end[-1].astype(jnp.int32)
    tile_ids = jnp.arange(n_tiles, dtype=jnp.int32)
    tile_expert = jnp.searchsorted(tile_end, jnp.minimum(tile_ids, n_valid - 1), side="right")
    tile_expert = jnp.minimum(tile_expert, N_EXPERTS - 1).astype(jnp.int32)
    valid = tile_ids < n_valid
    prev_expert = jnp.concatenate([jnp.full((1,), -1, jnp.int32), tile_expert[:-1]])
    tile_first = jnp.logical_and(valid, tile_expert != prev_expert).astype(jnp.int32)
    tile_slot = ((jnp.cumsum(tile_first) - 1) % 2).astype(jnp.int32)
    next_start = tile_end[tile_expert]
    tile_next = jnp.where(next_start < n_valid, tile_expert[jnp.minimum(next_start, n_tiles - 1)],
                          -1).astype(jnp.int32)
    tile_start = tile_end - tiles_per_e
    tile_rows = jnp.clip(counts[tile_expert] - (tile_ids - tile_start[tile_expert]) * tm, 0, tm)
    tile_rows = jnp.where(valid, tile_rows, 0).astype(jnp.int32)
    tiles = (tile_expert, n_valid.reshape(1), tile_first, tile_slot, tile_next, tile_rows)
    return pos.reshape(eid.shape), tiles


def kernel(x, rel_bias, ln_mix_g, w_in, conv_w, conv_b, b_i, b_f, lam_q1, lam_k1, lam_q2, lam_k2,
           diff_norm_g, mlstm_norm_g, w_out, ln_ffn_g, w_group, b_group, w_router, b_router,
           w_gate, w_up, w_down, ln_f_g):
    B, S, D = x.shape
    T = B * S
    depth = w_in.shape[0]
    assert depth == 1, "the final rmsnorm is fused into the single layer's combine kernel"
    Hm = N_MLSTM_HEADS
    n_main = w_in.shape[2] - 2 * Hm
    n_diff = N_DIFF_HEADS * 2 * DIFF_HEAD_DIM
    xf = x.reshape(T, D)

    for l in range(depth):
        lambda_init = 0.8 - 0.6 * math.exp(-0.3 * l)
        w_main = w_in[l, :, :n_main].astype(BF16)
        wgt = w_in[l, :, n_main:]
        w_gates = jnp.zeros((D, 2 * LANES), F32).at[:, :Hm].set(wgt[:, :Hm]).at[:, LANES:LANES + Hm].set(
            wgt[:, Hm:]).astype(BF16)
        bi_row = jnp.zeros((1, LANES), F32).at[0, :Hm].set(b_i[l].astype(F32))
        bf_row = jnp.zeros((1, LANES), F32).at[0, :Hm].set(b_f[l].astype(F32))
        lam = (jnp.exp(jnp.sum(lam_q1[l].astype(F32) * lam_k1[l].astype(F32)))
               - jnp.exp(jnp.sum(lam_q2[l].astype(F32) * lam_k2[l].astype(F32))) + lambda_init)
        tq = ATT_TQ
        assert tq >= MAX_DISTANCE and tq % CHUNK == 0
        rb = rel_bias.astype(F32)
        log2e = math.log2(math.e)
        xx = jnp.arange(2 * tq, dtype=jnp.int32)
        rel_vec = jnp.stack([-tq + tq - 1 - xx, tq - 1 - xx], axis=0)
        bias_vecs = jnp.take(rb, _t5_bucket(rel_vec), axis=0) * log2e
        bias_vecs = jnp.transpose(bias_vecs, (2, 0, 1))[:, :, None, :]
        cfar = rb[N_BUCKETS // 2 - 1] * log2e
        scal = jnp.concatenate([lam.reshape(1), cfar]).astype(F32)
        col_scale = jnp.ones((1, n_main), F32).at[:, :n_diff].set(DIFF_HEAD_DIM ** -0.5 * log2e)

        proj, gates = _proj(xf, ln_mix_g[l].reshape(1, D).astype(F32), w_main, col_scale, w_gates)
        proj3 = proj.reshape(B, S, n_main)
        a = _diff_attention(proj3, scal, bias_vecs, diff_norm_g[l].reshape(1, n_diff).astype(F32),
                            lambda_init)
        hm = _mlstm(proj3, gates.reshape(B, S, 2 * LANES), conv_w[l].astype(F32),
                    conv_b[l].reshape(1, -1).astype(F32), bi_row, bf_row,
                    mlstm_norm_g[l].reshape(1, -1).astype(F32))

        wo = w_out[l].astype(BF16)
        G, E = N_GROUPS, EXPERTS_PER_GROUP
        wr = jnp.zeros((D, LANES), F32).at[:, :G].set(w_group[l].astype(F32)).at[:, G:G + G * E].set(
            jnp.transpose(w_router[l].astype(F32), (1, 0, 2)).reshape(D, G * E)).astype(BF16)
        x1, h2, logits = _out_proj(xf, a.reshape(T, n_diff), hm.reshape(T, -1), wo[:n_diff], wo[n_diff:],
                                   ln_ffn_g[l].reshape(1, D).astype(F32), wr)

        eid, cw = _route(logits, b_group[l], b_router[l])
        n_tiles = (T * TOP_K_INNER) // MOE_TM + N_EXPERTS
        pos, tiles = _dispatch_plan(eid, MOE_TM, n_tiles)
        xs = _sc_scatter_rows(h2, pos.T, n_tiles * MOE_TM)
        Fe = w_gate.shape[-1]
        ys = _moe(*tiles, xs, w_gate[l].reshape(N_EXPERTS, D, Fe),
                  w_up[l].reshape(N_EXPERTS, D, Fe), w_down[l].reshape(N_EXPERTS, Fe, D))
        yw = _sc_gather_rows(ys, pos.T.reshape(-1))
        cwp = jnp.zeros((T, LANES), F32).at[:, :TOP_K_INNER].set(cw)
        xf = _final(x1, yw, cwp, ln_f_g.reshape(1, D).astype(F32))
    return xf.reshape(B, S, D)
```

```python
import functools
import math

import jax
import jax.numpy as jnp
from jax import lax
from jax.experimental import pallas as pl
from jax.experimental.pallas import tpu as pltpu
from jax.experimental.pallas import tpu_sc as plsc

F32 = jnp.float32
BF16 = jnp.bfloat16

EPS = 1e-6
CHUNK = 64
DIFF_HEAD_DIM = 64
N_DIFF_HEADS = 8
MLSTM_HEAD_DIM = 128
N_MLSTM_HEADS = 8
CONV_WIDTH = 4
N_BUCKETS = 32
MAX_DISTANCE = 128
N_GROUPS = 4
EXPERTS_PER_GROUP = 8
N_EXPERTS = N_GROUPS * EXPERTS_PER_GROUP
TOP_K_INNER = 2
LANES = 128
NEG_BIG = -1e30

VMEM_LIMIT = 56 * 1024 * 1024

PROJ_TM, PROJ_TN = 1024, 1024
ATT_TQ = 512
MLSTM_TS = 512
OUT_TM = 512
MOE_TM = 256
FIN_TM = 512


def _cparams(sem):
    return pltpu.CompilerParams(dimension_semantics=sem, vmem_limit_bytes=VMEM_LIMIT)


_HI_MASK = 0xFFFF0000


def _pack_bf16_pairs(x):
    half = x.shape[-1] // 2
    xb = x.astype(BF16).astype(F32)
    lo = pltpu.bitcast(xb[:, :half], jnp.uint32)
    hi = pltpu.bitcast(xb[:, half:], jnp.uint32)
    return (hi & jnp.uint32(_HI_MASK)) | (lo >> 16)


def _unpack_bf16_pairs(w):
    lo = pltpu.bitcast(w << 16, F32)
    hi = pltpu.bitcast(w & jnp.uint32(_HI_MASK), F32)
    return lo, hi


def _proj_kernel(x_ref, g_ref, w_ref, cs_ref, wg_ref, o_ref, og_ref, h_ref):
    @pl.when(pl.program_id(1) == 0)
    def _():
        x = x_ref[...]
        ms = jnp.mean(x * x, axis=-1, keepdims=True)
        h = (x * lax.rsqrt(ms + EPS) * g_ref[...]).astype(BF16)
        h_ref[...] = h
        og_ref[...] = jnp.dot(h, wg_ref[...], preferred_element_type=F32)

    o_ref[...] = (jnp.dot(h_ref[...], w_ref[...], preferred_element_type=F32) * cs_ref[...]).astype(o_ref.dtype)


def _proj(x2, g, w_main, col_scale, w_gates):
    T, D = x2.shape
    N = col_scale.shape[1]
    NG = w_gates.shape[1]
    return pl.pallas_call(
        _proj_kernel,
        grid=(T // PROJ_TM, N // PROJ_TN),
        in_specs=[
            pl.BlockSpec((PROJ_TM, D), lambda m, n: (m, 0)),
            pl.BlockSpec((1, D), lambda m, n: (0, 0)),
            pl.BlockSpec((D, PROJ_TN), lambda m, n: (0, n)),
            pl.BlockSpec((1, PROJ_TN), lambda m, n: (0, n)),
            pl.BlockSpec((D, NG), lambda m, n: (0, 0)),
        ],
        out_specs=[
            pl.BlockSpec((PROJ_TM, PROJ_TN), lambda m, n: (m, n)),
            pl.BlockSpec((PROJ_TM, NG), lambda m, n: (m, 0)),
        ],
        out_shape=[
            jax.ShapeDtypeStruct((T, N), BF16),
            jax.ShapeDtypeStruct((T, NG), F32),
        ],
        scratch_shapes=[pltpu.VMEM((PROJ_TM, D), BF16)],
        compiler_params=_cparams(("parallel", "arbitrary")),
        name="rms_in_proj",
    )(x2, g, w_main, col_scale, w_gates)


def _t5_bucket(rel):
    half = N_BUCKETS // 2
    max_exact = half // 2
    ret = jnp.where(rel > 0, half, 0)
    n = jnp.abs(rel)
    nf = jnp.maximum(n, 1).astype(F32)
    large = max_exact + (jnp.log(nf / max_exact) / math.log(MAX_DISTANCE / max_exact)
                         * (half - max_exact)).astype(jnp.int32)
    large = jnp.minimum(large, half - 1)
    return ret + jnp.where(n < max_exact, n, large)


def _attn_kernel(scal_ref, q_ref, k_ref, v_ref, bias_ref, g_ref, o_ref, m_ref, l_ref, acc_ref,
                 s0_ref, s1_ref, ml0_ref, ml1_ref, bt_ref, *, lambda_init):
    h = pl.program_id(1)
    qi = pl.program_id(2)
    tq = ATT_TQ
    lam = scal_ref[0]
    cfar = scal_ref[1 + h]

    q = q_ref[0]
    lane = lax.broadcasted_iota(jnp.int32, q.shape, 1)
    zero = jnp.zeros_like(q)
    qs = jnp.concatenate([jnp.where(lane < DIFF_HEAD_DIM, q, zero),
                          jnp.where(lane >= DIFF_HEAD_DIM, q, zero)], axis=0)

    m_ref[...] = jnp.full(m_ref.shape, NEG_BIG, F32)
    l_ref[...] = jnp.zeros(l_ref.shape, F32)
    acc_ref[...] = jnp.zeros(acc_ref.shape, F32)

    @pl.when(qi == 0)
    def _():
        kj = lax.broadcasted_iota(jnp.int32, (tq, tq), 0)
        qq = lax.broadcasted_iota(jnp.int32, (tq, tq), 1)
        allowed = (kj // CHUNK) <= (qq // CHUNK)
        for d in range(2):
            rows = jnp.broadcast_to(bias_ref[0, d], (tq, 2 * tq))
            tile = pltpu.roll(rows, tq + 1, 1, stride=1, stride_axis=0)[:, :tq]
            if d == 1:
                tile = jnp.where(allowed, tile, NEG_BIG)
            bt_ref[d] = tile

    bufs = ((s0_ref, ml0_ref), (s1_ref, ml1_ref))

    def score(ki, bias, slot):
        s_ref, ml_ref = bufs[slot]
        start = pl.multiple_of(ki * tq, tq)
        kt = k_ref[0, pl.ds(start, tq), :]
        s = lax.dot_general(kt, qs, (((1,), (1,)), ((), ())), preferred_element_type=F32)
        if bias is not None:
            s = s + jnp.concatenate([bias, bias], axis=1)
        s_ref[:, :2 * tq] = s
        ml_ref[...] = jnp.max(s, axis=0, keepdims=True)

    def accumulate(ki, shift, slot):
        s_ref, ml_ref = bufs[slot]
        start = pl.multiple_of(ki * tq, tq)
        vt = v_ref[0, pl.ds(start, tq), :]
        m_old = m_ref[...]
        m_new = jnp.maximum(m_old, ml_ref[...] + shift)
        alpha = jnp.exp2(m_old - m_new)
        p = jnp.exp2(s_ref[:, :2 * tq] - (m_new - shift))
        l_ref[...] = alpha * l_ref[...] + jnp.sum(p, axis=0, keepdims=True)
        pv = lax.dot_general(vt, p.astype(BF16), (((0,), (0,)), ((), ())), preferred_element_type=F32)
        acc_ref[...] = alpha * acc_ref[...] + pv
        m_ref[...] = m_new

    n_far = qi - 1
    score(qi, bt_ref[1], 0)

    @pl.when(qi == 0)
    def _():
        accumulate(qi, 0.0, 0)

    @pl.when(qi >= 1)
    def _():
        accumulate(qi, 0.0, 0)
        score(qi - 1, bt_ref[0], 1)

    @pl.when(qi == 1)
    def _():
        accumulate(qi - 1, 0.0, 1)

    @pl.when(qi >= 2)
    def _():
        accumulate(qi - 1, 0.0, 1)
        score(0, None, 0)
        trips = (n_far - 1) // 2

        def pair(j, c):
            accumulate(2 * j, cfar, 0)
            score(2 * j + 1, None, 1)
            accumulate(2 * j + 1, cfar, 1)
            score(2 * j + 2, None, 0)
            return c

        lax.fori_loop(0, trips, pair, 0)
        last = 2 * trips

        @pl.when(n_far - last == 2)
        def _():
            accumulate(last, cfar, 0)
            score(last + 1, None, 1)
            accumulate(last + 1, cfar, 1)

        @pl.when(n_far - last == 1)
        def _():
            accumulate(last, cfar, 0)

    acc = acc_ref[...] * (1.0 / l_ref[...])
    o_t = acc[:, 0:tq] - lam * acc[:, tq:2 * tq]
    ms = jnp.mean(o_t * o_t, axis=0, keepdims=True)
    y = (o_t * lax.rsqrt(ms + EPS)).T * (g_ref[...] * (1.0 - lambda_init))
    o_ref[0] = y.astype(o_ref.dtype)


def _diff_attention(proj3, scal, bias_vecs, gnorm, lambda_init):
    B, S, _ = proj3.shape
    H = N_DIFF_HEADS
    tq = ATT_TQ
    kern = functools.partial(_attn_kernel, lambda_init=lambda_init)
    return pl.pallas_call(
        kern,
        grid=(B, H, S // tq),
        in_specs=[
            pl.BlockSpec(memory_space=pltpu.SMEM),
            pl.BlockSpec((1, tq, LANES), lambda b, h, i: (b, i, h)),
            pl.BlockSpec((1, S, LANES), lambda b, h, i: (b, 0, H + h)),
            pl.BlockSpec((1, S, LANES), lambda b, h, i: (b, 0, 2 * H + h)),
            pl.BlockSpec((1, 2, 1, 2 * tq), lambda b, h, i: (h, 0, 0, 0)),
            pl.BlockSpec((1, LANES), lambda b, h, i: (0, h)),
        ],
        out_specs=pl.BlockSpec((1, tq, LANES), lambda b, h, i: (b, i, h)),
        out_shape=jax.ShapeDtypeStruct((B, S, H * LANES), BF16),
        scratch_shapes=[
            pltpu.VMEM((1, 2 * tq), F32),
            pltpu.VMEM((1, 2 * tq), F32),
            pltpu.VMEM((LANES, 2 * tq), F32),
            pltpu.VMEM((tq, 2 * tq + LANES), F32),
            pltpu.VMEM((tq, 2 * tq + LANES), F32),
            pltpu.VMEM((1, 2 * tq), F32),
            pltpu.VMEM((1, 2 * tq), F32),
            pltpu.VMEM((2, tq, tq), F32),
        ],
        compiler_params=_cparams(("parallel", "parallel", "arbitrary")),
        name="diff_attention",
    )(scal, proj3, proj3, proj3, bias_vecs, gnorm)


def _log_sigmoid(x):
    return jnp.minimum(x, 0.0) - jnp.log(1.0 + jnp.exp(-jnp.abs(x)))


def _sigmoid(x):
    return 1.0 / (1.0 + jnp.exp(-x))


def _mlstm_kernel(q_ref, k_ref, v_ref, o_ref, gi_ref, gf_ref, cw_ref, cb_ref, bi_ref, bf_ref, gn_ref,
                  out_ref, qext_ref, kext_ref, ct_ref, n_ref, m_ref):
    sb = pl.program_id(1)
    L = CHUNK
    dh = MLSTM_HEAD_DIM
    H = N_MLSTM_HEADS
    ts = MLSTM_TS
    pad = 8

    @pl.when(sb == 0)
    def _():
        qext_ref[0:pad, :] = jnp.zeros((pad, H * dh), F32)
        kext_ref[0:pad, :] = jnp.zeros((pad, H * dh), F32)
        ct_ref[...] = jnp.zeros(ct_ref.shape, F32)
        n_ref[...] = jnp.zeros(n_ref.shape, F32)
        m_ref[...] = jnp.zeros(m_ref.shape, F32)

    qext_ref[pad:pad + ts, :] = q_ref[0].astype(F32)
    kext_ref[pad:pad + ts, :] = k_ref[0].astype(F32)

    row = lax.broadcasted_iota(jnp.int32, (L, L), 0)
    col = lax.broadcasted_iota(jnp.int32, (L, L), 1)
    tril = col <= row
    ltri = tril.astype(F32)

    def conv_silu(ext_ref, base, h, off):
        win = ext_ref[pl.ds(base, L + pad), h * dh:(h + 1) * dh]
        w = cw_ref[:, off + h * dh:off + (h + 1) * dh]
        y = cb_ref[:, off + h * dh:off + (h + 1) * dh]
        for j in range(CONV_WIDTH):
            lo = pad - (CONV_WIDTH - 1) + j
            y = y + w[j:j + 1, :] * win[lo:lo + L, :]
        return y * _sigmoid(y)

    def chunk_body(c, carry):
        base = pl.multiple_of(c * L, L)
        li = gi_ref[0, pl.ds(base, L), :] + bi_ref[...]
        logf = _log_sigmoid(gf_ref[0, pl.ds(base, L), :] + bf_ref[...])
        b = jnp.dot(ltri, logf, preferred_element_type=F32, precision=lax.Precision.HIGHEST)
        a = li - b
        g_row = b[L - 1:L, :]
        m_row = m_ref[...]
        m_new_row = g_row + jnp.maximum(m_row, jnp.max(a, axis=0, keepdims=True))
        a_t = a.T

        for h in range(H):
            qc = conv_silu(qext_ref, base, h, 0)
            kc = conv_silu(kext_ref, base, h, H * dh) * (dh ** -0.5)
            qb = qc.astype(BF16)
            kb = kc.astype(BF16)
            vb = v_ref[0, pl.ds(base, L), h * dh:(h + 1) * dh]

            a_row = a_t[h:h + 1, :]
            a_col = a[:, h:h + 1]
            b_col = b[:, h:h + 1]
            m_prev = m_row[:, h:h + 1]
            m_next = m_new_row[:, h:h + 1]
            g_h = g_row[:, h:h + 1]

            amat = jnp.where(tril, a_row, NEG_BIG)
            mcol = jnp.maximum(jnp.max(amat, axis=-1, keepdims=True), m_prev)
            wts = jnp.exp(amat - mcol)
            inter = jnp.exp(m_prev - mcol)

            s = lax.dot_general(qb, kb, (((1,), (1,)), ((), ())), preferred_element_type=F32)
            sqk = s * wts
            ct = ct_ref[h]
            nrow = n_ref[h:h + 1, :]
            num = (jnp.dot(sqk.astype(BF16), vb, preferred_element_type=F32)
                   + inter * jnp.dot(qb, ct.astype(BF16), preferred_element_type=F32))
            den = (jnp.sum(sqk, axis=-1, keepdims=True)
                   + inter * jnp.sum(qb.astype(F32) * nrow, axis=-1, keepdims=True))
            hv = num / jnp.maximum(jnp.abs(den), jnp.exp(-(b_col + mcol)))

            wt = jnp.exp(g_h + a_col - m_next)
            decay = jnp.exp(g_h + m_prev - m_next)
            wv = (wt * vb.astype(F32)).astype(BF16)
            ct_ref[h] = decay * ct + lax.dot_general(kb, wv, (((0,), (0,)), ((), ())),
                                                     preferred_element_type=F32)
            n_ref[h:h + 1, :] = decay * nrow + jnp.sum(wt * kb.astype(F32), axis=0, keepdims=True)

            ms = jnp.mean(hv * hv, axis=-1, keepdims=True)
            y = hv * lax.rsqrt(ms + EPS) * gn_ref[:, h * dh:(h + 1) * dh]
            og = o_ref[0, pl.ds(base, L), h * dh:(h + 1) * dh].astype(F32)
            out_ref[0, pl.ds(base, L), h * dh:(h + 1) * dh] = (y * _sigmoid(og)).astype(out_ref.dtype)

        m_ref[...] = m_new_row
        return carry

    lax.fori_loop(0, ts // L, chunk_body, 0)

    qext_ref[0:pad, :] = qext_ref[ts:ts + pad, :]
    kext_ref[0:pad, :] = kext_ref[ts:ts + pad, :]


def _mlstm(proj3, gates3, conv_w, conv_b, bi_row, bf_row, gnorm):
    B, S, _ = proj3.shape
    W = N_MLSTM_HEADS * MLSTM_HEAD_DIM
    ts = MLSTM_TS
    first = 3
    blk = lambda j: pl.BlockSpec((1, ts, W), lambda b, s: (b, s, j))
    full = lambda shape: pl.BlockSpec(shape, lambda b, s: (0,) * len(shape))
    return pl.pallas_call(
        _mlstm_kernel,
        grid=(B, S // ts),
        in_specs=[
            blk(first), blk(first + 1), blk(first + 2), blk(first + 3),
            pl.BlockSpec((1, ts, LANES), lambda b, s: (b, s, 0)),
            pl.BlockSpec((1, ts, LANES), lambda b, s: (b, s, 1)),
            full((CONV_WIDTH, 2 * W)), full((1, 2 * W)),
            full((1, LANES)), full((1, LANES)), full((1, W)),
        ],
        out_specs=pl.BlockSpec((1, ts, W), lambda b, s: (b, s, 0)),
        out_shape=jax.ShapeDtypeStruct((B, S, W), BF16),
        scratch_shapes=[
            pltpu.VMEM((ts + 8, W), F32),
            pltpu.VMEM((ts + 8, W), F32),
            pltpu.VMEM((N_MLSTM_HEADS, MLSTM_HEAD_DIM, MLSTM_HEAD_DIM), F32),
            pltpu.VMEM((N_MLSTM_HEADS, MLSTM_HEAD_DIM), F32),
            pltpu.VMEM((1, LANES), F32),
        ],
        compiler_params=_cparams(("parallel", "arbitrary")),
        name="mlstm",
    )(proj3, proj3, proj3, proj3, gates3, gates3, conv_w, conv_b, bi_row, bf_row, gnorm)


def _out_kernel(x_ref, a_ref, hm_ref, wa_ref, wm_ref, g_ref, wr_ref, x1_ref, h2_ref, lg_ref):
    y = (jnp.dot(a_ref[...], wa_ref[...], preferred_element_type=F32)
         + jnp.dot(hm_ref[...], wm_ref[...], preferred_element_type=F32))
    x1 = x_ref[...] + y
    x1_ref[...] = x1
    ms = jnp.mean(x1 * x1, axis=-1, keepdims=True)
    h2 = x1 * lax.rsqrt(ms + EPS) * g_ref[...]
    h2_ref[...] = _pack_bf16_pairs(h2)
    lg_ref[...] = jnp.dot(h2.astype(BF16), wr_ref[...], preferred_element_type=F32)


def _out_proj(x2, a2, hm2, wa, wm, g, wr):
    T, D = x2.shape
    W = a2.shape[1]
    tm = OUT_TM
    const = lambda shape: pl.BlockSpec(shape, lambda m: (0, 0), pipeline_mode=pl.Buffered(1))
    return pl.pallas_call(
        _out_kernel,
        grid=(T // tm,),
        in_specs=[
            pl.BlockSpec((tm, D), lambda m: (m, 0)),
            pl.BlockSpec((tm, W), lambda m: (m, 0)),
            pl.BlockSpec((tm, W), lambda m: (m, 0)),
            const((W, D)), const((W, D)), const((1, D)), const((D, LANES)),
        ],
        out_specs=[
            pl.BlockSpec((tm, D), lambda m: (m, 0)),
            pl.BlockSpec((tm, D // 2), lambda m: (m, 0)),
            pl.BlockSpec((tm, LANES), lambda m: (m, 0)),
        ],
        out_shape=[
            jax.ShapeDtypeStruct((T, D), F32),
            jax.ShapeDtypeStruct((T, D // 2), jnp.uint32),
            jax.ShapeDtypeStruct((T, LANES), F32),
        ],
        compiler_params=_cparams(("parallel",)),
        name="out_proj_router",
    )(x2, a2, hm2, wa, wm, g, wr)


def _moe_kernel(te_ref, nv_ref, first_ref, slot_ref, nxt_ref, rows_ref, xs_ref, wg_hbm, wu_hbm, wd_hbm,
                ys_ref, wg_buf, wu_buf, wd_buf, sem):
    j = pl.program_id(0)
    valid = j < nv_ref[0]

    half_f = wd_buf.shape[1] // 2

    def weight_copies(e, s):
        lo, hi = pl.ds(0, half_f), pl.ds(half_f, half_f)
        return ((pltpu.make_async_copy(wg_hbm.at[e], wg_buf.at[s], sem.at[s, 0]), 0),
                (pltpu.make_async_copy(wu_hbm.at[e], wu_buf.at[s], sem.at[s, 1]), 1),
                (pltpu.make_async_copy(wd_hbm.at[e, lo], wd_buf.at[s, lo], sem.at[s, 2]), 0),
                (pltpu.make_async_copy(wd_hbm.at[e, hi], wd_buf.at[s, hi], sem.at[s, 3]), 1))

    @pl.when(j == 0)
    def _():
        for c, prio in weight_copies(te_ref[0], 0):
            c.start(priority=prio)

    @pl.when(jnp.logical_and(valid, first_ref[j] == 1))
    def _():
        for c, _ in weight_copies(te_ref[j], slot_ref[j]):
            c.wait()

        @pl.when(nxt_ref[j] >= 0)
        def _():
            for c, prio in weight_copies(nxt_ref[j], 1 - slot_ref[j]):
                c.start(priority=prio)

    @pl.when(valid)
    def _():
        s = slot_ref[j]
        row = lax.broadcasted_iota(jnp.int32, xs_ref.shape, 0)
        lo, hi = _unpack_bf16_pairs(jnp.where(row < rows_ref[j], xs_ref[...], jnp.uint32(0)))
        xs = jnp.concatenate([lo.astype(BF16), hi.astype(BF16)], axis=1)
        gt = jnp.dot(xs, wg_buf[s].astype(BF16), preferred_element_type=F32)
        up = jnp.dot(xs, wu_buf[s].astype(BF16), preferred_element_type=F32)
        hid = (gt * _sigmoid(gt) * up).astype(BF16)
        ys_ref[...] = _pack_bf16_pairs(jnp.dot(hid, wd_buf[s].astype(BF16), preferred_element_type=F32))

    @pl.when(jnp.logical_not(valid))
    def _():
        ys_ref[...] = jnp.zeros(ys_ref.shape, ys_ref.dtype)


def _moe(tile_expert, n_valid, tile_first, tile_slot, tile_next, tile_rows, xs, wg, wu, wd):
    R, Dw = xs.shape
    D, F = wg.shape[1], wg.shape[2]
    tm = MOE_TM
    hbm = pl.BlockSpec(memory_space=pl.ANY)
    grid_spec = pltpu.PrefetchScalarGridSpec(
        num_scalar_prefetch=6,
        grid=(R // tm,),
        in_specs=[pl.BlockSpec((tm, Dw), lambda j, *_: (j, 0)), hbm, hbm, hbm],
        out_specs=pl.BlockSpec((tm, Dw), lambda j, *_: (j, 0)),
        scratch_shapes=[
            pltpu.VMEM((2, D, F), wg.dtype),
            pltpu.VMEM((2, D, F), wu.dtype),
            pltpu.VMEM((2, F, D), wd.dtype),
            pltpu.SemaphoreType.DMA((2, 4)),
        ],
    )
    return pl.pallas_call(
        _moe_kernel,
        grid_spec=grid_spec,
        out_shape=jax.ShapeDtypeStruct((R, Dw), jnp.uint32),
        compiler_params=_cparams(("arbitrary",)),
        name="moe_experts",
    )(tile_expert, n_valid, tile_first, tile_slot, tile_next, tile_rows, xs, wg, wu, wd)


def _final_kernel(x1_ref, y0_ref, y1_ref, cw_ref, g_ref, o_ref):
    cw = cw_ref[...]
    lo0, hi0 = _unpack_bf16_pairs(y0_ref[...])
    lo1, hi1 = _unpack_bf16_pairs(y1_ref[...])
    w0, w1 = cw[:, 0:1], cw[:, 1:2]
    y = jnp.concatenate([w0 * lo0 + w1 * lo1, w0 * hi0 + w1 * hi1], axis=1)
    x = x1_ref[...] + y
    ms = jnp.mean(x * x, axis=-1, keepdims=True)
    o_ref[...] = x * lax.rsqrt(ms + EPS) * g_ref[...]


def _final(x1, yw, cw, g):
    T, D = x1.shape
    tm = FIN_TM
    row = lambda w: pl.BlockSpec((tm, w), lambda m: (m, 0))
    slot1 = pl.BlockSpec((tm, D // 2), lambda m: (m + T // tm, 0))
    return pl.pallas_call(
        _final_kernel,
        grid=(T // tm,),
        in_specs=[row(D), row(D // 2), slot1, row(LANES), pl.BlockSpec((1, D), lambda m: (0, 0))],
        out_specs=row(D),
        out_shape=jax.ShapeDtypeStruct((T, D), F32),
        compiler_params=_cparams(("parallel",)),
        name="combine_final_norm",
    )(x1, yw, yw, cw, g)


SC_CORES, SC_SUBCORES = 2, 16
SC_CHUNK = 32


def _sc_gather_rows(table, idx):
    V, Dw = table.shape
    R = idx.shape[0]
    n_workers = SC_CORES * SC_SUBCORES
    ch = SC_CHUNK
    per_w = R // n_workers
    n_chunks = per_w // ch
    assert per_w * n_workers == R and n_chunks * ch == per_w and n_chunks % 2 == 0
    idx3 = idx.reshape(n_workers, n_chunks, ch)
    mesh = plsc.VectorSubcoreMesh(core_axis_name="c", subcore_axis_name="s")

    def body(table_hbm, idx_hbm, out_hbm, idx_v, rows_v, gsem, osem):
        wid = lax.axis_index("s") * SC_CORES + lax.axis_index("c")
        base = wid * per_w
        pltpu.sync_copy(idx_hbm.at[wid], idx_v)

        def gather(c, slot):
            return pltpu.make_async_copy(table_hbm.at[idx_v.at[c]], rows_v.at[slot], gsem.at[slot])

        def put(c, slot):
            return pltpu.make_async_copy(rows_v.at[slot], out_hbm.at[pl.ds(base + c * ch, ch)],
                                         osem.at[slot])

        gather(0, 0).start()

        @pl.loop(0, n_chunks, step=2)
        def _(c):
            @pl.when(c > 0)
            def _():
                put(c - 1, 1).wait()

            gather(c + 1, 1).start()
            gather(c, 0).wait()
            put(c, 0).start()
            put(c, 0).wait()

            @pl.when(c + 2 < n_chunks)
            def _():
                gather(c + 2, 0).start()

            gather(c + 1, 1).wait()
            put(c + 1, 1).start()

        put(n_chunks - 1, 1).wait()

    return pl.kernel(
        body,
        out_type=jax.ShapeDtypeStruct((R, Dw), table.dtype),
        mesh=mesh,
        scratch_types=[
            pltpu.VMEM((n_chunks, ch), jnp.int32),
            pltpu.VMEM((2, ch, Dw), table.dtype),
            pltpu.SemaphoreType.DMA((2,)),
            pltpu.SemaphoreType.DMA((2,)),
        ],
        name="sc_gather_rows",
    )(table, idx3)


def _sc_scatter_rows(table, idx, n_rows_out):
    V, Dw = table.shape
    K = idx.shape[0]
    n_workers = SC_CORES * SC_SUBCORES
    ch = SC_CHUNK
    per_w = V // n_workers
    n_chunks = per_w // ch
    assert K == 2 and per_w * n_workers == V and n_chunks * ch == per_w and n_chunks % 2 == 0
    idx4 = jnp.transpose(idx.reshape(K, n_workers, n_chunks, ch), (1, 0, 2, 3))
    mesh = plsc.VectorSubcoreMesh(core_axis_name="c", subcore_axis_name="s")

    def body(table_hbm, idx_hbm, out_hbm, idx_v, rows_v, lsem, ssem):
        wid = lax.axis_index("s") * SC_CORES + lax.axis_index("c")
        base = wid * per_w
        pltpu.sync_copy(idx_hbm.at[wid], idx_v)

        def load(c, slot):
            return pltpu.make_async_copy(table_hbm.at[pl.ds(base + c * ch, ch)], rows_v.at[slot],
                                         lsem.at[slot])

        def scatter(c, slot, k):
            return pltpu.make_async_copy(rows_v.at[slot], out_hbm.at[idx_v.at[k, c]], ssem.at[slot, k])

        load(0, 0).start()

        @pl.loop(0, n_chunks, step=2)
        def _(c):
            @pl.when(c > 0)
            def _():
                scatter(c - 1, 1, 0).wait()
                scatter(c - 1, 1, 1).wait()

            load(c + 1, 1).start()
            load(c, 0).wait()
            scatter(c, 0, 0).start()
            scatter(c, 0, 1).start()
            scatter(c, 0, 0).wait()
            scatter(c, 0, 1).wait()

            @pl.when(c + 2 < n_chunks)
            def _():
                load(c + 2, 0).start()

            load(c + 1, 1).wait()
            scatter(c + 1, 1, 0).start()
            scatter(c + 1, 1, 1).start()

        scatter(n_chunks - 1, 1, 0).wait()
        scatter(n_chunks - 1, 1, 1).wait()

    return pl.kernel(
        body,
        out_type=jax.ShapeDtypeStruct((n_rows_out, Dw), table.dtype),
        mesh=mesh,
        scratch_types=[
            pltpu.VMEM((K, n_chunks, ch), jnp.int32),
            pltpu.VMEM((2, ch, Dw), table.dtype),
            pltpu.SemaphoreType.DMA((2,)),
            pltpu.SemaphoreType.DMA((2, K)),
        ],
        name="sc_scatter_rows",
    )(table, idx4)


def _route(logits, b_group, b_router):
    T = logits.shape[0]
    G, E = N_GROUPS, EXPERTS_PER_GROUP
    gl = logits[:, :G] + b_group.astype(F32)
    gp = jax.nn.softmax(gl, axis=-1)
    gsel = jnp.argmax(gl, axis=-1)
    gw = jnp.take_along_axis(gp, gsel[:, None], axis=1)[:, 0]
    el = logits[:, G:G + G * E].reshape(T, G, E) + b_router.astype(F32)
    el_sel = jnp.take_along_axis(el, gsel[:, None, None], axis=1)[:, 0, :]
    top_v, top_i = lax.top_k(el_sel, TOP_K_INNER)
    top_w = jax.nn.softmax(top_v, axis=-1)
    eid = (gsel[:, None] * E + top_i).astype(jnp.int32)
    cw = gw[:, None] * top_w
    return eid, cw


def _dispatch_plan(eid, tm, n_tiles):
    T2 = eid.size
    flat_e = eid.reshape(-1)
    onehot = (flat_e[:, None] == jnp.arange(N_EXPERTS, dtype=jnp.int32)[None, :]).astype(jnp.int32)
    csum = jnp.cumsum(onehot, axis=0)
    rank = jnp.sum((csum - onehot) * onehot, axis=1)
    counts = csum[-1]
    tiles_per_e = (counts + tm - 1) // tm
    tile_end = jnp.cumsum(tiles_per_e)
    row_start = (tile_end - tiles_per_e) * tm
    pos = (row_start[flat_e] + rank).astype(jnp.int32)
    n_valid = tile_end[-1].astype(jnp.int32)
    tile_ids = jnp.arange(n_tiles, dtype=jnp.int32)
    tile_expert = jnp.searchsorted(tile_end, jnp.minimum(tile_ids, n_valid - 1), side="right")
    tile_expert = jnp.minimum(tile_expert, N_EXPERTS - 1).astype(jnp.int32)
    valid = tile_ids < n_valid
    prev_expert = jnp.concatenate([jnp.full((1,), -1, jnp.int32), tile_expert[:-1]])
    tile_first = jnp.logical_and(valid, tile_expert != prev_expert).astype(jnp.int32)
    tile_slot = ((jnp.cumsum(tile_first) - 1) % 2).astype(jnp.int32)
    next_start = tile_end[tile_expert]
    tile_next = jnp.where(next_start < n_valid, tile_expert[jnp.minimum(next_start, n_tiles - 1)],
                          -1).astype(jnp.int32)
    tile_start = tile_end - tiles_per_e
    tile_rows = jnp.clip(counts[tile_expert] - (tile_ids - tile_start[tile_expert]) * tm, 0, tm)
    tile_rows = jnp.where(valid, tile_rows, 0).astype(jnp.int32)
    tiles = (tile_expert, n_valid.reshape(1), tile_first, tile_slot, tile_next, tile_rows)
    return pos.reshape(eid.shape), tiles


def kernel(x, rel_bias, ln_mix_g, w_in, conv_w, conv_b, b_i, b_f, lam_q1, lam_k1, lam_q2, lam_k2,
           diff_norm_g, mlstm_norm_g, w_out, ln_ffn_g, w_group, b_group, w_router, b_router,
           w_gate, w_up, w_down, ln_f_g):
    B, S, D = x.shape
    T = B * S
    depth = w_in.shape[0]
    assert depth == 1, "the final rmsnorm is fused into the single layer's combine kernel"
    Hm = N_MLSTM_HEADS
    n_main = w_in.shape[2] - 2 * Hm
    n_diff = N_DIFF_HEADS * 2 * DIFF_HEAD_DIM
    xf = x.reshape(T, D)

    for l in range(depth):
        lambda_init = 0.8 - 0.6 * math.exp(-0.3 * l)
        w_main = w_in[l].astype(BF16)
        wgt = w_in[l, :, n_main:]
        w_gates = jnp.zeros((D, 2 * LANES), F32).at[:, :Hm].set(wgt[:, :Hm]).at[:, LANES:LANES + Hm].set(
            wgt[:, Hm:]).astype(BF16)
        bi_row = jnp.zeros((1, LANES), F32).at[0, :Hm].set(b_i[l].astype(F32))
        bf_row = jnp.zeros((1, LANES), F32).at[0, :Hm].set(b_f[l].astype(F32))
        lam = (jnp.exp(jnp.sum(lam_q1[l].astype(F32) * lam_k1[l].astype(F32)))
               - jnp.exp(jnp.sum(lam_q2[l].astype(F32) * lam_k2[l].astype(F32))) + lambda_init)
        tq = ATT_TQ
        assert tq >= MAX_DISTANCE and tq % CHUNK == 0
        rb = rel_bias.astype(F32)
        log2e = math.log2(math.e)
        xx = jnp.arange(2 * tq, dtype=jnp.int32)
        rel_vec = jnp.stack([-tq + tq - 1 - xx, tq - 1 - xx], axis=0)
        bias_vecs = jnp.take(rb, _t5_bucket(rel_vec), axis=0) * log2e
        bias_vecs = jnp.transpose(bias_vecs, (2, 0, 1))[:, :, None, :]
        cfar = rb[N_BUCKETS // 2 - 1] * log2e
        scal = jnp.concatenate([lam.reshape(1), cfar]).astype(F32)
        col_scale = jnp.ones((1, n_main), F32).at[:, :n_diff].set(DIFF_HEAD_DIM ** -0.5 * log2e)

        proj, gates = _proj(xf, ln_mix_g[l].reshape(1, D).astype(F32), w_main, col_scale, w_gates)
        proj3 = proj.reshape(B, S, n_main)
        a = _diff_attention(proj3, scal, bias_vecs, diff_norm_g[l].reshape(1, n_diff).astype(F32),
                            lambda_init)
        hm = _mlstm(proj3, gates.reshape(B, S, 2 * LANES), conv_w[l].astype(F32),
                    conv_b[l].reshape(1, -1).astype(F32), bi_row, bf_row,
                    mlstm_norm_g[l].reshape(1, -1).astype(F32))

        wo = w_out[l].astype(BF16)
        G, E = N_GROUPS, EXPERTS_PER_GROUP
        wr = jnp.zeros((D, LANES), F32).at[:, :G].set(w_group[l].astype(F32)).at[:, G:G + G * E].set(
            jnp.transpose(w_router[l].astype(F32), (1, 0, 2)).reshape(D, G * E)).astype(BF16)
        x1, h2, logits = _out_proj(xf, a.reshape(T, n_diff), hm.reshape(T, -1), wo[:n_diff], wo[n_diff:],
                                   ln_ffn_g[l].reshape(1, D).astype(F32), wr)

        eid, cw = _route(logits, b_group[l], b_router[l])
        n_tiles = (T * TOP_K_INNER) // MOE_TM + N_EXPERTS
        pos, tiles = _dispatch_plan(eid, MOE_TM, n_tiles)
        xs = _sc_scatter_rows(h2, pos.T, n_tiles * MOE_TM)
        Fe = w_gate.shape[-1]
        ys = _moe(*tiles, xs, w_gate[l].reshape(N_EXPERTS, D, Fe),
                  w_up[l].reshape(N_EXPERTS, D, Fe), w_down[l].reshape(N_EXPERTS, Fe, D))
        yw = _sc_gather_rows(ys, pos.T.reshape(-1))
        cwp = jnp.zeros((T, LANES), F32).at[:, :TOP_K_INNER].set(cw)
        xf = _final(x1, yw, cwp, ln_f_g.reshape(1, D).astype(F32))
    return xf.reshape(B, S, D)
```

```python
import functools
import math

import jax
import jax.numpy as jnp
from jax import lax
from jax.experimental import pallas as pl
from jax.experimental.pallas import tpu as pltpu
from jax.experimental.pallas import tpu_sc as plsc

F32 = jnp.float32
BF16 = jnp.bfloat16

EPS = 1e-6
CHUNK = 64
DIFF_HEAD_DIM = 64
N_DIFF_HEADS = 8
MLSTM_HEAD_DIM = 128
N_MLSTM_HEADS = 8
CONV_WIDTH = 4
N_BUCKETS = 32
MAX_DISTANCE = 128
N_GROUPS = 4
EXPERTS_PER_GROUP = 8
N_EXPERTS = N_GROUPS * EXPERTS_PER_GROUP
TOP_K_INNER = 2
LANES = 128
NEG_BIG = -1e30

VMEM_LIMIT = 56 * 1024 * 1024

PROJ_TM, PROJ_TN = 1024, 1024
ATT_TQ = 512
MLSTM_TS = 512
OUT_TM = 512
MOE_TM = 512
FIN_TM = 512


def _cparams(sem):
    return pltpu.CompilerParams(dimension_semantics=sem, vmem_limit_bytes=VMEM_LIMIT)


_HI_MASK = 0xFFFF0000


def _pack_bf16_pairs(x):
    half = x.shape[-1] // 2
    xb = x.astype(BF16).astype(F32)
    lo = pltpu.bitcast(xb[:, :half], jnp.uint32)
    hi = pltpu.bitcast(xb[:, half:], jnp.uint32)
    return (hi & jnp.uint32(_HI_MASK)) | (lo >> 16)


def _unpack_bf16_pairs(w):
    lo = pltpu.bitcast(w << 16, F32)
    hi = pltpu.bitcast(w & jnp.uint32(_HI_MASK), F32)
    return lo, hi


def _proj_kernel(x_ref, g_ref, w_ref, cs_ref, wg_ref, o_ref, og_ref, h_ref):
    @pl.when(pl.program_id(1) == 0)
    def _():
        x = x_ref[...]
        ms = jnp.mean(x * x, axis=-1, keepdims=True)
        h = (x * lax.rsqrt(ms + EPS) * g_ref[...]).astype(BF16)
        h_ref[...] = h
        og_ref[...] = jnp.dot(h, wg_ref[...], preferred_element_type=F32)

    o_ref[...] = (jnp.dot(h_ref[...], w_ref[...], preferred_element_type=F32) * cs_ref[...]).astype(o_ref.dtype)


def _proj(x2, g, w_main, col_scale, w_gates):
    T, D = x2.shape
    N = w_main.shape[1]
    NG = w_gates.shape[1]
    return pl.pallas_call(
        _proj_kernel,
        grid=(T // PROJ_TM, N // PROJ_TN),
        in_specs=[
            pl.BlockSpec((PROJ_TM, D), lambda m, n: (m, 0)),
            pl.BlockSpec((1, D), lambda m, n: (0, 0)),
            pl.BlockSpec((D, PROJ_TN), lambda m, n: (0, n)),
            pl.BlockSpec((1, PROJ_TN), lambda m, n: (0, n)),
            pl.BlockSpec((D, NG), lambda m, n: (0, 0)),
        ],
        out_specs=[
            pl.BlockSpec((PROJ_TM, PROJ_TN), lambda m, n: (m, n)),
            pl.BlockSpec((PROJ_TM, NG), lambda m, n: (m, 0)),
        ],
        out_shape=[
            jax.ShapeDtypeStruct((T, N), BF16),
            jax.ShapeDtypeStruct((T, NG), F32),
        ],
        scratch_shapes=[pltpu.VMEM((PROJ_TM, D), BF16)],
        compiler_params=_cparams(("parallel", "arbitrary")),
        name="rms_in_proj",
    )(x2, g, w_main, col_scale, w_gates)


def _t5_bucket(rel):
    half = N_BUCKETS // 2
    max_exact = half // 2
    ret = jnp.where(rel > 0, half, 0)
    n = jnp.abs(rel)
    nf = jnp.maximum(n, 1).astype(F32)
    large = max_exact + (jnp.log(nf / max_exact) / math.log(MAX_DISTANCE / max_exact)
                         * (half - max_exact)).astype(jnp.int32)
    large = jnp.minimum(large, half - 1)
    return ret + jnp.where(n < max_exact, n, large)


def _attn_kernel(scal_ref, q_ref, k_ref, v_ref, bias_ref, g_ref, o_ref, m_ref, l_ref, acc_ref,
                 s0_ref, s1_ref, ml0_ref, ml1_ref, bt_ref, *, lambda_init):
    h = pl.program_id(1)
    qi = pl.program_id(2)
    tq = ATT_TQ
    lam = scal_ref[0]
    cfar = scal_ref[1 + h]

    q = q_ref[0]
    lane = lax.broadcasted_iota(jnp.int32, q.shape, 1)
    zero = jnp.zeros_like(q)
    qs = jnp.concatenate([jnp.where(lane < DIFF_HEAD_DIM, q, zero),
                          jnp.where(lane >= DIFF_HEAD_DIM, q, zero)], axis=0)

    m_ref[...] = jnp.full(m_ref.shape, NEG_BIG, F32)
    l_ref[...] = jnp.zeros(l_ref.shape, F32)
    acc_ref[...] = jnp.zeros(acc_ref.shape, F32)

    @pl.when(qi == 0)
    def _():
        kj = lax.broadcasted_iota(jnp.int32, (tq, tq), 0)
        qq = lax.broadcasted_iota(jnp.int32, (tq, tq), 1)
        allowed = (kj // CHUNK) <= (qq // CHUNK)
        for d in range(2):
            rows = jnp.broadcast_to(bias_ref[0, d], (tq, 2 * tq))
            tile = pltpu.roll(rows, tq + 1, 1, stride=1, stride_axis=0)[:, :tq]
            if d == 1:
                tile = jnp.where(allowed, tile, NEG_BIG)
            bt_ref[d] = tile

    bufs = ((s0_ref, ml0_ref), (s1_ref, ml1_ref))

    def score(ki, bias, slot):
        s_ref, ml_ref = bufs[slot]
        start = pl.multiple_of(ki * tq, tq)
        kt = k_ref[0, pl.ds(start, tq), :]
        s = lax.dot_general(kt, qs, (((1,), (1,)), ((), ())), preferred_element_type=F32)
        if bias is not None:
            s = s + jnp.concatenate([bias, bias], axis=1)
        s_ref[...] = s
        ml_ref[...] = jnp.max(s, axis=0, keepdims=True)

    def accumulate(ki, shift, slot):
        s_ref, ml_ref = bufs[slot]
        start = pl.multiple_of(ki * tq, tq)
        vt = v_ref[0, pl.ds(start, tq), :]
        m_old = m_ref[...]
        m_new = jnp.maximum(m_old, ml_ref[...] + shift)
        alpha = jnp.exp2(m_old - m_new)
        p = jnp.exp2(s_ref[...] - (m_new - shift))
        l_ref[...] = alpha * l_ref[...] + jnp.sum(p, axis=0, keepdims=True)
        pv = lax.dot_general(vt, p.astype(BF16), (((0,), (0,)), ((), ())), preferred_element_type=F32)
        acc_ref[...] = alpha * acc_ref[...] + pv
        m_ref[...] = m_new

    n_far = qi - 1
    score(qi, bt_ref[1], 0)

    @pl.when(qi == 0)
    def _():
        accumulate(qi, 0.0, 0)

    @pl.when(qi >= 1)
    def _():
        accumulate(qi, 0.0, 0)
        score(qi - 1, bt_ref[0], 1)

    @pl.when(qi == 1)
    def _():
        accumulate(qi - 1, 0.0, 1)

    @pl.when(qi >= 2)
    def _():
        accumulate(qi - 1, 0.0, 1)
        score(0, None, 0)
        trips = (n_far - 1) // 2

        def pair(j, c):
            accumulate(2 * j, cfar, 0)
            score(2 * j + 1, None, 1)
            accumulate(2 * j + 1, cfar, 1)
            score(2 * j + 2, None, 0)
            return c

        lax.fori_loop(0, trips, pair, 0)
        last = 2 * trips

        @pl.when(n_far - last == 2)
        def _():
            accumulate(last, cfar, 0)
            score(last + 1, None, 1)
            accumulate(last + 1, cfar, 1)

        @pl.when(n_far - last == 1)
        def _():
            accumulate(last, cfar, 0)

    acc = acc_ref[...] * (1.0 / l_ref[...])
    o_t = acc[:, 0:tq] - lam * acc[:, tq:2 * tq]
    ms = jnp.mean(o_t * o_t, axis=0, keepdims=True)
    y = (o_t * lax.rsqrt(ms + EPS)).T * (g_ref[...] * (1.0 - lambda_init))
    o_ref[0] = y.astype(o_ref.dtype)


def _diff_attention(proj3, scal, bias_vecs, gnorm, lambda_init):
    B, S, _ = proj3.shape
    H = N_DIFF_HEADS
    tq = ATT_TQ
    kern = functools.partial(_attn_kernel, lambda_init=lambda_init)
    return pl.pallas_call(
        kern,
        grid=(B, H, S // tq),
        in_specs=[
            pl.BlockSpec(memory_space=pltpu.SMEM),
            pl.BlockSpec((1, tq, LANES), lambda b, h, i: (b, i, h)),
            pl.BlockSpec((1, S, LANES), lambda b, h, i: (b, 0, H + h)),
            pl.BlockSpec((1, S, LANES), lambda b, h, i: (b, 0, 2 * H + h)),
            pl.BlockSpec((1, 2, 1, 2 * tq), lambda b, h, i: (h, 0, 0, 0)),
            pl.BlockSpec((1, LANES), lambda b, h, i: (0, h)),
        ],
        out_specs=pl.BlockSpec((1, tq, LANES), lambda b, h, i: (b, i, h)),
        out_shape=jax.ShapeDtypeStruct((B, S, H * LANES), BF16),
        scratch_shapes=[
            pltpu.VMEM((1, 2 * tq), F32),
            pltpu.VMEM((1, 2 * tq), F32),
            pltpu.VMEM((LANES, 2 * tq), F32),
            pltpu.VMEM((tq, 2 * tq), F32),
            pltpu.VMEM((tq, 2 * tq), F32),
            pltpu.VMEM((1, 2 * tq), F32),
            pltpu.VMEM((1, 2 * tq), F32),
            pltpu.VMEM((2, tq, tq), F32),
        ],
        compiler_params=_cparams(("parallel", "parallel", "arbitrary")),
        name="diff_attention",
    )(scal, proj3, proj3, proj3, bias_vecs, gnorm)


def _log_sigmoid(x):
    return jnp.minimum(x, 0.0) - jnp.log(1.0 + jnp.exp(-jnp.abs(x)))


def _sigmoid(x):
    return 1.0 / (1.0 + jnp.exp(-x))


def _mlstm_kernel(q_ref, k_ref, v_ref, o_ref, gi_ref, gf_ref, cw_ref, cb_ref, bi_ref, bf_ref, gn_ref,
                  out_ref, qext_ref, kext_ref, ct_ref, n_ref, m_ref):
    sb = pl.program_id(1)
    L = CHUNK
    dh = MLSTM_HEAD_DIM
    H = N_MLSTM_HEADS
    ts = MLSTM_TS
    pad = 8

    @pl.when(sb == 0)
    def _():
        qext_ref[0:pad, :] = jnp.zeros((pad, H * dh), F32)
        kext_ref[0:pad, :] = jnp.zeros((pad, H * dh), F32)
        ct_ref[...] = jnp.zeros(ct_ref.shape, F32)
        n_ref[...] = jnp.zeros(n_ref.shape, F32)
        m_ref[...] = jnp.zeros(m_ref.shape, F32)

    qext_ref[pad:pad + ts, :] = q_ref[0].astype(F32)
    kext_ref[pad:pad + ts, :] = k_ref[0].astype(F32)

    row = lax.broadcasted_iota(jnp.int32, (L, L), 0)
    col = lax.broadcasted_iota(jnp.int32, (L, L), 1)
    tril = col <= row
    ltri = tril.astype(F32)

    def conv_silu(ext_ref, base, h, off):
        win = ext_ref[pl.ds(base, L + pad), h * dh:(h + 1) * dh]
        w = cw_ref[:, off + h * dh:off + (h + 1) * dh]
        y = cb_ref[:, off + h * dh:off + (h + 1) * dh]
        for j in range(CONV_WIDTH):
            lo = pad - (CONV_WIDTH - 1) + j
            y = y + w[j:j + 1, :] * win[lo:lo + L, :]
        return y * _sigmoid(y)

    def chunk_body(c, carry):
        base = pl.multiple_of(c * L, L)
        li = gi_ref[0, pl.ds(base, L), :] + bi_ref[...]
        logf = _log_sigmoid(gf_ref[0, pl.ds(base, L), :] + bf_ref[...])
        b = jnp.dot(ltri, logf, preferred_element_type=F32, precision=lax.Precision.HIGHEST)
        a = li - b
        g_row = b[L - 1:L, :]
        m_row = m_ref[...]
        m_new_row = g_row + jnp.maximum(m_row, jnp.max(a, axis=0, keepdims=True))
        a_t = a.T

        for h in range(H):
            qc = conv_silu(qext_ref, base, h, 0)
            kc = conv_silu(kext_ref, base, h, H * dh) * (dh ** -0.5)
            qb = qc.astype(BF16)
            kb = kc.astype(BF16)
            vb = v_ref[0, pl.ds(base, L), h * dh:(h + 1) * dh]

            a_row = a_t[h:h + 1, :]
            a_col = a[:, h:h + 1]
            b_col = b[:, h:h + 1]
            m_prev = m_row[:, h:h + 1]
            m_next = m_new_row[:, h:h + 1]
            g_h = g_row[:, h:h + 1]

            amat = jnp.where(tril, a_row, NEG_BIG)
            mcol = jnp.maximum(jnp.max(amat, axis=-1, keepdims=True), m_prev)
            wts = jnp.exp(amat - mcol)
            inter = jnp.exp(m_prev - mcol)

            s = lax.dot_general(qb, kb, (((1,), (1,)), ((), ())), preferred_element_type=F32)
            sqk = s * wts
            ct = ct_ref[h]
            nrow = n_ref[h:h + 1, :]
            num = (jnp.dot(sqk.astype(BF16), vb, preferred_element_type=F32)
                   + inter * jnp.dot(qb, ct.astype(BF16), preferred_element_type=F32))
            den = (jnp.sum(sqk, axis=-1, keepdims=True)
                   + inter * jnp.sum(qb.astype(F32) * nrow, axis=-1, keepdims=True))
            hv = num / jnp.maximum(jnp.abs(den), jnp.exp(-(b_col + mcol)))

            wt = jnp.exp(g_h + a_col - m_next)
            decay = jnp.exp(g_h + m_prev - m_next)
            wv = (wt * vb.astype(F32)).astype(BF16)
            ct_ref[h] = decay * ct + lax.dot_general(kb, wv, (((0,), (0,)), ((), ())),
                                                     preferred_element_type=F32)
            n_ref[h:h + 1, :] = decay * nrow + jnp.sum(wt * kb.astype(F32), axis=0, keepdims=True)

            ms = jnp.mean(hv * hv, axis=-1, keepdims=True)
            y = hv * lax.rsqrt(ms + EPS) * gn_ref[:, h * dh:(h + 1) * dh]
            og = o_ref[0, pl.ds(base, L), h * dh:(h + 1) * dh].astype(F32)
            out_ref[0, pl.ds(base, L), h * dh:(h + 1) * dh] = (y * _sigmoid(og)).astype(out_ref.dtype)

        m_ref[...] = m_new_row
        return carry

    lax.fori_loop(0, ts // L, chunk_body, 0)

    qext_ref[0:pad, :] = qext_ref[ts:ts + pad, :]
    kext_ref[0:pad, :] = kext_ref[ts:ts + pad, :]


def _mlstm(proj3, gates3, conv_w, conv_b, bi_row, bf_row, gnorm):
    B, S, _ = proj3.shape
    W = N_MLSTM_HEADS * MLSTM_HEAD_DIM
    ts = MLSTM_TS
    first = 3
    blk = lambda j: pl.BlockSpec((1, ts, W), lambda b, s: (b, s, j))
    full = lambda shape: pl.BlockSpec(shape, lambda b, s: (0,) * len(shape))
    return pl.pallas_call(
        _mlstm_kernel,
        grid=(B, S // ts),
        in_specs=[
            blk(first), blk(first + 1), blk(first + 2), blk(first + 3),
            pl.BlockSpec((1, ts, LANES), lambda b, s: (b, s, 0)),
            pl.BlockSpec((1, ts, LANES), lambda b, s: (b, s, 1)),
            full((CONV_WIDTH, 2 * W)), full((1, 2 * W)),
            full((1, LANES)), full((1, LANES)), full((1, W)),
        ],
        out_specs=pl.BlockSpec((1, ts, W), lambda b, s: (b, s, 0)),
        out_shape=jax.ShapeDtypeStruct((B, S, W), BF16),
        scratch_shapes=[
            pltpu.VMEM((ts + 8, W), F32),
            pltpu.VMEM((ts + 8, W), F32),
            pltpu.VMEM((N_MLSTM_HEADS, MLSTM_HEAD_DIM, MLSTM_HEAD_DIM), F32),
            pltpu.VMEM((N_MLSTM_HEADS, MLSTM_HEAD_DIM), F32),
            pltpu.VMEM((1, LANES), F32),
        ],
        compiler_params=_cparams(("parallel", "arbitrary")),
        name="mlstm",
    )(proj3, proj3, proj3, proj3, gates3, gates3, conv_w, conv_b, bi_row, bf_row, gnorm)


def _out_kernel(x_ref, a_ref, hm_ref, wa_ref, wm_ref, g_ref, wr_ref, x1_ref, h2_ref, lg_ref):
    y = (jnp.dot(a_ref[...], wa_ref[...], preferred_element_type=F32)
         + jnp.dot(hm_ref[...], wm_ref[...], preferred_element_type=F32))
    x1 = x_ref[...] + y
    x1_ref[...] = x1
    ms = jnp.mean(x1 * x1, axis=-1, keepdims=True)
    h2 = x1 * lax.rsqrt(ms + EPS) * g_ref[...]
    h2_ref[...] = _pack_bf16_pairs(h2)
    lg_ref[...] = jnp.dot(h2.astype(BF16), wr_ref[...], preferred_element_type=F32)


def _out_proj(x2, a2, hm2, wa, wm, g, wr):
    T, D = x2.shape
    W = a2.shape[1]
    tm = OUT_TM
    const = lambda shape: pl.BlockSpec(shape, lambda m: (0, 0), pipeline_mode=pl.Buffered(1))
    return pl.pallas_call(
        _out_kernel,
        grid=(T // tm,),
        in_specs=[
            pl.BlockSpec((tm, D), lambda m: (m, 0)),
            pl.BlockSpec((tm, W), lambda m: (m, 0)),
            pl.BlockSpec((tm, W), lambda m: (m, 0)),
            const((W, D)), const((W, D)), const((1, D)), const((D, LANES)),
        ],
        out_specs=[
            pl.BlockSpec((tm, D), lambda m: (m, 0)),
            pl.BlockSpec((tm, D // 2), lambda m: (m, 0)),
            pl.BlockSpec((tm, LANES), lambda m: (m, 0)),
        ],
        out_shape=[
            jax.ShapeDtypeStruct((T, D), F32),
            jax.ShapeDtypeStruct((T, D // 2), jnp.uint32),
            jax.ShapeDtypeStruct((T, LANES), F32),
        ],
        compiler_params=_cparams(("parallel",)),
        name="out_proj_router",
    )(x2, a2, hm2, wa, wm, g, wr)


def _moe_kernel(te_ref, nv_ref, first_ref, slot_ref, nxt_ref, rows_ref, xs_ref, wg_hbm, wu_hbm, wd_hbm,
                ys_ref, wg_buf, wu_buf, wd_buf, sem):
    j = pl.program_id(0)
    valid = j < nv_ref[0]

    half_f = wd_buf.shape[1] // 2

    def weight_copies(e, s):
        lo, hi = pl.ds(0, half_f), pl.ds(half_f, half_f)
        return ((pltpu.make_async_copy(wg_hbm.at[e], wg_buf.at[s], sem.at[s, 0]), 0),
                (pltpu.make_async_copy(wu_hbm.at[e], wu_buf.at[s], sem.at[s, 1]), 1),
                (pltpu.make_async_copy(wd_hbm.at[e, lo], wd_buf.at[s, lo], sem.at[s, 2]), 0),
                (pltpu.make_async_copy(wd_hbm.at[e, hi], wd_buf.at[s, hi], sem.at[s, 3]), 1))

    @pl.when(j == 0)
    def _():
        for c, prio in weight_copies(te_ref[0], 0):
            c.start(priority=prio)

    @pl.when(jnp.logical_and(valid, first_ref[j] == 1))
    def _():
        for c, _ in weight_copies(te_ref[j], slot_ref[j]):
            c.wait()

        @pl.when(nxt_ref[j] >= 0)
        def _():
            for c, prio in weight_copies(nxt_ref[j], 1 - slot_ref[j]):
                c.start(priority=prio)

    @pl.when(valid)
    def _():
        s = slot_ref[j]
        row = lax.broadcasted_iota(jnp.int32, xs_ref.shape, 0)
        lo, hi = _unpack_bf16_pairs(jnp.where(row < rows_ref[j], xs_ref[...], jnp.uint32(0)))
        xs = jnp.concatenate([lo.astype(BF16), hi.astype(BF16)], axis=1)
        gt = jnp.dot(xs, wg_buf[s].astype(BF16), preferred_element_type=F32)
        up = jnp.dot(xs, wu_buf[s].astype(BF16), preferred_element_type=F32)
        hid = (gt * _sigmoid(gt) * up).astype(BF16)
        ys_ref[...] = _pack_bf16_pairs(jnp.dot(hid, wd_buf[s].astype(BF16), preferred_element_type=F32))

    @pl.when(jnp.logical_not(valid))
    def _():
        ys_ref[...] = jnp.zeros(ys_ref.shape, ys_ref.dtype)


def _moe(tile_expert, n_valid, tile_first, tile_slot, tile_next, tile_rows, xs, wg, wu, wd):
    R, Dw = xs.shape
    D, F = wg.shape[1], wg.shape[2]
    tm = MOE_TM
    hbm = pl.BlockSpec(memory_space=pl.ANY)
    grid_spec = pltpu.PrefetchScalarGridSpec(
        num_scalar_prefetch=6,
        grid=(R // tm,),
        in_specs=[pl.BlockSpec((tm, Dw), lambda j, *_: (j, 0)), hbm, hbm, hbm],
        out_specs=pl.BlockSpec((tm, Dw), lambda j, *_: (j, 0)),
        scratch_shapes=[
            pltpu.VMEM((2, D, F), wg.dtype),
            pltpu.VMEM((2, D, F), wu.dtype),
            pltpu.VMEM((2, F, D), wd.dtype),
            pltpu.SemaphoreType.DMA((2, 4)),
        ],
    )
    return pl.pallas_call(
        _moe_kernel,
        grid_spec=grid_spec,
        out_shape=jax.ShapeDtypeStruct((R, Dw), jnp.uint32),
        compiler_params=_cparams(("arbitrary",)),
        name="moe_experts",
    )(tile_expert, n_valid, tile_first, tile_slot, tile_next, tile_rows, xs, wg, wu, wd)


def _final_kernel(x1_ref, y0_ref, y1_ref, cw_ref, g_ref, o_ref):
    cw = cw_ref[...]
    lo0, hi0 = _unpack_bf16_pairs(y0_ref[...])
    lo1, hi1 = _unpack_bf16_pairs(y1_ref[...])
    w0, w1 = cw[:, 0:1], cw[:, 1:2]
    y = jnp.concatenate([w0 * lo0 + w1 * lo1, w0 * hi0 + w1 * hi1], axis=1)
    x = x1_ref[...] + y
    ms = jnp.mean(x * x, axis=-1, keepdims=True)
    o_ref[...] = x * lax.rsqrt(ms + EPS) * g_ref[...]


def _final(x1, yw, cw, g):
    T, D = x1.shape
    tm = FIN_TM
    row = lambda w: pl.BlockSpec((tm, w), lambda m: (m, 0))
    slot1 = pl.BlockSpec((tm, D // 2), lambda m: (m + T // tm, 0))
    return pl.pallas_call(
        _final_kernel,
        grid=(T // tm,),
        in_specs=[row(D), row(D // 2), slot1, row(LANES), pl.BlockSpec((1, D), lambda m: (0, 0))],
        out_specs=row(D),
        out_shape=jax.ShapeDtypeStruct((T, D), F32),
        compiler_params=_cparams(("parallel",)),
        name="combine_final_norm",
    )(x1, yw, yw, cw, g)


SC_CORES, SC_SUBCORES = 2, 16
SC_CHUNK = 32


def _sc_gather_rows(table, idx):
    V, Dw = table.shape
    R = idx.shape[0]
    n_workers = SC_CORES * SC_SUBCORES
    ch = SC_CHUNK
    per_w = R // n_workers
    n_chunks = per_w // ch
    assert per_w * n_workers == R and n_chunks * ch == per_w and n_chunks % 2 == 0
    idx3 = idx.reshape(n_workers, n_chunks, ch)
    mesh = plsc.VectorSubcoreMesh(core_axis_name="c", subcore_axis_name="s")

    def body(table_hbm, idx_hbm, out_hbm, idx_v, rows_v, gsem, osem):
        wid = lax.axis_index("s") * SC_CORES + lax.axis_index("c")
        base = wid * per_w
        pltpu.sync_copy(idx_hbm.at[wid], idx_v)

        def gather(c, slot):
            return pltpu.make_async_copy(table_hbm.at[idx_v.at[c]], rows_v.at[slot], gsem.at[slot])

        def put(c, slot):
            return pltpu.make_async_copy(rows_v.at[slot], out_hbm.at[pl.ds(base + c * ch, ch)],
                                         osem.at[slot])

        gather(0, 0).start()

        @pl.loop(0, n_chunks, step=2)
        def _(c):
            @pl.when(c > 0)
            def _():
                put(c - 1, 1).wait()

            gather(c + 1, 1).start()
            gather(c, 0).wait()
            put(c, 0).start()
            put(c, 0).wait()

            @pl.when(c + 2 < n_chunks)
            def _():
                gather(c + 2, 0).start()

            gather(c + 1, 1).wait()
            put(c + 1, 1).start()

        put(n_chunks - 1, 1).wait()

    return pl.kernel(
        body,
        out_type=jax.ShapeDtypeStruct((R, Dw), table.dtype),
        mesh=mesh,
        scratch_types=[
            pltpu.VMEM((n_chunks, ch), jnp.int32),
            pltpu.VMEM((2, ch, Dw), table.dtype),
            pltpu.SemaphoreType.DMA((2,)),
            pltpu.SemaphoreType.DMA((2,)),
        ],
        name="sc_gather_rows",
    )(table, idx3)


def _sc_scatter_rows(table, idx, n_rows_out):
    V, Dw = table.shape
    K = idx.shape[0]
    n_workers = SC_CORES * SC_SUBCORES
    ch = SC_CHUNK
    per_w = V // n_workers
    n_chunks = per_w // ch
    assert K == 2 and per_w * n_workers == V and n_chunks * ch == per_w and n_chunks % 2 == 0
    idx4 = jnp.transpose(idx.reshape(K, n_workers, n_chunks, ch), (1, 0, 2, 3))
    mesh = plsc.VectorSubcoreMesh(core_axis_name="c", subcore_axis_name="s")

    def body(table_hbm, idx_hbm, out_hbm, idx_v, rows_v, lsem, ssem):
        wid = lax.axis_index("s") * SC_CORES + lax.axis_index("c")
        base = wid * per_w
        pltpu.sync_copy(idx_hbm.at[wid], idx_v)

        def load(c, slot):
            return pltpu.make_async_copy(table_hbm.at[pl.ds(base + c * ch, ch)], rows_v.at[slot],
                                         lsem.at[slot])

        def scatter(c, slot, k):
            return pltpu.make_async_copy(rows_v.at[slot], out_hbm.at[idx_v.at[k, c]], ssem.at[slot, k])

        load(0, 0).start()

        @pl.loop(0, n_chunks, step=2)
        def _(c):
            @pl.when(c > 0)
            def _():
                scatter(c - 1, 1, 0).wait()
                scatter(c - 1, 1, 1).wait()

            load(c + 1, 1).start()
            load(c, 0).wait()
            scatter(c, 0, 0).start()
            scatter(c, 0, 1).start()
            scatter(c, 0, 0).wait()
            scatter(c, 0, 1).wait()

            @pl.when(c + 2 < n_chunks)
            def _():
                load(c + 2, 0).start()

            load(c + 1, 1).wait()
            scatter(c + 1, 1, 0).start()
            scatter(c + 1, 1, 1).start()

        scatter(n_chunks - 1, 1, 0).wait()
        scatter(n_chunks - 1, 1, 1).wait()

    return pl.kernel(
        body,
        out_type=jax.ShapeDtypeStruct((n_rows_out, Dw), table.dtype),
        mesh=mesh,
        scratch_types=[
            pltpu.VMEM((K, n_chunks, ch), jnp.int32),
            pltpu.VMEM((2, ch, Dw), table.dtype),
            pltpu.SemaphoreType.DMA((2,)),
            pltpu.SemaphoreType.DMA((2, K)),
        ],
        name="sc_scatter_rows",
    )(table, idx4)


def _route(logits, b_group, b_router):
    T = logits.shape[0]
    G, E = N_GROUPS, EXPERTS_PER_GROUP
    gl = logits[:, :G] + b_group.astype(F32)
    gp = jax.nn.softmax(gl, axis=-1)
    gsel = jnp.argmax(gl, axis=-1)
    gw = jnp.take_along_axis(gp, gsel[:, None], axis=1)[:, 0]
    el = logits[:, G:G + G * E].reshape(T, G, E) + b_router.astype(F32)
    el_sel = jnp.take_along_axis(el, gsel[:, None, None], axis=1)[:, 0, :]
    top_v, top_i = lax.top_k(el_sel, TOP_K_INNER)
    top_w = jax.nn.softmax(top_v, axis=-1)
    eid = (gsel[:, None] * E + top_i).astype(jnp.int32)
    cw = gw[:, None] * top_w
    return eid, cw


def _dispatch_plan(eid, tm, n_tiles):
    T2 = eid.size
    flat_e = eid.reshape(-1)
    onehot = (flat_e[:, None] == jnp.arange(N_EXPERTS, dtype=jnp.int32)[None, :]).astype(jnp.int32)
    csum = jnp.cumsum(onehot, axis=0)
    rank = jnp.sum((csum - onehot) * onehot, axis=1)
    counts = csum[-1]
    tiles_per_e = (counts + tm - 1) // tm
    tile_end = jnp.cumsum(tiles_per_e)
    row_start = (tile_end - tiles_per_e) * tm
    pos = (row_start[flat_e] + rank).astype(jnp.int32)
    n_valid = tile_end[-1].astype(jnp.int32)
    tile_ids = jnp.arange(n_tiles, dtype=jnp.int32)
    tile_expert = jnp.searchsorted(tile_end, jnp.minimum(tile_ids, n_valid - 1), side="right")
    tile_expert = jnp.minimum(tile_expert, N_EXPERTS - 1).astype(jnp.int32)
    valid = tile_ids < n_valid
    prev_expert = jnp.concatenate([jnp.full((1,), -1, jnp.int32), tile_expert[:-1]])
    tile_first = jnp.logical_and(valid, tile_expert != prev_expert).astype(jnp.int32)
    tile_slot = ((jnp.cumsum(tile_first) - 1) % 2).astype(jnp.int32)
    next_start = tile_end[tile_expert]
    tile_next = jnp.where(next_start < n_valid, tile_expert[jnp.minimum(next_start, n_tiles - 1)],
                          -1).astype(jnp.int32)
    tile_start = tile_end - tiles_per_e
    tile_rows = jnp.clip(counts[tile_expert] - (tile_ids - tile_start[tile_expert]) * tm, 0, tm)
    tile_rows = jnp.where(valid, tile_rows, 0).astype(jnp.int32)
    tiles = (tile_expert, n_valid.reshape(1), tile_first, tile_slot, tile_next, tile_rows)
    return pos.reshape(eid.shape), tiles


def kernel(x, rel_bias, ln_mix_g, w_in, conv_w, conv_b, b_i, b_f, lam_q1, lam_k1, lam_q2, lam_k2,
           diff_norm_g, mlstm_norm_g, w_out, ln_ffn_g, w_group, b_group, w_router, b_router,
           w_gate, w_up, w_down, ln_f_g):
    B, S, D = x.shape
    T = B * S
    depth = w_in.shape[0]
    assert depth == 1, "the final rmsnorm is fused into the single layer's combine kernel"
    Hm = N_MLSTM_HEADS
    n_main = w_in.shape[2] - 2 * Hm
    n_diff = N_DIFF_HEADS * 2 * DIFF_HEAD_DIM
    xf = x.reshape(T, D)

    for l in range(depth):
        lambda_init = 0.8 - 0.6 * math.exp(-0.3 * l)
        w_main = w_in[l, :, :n_main].astype(BF16)
        wgt = w_in[l, :, n_main:]
        w_gates = jnp.zeros((D, 2 * LANES), F32).at[:, :Hm].set(wgt[:, :Hm]).at[:, LANES:LANES + Hm].set(
            wgt[:, Hm:]).astype(BF16)
        bi_row = jnp.zeros((1, LANES), F32).at[0, :Hm].set(b_i[l].astype(F32))
        bf_row = jnp.zeros((1, LANES), F32).at[0, :Hm].set(b_f[l].astype(F32))
        lam = (jnp.exp(jnp.sum(lam_q1[l].astype(F32) * lam_k1[l].astype(F32)))
               - jnp.exp(jnp.sum(lam_q2[l].astype(F32) * lam_k2[l].astype(F32))) + lambda_init)
        tq = ATT_TQ
        assert tq >= MAX_DISTANCE and tq % CHUNK == 0
        rb = rel_bias.astype(F32)
        log2e = math.log2(math.e)
        xx = jnp.arange(2 * tq, dtype=jnp.int32)
        rel_vec = jnp.stack([-tq + tq - 1 - xx, tq - 1 - xx], axis=0)
        bias_vecs = jnp.take(rb, _t5_bucket(rel_vec), axis=0) * log2e
        bias_vecs = jnp.transpose(bias_vecs, (2, 0, 1))[:, :, None, :]
        cfar = rb[N_BUCKETS // 2 - 1] * log2e
        scal = jnp.concatenate([lam.reshape(1), cfar]).astype(F32)
        col_scale = jnp.ones((1, n_main), F32).at[:, :n_diff].set(DIFF_HEAD_DIM ** -0.5 * log2e)

        proj, gates = _proj(xf, ln_mix_g[l].reshape(1, D).astype(F32), w_main, col_scale, w_gates)
        proj3 = proj.reshape(B, S, n_main)
        a = _diff_attention(proj3, scal, bias_vecs, diff_norm_g[l].reshape(1, n_diff).astype(F32),
                            lambda_init)
        hm = _mlstm(proj3, gates.reshape(B, S, 2 * LANES), conv_w[l].astype(F32),
                    conv_b[l].reshape(1, -1).astype(F32), bi_row, bf_row,
                    mlstm_norm_g[l].reshape(1, -1).astype(F32))

        wo = w_out[l].astype(BF16)
        G, E = N_GROUPS, EXPERTS_PER_GROUP
        wr = jnp.zeros((D, LANES), F32).at[:, :G].set(w_group[l].astype(F32)).at[:, G:G + G * E].set(
            jnp.transpose(w_router[l].astype(F32), (1, 0, 2)).reshape(D, G * E)).astype(BF16)
        x1, h2, logits = _out_proj(xf, a.reshape(T, n_diff), hm.reshape(T, -1), wo[:n_diff], wo[n_diff:],
                                   ln_ffn_g[l].reshape(1, D).astype(F32), wr)

        eid, cw = _route(logits, b_group[l], b_router[l])
        n_tiles = (T * TOP_K_INNER) // MOE_TM + N_EXPERTS
        pos, tiles = _dispatch_plan(eid, MOE_TM, n_tiles)
        xs = _sc_scatter_rows(h2, pos.T, n_tiles * MOE_TM)
        Fe = w_gate.shape[-1]
        ys = _moe(*tiles, xs, w_gate[l].reshape(N_EXPERTS, D, Fe),
                  w_up[l].reshape(N_EXPERTS, D, Fe), w_down[l].reshape(N_EXPERTS, Fe, D))
        yw = _sc_gather_rows(ys, pos.T.reshape(-1))
        cwp = jnp.zeros((T, LANES), F32).at[:, :TOP_K_INNER].set(cw)
        xf = _final(x1, yw, cwp, ln_f_g.reshape(1, D).astype(F32))
    return xf.reshape(B, S, D)
```

```python
import functools
import math

import jax
import jax.numpy as jnp
from jax import lax
from jax.experimental import pallas as pl
from jax.experimental.pallas import tpu as pltpu
from jax.experimental.pallas import tpu_sc as plsc

F32 = jnp.float32
BF16 = jnp.bfloat16

EPS = 1e-6
CHUNK = 64
DIFF_HEAD_DIM = 64
N_DIFF_HEADS = 8
MLSTM_HEAD_DIM = 128
N_MLSTM_HEADS = 8
CONV_WIDTH = 4
N_BUCKETS = 32
MAX_DISTANCE = 128
N_GROUPS = 4
EXPERTS_PER_GROUP = 8
N_EXPERTS = N_GROUPS * EXPERTS_PER_GROUP
TOP_K_INNER = 2
LANES = 128
NEG_BIG = -1e30

VMEM_LIMIT = 56 * 1024 * 1024

PROJ_TM, PROJ_TN = 1024, 1024
ATT_TQ = 512
MLSTM_TS = 512
OUT_TM = 512
MOE_TM = 256
FIN_TM = 512


def _cparams(sem):
    return pltpu.CompilerParams(dimension_semantics=sem, vmem_limit_bytes=VMEM_LIMIT)


_HI_MASK = 0xFFFF0000


def _pack_bf16_pairs(x):
    half = x.shape[-1] // 2
    xb = x.astype(BF16).astype(F32)
    lo = pltpu.bitcast(xb[:, :half], jnp.uint32)
    hi = pltpu.bitcast(xb[:, half:], jnp.uint32)
    return (hi & jnp.uint32(_HI_MASK)) | (lo >> 16)


def _unpack_bf16_pairs(w):
    lo = pltpu.bitcast(w << 16, F32)
    hi = pltpu.bitcast(w & jnp.uint32(_HI_MASK), F32)
    return lo, hi


def _proj_kernel(x_ref, g_ref, w_ref, cs_ref, wg_ref, o_ref, og_ref, h_ref):
    @pl.when(pl.program_id(1) == 0)
    def _():
        x = x_ref[...]
        ms = jnp.mean(x * x, axis=-1, keepdims=True)
        h = (x * lax.rsqrt(ms + EPS) * g_ref[...]).astype(BF16)
        h_ref[...] = h
        og_ref[...] = jnp.dot(h, wg_ref[...], preferred_element_type=F32)

    o_ref[...] = (jnp.dot(h_ref[...], w_ref[...], preferred_element_type=F32) * cs_ref[...]).astype(o_ref.dtype)


def _proj(x2, g, w_main, col_scale, w_gates):
    T, D = x2.shape
    N = w_main.shape[1]
    NG = w_gates.shape[1]
    return pl.pallas_call(
        _proj_kernel,
        grid=(T // PROJ_TM, N // PROJ_TN),
        in_specs=[
            pl.BlockSpec((PROJ_TM, D), lambda m, n: (m, 0)),
            pl.BlockSpec((1, D), lambda m, n: (0, 0)),
            pl.BlockSpec((D, PROJ_TN), lambda m, n: (0, n)),
            pl.BlockSpec((1, PROJ_TN), lambda m, n: (0, n)),
            pl.BlockSpec((D, NG), lambda m, n: (0, 0)),
        ],
        out_specs=[
            pl.BlockSpec((PROJ_TM, PROJ_TN), lambda m, n: (m, n)),
            pl.BlockSpec((PROJ_TM, NG), lambda m, n: (m, 0)),
        ],
        out_shape=[
            jax.ShapeDtypeStruct((T, N), BF16),
            jax.ShapeDtypeStruct((T, NG), F32),
        ],
        scratch_shapes=[pltpu.VMEM((PROJ_TM, D), BF16)],
        compiler_params=_cparams(("parallel", "arbitrary")),
        name="rms_in_proj",
    )(x2, g, w_main, col_scale, w_gates)


def _t5_bucket(rel):
    half = N_BUCKETS // 2
    max_exact = half // 2
    ret = jnp.where(rel > 0, half, 0)
    n = jnp.abs(rel)
    nf = jnp.maximum(n, 1).astype(F32)
    large = max_exact + (jnp.log(nf / max_exact) / math.log(MAX_DISTANCE / max_exact)
                         * (half - max_exact)).astype(jnp.int32)
    large = jnp.minimum(large, half - 1)
    return ret + jnp.where(n < max_exact, n, large)


def _attn_kernel(scal_ref, q_ref, k_ref, v_ref, bias_ref, g_ref, o_ref, m_ref, l_ref, acc_ref,
                 s0_ref, s1_ref, ml0_ref, ml1_ref, bt_ref, *, lambda_init):
    h = pl.program_id(1)
    qi = pl.program_id(2)
    tq = ATT_TQ
    lam = scal_ref[0]
    cfar = scal_ref[1 + h]

    q = q_ref[0]
    lane = lax.broadcasted_iota(jnp.int32, q.shape, 1)
    zero = jnp.zeros_like(q)
    qs = jnp.concatenate([jnp.where(lane < DIFF_HEAD_DIM, q, zero),
                          jnp.where(lane >= DIFF_HEAD_DIM, q, zero)], axis=0)

    m_ref[...] = jnp.full(m_ref.shape, NEG_BIG, F32)
    l_ref[...] = jnp.zeros(l_ref.shape, F32)
    acc_ref[...] = jnp.zeros(acc_ref.shape, F32)

    @pl.when(qi == 0)
    def _():
        kj = lax.broadcasted_iota(jnp.int32, (tq, tq), 0)
        qq = lax.broadcasted_iota(jnp.int32, (tq, tq), 1)
        allowed = (kj // CHUNK) <= (qq // CHUNK)
        for d in range(2):
            rows = jnp.broadcast_to(bias_ref[0, d], (tq, 2 * tq))
            tile = pltpu.roll(rows, tq + 1, 1, stride=1, stride_axis=0)[:, :tq]
            if d == 1:
                tile = jnp.where(allowed, tile, NEG_BIG)
            bt_ref[d] = tile

    bufs = ((s0_ref, ml0_ref), (s1_ref, ml1_ref))

    def score(ki, bias, slot):
        s_ref, ml_ref = bufs[slot]
        start = pl.multiple_of(ki * tq, tq)
        kt = k_ref[0, pl.ds(start, tq), :]
        s = lax.dot_general(kt, qs, (((1,), (1,)), ((), ())), preferred_element_type=F32)
        if bias is not None:
            s = s + jnp.concatenate([bias, bias], axis=1)
        s_ref[...] = s
        ml_ref[...] = jnp.max(s, axis=0, keepdims=True)

    def accumulate(ki, shift, slot):
        s_ref, ml_ref = bufs[slot]
        start = pl.multiple_of(ki * tq, tq)
        vt = v_ref[0, pl.ds(start, tq), :]
        m_old = m_ref[...]
        m_new = jnp.maximum(m_old, ml_ref[...] + shift)
        alpha = jnp.exp2(m_old - m_new)
        p = jnp.exp2(s_ref[...] - (m_new - shift))
        l_ref[...] = alpha * l_ref[...] + jnp.sum(p, axis=0, keepdims=True)
        pv = lax.dot_general(vt, p.astype(BF16), (((0,), (0,)), ((), ())), preferred_element_type=F32)
        acc_ref[...] = alpha * acc_ref[...] + pv
        m_ref[...] = m_new

    n_far = qi - 1
    score(qi, bt_ref[1], 0)

    @pl.when(qi == 0)
    def _():
        accumulate(qi, 0.0, 0)

    @pl.when(qi >= 1)
    def _():
        accumulate(qi, 0.0, 0)
        score(qi - 1, bt_ref[0], 1)

    @pl.when(qi == 1)
    def _():
        accumulate(qi - 1, 0.0, 1)

    @pl.when(qi >= 2)
    def _():
        accumulate(qi - 1, 0.0, 1)
        score(0, None, 0)
        trips = (n_far - 1) // 2

        def pair(j, c):
            accumulate(2 * j, cfar, 0)
            score(2 * j + 1, None, 1)
            accumulate(2 * j + 1, cfar, 1)
            score(2 * j + 2, None, 0)
            return c

        lax.fori_loop(0, trips, pair, 0)
        last = 2 * trips

        @pl.when(n_far - last == 2)
        def _():
            accumulate(last, cfar, 0)
            score(last + 1, None, 1)
            accumulate(last + 1, cfar, 1)

        @pl.when(n_far - last == 1)
        def _():
            accumulate(last, cfar, 0)

    acc = acc_ref[...] * (1.0 / l_ref[...])
    o_t = acc[:, 0:tq] - lam * acc[:, tq:2 * tq]
    ms = jnp.mean(o_t * o_t, axis=0, keepdims=True)
    y = (o_t * lax.rsqrt(ms + EPS)).T * (g_ref[...] * (1.0 - lambda_init))
    o_ref[0] = y.astype(o_ref.dtype)


def _diff_attention(proj3, scal, bias_vecs, gnorm, lambda_init):
    B, S, _ = proj3.shape
    H = N_DIFF_HEADS
    tq = ATT_TQ
    kern = functools.partial(_attn_kernel, lambda_init=lambda_init)
    return pl.pallas_call(
        kern,
        grid=(B, H, S // tq),
        in_specs=[
            pl.BlockSpec(memory_space=pltpu.SMEM),
            pl.BlockSpec((1, tq, LANES), lambda b, h, i: (b, i, h)),
            pl.BlockSpec((1, S, LANES), lambda b, h, i: (b, 0, H + h)),
            pl.BlockSpec((1, S, LANES), lambda b, h, i: (b, 0, 2 * H + h)),
            pl.BlockSpec((1, 2, 1, 2 * tq), lambda b, h, i: (h, 0, 0, 0)),
            pl.BlockSpec((1, LANES), lambda b, h, i: (0, h)),
        ],
        out_specs=pl.BlockSpec((1, tq, LANES), lambda b, h, i: (b, i, h)),
        out_shape=jax.ShapeDtypeStruct((B, S, H * LANES), BF16),
        scratch_shapes=[
            pltpu.VMEM((1, 2 * tq), F32),
            pltpu.VMEM((1, 2 * tq), F32),
            pltpu.VMEM((LANES, 2 * tq), F32),
            pltpu.VMEM((tq, 2 * tq), F32),
            pltpu.VMEM((tq, 2 * tq), F32),
            pltpu.VMEM((1, 2 * tq), F32),
            pltpu.VMEM((1, 2 * tq), F32),
            pltpu.VMEM((2, tq, tq), F32),
        ],
        compiler_params=_cparams(("parallel", "parallel", "arbitrary")),
        name="diff_attention",
    )(scal, proj3, proj3, proj3, bias_vecs, gnorm)


def _log_sigmoid(x):
    return jnp.minimum(x, 0.0) - jnp.log(1.0 + jnp.exp(-jnp.abs(x)))


def _sigmoid(x):
    return 1.0 / (1.0 + jnp.exp(-x))


def _mlstm_kernel(q_ref, k_ref, v_ref, o_ref, gi_ref, gf_ref, cw_ref, cb_ref, bi_ref, bf_ref, gn_ref,
                  out_ref, qext_ref, kext_ref, ct_ref, n_ref, m_ref):
    sb = pl.program_id(1)
    L = CHUNK
    dh = MLSTM_HEAD_DIM
    H = N_MLSTM_HEADS
    ts = MLSTM_TS
    pad = 8

    @pl.when(sb == 0)
    def _():
        qext_ref[0:pad, :] = jnp.zeros((pad, H * dh), F32)
        kext_ref[0:pad, :] = jnp.zeros((pad, H * dh), F32)
        ct_ref[...] = jnp.zeros(ct_ref.shape, F32)
        n_ref[...] = jnp.zeros(n_ref.shape, F32)
        m_ref[...] = jnp.zeros(m_ref.shape, F32)

    qext_ref[pad:pad + ts, :] = q_ref[0].astype(F32)
    kext_ref[pad:pad + ts, :] = k_ref[0].astype(F32)

    row = lax.broadcasted_iota(jnp.int32, (L, L), 0)
    col = lax.broadcasted_iota(jnp.int32, (L, L), 1)
    tril = col <= row
    ltri = tril.astype(F32)

    def conv_silu(ext_ref, base, h, off):
        win = ext_ref[pl.ds(base, L + pad), h * dh:(h + 1) * dh]
        w = cw_ref[:, off + h * dh:off + (h + 1) * dh]
        y = cb_ref[:, off + h * dh:off + (h + 1) * dh]
        for j in range(CONV_WIDTH):
            lo = pad - (CONV_WIDTH - 1) + j
            y = y + w[j:j + 1, :] * win[lo:lo + L, :]
        return y * _sigmoid(y)

    def chunk_body(c, carry):
        base = pl.multiple_of(c * L, L)
        li = gi_ref[0, pl.ds(base, L), :] + bi_ref[...]
        logf = _log_sigmoid(gf_ref[0, pl.ds(base, L), :] + bf_ref[...])
        b = jnp.dot(ltri, logf, preferred_element_type=F32, precision=lax.Precision.HIGHEST)
        a = li - b
        g_row = b[L - 1:L, :]
        m_row = m_ref[...]
        m_new_row = g_row + jnp.maximum(m_row, jnp.max(a, axis=0, keepdims=True))
        a_t = a.T

        for h in range(H):
            qc = conv_silu(qext_ref, base, h, 0)
            kc = conv_silu(kext_ref, base, h, H * dh) * (dh ** -0.5)
            qb = qc.astype(BF16)
            kb = kc.astype(BF16)
            vb = v_ref[0, pl.ds(base, L), h * dh:(h + 1) * dh]

            a_row = a_t[h:h + 1, :]
            a_col = a[:, h:h + 1]
            b_col = b[:, h:h + 1]
            m_prev = m_row[:, h:h + 1]
            m_next = m_new_row[:, h:h + 1]
            g_h = g_row[:, h:h + 1]

            amat = jnp.where(tril, a_row, NEG_BIG)
            mcol = jnp.maximum(jnp.max(amat, axis=-1, keepdims=True), m_prev)
            wts = jnp.exp(amat - mcol)
            inter = jnp.exp(m_prev - mcol)

            s = lax.dot_general(qb, kb, (((1,), (1,)), ((), ())), preferred_element_type=F32)
            sqk = s * wts
            ct = ct_ref[h]
            nrow = n_ref[h:h + 1, :]
            num = (jnp.dot(sqk.astype(BF16), vb, preferred_element_type=F32)
                   + inter * jnp.dot(qb, ct.astype(BF16), preferred_element_type=F32))
            den = (jnp.sum(sqk, axis=-1, keepdims=True)
                   + inter * jnp.sum(qb.astype(F32) * nrow, axis=-1, keepdims=True))
            hv = num / jnp.maximum(jnp.abs(den), jnp.exp(-(b_col + mcol)))

            wt = jnp.exp(g_h + a_col - m_next)
            decay = jnp.exp(g_h + m_prev - m_next)
            wv = (wt * vb.astype(F32)).astype(BF16)
            ct_ref[h] = decay * ct + lax.dot_general(kb, wv, (((0,), (0,)), ((), ())),
                                                     preferred_element_type=F32)
            n_ref[h:h + 1, :] = decay * nrow + jnp.sum(wt * kb.astype(F32), axis=0, keepdims=True)

            ms = jnp.mean(hv * hv, axis=-1, keepdims=True)
            y = hv * lax.rsqrt(ms + EPS) * gn_ref[:, h * dh:(h + 1) * dh]
            og = o_ref[0, pl.ds(base, L), h * dh:(h + 1) * dh].astype(F32)
            out_ref[0, pl.ds(base, L), h * dh:(h + 1) * dh] = (y * _sigmoid(og)).astype(out_ref.dtype)

        m_ref[...] = m_new_row
        return carry

    lax.fori_loop(0, ts // L, chunk_body, 0)

    qext_ref[0:pad, :] = qext_ref[ts:ts + pad, :]
    kext_ref[0:pad, :] = kext_ref[ts:ts + pad, :]


def _mlstm(proj3, gates3, conv_w, conv_b, bi_row, bf_row, gnorm):
    B, S, _ = proj3.shape
    W = N_MLSTM_HEADS * MLSTM_HEAD_DIM
    ts = MLSTM_TS
    first = 3
    blk = lambda j: pl.BlockSpec((1, ts, W), lambda b, s: (b, s, j))
    full = lambda shape: pl.BlockSpec(shape, lambda b, s: (0,) * len(shape))
    return pl.pallas_call(
        _mlstm_kernel,
        grid=(B, S // ts),
        in_specs=[
            blk(first), blk(first + 1), blk(first + 2), blk(first + 3),
            pl.BlockSpec((1, ts, LANES), lambda b, s: (b, s, 0)),
            pl.BlockSpec((1, ts, LANES), lambda b, s: (b, s, 1)),
            full((CONV_WIDTH, 2 * W)), full((1, 2 * W)),
            full((1, LANES)), full((1, LANES)), full((1, W)),
        ],
        out_specs=pl.BlockSpec((1, ts, W), lambda b, s: (b, s, 0)),
        out_shape=jax.ShapeDtypeStruct((B, S, W), BF16),
        scratch_shapes=[
            pltpu.VMEM((ts + 8, W), F32),
            pltpu.VMEM((ts + 8, W), F32),
            pltpu.VMEM((N_MLSTM_HEADS, MLSTM_HEAD_DIM, MLSTM_HEAD_DIM), F32),
            pltpu.VMEM((N_MLSTM_HEADS, MLSTM_HEAD_DIM), F32),
            pltpu.VMEM((1, LANES), F32),
        ],
        compiler_params=_cparams(("parallel", "arbitrary")),
        name="mlstm",
    )(proj3, proj3, proj3, proj3, gates3, gates3, conv_w, conv_b, bi_row, bf_row, gnorm)


def _out_kernel(x_ref, a_ref, hm_ref, wa_ref, wm_ref, g_ref, wr_ref, x1_ref, h2_ref, lg_ref):
    y = (jnp.dot(a_ref[...], wa_ref[...], preferred_element_type=F32)
         + jnp.dot(hm_ref[...], wm_ref[...], preferred_element_type=F32))
    x1 = x_ref[...] + y
    x1_ref[...] = x1
    ms = jnp.mean(x1 * x1, axis=-1, keepdims=True)
    h2 = x1 * lax.rsqrt(ms + EPS) * g_ref[...]
    h2_ref[...] = _pack_bf16_pairs(h2)
    lg_ref[...] = jnp.dot(h2.astype(BF16), wr_ref[...], preferred_element_type=F32)


def _out_proj(x2, a2, hm2, wa, wm, g, wr):
    T, D = x2.shape
    W = a2.shape[1]
    tm = OUT_TM
    const = lambda shape: pl.BlockSpec(shape, lambda m: (0, 0), pipeline_mode=pl.Buffered(1))
    return pl.pallas_call(
        _out_kernel,
        grid=(T // tm,),
        in_specs=[
            pl.BlockSpec((tm, D), lambda m: (m, 0)),
            pl.BlockSpec((tm, W), lambda m: (m, 0)),
            pl.BlockSpec((tm, W), lambda m: (m, 0)),
            const((W, D)), const((W, D)), const((1, D)), const((D, LANES)),
        ],
        out_specs=[
            pl.BlockSpec((tm, D), lambda m: (m, 0)),
            pl.BlockSpec((tm, D // 2), lambda m: (m, 0)),
            pl.BlockSpec((tm, LANES), lambda m: (m, 0)),
        ],
        out_shape=[
            jax.ShapeDtypeStruct((T, D), F32),
            jax.ShapeDtypeStruct((T, D // 2), jnp.uint32),
            jax.ShapeDtypeStruct((T, LANES), F32),
        ],
        compiler_params=_cparams(("parallel",)),
        name="out_proj_router",
    )(x2, a2, hm2, wa, wm, g, wr)


def _moe_kernel(te_ref, nv_ref, first_ref, slot_ref, nxt_ref, rows_ref, xs_ref, wg_hbm, wu_hbm, wd_hbm,
                ys_ref, wg_buf, wu_buf, wd_buf, sem):
    j = pl.program_id(0)
    valid = j < nv_ref[0]

    half_f = wd_buf.shape[1] // 2

    def weight_copies(e, s):
        lo, hi = pl.ds(0, half_f), pl.ds(half_f, half_f)
        return ((pltpu.make_async_copy(wg_hbm.at[e], wg_buf.at[s], sem.at[s, 0]), 0),
                (pltpu.make_async_copy(wu_hbm.at[e], wu_buf.at[s], sem.at[s, 1]), 1),
                (pltpu.make_async_copy(wd_hbm.at[e, lo], wd_buf.at[s, lo], sem.at[s, 2]), 0),
                (pltpu.make_async_copy(wd_hbm.at[e, hi], wd_buf.at[s, hi], sem.at[s, 3]), 1))

    @pl.when(j == 0)
    def _():
        for c, prio in weight_copies(te_ref[0], 0):
            c.start(priority=prio)

    @pl.when(jnp.logical_and(valid, first_ref[j] == 1))
    def _():
        for c, _ in weight_copies(te_ref[j], slot_ref[j]):
            c.wait()

        @pl.when(nxt_ref[j] >= 0)
        def _():
            for c, prio in weight_copies(nxt_ref[j], 1 - slot_ref[j]):
                c.start(priority=prio)

    @pl.when(valid)
    def _():
        s = slot_ref[j]
        row = lax.broadcasted_iota(jnp.int32, xs_ref.shape, 0)
        lo, hi = _unpack_bf16_pairs(jnp.where(row < rows_ref[j], xs_ref[...], jnp.uint32(0)))
        xs = jnp.concatenate([lo.astype(BF16), hi.astype(BF16)], axis=1)
        gt = jnp.dot(xs, wg_buf[s].astype(BF16), preferred_element_type=F32)
        up = jnp.dot(xs, wu_buf[s].astype(BF16), preferred_element_type=F32)
        hid = (gt * _sigmoid(gt) * up).astype(BF16)
        ys_ref[...] = _pack_bf16_pairs(jnp.dot(hid, wd_buf[s].astype(BF16), preferred_element_type=F32))

    @pl.when(jnp.logical_not(valid))
    def _():
        ys_ref[...] = jnp.zeros(ys_ref.shape, ys_ref.dtype)


def _moe(tile_expert, n_valid, tile_first, tile_slot, tile_next, tile_rows, xs, wg, wu, wd):
    R, Dw = xs.shape
    D, F = wg.shape[1], wg.shape[2]
    tm = MOE_TM
    hbm = pl.BlockSpec(memory_space=pl.ANY)
    grid_spec = pltpu.PrefetchScalarGridSpec(
        num_scalar_prefetch=6,
        grid=(R // tm,),
        in_specs=[pl.BlockSpec((tm, Dw), lambda j, *_: (j, 0)), hbm, hbm, hbm],
        out_specs=pl.BlockSpec((tm, Dw), lambda j, *_: (j, 0)),
        scratch_shapes=[
            pltpu.VMEM((2, D, F), wg.dtype),
            pltpu.VMEM((2, D, F), wu.dtype),
            pltpu.VMEM((2, F, D), wd.dtype),
            pltpu.SemaphoreType.DMA((2, 4)),
        ],
    )
    return pl.pallas_call(
        _moe_kernel,
        grid_spec=grid_spec,
        out_shape=jax.ShapeDtypeStruct((R, Dw), jnp.uint32),
        compiler_params=_cparams(("arbitrary",)),
        name="moe_experts",
    )(tile_expert, n_valid, tile_first, tile_slot, tile_next, tile_rows, xs, wg, wu, wd)


def _final_kernel(x1_ref, y0_ref, y1_ref, cw_ref, g_ref, o_ref):
    cw = cw_ref[...]
    lo0, hi0 = _unpack_bf16_pairs(y0_ref[...])
    lo1, hi1 = _unpack_bf16_pairs(y1_ref[...])
    w0, w1 = cw[:, 0:1], cw[:, 1:2]
    y = jnp.concatenate([w0 * lo0 + w1 * lo1, w0 * hi0 + w1 * hi1], axis=1)
    x = x1_ref[...] + y
    ms = jnp.mean(x * x, axis=-1, keepdims=True)
    o_ref[...] = x * lax.rsqrt(ms + EPS) * g_ref[...]


def _final(x1, yw, cw, g):
    T, D = x1.shape
    tm = FIN_TM
    row = lambda w: pl.BlockSpec((tm, w), lambda m: (m, 0))
    slot1 = pl.BlockSpec((tm, D // 2), lambda m: (m + T // tm, 0))
    return pl.pallas_call(
        _final_kernel,
        grid=(T // tm,),
        in_specs=[row(D), row(D // 2), slot1, row(LANES), pl.BlockSpec((1, D), lambda m: (0, 0))],
        out_specs=row(D),
        out_shape=jax.ShapeDtypeStruct((T, D), F32),
        compiler_params=_cparams(("parallel",)),
        name="combine_final_norm",
    )(x1, yw, yw, cw, g)


SC_CORES, SC_SUBCORES = 2, 16
SC_CHUNK = 32


def _sc_gather_rows(table, idx):
    V, Dw = table.shape
    R = idx.shape[0]
    n_workers = SC_CORES * SC_SUBCORES
    ch = SC_CHUNK
    per_w = R // n_workers
    n_chunks = per_w // ch
    assert per_w * n_workers == R and n_chunks * ch == per_w and n_chunks % 2 == 0
    idx3 = idx.reshape(n_workers, n_chunks, ch)
    mesh = plsc.VectorSubcoreMesh(core_axis_name="c", subcore_axis_name="s")

    def body(table_hbm, idx_hbm, out_hbm, idx_v, rows_v, gsem, osem):
        wid = lax.axis_index("s") * SC_CORES + lax.axis_index("c")
        base = wid * per_w
        pltpu.sync_copy(idx_hbm.at[wid], idx_v)

        def gather(c, slot):
            return pltpu.make_async_copy(table_hbm.at[idx_v.at[c]], rows_v.at[slot], gsem.at[slot])

        def put(c, slot):
            return pltpu.make_async_copy(rows_v.at[slot], out_hbm.at[pl.ds(base + c * ch, ch)],
                                         osem.at[slot])

        gather(0, 0).start()

        @pl.loop(0, n_chunks, step=2)
        def _(c):
            @pl.when(c > 0)
            def _():
                put(c - 1, 1).wait()

            gather(c + 1, 1).start()
            gather(c, 0).wait()
            put(c, 0).start()
            put(c, 0).wait()

            @pl.when(c + 2 < n_chunks)
            def _():
                gather(c + 2, 0).start()

            gather(c + 1, 1).wait()
            put(c + 1, 1).start()

        put(n_chunks - 1, 1).wait()

    return pl.kernel(
        body,
        out_type=jax.ShapeDtypeStruct((R, Dw), table.dtype),
        mesh=mesh,
        scratch_types=[
            pltpu.VMEM((n_chunks, ch), jnp.int32),
            pltpu.VMEM((2, ch, Dw), table.dtype),
            pltpu.SemaphoreType.DMA((2,)),
            pltpu.SemaphoreType.DMA((2,)),
        ],
        name="sc_gather_rows",
    )(table, idx3)


def _sc_scatter_rows(table, idx, n_rows_out):
    V, Dw = table.shape
    K = idx.shape[0]
    n_workers = SC_CORES * SC_SUBCORES
    ch = SC_CHUNK
    per_w = V // n_workers
    n_chunks = per_w // ch
    assert K == 2 and per_w * n_workers == V and n_chunks * ch == per_w and n_chunks % 2 == 0
    idx4 = jnp.transpose(idx.reshape(K, n_workers, n_chunks, ch), (1, 0, 2, 3))
    mesh = plsc.VectorSubcoreMesh(core_axis_name="c", subcore_axis_name="s")

    def body(table_hbm, idx_hbm, out_hbm, idx_v, rows_v, lsem, ssem):
        wid = lax.axis_index("s") * SC_CORES + lax.axis_index("c")
        base = wid * per_w
        pltpu.sync_copy(idx_hbm.at[wid], idx_v)

        def load(c, slot):
            return pltpu.make_async_copy(table_hbm.at[pl.ds(base + c * ch, ch)], rows_v.at[slot],
                                         lsem.at[slot])

        def scatter(c, slot, k):
            return pltpu.make_async_copy(rows_v.at[slot], out_hbm.at[idx_v.at[k, c]], ssem.at[slot, k])

        load(0, 0).start()

        @pl.loop(0, n_chunks, step=2)
        def _(c):
            @pl.when(c > 0)
            def _():
                scatter(c - 1, 1, 0).wait()
                scatter(c - 1, 1, 1).wait()

            load(c + 1, 1).start()
            load(c, 0).wait()
            scatter(c, 0, 0).start()
            scatter(c, 0, 1).start()
            scatter(c, 0, 0).wait()
            scatter(c, 0, 1).wait()

            @pl.when(c + 2 < n_chunks)
            def _():
                load(c + 2, 0).start()

            load(c + 1, 1).wait()
            scatter(c + 1, 1, 0).start()
            scatter(c + 1, 1, 1).start()

        scatter(n_chunks - 1, 1, 0).wait()
        scatter(n_chunks - 1, 1, 1).wait()

    return pl.kernel(
        body,
        out_type=jax.ShapeDtypeStruct((n_rows_out, Dw), table.dtype),
        mesh=mesh,
        scratch_types=[
            pltpu.VMEM((K, n_chunks, ch), jnp.int32),
            pltpu.VMEM((2, ch, Dw), table.dtype),
            pltpu.SemaphoreType.DMA((2,)),
            pltpu.SemaphoreType.DMA((2, K)),
        ],
        name="sc_scatter_rows",
    )(table, idx4)


ROUTE_BLK = 256


def _route_kernel(lg_ref, bias_ref, pos_ref, cw_ref, cnt_ref, oh_ref, pre_ref, *, tm):
    T = lg_ref.shape[0]
    G, E, NE = N_GROUPS, EXPERTS_PER_GROUP, N_EXPERTS
    blk = ROUTE_BLK
    lane = lax.broadcasted_iota(jnp.int32, (blk, LANES), 1).astype(F32)
    lane1 = lax.broadcasted_iota(jnp.int32, (1, LANES), 1)
    r_i = lax.broadcasted_iota(jnp.int32, (blk, blk), 0)
    c_i = lax.broadcasted_iota(jnp.int32, (blk, blk), 1)
    ltri = (c_i < r_i).astype(BF16)

    def first_argmax(v):
        mx = jnp.max(v, axis=1, keepdims=True)
        return mx, jnp.min(jnp.where(v == mx, lane, float(LANES)), axis=1, keepdims=True)

    def phase1(b, carry):
        rows = pl.ds(pl.multiple_of(b * blk, blk), blk)
        x = lg_ref[rows, :] + bias_ref[...]
        gl = jnp.where(lane < G, x, NEG_BIG)
        gmax, gsel = first_argmax(gl)
        gw = 1.0 / jnp.sum(jnp.exp(gl - gmax), axis=1, keepdims=True)
        lo = G + E * gsel
        el = jnp.where(jnp.logical_and(lane >= lo, lane < lo + E), x, NEG_BIG)
        v1, i1 = first_argmax(el)
        v2, i2 = first_argmax(jnp.where(lane == i1, NEG_BIG, el))
        e21 = jnp.exp(v2 - v1)
        w1 = 1.0 / (1.0 + e21)
        cw_ref[rows, :] = jnp.where(lane == 0, gw * w1, jnp.where(lane == 1, gw * (e21 * w1), 0.0))
        oh = jnp.where(jnp.logical_or(lane == i1 - G, lane == i2 - G + NE), 1.0, 0.0)
        oh_ref[rows, :] = oh
        pre_ref[rows, :] = jnp.dot(ltri, oh.astype(BF16), preferred_element_type=F32) + carry
        return carry + jnp.sum(oh, axis=0, keepdims=True)

    counts = lax.fori_loop(0, T // blk, phase1, jnp.zeros((1, LANES), F32))

    in_e = lane1 < NE
    c0 = jnp.where(in_e, counts, 0.0)
    ctot = jnp.where(in_e, counts + pltpu.roll(counts, LANES - NE, 1), 0.0)
    tiles = jnp.floor((ctot + (tm - 1)) * (1.0 / tm))
    u_r = lax.broadcasted_iota(jnp.int32, (LANES, LANES), 0)
    u_c = lax.broadcasted_iota(jnp.int32, (LANES, LANES), 1)
    utri = (u_r <= u_c).astype(BF16)
    tile_end = jnp.dot(jnp.broadcast_to(tiles, (8, LANES)).astype(BF16), utri,
                       preferred_element_type=F32)[0:1, :]
    row_start = (tile_end - tiles) * tm
    base = jnp.where(in_e, row_start, pltpu.roll(row_start + c0, NE, 1))
    cnt_ref[...] = jnp.concatenate([ctot, tiles, tile_end, jnp.zeros((5, LANES), F32)], axis=0)

    def phase2(b, carry):
        rows = pl.ds(pl.multiple_of(b * blk, blk), blk)
        t = oh_ref[rows, :] * (pre_ref[rows, :] + base)
        p0 = jnp.sum(jnp.where(lane < NE, t, 0.0), axis=1, keepdims=True)
        p1 = jnp.sum(jnp.where(lane >= NE, t, 0.0), axis=1, keepdims=True)
        pos_ref[rows, :] = jnp.where(lane == 0, p0, jnp.where(lane == 1, p1, 0.0)).astype(jnp.int32)
        return carry

    lax.fori_loop(0, T // blk, phase2, 0)


def _route(logits, bias_row, tm):
    T = logits.shape[0]
    whole = lambda shape: pl.BlockSpec(shape, lambda: (0,) * len(shape))
    return pl.pallas_call(
        functools.partial(_route_kernel, tm=tm),
        in_specs=[whole((T, LANES)), whole((1, LANES))],
        out_specs=[whole((T, LANES)), whole((T, LANES)), whole((8, LANES))],
        out_shape=[
            jax.ShapeDtypeStruct((T, LANES), jnp.int32),
            jax.ShapeDtypeStruct((T, LANES), F32),
            jax.ShapeDtypeStruct((8, LANES), F32),
        ],
        scratch_shapes=[pltpu.VMEM((T, LANES), F32), pltpu.VMEM((T, LANES), F32)],
        compiler_params=pltpu.CompilerParams(vmem_limit_bytes=VMEM_LIMIT),
        name="route_positions",
    )(logits, bias_row)


def _tile_plan(cnt, tm, n_tiles):
    counts = cnt[0, :N_EXPERTS].astype(jnp.int32)
    tiles_per_e = cnt[1, :N_EXPERTS].astype(jnp.int32)
    tile_end = cnt[2, :N_EXPERTS].astype(jnp.int32)
    n_valid = tile_end[-1]
    tile_ids = jnp.arange(n_tiles, dtype=jnp.int32)
    experts = jnp.arange(N_EXPERTS, dtype=jnp.int32)
    valid = tile_ids < n_valid
    jc = jnp.minimum(tile_ids, n_valid - 1)
    tile_expert = jnp.minimum(jnp.sum((tile_end[None, :] <= jc[:, None]).astype(jnp.int32), axis=1),
                              N_EXPERTS - 1)
    of_tile = (tile_expert[:, None] == experts[None, :]).astype(jnp.int32)
    pick = lambda per_expert: jnp.sum(of_tile * per_expert[None, :], axis=1)
    tile_start = tile_end - tiles_per_e
    nonempty = tiles_per_e > 0
    group_idx = jnp.sum(jnp.logical_and(nonempty[None, :], experts[None, :] <= experts[:, None]).astype(jnp.int32),
                        axis=1) - 1
    later = jnp.logical_and(nonempty[None, :], experts[None, :] > experts[:, None])
    next_e = jnp.min(jnp.where(later, experts[None, :], N_EXPERTS), axis=1)
    next_e = jnp.where(next_e == N_EXPERTS, -1, next_e)
    tile_first = jnp.logical_and(valid, tile_ids == pick(tile_start)).astype(jnp.int32)
    tile_slot = (pick(group_idx) % 2).astype(jnp.int32)
    tile_next = pick(next_e).astype(jnp.int32)
    tile_rows = jnp.clip(pick(counts) - (tile_ids - pick(tile_start)) * tm, 0, tm)
    tile_rows = jnp.where(valid, tile_rows, 0).astype(jnp.int32)
    return (tile_expert.astype(jnp.int32), n_valid.reshape(1), tile_first, tile_slot, tile_next, tile_rows)


def kernel(x, rel_bias, ln_mix_g, w_in, conv_w, conv_b, b_i, b_f, lam_q1, lam_k1, lam_q2, lam_k2,
           diff_norm_g, mlstm_norm_g, w_out, ln_ffn_g, w_group, b_group, w_router, b_router,
           w_gate, w_up, w_down, ln_f_g):
    B, S, D = x.shape
    T = B * S
    depth = w_in.shape[0]
    assert depth == 1, "the final rmsnorm is fused into the single layer's combine kernel"
    Hm = N_MLSTM_HEADS
    n_main = w_in.shape[2] - 2 * Hm
    n_diff = N_DIFF_HEADS * 2 * DIFF_HEAD_DIM
    xf = x.reshape(T, D)

    for l in range(depth):
        lambda_init = 0.8 - 0.6 * math.exp(-0.3 * l)
        w_main = w_in[l, :, :n_main].astype(BF16)
        wgt = w_in[l, :, n_main:]
        w_gates = jnp.zeros((D, 2 * LANES), F32).at[:, :Hm].set(wgt[:, :Hm]).at[:, LANES:LANES + Hm].set(
            wgt[:, Hm:]).astype(BF16)
        bi_row = jnp.zeros((1, LANES), F32).at[0, :Hm].set(b_i[l].astype(F32))
        bf_row = jnp.zeros((1, LANES), F32).at[0, :Hm].set(b_f[l].astype(F32))
        lam = (jnp.exp(jnp.sum(lam_q1[l].astype(F32) * lam_k1[l].astype(F32)))
               - jnp.exp(jnp.sum(lam_q2[l].astype(F32) * lam_k2[l].astype(F32))) + lambda_init)
        tq = ATT_TQ
        assert tq >= MAX_DISTANCE and tq % CHUNK == 0
        rb = rel_bias.astype(F32)
        log2e = math.log2(math.e)
        xx = jnp.arange(2 * tq, dtype=jnp.int32)
        rel_vec = jnp.stack([-tq + tq - 1 - xx, tq - 1 - xx], axis=0)
        bias_vecs = jnp.take(rb, _t5_bucket(rel_vec), axis=0) * log2e
        bias_vecs = jnp.transpose(bias_vecs, (2, 0, 1))[:, :, None, :]
        cfar = rb[N_BUCKETS // 2 - 1] * log2e
        scal = jnp.concatenate([lam.reshape(1), cfar]).astype(F32)
        col_scale = jnp.ones((1, n_main), F32).at[:, :n_diff].set(DIFF_HEAD_DIM ** -0.5 * log2e)

        proj, gates = _proj(xf, ln_mix_g[l].reshape(1, D).astype(F32), w_main, col_scale, w_gates)
        proj3 = proj.reshape(B, S, n_main)
        a = _diff_attention(proj3, scal, bias_vecs, diff_norm_g[l].reshape(1, n_diff).astype(F32),
                            lambda_init)
        hm = _mlstm(proj3, gates.reshape(B, S, 2 * LANES), conv_w[l].astype(F32),
                    conv_b[l].reshape(1, -1).astype(F32), bi_row, bf_row,
                    mlstm_norm_g[l].reshape(1, -1).astype(F32))

        wo = w_out[l].astype(BF16)
        G, E = N_GROUPS, EXPERTS_PER_GROUP
        wr = jnp.zeros((D, LANES), F32).at[:, :G].set(w_group[l].astype(F32)).at[:, G:G + G * E].set(
            jnp.transpose(w_router[l].astype(F32), (1, 0, 2)).reshape(D, G * E)).astype(BF16)
        x1, h2, logits = _out_proj(xf, a.reshape(T, n_diff), hm.reshape(T, -1), wo[:n_diff], wo[n_diff:],
                                   ln_ffn_g[l].reshape(1, D).astype(F32), wr)

        route_bias = jnp.concatenate([b_group[l].astype(F32).reshape(-1), b_router[l].astype(F32).reshape(-1),
                                      jnp.zeros((LANES - G - G * E,), F32)]).reshape(1, LANES)
        n_tiles = (T * TOP_K_INNER) // MOE_TM + N_EXPERTS
        pos128, cwp, cnt = _route(logits, route_bias, MOE_TM)
        pos_t = pos128[:, :TOP_K_INNER].T
        tiles = _tile_plan(cnt, MOE_TM, n_tiles)
        xs = _sc_scatter_rows(h2, pos_t, n_tiles * MOE_TM)
        Fe = w_gate.shape[-1]
        ys = _moe(*tiles, xs, w_gate[l].reshape(N_EXPERTS, D, Fe),
                  w_up[l].reshape(N_EXPERTS, D, Fe), w_down[l].reshape(N_EXPERTS, Fe, D))
        yw = _sc_gather_rows(ys, pos_t.reshape(-1))
        xf = _final(x1, yw, cwp, ln_f_g.reshape(1, D).astype(F32))
    return xf.reshape(B, S, D)
```

```python
import functools
import math

import jax
import jax.numpy as jnp
from jax import lax
from jax.experimental import pallas as pl
from jax.experimental.pallas import tpu as pltpu
from jax.experimental.pallas import tpu_sc as plsc

F32 = jnp.float32
BF16 = jnp.bfloat16

EPS = 1e-6
CHUNK = 64
DIFF_HEAD_DIM = 64
N_DIFF_HEADS = 8
MLSTM_HEAD_DIM = 128
N_MLSTM_HEADS = 8
CONV_WIDTH = 4
N_BUCKETS = 32
MAX_DISTANCE = 128
N_GROUPS = 4
EXPERTS_PER_GROUP = 8
N_EXPERTS = N_GROUPS * EXPERTS_PER_GROUP
TOP_K_INNER = 2
LANES = 128
NEG_BIG = -1e30

VMEM_LIMIT = 56 * 1024 * 1024

PROJ_TM, PROJ_TN = 1024, 1024
ATT_TQ = 512
MLSTM_TS = 512
OUT_TM = 512
MOE_TM = 256
FIN_TM = 512


def _cparams(sem):
    return pltpu.CompilerParams(dimension_semantics=sem, vmem_limit_bytes=VMEM_LIMIT)


_HI_MASK = 0xFFFF0000


def _pack_bf16_pairs(x):
    half = x.shape[-1] // 2
    xb = x.astype(BF16).astype(F32)
    lo = pltpu.bitcast(xb[:, :half], jnp.uint32)
    hi = pltpu.bitcast(xb[:, half:], jnp.uint32)
    return (hi & jnp.uint32(_HI_MASK)) | (lo >> 16)


def _unpack_bf16_pairs(w):
    lo = pltpu.bitcast(w << 16, F32)
    hi = pltpu.bitcast(w & jnp.uint32(_HI_MASK), F32)
    return lo, hi


def _proj_kernel(x_ref, g_ref, w_ref, cs_ref, wg_ref, o_ref, og_ref, h_ref):
    @pl.when(pl.program_id(1) == 0)
    def _():
        x = x_ref[...]
        ms = jnp.mean(x * x, axis=-1, keepdims=True)
        h = (x * lax.rsqrt(ms + EPS) * g_ref[...]).astype(BF16)
        h_ref[...] = h
        og_ref[...] = jnp.dot(h, wg_ref[...], preferred_element_type=F32)

    o_ref[...] = (jnp.dot(h_ref[...], w_ref[...].astype(BF16), preferred_element_type=F32)
                  * cs_ref[...]).astype(o_ref.dtype)


def _proj(x2, g, w_main, col_scale, w_gates):
    T, D = x2.shape
    N = col_scale.shape[1]
    NG = w_gates.shape[1]
    return pl.pallas_call(
        _proj_kernel,
        grid=(T // PROJ_TM, N // PROJ_TN),
        in_specs=[
            pl.BlockSpec((PROJ_TM, D), lambda m, n: (m, 0)),
            pl.BlockSpec((1, D), lambda m, n: (0, 0)),
            pl.BlockSpec((D, PROJ_TN), lambda m, n: (0, n)),
            pl.BlockSpec((1, PROJ_TN), lambda m, n: (0, n)),
            pl.BlockSpec((D, NG), lambda m, n: (0, 0)),
        ],
        out_specs=[
            pl.BlockSpec((PROJ_TM, PROJ_TN), lambda m, n: (m, n)),
            pl.BlockSpec((PROJ_TM, NG), lambda m, n: (m, 0)),
        ],
        out_shape=[
            jax.ShapeDtypeStruct((T, N), BF16),
            jax.ShapeDtypeStruct((T, NG), F32),
        ],
        scratch_shapes=[pltpu.VMEM((PROJ_TM, D), BF16)],
        compiler_params=_cparams(("parallel", "arbitrary")),
        name="rms_in_proj",
    )(x2, g, w_main, col_scale, w_gates)


def _t5_bucket(rel):
    half = N_BUCKETS // 2
    max_exact = half // 2
    ret = jnp.where(rel > 0, half, 0)
    n = jnp.abs(rel)
    nf = jnp.maximum(n, 1).astype(F32)
    large = max_exact + (jnp.log(nf / max_exact) / math.log(MAX_DISTANCE / max_exact)
                         * (half - max_exact)).astype(jnp.int32)
    large = jnp.minimum(large, half - 1)
    return ret + jnp.where(n < max_exact, n, large)


def _attn_kernel(scal_ref, q_ref, k_ref, v_ref, bias_ref, g_ref, o_ref, m_ref, l_ref, acc_ref,
                 s0_ref, s1_ref, ml0_ref, ml1_ref, bt_ref, *, lambda_init):
    h = pl.program_id(1)
    qi = pl.program_id(2)
    tq = ATT_TQ
    lam = scal_ref[0]
    cfar = scal_ref[1 + h]

    q = q_ref[0]
    lane = lax.broadcasted_iota(jnp.int32, q.shape, 1)
    zero = jnp.zeros_like(q)
    qs = jnp.concatenate([jnp.where(lane < DIFF_HEAD_DIM, q, zero),
                          jnp.where(lane >= DIFF_HEAD_DIM, q, zero)], axis=0)

    m_ref[...] = jnp.full(m_ref.shape, NEG_BIG, F32)
    l_ref[...] = jnp.zeros(l_ref.shape, F32)
    acc_ref[...] = jnp.zeros(acc_ref.shape, F32)

    @pl.when(qi == 0)
    def _():
        kj = lax.broadcasted_iota(jnp.int32, (tq, tq), 0)
        qq = lax.broadcasted_iota(jnp.int32, (tq, tq), 1)
        allowed = (kj // CHUNK) <= (qq // CHUNK)
        for d in range(2):
            rows = jnp.broadcast_to(bias_ref[0, d], (tq, 2 * tq))
            tile = pltpu.roll(rows, tq + 1, 1, stride=1, stride_axis=0)[:, :tq]
            if d == 1:
                tile = jnp.where(allowed, tile, NEG_BIG)
            bt_ref[d] = tile

    bufs = ((s0_ref, ml0_ref), (s1_ref, ml1_ref))

    def score(ki, bias, slot):
        s_ref, ml_ref = bufs[slot]
        start = pl.multiple_of(ki * tq, tq)
        kt = k_ref[0, pl.ds(start, tq), :]
        s = lax.dot_general(kt, qs, (((1,), (1,)), ((), ())), preferred_element_type=F32)
        if bias is not None:
            s = s + jnp.concatenate([bias, bias], axis=1)
        s_ref[...] = s
        ml_ref[...] = jnp.max(s, axis=0, keepdims=True)

    def accumulate(ki, shift, slot):
        s_ref, ml_ref = bufs[slot]
        start = pl.multiple_of(ki * tq, tq)
        vt = v_ref[0, pl.ds(start, tq), :]
        m_old = m_ref[...]
        m_new = jnp.maximum(m_old, ml_ref[...] + shift)
        alpha = jnp.exp2(m_old - m_new)
        p = jnp.exp2(s_ref[...] - (m_new - shift))
        l_ref[...] = alpha * l_ref[...] + jnp.sum(p, axis=0, keepdims=True)
        pv = lax.dot_general(vt, p.astype(BF16), (((0,), (0,)), ((), ())), preferred_element_type=F32)
        acc_ref[...] = alpha * acc_ref[...] + pv
        m_ref[...] = m_new

    n_far = qi - 1
    score(qi, bt_ref[1], 0)

    @pl.when(qi == 0)
    def _():
        accumulate(qi, 0.0, 0)

    @pl.when(qi >= 1)
    def _():
        accumulate(qi, 0.0, 0)
        score(qi - 1, bt_ref[0], 1)

    @pl.when(qi == 1)
    def _():
        accumulate(qi - 1, 0.0, 1)

    @pl.when(qi >= 2)
    def _():
        accumulate(qi - 1, 0.0, 1)
        score(0, None, 0)
        trips = (n_far - 1) // 2

        def pair(j, c):
            accumulate(2 * j, cfar, 0)
            score(2 * j + 1, None, 1)
            accumulate(2 * j + 1, cfar, 1)
            score(2 * j + 2, None, 0)
            return c

        lax.fori_loop(0, trips, pair, 0)
        last = 2 * trips

        @pl.when(n_far - last == 2)
        def _():
            accumulate(last, cfar, 0)
            score(last + 1, None, 1)
            accumulate(last + 1, cfar, 1)

        @pl.when(n_far - last == 1)
        def _():
            accumulate(last, cfar, 0)

    acc = acc_ref[...] * (1.0 / l_ref[...])
    o_t = acc[:, 0:tq] - lam * acc[:, tq:2 * tq]
    ms = jnp.mean(o_t * o_t, axis=0, keepdims=True)
    y = (o_t * lax.rsqrt(ms + EPS)).T * (g_ref[...] * (1.0 - lambda_init))
    o_ref[0] = y.astype(o_ref.dtype)


def _diff_attention(proj3, scal, bias_vecs, gnorm, lambda_init):
    B, S, _ = proj3.shape
    H = N_DIFF_HEADS
    tq = ATT_TQ
    kern = functools.partial(_attn_kernel, lambda_init=lambda_init)
    return pl.pallas_call(
        kern,
        grid=(B, H, S // tq),
        in_specs=[
            pl.BlockSpec(memory_space=pltpu.SMEM),
            pl.BlockSpec((1, tq, LANES), lambda b, h, i: (b, i, h)),
            pl.BlockSpec((1, S, LANES), lambda b, h, i: (b, 0, H + h)),
            pl.BlockSpec((1, S, LANES), lambda b, h, i: (b, 0, 2 * H + h)),
            pl.BlockSpec((1, 2, 1, 2 * tq), lambda b, h, i: (h, 0, 0, 0)),
            pl.BlockSpec((1, LANES), lambda b, h, i: (0, h)),
        ],
        out_specs=pl.BlockSpec((1, tq, LANES), lambda b, h, i: (b, i, h)),
        out_shape=jax.ShapeDtypeStruct((B, S, H * LANES), BF16),
        scratch_shapes=[
            pltpu.VMEM((1, 2 * tq), F32),
            pltpu.VMEM((1, 2 * tq), F32),
            pltpu.VMEM((LANES, 2 * tq), F32),
            pltpu.VMEM((tq, 2 * tq), F32),
            pltpu.VMEM((tq, 2 * tq), F32),
            pltpu.VMEM((1, 2 * tq), F32),
            pltpu.VMEM((1, 2 * tq), F32),
            pltpu.VMEM((2, tq, tq), F32),
        ],
        compiler_params=_cparams(("parallel", "parallel", "arbitrary")),
        name="diff_attention",
    )(scal, proj3, proj3, proj3, bias_vecs, gnorm)


def _log_sigmoid(x):
    return jnp.minimum(x, 0.0) - jnp.log(1.0 + jnp.exp(-jnp.abs(x)))


def _sigmoid(x):
    return 1.0 / (1.0 + jnp.exp(-x))


def _mlstm_kernel(q_ref, k_ref, v_ref, o_ref, gi_ref, gf_ref, cw_ref, cb_ref, bi_ref, bf_ref, gn_ref,
                  out_ref, qext_ref, kext_ref, ct_ref, n_ref, m_ref):
    sb = pl.program_id(1)
    L = CHUNK
    dh = MLSTM_HEAD_DIM
    H = N_MLSTM_HEADS
    ts = MLSTM_TS
    pad = 8

    @pl.when(sb == 0)
    def _():
        qext_ref[0:pad, :] = jnp.zeros((pad, H * dh), F32)
        kext_ref[0:pad, :] = jnp.zeros((pad, H * dh), F32)
        ct_ref[...] = jnp.zeros(ct_ref.shape, F32)
        n_ref[...] = jnp.zeros(n_ref.shape, F32)
        m_ref[...] = jnp.zeros(m_ref.shape, F32)

    qext_ref[pad:pad + ts, :] = q_ref[0].astype(F32)
    kext_ref[pad:pad + ts, :] = k_ref[0].astype(F32)

    row = lax.broadcasted_iota(jnp.int32, (L, L), 0)
    col = lax.broadcasted_iota(jnp.int32, (L, L), 1)
    tril = col <= row
    ltri = tril.astype(F32)

    def conv_silu(ext_ref, base, h, off):
        win = ext_ref[pl.ds(base, L + pad), h * dh:(h + 1) * dh]
        w = cw_ref[:, off + h * dh:off + (h + 1) * dh]
        y = cb_ref[:, off + h * dh:off + (h + 1) * dh]
        for j in range(CONV_WIDTH):
            lo = pad - (CONV_WIDTH - 1) + j
            y = y + w[j:j + 1, :] * win[lo:lo + L, :]
        return y * _sigmoid(y)

    def chunk_body(c, carry):
        base = pl.multiple_of(c * L, L)
        li = gi_ref[0, pl.ds(base, L), :] + bi_ref[...]
        logf = _log_sigmoid(gf_ref[0, pl.ds(base, L), :] + bf_ref[...])
        b = jnp.dot(ltri, logf, preferred_element_type=F32, precision=lax.Precision.HIGHEST)
        a = li - b
        g_row = b[L - 1:L, :]
        m_row = m_ref[...]
        m_new_row = g_row + jnp.maximum(m_row, jnp.max(a, axis=0, keepdims=True))
        a_t = a.T

        for h in range(H):
            qc = conv_silu(qext_ref, base, h, 0)
            kc = conv_silu(kext_ref, base, h, H * dh) * (dh ** -0.5)
            qb = qc.astype(BF16)
            kb = kc.astype(BF16)
            vb = v_ref[0, pl.ds(base, L), h * dh:(h + 1) * dh]

            a_row = a_t[h:h + 1, :]
            a_col = a[:, h:h + 1]
            b_col = b[:, h:h + 1]
            m_prev = m_row[:, h:h + 1]
            m_next = m_new_row[:, h:h + 1]
            g_h = g_row[:, h:h + 1]

            amat = jnp.where(tril, a_row, NEG_BIG)
            mcol = jnp.maximum(jnp.max(amat, axis=-1, keepdims=True), m_prev)
            wts = jnp.exp(amat - mcol)
            inter = jnp.exp(m_prev - mcol)

            s = lax.dot_general(qb, kb, (((1,), (1,)), ((), ())), preferred_element_type=F32)
            sqk = s * wts
            ct = ct_ref[h]
            nrow = n_ref[h:h + 1, :]
            num = (jnp.dot(sqk.astype(BF16), vb, preferred_element_type=F32)
                   + inter * jnp.dot(qb, ct.astype(BF16), preferred_element_type=F32))
            den = (jnp.sum(sqk, axis=-1, keepdims=True)
                   + inter * jnp.sum(qb.astype(F32) * nrow, axis=-1, keepdims=True))
            hv = num / jnp.maximum(jnp.abs(den), jnp.exp(-(b_col + mcol)))

            wt = jnp.exp(g_h + a_col - m_next)
            decay = jnp.exp(g_h + m_prev - m_next)
            wv = (wt * vb.astype(F32)).astype(BF16)
            ct_ref[h] = decay * ct + lax.dot_general(kb, wv, (((0,), (0,)), ((), ())),
                                                     preferred_element_type=F32)
            n_ref[h:h + 1, :] = decay * nrow + jnp.sum(wt * kb.astype(F32), axis=0, keepdims=True)

            ms = jnp.mean(hv * hv, axis=-1, keepdims=True)
            y = hv * lax.rsqrt(ms + EPS) * gn_ref[:, h * dh:(h + 1) * dh]
            og = o_ref[0, pl.ds(base, L), h * dh:(h + 1) * dh].astype(F32)
            out_ref[0, pl.ds(base, L), h * dh:(h + 1) * dh] = (y * _sigmoid(og)).astype(out_ref.dtype)

        m_ref[...] = m_new_row
        return carry

    lax.fori_loop(0, ts // L, chunk_body, 0)

    qext_ref[0:pad, :] = qext_ref[ts:ts + pad, :]
    kext_ref[0:pad, :] = kext_ref[ts:ts + pad, :]


def _mlstm(proj3, gates3, conv_w, conv_b, bi_row, bf_row, gnorm):
    B, S, _ = proj3.shape
    W = N_MLSTM_HEADS * MLSTM_HEAD_DIM
    ts = MLSTM_TS
    first = 3
    blk = lambda j: pl.BlockSpec((1, ts, W), lambda b, s: (b, s, j))
    full = lambda shape: pl.BlockSpec(shape, lambda b, s: (0,) * len(shape))
    return pl.pallas_call(
        _mlstm_kernel,
        grid=(B, S // ts),
        in_specs=[
            blk(first), blk(first + 1), blk(first + 2), blk(first + 3),
            pl.BlockSpec((1, ts, LANES), lambda b, s: (b, s, 0)),
            pl.BlockSpec((1, ts, LANES), lambda b, s: (b, s, 1)),
            full((CONV_WIDTH, 2 * W)), full((1, 2 * W)),
            full((1, LANES)), full((1, LANES)), full((1, W)),
        ],
        out_specs=pl.BlockSpec((1, ts, W), lambda b, s: (b, s, 0)),
        out_shape=jax.ShapeDtypeStruct((B, S, W), BF16),
        scratch_shapes=[
            pltpu.VMEM((ts + 8, W), F32),
            pltpu.VMEM((ts + 8, W), F32),
            pltpu.VMEM((N_MLSTM_HEADS, MLSTM_HEAD_DIM, MLSTM_HEAD_DIM), F32),
            pltpu.VMEM((N_MLSTM_HEADS, MLSTM_HEAD_DIM), F32),
            pltpu.VMEM((1, LANES), F32),
        ],
        compiler_params=_cparams(("parallel", "arbitrary")),
        name="mlstm",
    )(proj3, proj3, proj3, proj3, gates3, gates3, conv_w, conv_b, bi_row, bf_row, gnorm)


def _out_kernel(x_ref, a_ref, hm_ref, wa_ref, wm_ref, g_ref, wr_ref, x1_ref, h2_ref, lg_ref):
    y = (jnp.dot(a_ref[...], wa_ref[...].astype(BF16), preferred_element_type=F32)
         + jnp.dot(hm_ref[...], wm_ref[...].astype(BF16), preferred_element_type=F32))
    x1 = x_ref[...] + y
    x1_ref[...] = x1
    ms = jnp.mean(x1 * x1, axis=-1, keepdims=True)
    h2 = x1 * lax.rsqrt(ms + EPS) * g_ref[...]
    h2_ref[...] = _pack_bf16_pairs(h2)
    lg_ref[...] = jnp.dot(h2.astype(BF16), wr_ref[...], preferred_element_type=F32)


def _out_proj(x2, a2, hm2, wa, wm, g, wr):
    T, D = x2.shape
    W = a2.shape[1]
    tm = OUT_TM
    const = lambda shape: pl.BlockSpec(shape, lambda m: (0, 0), pipeline_mode=pl.Buffered(1))
    return pl.pallas_call(
        _out_kernel,
        grid=(T // tm,),
        in_specs=[
            pl.BlockSpec((tm, D), lambda m: (m, 0)),
            pl.BlockSpec((tm, W), lambda m: (m, 0)),
            pl.BlockSpec((tm, W), lambda m: (m, 0)),
            const((W, D)),
            pl.BlockSpec((W, D), lambda m: (1, 0), pipeline_mode=pl.Buffered(1)),
            const((1, D)), const((D, LANES)),
        ],
        out_specs=[
            pl.BlockSpec((tm, D), lambda m: (m, 0)),
            pl.BlockSpec((tm, D // 2), lambda m: (m, 0)),
            pl.BlockSpec((tm, LANES), lambda m: (m, 0)),
        ],
        out_shape=[
            jax.ShapeDtypeStruct((T, D), F32),
            jax.ShapeDtypeStruct((T, D // 2), jnp.uint32),
            jax.ShapeDtypeStruct((T, LANES), F32),
        ],
        compiler_params=_cparams(("parallel",)),
        name="out_proj_router",
    )(x2, a2, hm2, wa, wm, g, wr)


def _moe_kernel(te_ref, nv_ref, first_ref, slot_ref, nxt_ref, rows_ref, xs_ref, wg_hbm, wu_hbm, wd_hbm,
                ys_ref, wg_buf, wu_buf, wd_buf, sem):
    j = pl.program_id(0)
    valid = j < nv_ref[0]

    half_f = wd_buf.shape[1] // 2

    def weight_copies(e, s):
        lo, hi = pl.ds(0, half_f), pl.ds(half_f, half_f)
        return ((pltpu.make_async_copy(wg_hbm.at[e], wg_buf.at[s], sem.at[s, 0]), 0),
                (pltpu.make_async_copy(wu_hbm.at[e], wu_buf.at[s], sem.at[s, 1]), 1),
                (pltpu.make_async_copy(wd_hbm.at[e, lo], wd_buf.at[s, lo], sem.at[s, 2]), 0),
                (pltpu.make_async_copy(wd_hbm.at[e, hi], wd_buf.at[s, hi], sem.at[s, 3]), 1))

    @pl.when(j == 0)
    def _():
        for c, prio in weight_copies(te_ref[0], 0):
            c.start(priority=prio)

    @pl.when(jnp.logical_and(valid, first_ref[j] == 1))
    def _():
        for c, _ in weight_copies(te_ref[j], slot_ref[j]):
            c.wait()

        @pl.when(nxt_ref[j] >= 0)
        def _():
            for c, prio in weight_copies(nxt_ref[j], 1 - slot_ref[j]):
                c.start(priority=prio)

    @pl.when(valid)
    def _():
        s = slot_ref[j]
        row = lax.broadcasted_iota(jnp.int32, xs_ref.shape, 0)
        lo, hi = _unpack_bf16_pairs(jnp.where(row < rows_ref[j], xs_ref[...], jnp.uint32(0)))
        xs = jnp.concatenate([lo.astype(BF16), hi.astype(BF16)], axis=1)
        gt = jnp.dot(xs, wg_buf[s].astype(BF16), preferred_element_type=F32)
        up = jnp.dot(xs, wu_buf[s].astype(BF16), preferred_element_type=F32)
        hid = (gt * _sigmoid(gt) * up).astype(BF16)
        ys_ref[...] = _pack_bf16_pairs(jnp.dot(hid, wd_buf[s].astype(BF16), preferred_element_type=F32))

    @pl.when(jnp.logical_not(valid))
    def _():
        ys_ref[...] = jnp.zeros(ys_ref.shape, ys_ref.dtype)


def _moe(tile_expert, n_valid, tile_first, tile_slot, tile_next, tile_rows, xs, wg, wu, wd):
    R, Dw = xs.shape
    D, F = wg.shape[1], wg.shape[2]
    tm = MOE_TM
    hbm = pl.BlockSpec(memory_space=pl.ANY)
    grid_spec = pltpu.PrefetchScalarGridSpec(
        num_scalar_prefetch=6,
        grid=(R // tm,),
        in_specs=[pl.BlockSpec((tm, Dw), lambda j, *_: (j, 0)), hbm, hbm, hbm],
        out_specs=pl.BlockSpec((tm, Dw), lambda j, *_: (j, 0)),
        scratch_shapes=[
            pltpu.VMEM((2, D, F), wg.dtype),
            pltpu.VMEM((2, D, F), wu.dtype),
            pltpu.VMEM((2, F, D), wd.dtype),
            pltpu.SemaphoreType.DMA((2, 4)),
        ],
    )
    return pl.pallas_call(
        _moe_kernel,
        grid_spec=grid_spec,
        out_shape=jax.ShapeDtypeStruct((R, Dw), jnp.uint32),
        compiler_params=_cparams(("arbitrary",)),
        name="moe_experts",
    )(tile_expert, n_valid, tile_first, tile_slot, tile_next, tile_rows, xs, wg, wu, wd)


def _final_kernel(x1_ref, y0_ref, y1_ref, cw_ref, g_ref, o_ref):
    cw = cw_ref[...]
    lo0, hi0 = _unpack_bf16_pairs(y0_ref[...])
    lo1, hi1 = _unpack_bf16_pairs(y1_ref[...])
    w0, w1 = cw[:, 0:1], cw[:, 1:2]
    y = jnp.concatenate([w0 * lo0 + w1 * lo1, w0 * hi0 + w1 * hi1], axis=1)
    x = x1_ref[...] + y
    ms = jnp.mean(x * x, axis=-1, keepdims=True)
    o_ref[...] = x * lax.rsqrt(ms + EPS) * g_ref[...]


def _final(x1, yw, cw, g):
    T, D = x1.shape
    tm = FIN_TM
    row = lambda w: pl.BlockSpec((tm, w), lambda m: (m, 0))
    slot1 = pl.BlockSpec((tm, D // 2), lambda m: (m + T // tm, 0))
    return pl.pallas_call(
        _final_kernel,
        grid=(T // tm,),
        in_specs=[row(D), row(D // 2), slot1, row(LANES), pl.BlockSpec((1, D), lambda m: (0, 0))],
        out_specs=row(D),
        out_shape=jax.ShapeDtypeStruct((T, D), F32),
        compiler_params=_cparams(("parallel",)),
        name="combine_final_norm",
    )(x1, yw, yw, cw, g)


SC_CORES, SC_SUBCORES = 2, 16
SC_CHUNK = 32


def _sc_gather_rows(table, idx):
    V, Dw = table.shape
    R = idx.shape[0]
    n_workers = SC_CORES * SC_SUBCORES
    ch = SC_CHUNK
    per_w = R // n_workers
    n_chunks = per_w // ch
    assert per_w * n_workers == R and n_chunks * ch == per_w and n_chunks % 2 == 0
    idx3 = idx.reshape(n_workers, n_chunks, ch)
    mesh = plsc.VectorSubcoreMesh(core_axis_name="c", subcore_axis_name="s")

    def body(table_hbm, idx_hbm, out_hbm, idx_v, rows_v, gsem, osem):
        wid = lax.axis_index("s") * SC_CORES + lax.axis_index("c")
        base = wid * per_w
        pltpu.sync_copy(idx_hbm.at[wid], idx_v)

        def gather(c, slot):
            return pltpu.make_async_copy(table_hbm.at[idx_v.at[c]], rows_v.at[slot], gsem.at[slot])

        def put(c, slot):
            return pltpu.make_async_copy(rows_v.at[slot], out_hbm.at[pl.ds(base + c * ch, ch)],
                                         osem.at[slot])

        gather(0, 0).start()

        @pl.loop(0, n_chunks, step=2)
        def _(c):
            @pl.when(c > 0)
            def _():
                put(c - 1, 1).wait()

            gather(c + 1, 1).start()
            gather(c, 0).wait()
            put(c, 0).start()
            put(c, 0).wait()

            @pl.when(c + 2 < n_chunks)
            def _():
                gather(c + 2, 0).start()

            gather(c + 1, 1).wait()
            put(c + 1, 1).start()

        put(n_chunks - 1, 1).wait()

    return pl.kernel(
        body,
        out_type=jax.ShapeDtypeStruct((R, Dw), table.dtype),
        mesh=mesh,
        scratch_types=[
            pltpu.VMEM((n_chunks, ch), jnp.int32),
            pltpu.VMEM((2, ch, Dw), table.dtype),
            pltpu.SemaphoreType.DMA((2,)),
            pltpu.SemaphoreType.DMA((2,)),
        ],
        name="sc_gather_rows",
    )(table, idx3)


def _sc_scatter_rows(table, idx, n_rows_out):
    V, Dw = table.shape
    K = idx.shape[0]
    n_workers = SC_CORES * SC_SUBCORES
    ch = SC_CHUNK
    per_w = V // n_workers
    n_chunks = per_w // ch
    assert K == 2 and per_w * n_workers == V and n_chunks * ch == per_w and n_chunks % 2 == 0
    idx4 = jnp.transpose(idx.reshape(K, n_workers, n_chunks, ch), (1, 0, 2, 3))
    mesh = plsc.VectorSubcoreMesh(core_axis_name="c", subcore_axis_name="s")

    def body(table_hbm, idx_hbm, out_hbm, idx_v, rows_v, lsem, ssem):
        wid = lax.axis_index("s") * SC_CORES + lax.axis_index("c")
        base = wid * per_w
        pltpu.sync_copy(idx_hbm.at[wid], idx_v)

        def load(c, slot):
            return pltpu.make_async_copy(table_hbm.at[pl.ds(base + c * ch, ch)], rows_v.at[slot],
                                         lsem.at[slot])

        def scatter(c, slot, k):
            return pltpu.make_async_copy(rows_v.at[slot], out_hbm.at[idx_v.at[k, c]], ssem.at[slot, k])

        load(0, 0).start()

        @pl.loop(0, n_chunks, step=2)
        def _(c):
            @pl.when(c > 0)
            def _():
                scatter(c - 1, 1, 0).wait()
                scatter(c - 1, 1, 1).wait()

            load(c + 1, 1).start()
            load(c, 0).wait()
            scatter(c, 0, 0).start()
            scatter(c, 0, 1).start()
            scatter(c, 0, 0).wait()
            scatter(c, 0, 1).wait()

            @pl.when(c + 2 < n_chunks)
            def _():
                load(c + 2, 0).start()

            load(c + 1, 1).wait()
            scatter(c + 1, 1, 0).start()
            scatter(c + 1, 1, 1).start()

        scatter(n_chunks - 1, 1, 0).wait()
        scatter(n_chunks - 1, 1, 1).wait()

    return pl.kernel(
        body,
        out_type=jax.ShapeDtypeStruct((n_rows_out, Dw), table.dtype),
        mesh=mesh,
        scratch_types=[
            pltpu.VMEM((K, n_chunks, ch), jnp.int32),
            pltpu.VMEM((2, ch, Dw), table.dtype),
            pltpu.SemaphoreType.DMA((2,)),
            pltpu.SemaphoreType.DMA((2, K)),
        ],
        name="sc_scatter_rows",
    )(table, idx4)


ROUTE_BLK = 256


def _route_kernel(lg_ref, bias_ref, pos_ref, cw_ref, cnt_ref, oh_ref, pre_ref, *, tm):
    T = lg_ref.shape[0]
    G, E, NE = N_GROUPS, EXPERTS_PER_GROUP, N_EXPERTS
    blk = ROUTE_BLK
    lane = lax.broadcasted_iota(jnp.int32, (blk, LANES), 1).astype(F32)
    lane1 = lax.broadcasted_iota(jnp.int32, (1, LANES), 1)
    r_i = lax.broadcasted_iota(jnp.int32, (blk, blk), 0)
    c_i = lax.broadcasted_iota(jnp.int32, (blk, blk), 1)
    ltri = (c_i < r_i).astype(BF16)

    def first_argmax(v):
        mx = jnp.max(v, axis=1, keepdims=True)
        return mx, jnp.min(jnp.where(v == mx, lane, float(LANES)), axis=1, keepdims=True)

    def phase1(b, carry):
        rows = pl.ds(pl.multiple_of(b * blk, blk), blk)
        x = lg_ref[rows, :] + bias_ref[...]
        gl = jnp.where(lane < G, x, NEG_BIG)
        gmax, gsel = first_argmax(gl)
        gw = 1.0 / jnp.sum(jnp.exp(gl - gmax), axis=1, keepdims=True)
        lo = G + E * gsel
        el = jnp.where(jnp.logical_and(lane >= lo, lane < lo + E), x, NEG_BIG)
        v1, i1 = first_argmax(el)
        v2, i2 = first_argmax(jnp.where(lane == i1, NEG_BIG, el))
        e21 = jnp.exp(v2 - v1)
        w1 = 1.0 / (1.0 + e21)
        cw_ref[rows, :] = jnp.where(lane == 0, gw * w1, jnp.where(lane == 1, gw * (e21 * w1), 0.0))
        oh = jnp.where(jnp.logical_or(lane == i1 - G, lane == i2 - G + NE), 1.0, 0.0)
        oh_ref[rows, :] = oh
        pre_ref[rows, :] = jnp.dot(ltri, oh.astype(BF16), preferred_element_type=F32) + carry
        return carry + jnp.sum(oh, axis=0, keepdims=True)

    counts = lax.fori_loop(0, T // blk, phase1, jnp.zeros((1, LANES), F32))

    in_e = lane1 < NE
    c0 = jnp.where(in_e, counts, 0.0)
    ctot = jnp.where(in_e, counts + pltpu.roll(counts, LANES - NE, 1), 0.0)
    tiles = jnp.floor((ctot + (tm - 1)) * (1.0 / tm))
    u_r = lax.broadcasted_iota(jnp.int32, (LANES, LANES), 0)
    u_c = lax.broadcasted_iota(jnp.int32, (LANES, LANES), 1)
    utri = (u_r <= u_c).astype(BF16)
    tile_end = jnp.dot(jnp.broadcast_to(tiles, (8, LANES)).astype(BF16), utri,
                       preferred_element_type=F32)[0:1, :]
    row_start = (tile_end - tiles) * tm
    base = jnp.where(in_e, row_start, pltpu.roll(row_start + c0, NE, 1))
    cnt_ref[...] = jnp.concatenate([ctot, tiles, tile_end, jnp.zeros((5, LANES), F32)], axis=0)

    def phase2(b, carry):
        rows = pl.ds(pl.multiple_of(b * blk, blk), blk)
        t = oh_ref[rows, :] * (pre_ref[rows, :] + base)
        p0 = jnp.sum(jnp.where(lane < NE, t, 0.0), axis=1, keepdims=True)
        p1 = jnp.sum(jnp.where(lane >= NE, t, 0.0), axis=1, keepdims=True)
        pos_ref[rows, :] = jnp.where(lane == 0, p0, jnp.where(lane == 1, p1, 0.0)).astype(jnp.int32)
        return carry

    lax.fori_loop(0, T // blk, phase2, 0)


def _route(logits, bias_row, tm):
    T = logits.shape[0]
    whole = lambda shape: pl.BlockSpec(shape, lambda: (0,) * len(shape))
    return pl.pallas_call(
        functools.partial(_route_kernel, tm=tm),
        in_specs=[whole((T, LANES)), whole((1, LANES))],
        out_specs=[whole((T, LANES)), whole((T, LANES)), whole((8, LANES))],
        out_shape=[
            jax.ShapeDtypeStruct((T, LANES), jnp.int32),
            jax.ShapeDtypeStruct((T, LANES), F32),
            jax.ShapeDtypeStruct((8, LANES), F32),
        ],
        scratch_shapes=[pltpu.VMEM((T, LANES), F32), pltpu.VMEM((T, LANES), F32)],
        compiler_params=pltpu.CompilerParams(vmem_limit_bytes=VMEM_LIMIT),
        name="route_positions",
    )(logits, bias_row)


def _tile_plan(cnt, tm, n_tiles):
    counts = cnt[0, :N_EXPERTS].astype(jnp.int32)
    tiles_per_e = cnt[1, :N_EXPERTS].astype(jnp.int32)
    tile_end = cnt[2, :N_EXPERTS].astype(jnp.int32)
    n_valid = tile_end[-1]
    tile_ids = jnp.arange(n_tiles, dtype=jnp.int32)
    experts = jnp.arange(N_EXPERTS, dtype=jnp.int32)
    valid = tile_ids < n_valid
    jc = jnp.minimum(tile_ids, n_valid - 1)
    tile_expert = jnp.minimum(jnp.sum((tile_end[None, :] <= jc[:, None]).astype(jnp.int32), axis=1),
                              N_EXPERTS - 1)
    of_tile = (tile_expert[:, None] == experts[None, :]).astype(jnp.int32)
    pick = lambda per_expert: jnp.sum(of_tile * per_expert[None, :], axis=1)
    tile_start = tile_end - tiles_per_e
    nonempty = tiles_per_e > 0
    group_idx = jnp.sum(jnp.logical_and(nonempty[None, :], experts[None, :] <= experts[:, None]).astype(jnp.int32),
                        axis=1) - 1
    later = jnp.logical_and(nonempty[None, :], experts[None, :] > experts[:, None])
    next_e = jnp.min(jnp.where(later, experts[None, :], N_EXPERTS), axis=1)
    next_e = jnp.where(next_e == N_EXPERTS, -1, next_e)
    tile_first = jnp.logical_and(valid, tile_ids == pick(tile_start)).astype(jnp.int32)
    tile_slot = (pick(group_idx) % 2).astype(jnp.int32)
    tile_next = pick(next_e).astype(jnp.int32)
    tile_rows = jnp.clip(pick(counts) - (tile_ids - pick(tile_start)) * tm, 0, tm)
    tile_rows = jnp.where(valid, tile_rows, 0).astype(jnp.int32)
    return (tile_expert.astype(jnp.int32), n_valid.reshape(1), tile_first, tile_slot, tile_next, tile_rows)


def kernel(x, rel_bias, ln_mix_g, w_in, conv_w, conv_b, b_i, b_f, lam_q1, lam_k1, lam_q2, lam_k2,
           diff_norm_g, mlstm_norm_g, w_out, ln_ffn_g, w_group, b_group, w_router, b_router,
           w_gate, w_up, w_down, ln_f_g):
    B, S, D = x.shape
    T = B * S
    depth = w_in.shape[0]
    assert depth == 1, "the final rmsnorm is fused into the single layer's combine kernel"
    Hm = N_MLSTM_HEADS
    n_main = w_in.shape[2] - 2 * Hm
    n_diff = N_DIFF_HEADS * 2 * DIFF_HEAD_DIM
    xf = x.reshape(T, D)

    for l in range(depth):
        lambda_init = 0.8 - 0.6 * math.exp(-0.3 * l)
        w_main = w_in[l]
        wgt = w_in[l, :, n_main:]
        w_gates = jnp.zeros((D, 2 * LANES), F32).at[:, :Hm].set(wgt[:, :Hm]).at[:, LANES:LANES + Hm].set(
            wgt[:, Hm:]).astype(BF16)
        bi_row = jnp.zeros((1, LANES), F32).at[0, :Hm].set(b_i[l].astype(F32))
        bf_row = jnp.zeros((1, LANES), F32).at[0, :Hm].set(b_f[l].astype(F32))
        lam = (jnp.exp(jnp.sum(lam_q1[l].astype(F32) * lam_k1[l].astype(F32)))
               - jnp.exp(jnp.sum(lam_q2[l].astype(F32) * lam_k2[l].astype(F32))) + lambda_init)
        tq = ATT_TQ
        assert tq >= MAX_DISTANCE and tq % CHUNK == 0
        rb = rel_bias.astype(F32)
        log2e = math.log2(math.e)
        xx = jnp.arange(2 * tq, dtype=jnp.int32)
        rel_vec = jnp.stack([-tq + tq - 1 - xx, tq - 1 - xx], axis=0)
        bias_vecs = jnp.take(rb, _t5_bucket(rel_vec), axis=0) * log2e
        bias_vecs = jnp.transpose(bias_vecs, (2, 0, 1))[:, :, None, :]
        cfar = rb[N_BUCKETS // 2 - 1] * log2e
        scal = jnp.concatenate([lam.reshape(1), cfar]).astype(F32)
        col_scale = jnp.ones((1, n_main), F32).at[:, :n_diff].set(DIFF_HEAD_DIM ** -0.5 * log2e)

        proj, gates = _proj(xf, ln_mix_g[l].reshape(1, D).astype(F32), w_main, col_scale, w_gates)
        proj3 = proj.reshape(B, S, n_main)
        a = _diff_attention(proj3, scal, bias_vecs, diff_norm_g[l].reshape(1, n_diff).astype(F32),
                            lambda_init)
        hm = _mlstm(proj3, gates.reshape(B, S, 2 * LANES), conv_w[l].astype(F32),
                    conv_b[l].reshape(1, -1).astype(F32), bi_row, bf_row,
                    mlstm_norm_g[l].reshape(1, -1).astype(F32))

        wo = w_out[l]
        G, E = N_GROUPS, EXPERTS_PER_GROUP
        wr = jnp.zeros((D, LANES), F32).at[:, :G].set(w_group[l].astype(F32)).at[:, G:G + G * E].set(
            jnp.transpose(w_router[l].astype(F32), (1, 0, 2)).reshape(D, G * E)).astype(BF16)
        x1, h2, logits = _out_proj(xf, a.reshape(T, n_diff), hm.reshape(T, -1), wo, wo,
                                   ln_ffn_g[l].reshape(1, D).astype(F32), wr)

        route_bias = jnp.concatenate([b_group[l].astype(F32).reshape(-1), b_router[l].astype(F32).reshape(-1),
                                      jnp.zeros((LANES - G - G * E,), F32)]).reshape(1, LANES)
        n_tiles = (T * TOP_K_INNER) // MOE_TM + N_EXPERTS
        pos128, cwp, cnt = _route(logits, route_bias, MOE_TM)
        pos_t = pos128[:, :TOP_K_INNER].T
        tiles = _tile_plan(cnt, MOE_TM, n_tiles)
        xs = _sc_scatter_rows(h2, pos_t, n_tiles * MOE_TM)
        Fe = w_gate.shape[-1]
        ys = _moe(*tiles, xs, w_gate[l].reshape(N_EXPERTS, D, Fe),
                  w_up[l].reshape(N_EXPERTS, D, Fe), w_down[l].reshape(N_EXPERTS, Fe, D))
        yw = _sc_gather_rows(ys, pos_t.reshape(-1))
        xf = _final(x1, yw, cwp, ln_f_g.reshape(1, D).astype(F32))
    return xf.reshape(B, S, D)
```

```python
import functools
import math

import jax
import jax.numpy as jnp
from jax import lax
from jax.experimental import pallas as pl
from jax.experimental.pallas import tpu as pltpu
from jax.experimental.pallas import tpu_sc as plsc

F32 = jnp.float32
BF16 = jnp.bfloat16

EPS = 1e-6
CHUNK = 64
DIFF_HEAD_DIM = 64
N_DIFF_HEADS = 8
MLSTM_HEAD_DIM = 128
N_MLSTM_HEADS = 8
CONV_WIDTH = 4
N_BUCKETS = 32
MAX_DISTANCE = 128
N_GROUPS = 4
EXPERTS_PER_GROUP = 8
N_EXPERTS = N_GROUPS * EXPERTS_PER_GROUP
TOP_K_INNER = 2
LANES = 128
NEG_BIG = -1e30

VMEM_LIMIT = 56 * 1024 * 1024

PROJ_TM, PROJ_TN = 1024, 1024
ATT_TQ = 512
MLSTM_TS = 512
OUT_TM = 512
MOE_TM = 256
FIN_TM = 512


def _cparams(sem):
    return pltpu.CompilerParams(dimension_semantics=sem, vmem_limit_bytes=VMEM_LIMIT)


_HI_MASK = 0xFFFF0000


def _pack_bf16_pairs(x):
    half = x.shape[-1] // 2
    xb = x.astype(BF16).astype(F32)
    lo = pltpu.bitcast(xb[:, :half], jnp.uint32)
    hi = pltpu.bitcast(xb[:, half:], jnp.uint32)
    return (hi & jnp.uint32(_HI_MASK)) | (lo >> 16)


def _unpack_bf16_pairs(w):
    lo = pltpu.bitcast(w << 16, F32)
    hi = pltpu.bitcast(w & jnp.uint32(_HI_MASK), F32)
    return lo, hi


def _proj_kernel(x_ref, g_ref, w_ref, cs_ref, wg_ref, o_ref, og_ref, h_ref):
    @pl.when(pl.program_id(1) == 0)
    def _():
        x = x_ref[...]
        ms = jnp.mean(x * x, axis=-1, keepdims=True)
        h = (x * lax.rsqrt(ms + EPS) * g_ref[...]).astype(BF16)
        h_ref[...] = h
        og_ref[...] = jnp.dot(h, wg_ref[...], preferred_element_type=F32)

    o_ref[...] = (jnp.dot(h_ref[...], w_ref[...], preferred_element_type=F32) * cs_ref[...]).astype(o_ref.dtype)


def _proj(x2, g, w_main, col_scale, w_gates):
    T, D = x2.shape
    N = w_main.shape[1]
    NG = w_gates.shape[1]
    return pl.pallas_call(
        _proj_kernel,
        grid=(T // PROJ_TM, N // PROJ_TN),
        in_specs=[
            pl.BlockSpec((PROJ_TM, D), lambda m, n: (m, 0)),
            pl.BlockSpec((1, D), lambda m, n: (0, 0)),
            pl.BlockSpec((D, PROJ_TN), lambda m, n: (0, n)),
            pl.BlockSpec((1, PROJ_TN), lambda m, n: (0, n)),
            pl.BlockSpec((D, NG), lambda m, n: (0, 0)),
        ],
        out_specs=[
            pl.BlockSpec((PROJ_TM, PROJ_TN), lambda m, n: (m, n)),
            pl.BlockSpec((PROJ_TM, NG), lambda m, n: (m, 0)),
        ],
        out_shape=[
            jax.ShapeDtypeStruct((T, N), BF16),
            jax.ShapeDtypeStruct((T, NG), F32),
        ],
        scratch_shapes=[pltpu.VMEM((PROJ_TM, D), BF16)],
        compiler_params=_cparams(("parallel", "arbitrary")),
        name="rms_in_proj",
    )(x2, g, w_main, col_scale, w_gates)


def _t5_bucket(rel):
    half = N_BUCKETS // 2
    max_exact = half // 2
    ret = jnp.where(rel > 0, half, 0)
    n = jnp.abs(rel)
    nf = jnp.maximum(n, 1).astype(F32)
    large = max_exact + (jnp.log(nf / max_exact) / math.log(MAX_DISTANCE / max_exact)
                         * (half - max_exact)).astype(jnp.int32)
    large = jnp.minimum(large, half - 1)
    return ret + jnp.where(n < max_exact, n, large)


def _attn_kernel(scal_ref, q_ref, k_ref, v_ref, bias_ref, g_ref, o_ref, m_ref, l_ref, acc_ref,
                 s0_ref, s1_ref, ml0_ref, ml1_ref, bt_ref, *, lambda_init):
    h = pl.program_id(1)
    qi = pl.program_id(2)
    tq = ATT_TQ
    lam = scal_ref[0]
    cfar = scal_ref[1 + h]

    q = q_ref[0]
    lane = lax.broadcasted_iota(jnp.int32, q.shape, 1)
    zero = jnp.zeros_like(q)
    qs = jnp.concatenate([jnp.where(lane < DIFF_HEAD_DIM, q, zero),
                          jnp.where(lane >= DIFF_HEAD_DIM, q, zero)], axis=0)

    m_ref[...] = jnp.full(m_ref.shape, NEG_BIG, F32)
    l_ref[...] = jnp.zeros(l_ref.shape, F32)
    acc_ref[...] = jnp.zeros(acc_ref.shape, F32)

    @pl.when(qi == 0)
    def _():
        kj = lax.broadcasted_iota(jnp.int32, (tq, tq), 0)
        qq = lax.broadcasted_iota(jnp.int32, (tq, tq), 1)
        allowed = (kj // CHUNK) <= (qq // CHUNK)
        for d in range(2):
            rows = jnp.broadcast_to(bias_ref[0, d], (tq, 2 * tq))
            tile = pltpu.roll(rows, tq + 1, 1, stride=1, stride_axis=0)[:, :tq]
            if d == 1:
                tile = jnp.where(allowed, tile, NEG_BIG)
            bt_ref[d] = tile

    bufs = ((s0_ref, ml0_ref), (s1_ref, ml1_ref))

    def score(ki, bias, slot):
        s_ref, ml_ref = bufs[slot]
        start = pl.multiple_of(ki * tq, tq)
        kt = k_ref[0, pl.ds(start, tq), :]
        s = lax.dot_general(kt, qs, (((1,), (1,)), ((), ())), preferred_element_type=F32)
        if bias is not None:
            s = s + jnp.concatenate([bias, bias], axis=1)
        s_ref[...] = s
        ml_ref[...] = jnp.max(s, axis=0, keepdims=True)

    def accumulate(ki, shift, slot):
        s_ref, ml_ref = bufs[slot]
        start = pl.multiple_of(ki * tq, tq)
        vt = v_ref[0, pl.ds(start, tq), :]
        m_old = m_ref[...]
        m_new = jnp.maximum(m_old, ml_ref[...] + shift)
        alpha = jnp.exp2(m_old - m_new)
        p = jnp.exp2(s_ref[...] - (m_new - shift))
        l_ref[...] = alpha * l_ref[...] + jnp.sum(p, axis=0, keepdims=True)
        pv = lax.dot_general(vt, p.astype(BF16), (((0,), (0,)), ((), ())), preferred_element_type=F32)
        acc_ref[...] = alpha * acc_ref[...] + pv
        m_ref[...] = m_new

    n_far = qi - 1
    score(qi, bt_ref[1], 0)

    @pl.when(qi == 0)
    def _():
        accumulate(qi, 0.0, 0)

    @pl.when(qi >= 1)
    def _():
        accumulate(qi, 0.0, 0)
        score(qi - 1, bt_ref[0], 1)

    @pl.when(qi == 1)
    def _():
        accumulate(qi - 1, 0.0, 1)

    @pl.when(qi >= 2)
    def _():
        accumulate(qi - 1, 0.0, 1)
        score(0, None, 0)
        trips = (n_far - 1) // 2

        def pair(j, c):
            accumulate(2 * j, cfar, 0)
            score(2 * j + 1, None, 1)
            accumulate(2 * j + 1, cfar, 1)
            score(2 * j + 2, None, 0)
            return c

        lax.fori_loop(0, trips, pair, 0)
        last = 2 * trips

        @pl.when(n_far - last == 2)
        def _():
            accumulate(last, cfar, 0)
            score(last + 1, None, 1)
            accumulate(last + 1, cfar, 1)

        @pl.when(n_far - last == 1)
        def _():
            accumulate(last, cfar, 0)

    acc = acc_ref[...] * (1.0 / l_ref[...])
    o_t = acc[:, 0:tq] - lam * acc[:, tq:2 * tq]
    ms = jnp.mean(o_t * o_t, axis=0, keepdims=True)
    y = (o_t * lax.rsqrt(ms + EPS)).T * (g_ref[...] * (1.0 - lambda_init))
    o_ref[0] = y.astype(o_ref.dtype)


def _diff_attention(proj3, scal, bias_vecs, gnorm, lambda_init):
    B, S, _ = proj3.shape
    H = N_DIFF_HEADS
    tq = ATT_TQ
    kern = functools.partial(_attn_kernel, lambda_init=lambda_init)
    return pl.pallas_call(
        kern,
        grid=(B, H, S // tq),
        in_specs=[
            pl.BlockSpec(memory_space=pltpu.SMEM),
            pl.BlockSpec((1, tq, LANES), lambda b, h, i: (b, i, h)),
            pl.BlockSpec((1, S, LANES), lambda b, h, i: (b, 0, H + h)),
            pl.BlockSpec((1, S, LANES), lambda b, h, i: (b, 0, 2 * H + h)),
            pl.BlockSpec((1, 2, 1, 2 * tq), lambda b, h, i: (h, 0, 0, 0)),
            pl.BlockSpec((1, LANES), lambda b, h, i: (0, h)),
        ],
        out_specs=pl.BlockSpec((1, tq, LANES), lambda b, h, i: (b, i, h)),
        out_shape=jax.ShapeDtypeStruct((B, S, H * LANES), BF16),
        scratch_shapes=[
            pltpu.VMEM((1, 2 * tq), F32),
            pltpu.VMEM((1, 2 * tq), F32),
            pltpu.VMEM((LANES, 2 * tq), F32),
            pltpu.VMEM((tq, 2 * tq), F32),
            pltpu.VMEM((tq, 2 * tq), F32),
            pltpu.VMEM((1, 2 * tq), F32),
            pltpu.VMEM((1, 2 * tq), F32),
            pltpu.VMEM((2, tq, tq), F32),
        ],
        compiler_params=_cparams(("parallel", "parallel", "arbitrary")),
        name="diff_attention",
    )(scal, proj3, proj3, proj3, bias_vecs, gnorm)


def _log_sigmoid(x):
    return jnp.minimum(x, 0.0) - jnp.log(1.0 + jnp.exp(-jnp.abs(x)))


def _sigmoid(x):
    return 1.0 / (1.0 + jnp.exp(-x))


def _mlstm_kernel(q_ref, k_ref, v_ref, o_ref, gi_ref, gf_ref, cw_ref, cb_ref, bi_ref, bf_ref, gn_ref,
                  out_ref, qext_ref, kext_ref, ct_ref, n_ref, m_ref):
    sb = pl.program_id(1)
    L = CHUNK
    dh = MLSTM_HEAD_DIM
    H = N_MLSTM_HEADS
    ts = MLSTM_TS
    pad = 8

    @pl.when(sb == 0)
    def _():
        qext_ref[0:pad, :] = jnp.zeros((pad, H * dh), F32)
        kext_ref[0:pad, :] = jnp.zeros((pad, H * dh), F32)
        ct_ref[...] = jnp.zeros(ct_ref.shape, F32)
        n_ref[...] = jnp.zeros(n_ref.shape, F32)
        m_ref[...] = jnp.zeros(m_ref.shape, F32)

    qext_ref[pad:pad + ts, :] = q_ref[0].astype(F32)
    kext_ref[pad:pad + ts, :] = k_ref[0].astype(F32)

    row = lax.broadcasted_iota(jnp.int32, (L, L), 0)
    col = lax.broadcasted_iota(jnp.int32, (L, L), 1)
    tril = col <= row
    ltri = tril.astype(F32)

    def conv_silu(ext_ref, base, h, off):
        win = ext_ref[pl.ds(base, L + pad), h * dh:(h + 1) * dh]
        w = cw_ref[:, off + h * dh:off + (h + 1) * dh]
        y = cb_ref[:, off + h * dh:off + (h + 1) * dh]
        for j in range(CONV_WIDTH):
            lo = pad - (CONV_WIDTH - 1) + j
            y = y + w[j:j + 1, :] * win[lo:lo + L, :]
        return y * _sigmoid(y)

    def chunk_body(c, carry):
        base = pl.multiple_of(c * L, L)
        li = gi_ref[0, pl.ds(base, L), :] + bi_ref[...]
        logf = _log_sigmoid(gf_ref[0, pl.ds(base, L), :] + bf_ref[...])
        b = jnp.dot(ltri, logf, preferred_element_type=F32, precision=lax.Precision.HIGHEST)
        a = li - b
        g_row = b[L - 1:L, :]
        m_row = m_ref[...]
        m_new_row = g_row + jnp.maximum(m_row, jnp.max(a, axis=0, keepdims=True))
        a_t = a.T

        for h in range(H):
            qc = conv_silu(qext_ref, base, h, 0)
            kc = conv_silu(kext_ref, base, h, H * dh) * (dh ** -0.5)
            qb = qc.astype(BF16)
            kb = kc.astype(BF16)
            vb = v_ref[0, pl.ds(base, L), h * dh:(h + 1) * dh]

            a_row = a_t[h:h + 1, :]
            a_col = a[:, h:h + 1]
            b_col = b[:, h:h + 1]
            m_prev = m_row[:, h:h + 1]
            m_next = m_new_row[:, h:h + 1]
            g_h = g_row[:, h:h + 1]

            amat = jnp.where(tril, a_row, NEG_BIG)
            mcol = jnp.maximum(jnp.max(amat, axis=-1, keepdims=True), m_prev)
            wts = jnp.exp(amat - mcol)
            inter = jnp.exp(m_prev - mcol)

            s = lax.dot_general(qb, kb, (((1,), (1,)), ((), ())), preferred_element_type=F32)
            sqk = s * wts
            ct = ct_ref[h]
            nrow = n_ref[h:h + 1, :]
            num = (jnp.dot(sqk.astype(BF16), vb, preferred_element_type=F32)
                   + inter * jnp.dot(qb, ct.astype(BF16), preferred_element_type=F32))
            den = (jnp.sum(sqk, axis=-1, keepdims=True)
                   + inter * jnp.sum(qb.astype(F32) * nrow, axis=-1, keepdims=True))
            hv = num / jnp.maximum(jnp.abs(den), jnp.exp(-(b_col + mcol)))

            wt = jnp.exp(g_h + a_col - m_next)
            decay = jnp.exp(g_h + m_prev - m_next)
            wv = (wt * vb.astype(F32)).astype(BF16)
            ct_ref[h] = decay * ct + lax.dot_general(kb, wv, (((0,), (0,)), ((), ())),
                                                     preferred_element_type=F32)
            n_ref[h:h + 1, :] = decay * nrow + jnp.sum(wt * kb.astype(F32), axis=0, keepdims=True)

            ms = jnp.mean(hv * hv, axis=-1, keepdims=True)
            y = hv * lax.rsqrt(ms + EPS) * gn_ref[:, h * dh:(h + 1) * dh]
            og = o_ref[0, pl.ds(base, L), h * dh:(h + 1) * dh].astype(F32)
            out_ref[0, pl.ds(base, L), h * dh:(h + 1) * dh] = (y * _sigmoid(og)).astype(out_ref.dtype)

        m_ref[...] = m_new_row
        return carry

    lax.fori_loop(0, ts // L, chunk_body, 0, unroll=2)

    qext_ref[0:pad, :] = qext_ref[ts:ts + pad, :]
    kext_ref[0:pad, :] = kext_ref[ts:ts + pad, :]


def _mlstm(proj3, gates3, conv_w, conv_b, bi_row, bf_row, gnorm):
    B, S, _ = proj3.shape
    W = N_MLSTM_HEADS * MLSTM_HEAD_DIM
    ts = MLSTM_TS
    first = 3
    blk = lambda j: pl.BlockSpec((1, ts, W), lambda b, s: (b, s, j))
    full = lambda shape: pl.BlockSpec(shape, lambda b, s: (0,) * len(shape))
    return pl.pallas_call(
        _mlstm_kernel,
        grid=(B, S // ts),
        in_specs=[
            blk(first), blk(first + 1), blk(first + 2), blk(first + 3),
            pl.BlockSpec((1, ts, LANES), lambda b, s: (b, s, 0)),
            pl.BlockSpec((1, ts, LANES), lambda b, s: (b, s, 1)),
            full((CONV_WIDTH, 2 * W)), full((1, 2 * W)),
            full((1, LANES)), full((1, LANES)), full((1, W)),
        ],
        out_specs=pl.BlockSpec((1, ts, W), lambda b, s: (b, s, 0)),
        out_shape=jax.ShapeDtypeStruct((B, S, W), BF16),
        scratch_shapes=[
            pltpu.VMEM((ts + 8, W), F32),
            pltpu.VMEM((ts + 8, W), F32),
            pltpu.VMEM((N_MLSTM_HEADS, MLSTM_HEAD_DIM, MLSTM_HEAD_DIM), F32),
            pltpu.VMEM((N_MLSTM_HEADS, MLSTM_HEAD_DIM), F32),
            pltpu.VMEM((1, LANES), F32),
        ],
        compiler_params=_cparams(("parallel", "arbitrary")),
        name="mlstm",
    )(proj3, proj3, proj3, proj3, gates3, gates3, conv_w, conv_b, bi_row, bf_row, gnorm)


def _out_kernel(x_ref, a_ref, hm_ref, wa_ref, wm_ref, g_ref, wr_ref, x1_ref, h2_ref, lg_ref):
    y = (jnp.dot(a_ref[...], wa_ref[...].astype(BF16), preferred_element_type=F32)
         + jnp.dot(hm_ref[...], wm_ref[...].astype(BF16), preferred_element_type=F32))
    x1 = x_ref[...] + y
    x1_ref[...] = x1
    ms = jnp.mean(x1 * x1, axis=-1, keepdims=True)
    h2 = x1 * lax.rsqrt(ms + EPS) * g_ref[...]
    h2_ref[...] = _pack_bf16_pairs(h2)
    lg_ref[...] = jnp.dot(h2.astype(BF16), wr_ref[...], preferred_element_type=F32)


def _out_proj(x2, a2, hm2, wa, wm, g, wr):
    T, D = x2.shape
    W = a2.shape[1]
    tm = OUT_TM
    const = lambda shape: pl.BlockSpec(shape, lambda m: (0, 0), pipeline_mode=pl.Buffered(1))
    return pl.pallas_call(
        _out_kernel,
        grid=(T // tm,),
        in_specs=[
            pl.BlockSpec((tm, D), lambda m: (m, 0)),
            pl.BlockSpec((tm, W), lambda m: (m, 0)),
            pl.BlockSpec((tm, W), lambda m: (m, 0)),
            const((W, D)),
            pl.BlockSpec((W, D), lambda m: (1, 0), pipeline_mode=pl.Buffered(1)),
            const((1, D)), const((D, LANES)),
        ],
        out_specs=[
            pl.BlockSpec((tm, D), lambda m: (m, 0)),
            pl.BlockSpec((tm, D // 2), lambda m: (m, 0)),
            pl.BlockSpec((tm, LANES), lambda m: (m, 0)),
        ],
        out_shape=[
            jax.ShapeDtypeStruct((T, D), F32),
            jax.ShapeDtypeStruct((T, D // 2), jnp.uint32),
            jax.ShapeDtypeStruct((T, LANES), F32),
        ],
        compiler_params=_cparams(("parallel",)),
        name="out_proj_router",
    )(x2, a2, hm2, wa, wm, g, wr)


def _moe_kernel(te_ref, nv_ref, first_ref, slot_ref, nxt_ref, rows_ref, xs_ref, wg_hbm, wu_hbm, wd_hbm,
                ys_ref, wg_buf, wu_buf, wd_buf, sem):
    j = pl.program_id(0)
    valid = j < nv_ref[0]

    half_f = wd_buf.shape[1] // 2

    def weight_copies(e, s):
        lo, hi = pl.ds(0, half_f), pl.ds(half_f, half_f)
        return ((pltpu.make_async_copy(wg_hbm.at[e], wg_buf.at[s], sem.at[s, 0]), 0),
                (pltpu.make_async_copy(wu_hbm.at[e], wu_buf.at[s], sem.at[s, 1]), 1),
                (pltpu.make_async_copy(wd_hbm.at[e, lo], wd_buf.at[s, lo], sem.at[s, 2]), 0),
                (pltpu.make_async_copy(wd_hbm.at[e, hi], wd_buf.at[s, hi], sem.at[s, 3]), 1))

    @pl.when(j == 0)
    def _():
        for c, prio in weight_copies(te_ref[0], 0):
            c.start(priority=prio)

    @pl.when(jnp.logical_and(valid, first_ref[j] == 1))
    def _():
        for c, _ in weight_copies(te_ref[j], slot_ref[j]):
            c.wait()

        @pl.when(nxt_ref[j] >= 0)
        def _():
            for c, prio in weight_copies(nxt_ref[j], 1 - slot_ref[j]):
                c.start(priority=prio)

    @pl.when(valid)
    def _():
        s = slot_ref[j]
        row = lax.broadcasted_iota(jnp.int32, xs_ref.shape, 0)
        lo, hi = _unpack_bf16_pairs(jnp.where(row < rows_ref[j], xs_ref[...], jnp.uint32(0)))
        xs = jnp.concatenate([lo.astype(BF16), hi.astype(BF16)], axis=1)
        gt = jnp.dot(xs, wg_buf[s].astype(BF16), preferred_element_type=F32)
        up = jnp.dot(xs, wu_buf[s].astype(BF16), preferred_element_type=F32)
        hid = (gt * _sigmoid(gt) * up).astype(BF16)
        ys_ref[...] = _pack_bf16_pairs(jnp.dot(hid, wd_buf[s].astype(BF16), preferred_element_type=F32))

    @pl.when(jnp.logical_not(valid))
    def _():
        ys_ref[...] = jnp.zeros(ys_ref.shape, ys_ref.dtype)


def _moe(tile_expert, n_valid, tile_first, tile_slot, tile_next, tile_rows, xs, wg, wu, wd):
    R, Dw = xs.shape
    D, F = wg.shape[1], wg.shape[2]
    tm = MOE_TM
    hbm = pl.BlockSpec(memory_space=pl.ANY)
    grid_spec = pltpu.PrefetchScalarGridSpec(
        num_scalar_prefetch=6,
        grid=(R // tm,),
        in_specs=[pl.BlockSpec((tm, Dw), lambda j, *_: (j, 0)), hbm, hbm, hbm],
        out_specs=pl.BlockSpec((tm, Dw), lambda j, *_: (j, 0)),
        scratch_shapes=[
            pltpu.VMEM((2, D, F), wg.dtype),
            pltpu.VMEM((2, D, F), wu.dtype),
            pltpu.VMEM((2, F, D), wd.dtype),
            pltpu.SemaphoreType.DMA((2, 4)),
        ],
    )
    return pl.pallas_call(
        _moe_kernel,
        grid_spec=grid_spec,
        out_shape=jax.ShapeDtypeStruct((R, Dw), jnp.uint32),
        compiler_params=_cparams(("arbitrary",)),
        name="moe_experts",
    )(tile_expert, n_valid, tile_first, tile_slot, tile_next, tile_rows, xs, wg, wu, wd)


def _final_kernel(x1_ref, y0_ref, y1_ref, cw_ref, g_ref, o_ref):
    cw = cw_ref[...]
    lo0, hi0 = _unpack_bf16_pairs(y0_ref[...])
    lo1, hi1 = _unpack_bf16_pairs(y1_ref[...])
    w0, w1 = cw[:, 0:1], cw[:, 1:2]
    y = jnp.concatenate([w0 * lo0 + w1 * lo1, w0 * hi0 + w1 * hi1], axis=1)
    x = x1_ref[...] + y
    ms = jnp.mean(x * x, axis=-1, keepdims=True)
    o_ref[...] = x * lax.rsqrt(ms + EPS) * g_ref[...]


def _final(x1, yw, cw, g):
    T, D = x1.shape
    tm = FIN_TM
    row = lambda w: pl.BlockSpec((tm, w), lambda m: (m, 0))
    slot1 = pl.BlockSpec((tm, D // 2), lambda m: (m + T // tm, 0))
    return pl.pallas_call(
        _final_kernel,
        grid=(T // tm,),
        in_specs=[row(D), row(D // 2), slot1, row(LANES), pl.BlockSpec((1, D), lambda m: (0, 0))],
        out_specs=row(D),
        out_shape=jax.ShapeDtypeStruct((T, D), F32),
        compiler_params=_cparams(("parallel",)),
        name="combine_final_norm",
    )(x1, yw, yw, cw, g)


SC_CORES, SC_SUBCORES = 2, 16
SC_CHUNK = 32


def _sc_gather_rows(table, idx):
    V, Dw = table.shape
    R = idx.shape[0]
    n_workers = SC_CORES * SC_SUBCORES
    ch = SC_CHUNK
    per_w = R // n_workers
    n_chunks = per_w // ch
    assert per_w * n_workers == R and n_chunks * ch == per_w and n_chunks % 2 == 0
    idx3 = idx.reshape(n_workers, n_chunks, ch)
    mesh = plsc.VectorSubcoreMesh(core_axis_name="c", subcore_axis_name="s")

    def body(table_hbm, idx_hbm, out_hbm, idx_v, rows_v, gsem, osem):
        wid = lax.axis_index("s") * SC_CORES + lax.axis_index("c")
        base = wid * per_w
        pltpu.sync_copy(idx_hbm.at[wid], idx_v)

        def gather(c, slot):
            return pltpu.make_async_copy(table_hbm.at[idx_v.at[c]], rows_v.at[slot], gsem.at[slot])

        def put(c, slot):
            return pltpu.make_async_copy(rows_v.at[slot], out_hbm.at[pl.ds(base + c * ch, ch)],
                                         osem.at[slot])

        gather(0, 0).start()

        @pl.loop(0, n_chunks, step=2)
        def _(c):
            @pl.when(c > 0)
            def _():
                put(c - 1, 1).wait()

            gather(c + 1, 1).start()
            gather(c, 0).wait()
            put(c, 0).start()
            put(c, 0).wait()

            @pl.when(c + 2 < n_chunks)
            def _():
                gather(c + 2, 0).start()

            gather(c + 1, 1).wait()
            put(c + 1, 1).start()

        put(n_chunks - 1, 1).wait()

    return pl.kernel(
        body,
        out_type=jax.ShapeDtypeStruct((R, Dw), table.dtype),
        mesh=mesh,
        scratch_types=[
            pltpu.VMEM((n_chunks, ch), jnp.int32),
            pltpu.VMEM((2, ch, Dw), table.dtype),
            pltpu.SemaphoreType.DMA((2,)),
            pltpu.SemaphoreType.DMA((2,)),
        ],
        name="sc_gather_rows",
    )(table, idx3)


def _sc_scatter_rows(table, idx, n_rows_out):
    V, Dw = table.shape
    K = idx.shape[0]
    n_workers = SC_CORES * SC_SUBCORES
    ch = SC_CHUNK
    per_w = V // n_workers
    n_chunks = per_w // ch
    assert K == 2 and per_w * n_workers == V and n_chunks * ch == per_w and n_chunks % 2 == 0
    idx4 = jnp.transpose(idx.reshape(K, n_workers, n_chunks, ch), (1, 0, 2, 3))
    mesh = plsc.VectorSubcoreMesh(core_axis_name="c", subcore_axis_name="s")

    def body(table_hbm, idx_hbm, out_hbm, idx_v, rows_v, lsem, ssem):
        wid = lax.axis_index("s") * SC_CORES + lax.axis_index("c")
        base = wid * per_w
        pltpu.sync_copy(idx_hbm.at[wid], idx_v)

        def load(c, slot):
            return pltpu.make_async_copy(table_hbm.at[pl.ds(base + c * ch, ch)], rows_v.at[slot],
                                         lsem.at[slot])

        def scatter(c, slot, k):
            return pltpu.make_async_copy(rows_v.at[slot], out_hbm.at[idx_v.at[k, c]], ssem.at[slot, k])

        load(0, 0).start()

        @pl.loop(0, n_chunks, step=2)
        def _(c):
            @pl.when(c > 0)
            def _():
                scatter(c - 1, 1, 0).wait()
                scatter(c - 1, 1, 1).wait()

            load(c + 1, 1).start()
            load(c, 0).wait()
            scatter(c, 0, 0).start()
            scatter(c, 0, 1).start()
            scatter(c, 0, 0).wait()
            scatter(c, 0, 1).wait()

            @pl.when(c + 2 < n_chunks)
            def _():
                load(c + 2, 0).start()

            load(c + 1, 1).wait()
            scatter(c + 1, 1, 0).start()
            scatter(c + 1, 1, 1).start()

        scatter(n_chunks - 1, 1, 0).wait()
        scatter(n_chunks - 1, 1, 1).wait()

    return pl.kernel(
        body,
        out_type=jax.ShapeDtypeStruct((n_rows_out, Dw), table.dtype),
        mesh=mesh,
        scratch_types=[
            pltpu.VMEM((K, n_chunks, ch), jnp.int32),
            pltpu.VMEM((2, ch, Dw), table.dtype),
            pltpu.SemaphoreType.DMA((2,)),
            pltpu.SemaphoreType.DMA((2, K)),
        ],
        name="sc_scatter_rows",
    )(table, idx4)


ROUTE_BLK = 256


def _route_kernel(lg_ref, bias_ref, pos_ref, cw_ref, cnt_ref, oh_ref, pre_ref, *, tm):
    T = lg_ref.shape[0]
    G, E, NE = N_GROUPS, EXPERTS_PER_GROUP, N_EXPERTS
    blk = ROUTE_BLK
    lane = lax.broadcasted_iota(jnp.int32, (blk, LANES), 1).astype(F32)
    lane1 = lax.broadcasted_iota(jnp.int32, (1, LANES), 1)
    r_i = lax.broadcasted_iota(jnp.int32, (blk, blk), 0)
    c_i = lax.broadcasted_iota(jnp.int32, (blk, blk), 1)
    ltri = (c_i < r_i).astype(BF16)

    def first_argmax(v):
        mx = jnp.max(v, axis=1, keepdims=True)
        return mx, jnp.min(jnp.where(v == mx, lane, float(LANES)), axis=1, keepdims=True)

    def phase1(b, carry):
        rows = pl.ds(pl.multiple_of(b * blk, blk), blk)
        x = lg_ref[rows, :] + bias_ref[...]
        gl = jnp.where(lane < G, x, NEG_BIG)
        gmax, gsel = first_argmax(gl)
        gw = 1.0 / jnp.sum(jnp.exp(gl - gmax), axis=1, keepdims=True)
        lo = G + E * gsel
        el = jnp.where(jnp.logical_and(lane >= lo, lane < lo + E), x, NEG_BIG)
        v1, i1 = first_argmax(el)
        v2, i2 = first_argmax(jnp.where(lane == i1, NEG_BIG, el))
        e21 = jnp.exp(v2 - v1)
        w1 = 1.0 / (1.0 + e21)
        cw_ref[rows, :] = jnp.where(lane == 0, gw * w1, jnp.where(lane == 1, gw * (e21 * w1), 0.0))
        oh = jnp.where(jnp.logical_or(lane == i1 - G, lane == i2 - G + NE), 1.0, 0.0)
        oh_ref[rows, :] = oh
        pre_ref[rows, :] = jnp.dot(ltri, oh.astype(BF16), preferred_element_type=F32) + carry
        return carry + jnp.sum(oh, axis=0, keepdims=True)

    counts = lax.fori_loop(0, T // blk, phase1, jnp.zeros((1, LANES), F32))

    in_e = lane1 < NE
    c0 = jnp.where(in_e, counts, 0.0)
    ctot = jnp.where(in_e, counts + pltpu.roll(counts, LANES - NE, 1), 0.0)
    tiles = jnp.floor((ctot + (tm - 1)) * (1.0 / tm))
    u_r = lax.broadcasted_iota(jnp.int32, (LANES, LANES), 0)
    u_c = lax.broadcasted_iota(jnp.int32, (LANES, LANES), 1)
    utri = (u_r <= u_c).astype(BF16)
    tile_end = jnp.dot(jnp.broadcast_to(tiles, (8, LANES)).astype(BF16), utri,
                       preferred_element_type=F32)[0:1, :]
    row_start = (tile_end - tiles) * tm
    base = jnp.where(in_e, row_start, pltpu.roll(row_start + c0, NE, 1))
    cnt_ref[...] = jnp.concatenate([ctot, tiles, tile_end, jnp.zeros((5, LANES), F32)], axis=0)

    def phase2(b, carry):
        rows = pl.ds(pl.multiple_of(b * blk, blk), blk)
        t = oh_ref[rows, :] * (pre_ref[rows, :] + base)
        p0 = jnp.sum(jnp.where(lane < NE, t, 0.0), axis=1, keepdims=True)
        p1 = jnp.sum(jnp.where(lane >= NE, t, 0.0), axis=1, keepdims=True)
        pos_ref[rows, :] = jnp.where(lane == 0, p0, jnp.where(lane == 1, p1, 0.0)).astype(jnp.int32)
        return carry

    lax.fori_loop(0, T // blk, phase2, 0)


def _route(logits, bias_row, tm):
    T = logits.shape[0]
    whole = lambda shape: pl.BlockSpec(shape, lambda: (0,) * len(shape))
    return pl.pallas_call(
        functools.partial(_route_kernel, tm=tm),
        in_specs=[whole((T, LANES)), whole((1, LANES))],
        out_specs=[whole((T, LANES)), whole((T, LANES)), whole((8, LANES))],
        out_shape=[
            jax.ShapeDtypeStruct((T, LANES), jnp.int32),
            jax.ShapeDtypeStruct((T, LANES), F32),
            jax.ShapeDtypeStruct((8, LANES), F32),
        ],
        scratch_shapes=[pltpu.VMEM((T, LANES), F32), pltpu.VMEM((T, LANES), F32)],
        compiler_params=pltpu.CompilerParams(vmem_limit_bytes=VMEM_LIMIT),
        name="route_positions",
    )(logits, bias_row)


def _tile_plan(cnt, tm, n_tiles):
    counts = cnt[0, :N_EXPERTS].astype(jnp.int32)
    tiles_per_e = cnt[1, :N_EXPERTS].astype(jnp.int32)
    tile_end = cnt[2, :N_EXPERTS].astype(jnp.int32)
    n_valid = tile_end[-1]
    tile_ids = jnp.arange(n_tiles, dtype=jnp.int32)
    experts = jnp.arange(N_EXPERTS, dtype=jnp.int32)
    valid = tile_ids < n_valid
    jc = jnp.minimum(tile_ids, n_valid - 1)
    tile_expert = jnp.minimum(jnp.sum((tile_end[None, :] <= jc[:, None]).astype(jnp.int32), axis=1),
                              N_EXPERTS - 1)
    of_tile = (tile_expert[:, None] == experts[None, :]).astype(jnp.int32)
    pick = lambda per_expert: jnp.sum(of_tile * per_expert[None, :], axis=1)
    tile_start = tile_end - tiles_per_e
    nonempty = tiles_per_e > 0
    group_idx = jnp.sum(jnp.logical_and(nonempty[None, :], experts[None, :] <= experts[:, None]).astype(jnp.int32),
                        axis=1) - 1
    later = jnp.logical_and(nonempty[None, :], experts[None, :] > experts[:, None])
    next_e = jnp.min(jnp.where(later, experts[None, :], N_EXPERTS), axis=1)
    next_e = jnp.where(next_e == N_EXPERTS, -1, next_e)
    tile_first = jnp.logical_and(valid, tile_ids == pick(tile_start)).astype(jnp.int32)
    tile_slot = (pick(group_idx) % 2).astype(jnp.int32)
    tile_next = pick(next_e).astype(jnp.int32)
    tile_rows = jnp.clip(pick(counts) - (tile_ids - pick(tile_start)) * tm, 0, tm)
    tile_rows = jnp.where(valid, tile_rows, 0).astype(jnp.int32)
    return (tile_expert.astype(jnp.int32), n_valid.reshape(1), tile_first, tile_slot, tile_next, tile_rows)


def kernel(x, rel_bias, ln_mix_g, w_in, conv_w, conv_b, b_i, b_f, lam_q1, lam_k1, lam_q2, lam_k2,
           diff_norm_g, mlstm_norm_g, w_out, ln_ffn_g, w_group, b_group, w_router, b_router,
           w_gate, w_up, w_down, ln_f_g):
    B, S, D = x.shape
    T = B * S
    depth = w_in.shape[0]
    assert depth == 1, "the final rmsnorm is fused into the single layer's combine kernel"
    Hm = N_MLSTM_HEADS
    n_main = w_in.shape[2] - 2 * Hm
    n_diff = N_DIFF_HEADS * 2 * DIFF_HEAD_DIM
    xf = x.reshape(T, D)

    for l in range(depth):
        lambda_init = 0.8 - 0.6 * math.exp(-0.3 * l)
        w_main = w_in[l, :, :n_main].astype(BF16)
        wgt = w_in[l, :, n_main:]
        w_gates = jnp.zeros((D, 2 * LANES), F32).at[:, :Hm].set(wgt[:, :Hm]).at[:, LANES:LANES + Hm].set(
            wgt[:, Hm:]).astype(BF16)
        bi_row = jnp.zeros((1, LANES), F32).at[0, :Hm].set(b_i[l].astype(F32))
        bf_row = jnp.zeros((1, LANES), F32).at[0, :Hm].set(b_f[l].astype(F32))
        lam = (jnp.exp(jnp.sum(lam_q1[l].astype(F32) * lam_k1[l].astype(F32)))
               - jnp.exp(jnp.sum(lam_q2[l].astype(F32) * lam_k2[l].astype(F32))) + lambda_init)
        tq = ATT_TQ
        assert tq >= MAX_DISTANCE and tq % CHUNK == 0
        rb = rel_bias.astype(F32)
        log2e = math.log2(math.e)
        xx = jnp.arange(2 * tq, dtype=jnp.int32)
        rel_vec = jnp.stack([-tq + tq - 1 - xx, tq - 1 - xx], axis=0)
        bias_vecs = jnp.take(rb, _t5_bucket(rel_vec), axis=0) * log2e
        bias_vecs = jnp.transpose(bias_vecs, (2, 0, 1))[:, :, None, :]
        cfar = rb[N_BUCKETS // 2 - 1] * log2e
        scal = jnp.concatenate([lam.reshape(1), cfar]).astype(F32)
        col_scale = jnp.ones((1, n_main), F32).at[:, :n_diff].set(DIFF_HEAD_DIM ** -0.5 * log2e)

        proj, gates = _proj(xf, ln_mix_g[l].reshape(1, D).astype(F32), w_main, col_scale, w_gates)
        proj3 = proj.reshape(B, S, n_main)
        a = _diff_attention(proj3, scal, bias_vecs, diff_norm_g[l].reshape(1, n_diff).astype(F32),
                            lambda_init)
        hm = _mlstm(proj3, gates.reshape(B, S, 2 * LANES), conv_w[l].astype(F32),
                    conv_b[l].reshape(1, -1).astype(F32), bi_row, bf_row,
                    mlstm_norm_g[l].reshape(1, -1).astype(F32))

        wo = w_out[l]
        G, E = N_GROUPS, EXPERTS_PER_GROUP
        wr = jnp.zeros((D, LANES), F32).at[:, :G].set(w_group[l].astype(F32)).at[:, G:G + G * E].set(
            jnp.transpose(w_router[l].astype(F32), (1, 0, 2)).reshape(D, G * E)).astype(BF16)
        x1, h2, logits = _out_proj(xf, a.reshape(T, n_diff), hm.reshape(T, -1), wo, wo,
                                   ln_ffn_g[l].reshape(1, D).astype(F32), wr)

        route_bias = jnp.concatenate([b_group[l].astype(F32).reshape(-1), b_router[l].astype(F32).reshape(-1),
                                      jnp.zeros((LANES - G - G * E,), F32)]).reshape(1, LANES)
        n_tiles = (T * TOP_K_INNER) // MOE_TM + N_EXPERTS
        pos128, cwp, cnt = _route(logits, route_bias, MOE_TM)
        pos_t = pos128[:, :TOP_K_INNER].T
        tiles = _tile_plan(cnt, MOE_TM, n_tiles)
        xs = _sc_scatter_rows(h2, pos_t, n_tiles * MOE_TM)
        Fe = w_gate.shape[-1]
        ys = _moe(*tiles, xs, w_gate[l].reshape(N_EXPERTS, D, Fe),
                  w_up[l].reshape(N_EXPERTS, D, Fe), w_down[l].reshape(N_EXPERTS, Fe, D))
        yw = _sc_gather_rows(ys, pos_t.reshape(-1))
        xf = _final(x1, yw, cwp, ln_f_g.reshape(1, D).astype(F32))
    return xf.reshape(B, S, D)
```

```python
import functools
import math

import jax
import jax.numpy as jnp
from jax import lax
from jax.experimental import pallas as pl
from jax.experimental.pallas import tpu as pltpu
from jax.experimental.pallas import tpu_sc as plsc

F32 = jnp.float32
BF16 = jnp.bfloat16

EPS = 1e-6
CHUNK = 64
DIFF_HEAD_DIM = 64
N_DIFF_HEADS = 8
MLSTM_HEAD_DIM = 128
N_MLSTM_HEADS = 8
CONV_WIDTH = 4
N_BUCKETS = 32
MAX_DISTANCE = 128
N_GROUPS = 4
EXPERTS_PER_GROUP = 8
N_EXPERTS = N_GROUPS * EXPERTS_PER_GROUP
TOP_K_INNER = 2
LANES = 128
NEG_BIG = -1e30

VMEM_LIMIT = 56 * 1024 * 1024

PROJ_TM, PROJ_TN = 1024, 1024
ATT_TQ = 512
MLSTM_TS = 1024
OUT_TM = 512
MOE_TM = 256
FIN_TM = 512


def _cparams(sem):
    return pltpu.CompilerParams(dimension_semantics=sem, vmem_limit_bytes=VMEM_LIMIT)


_HI_MASK = 0xFFFF0000


def _pack_bf16_pairs(x):
    half = x.shape[-1] // 2
    xb = x.astype(BF16).astype(F32)
    lo = pltpu.bitcast(xb[:, :half], jnp.uint32)
    hi = pltpu.bitcast(xb[:, half:], jnp.uint32)
    return (hi & jnp.uint32(_HI_MASK)) | (lo >> 16)


def _unpack_bf16_pairs(w):
    lo = pltpu.bitcast(w << 16, F32)
    hi = pltpu.bitcast(w & jnp.uint32(_HI_MASK), F32)
    return lo, hi


def _proj_kernel(x_ref, g_ref, w_ref, cs_ref, wg_ref, o_ref, og_ref, h_ref):
    @pl.when(pl.program_id(1) == 0)
    def _():
        x = x_ref[...]
        ms = jnp.mean(x * x, axis=-1, keepdims=True)
        h = (x * lax.rsqrt(ms + EPS) * g_ref[...]).astype(BF16)
        h_ref[...] = h
        og_ref[...] = jnp.dot(h, wg_ref[...], preferred_element_type=F32)

    o_ref[...] = (jnp.dot(h_ref[...], w_ref[...], preferred_element_type=F32) * cs_ref[...]).astype(o_ref.dtype)


def _proj(x2, g, w_main, col_scale, w_gates):
    T, D = x2.shape
    N = w_main.shape[1]
    NG = w_gates.shape[1]
    return pl.pallas_call(
        _proj_kernel,
        grid=(T // PROJ_TM, N // PROJ_TN),
        in_specs=[
            pl.BlockSpec((PROJ_TM, D), lambda m, n: (m, 0)),
            pl.BlockSpec((1, D), lambda m, n: (0, 0)),
            pl.BlockSpec((D, PROJ_TN), lambda m, n: (0, n)),
            pl.BlockSpec((1, PROJ_TN), lambda m, n: (0, n)),
            pl.BlockSpec((D, NG), lambda m, n: (0, 0)),
        ],
        out_specs=[
            pl.BlockSpec((PROJ_TM, PROJ_TN), lambda m, n: (m, n)),
            pl.BlockSpec((PROJ_TM, NG), lambda m, n: (m, 0)),
        ],
        out_shape=[
            jax.ShapeDtypeStruct((T, N), BF16),
            jax.ShapeDtypeStruct((T, NG), F32),
        ],
        scratch_shapes=[pltpu.VMEM((PROJ_TM, D), BF16)],
        compiler_params=_cparams(("parallel", "arbitrary")),
        name="rms_in_proj",
    )(x2, g, w_main, col_scale, w_gates)


def _t5_bucket(rel):
    half = N_BUCKETS // 2
    max_exact = half // 2
    ret = jnp.where(rel > 0, half, 0)
    n = jnp.abs(rel)
    nf = jnp.maximum(n, 1).astype(F32)
    large = max_exact + (jnp.log(nf / max_exact) / math.log(MAX_DISTANCE / max_exact)
                         * (half - max_exact)).astype(jnp.int32)
    large = jnp.minimum(large, half - 1)
    return ret + jnp.where(n < max_exact, n, large)


def _attn_kernel(scal_ref, q_ref, k_ref, v_ref, bias_ref, g_ref, o_ref, m_ref, l_ref, acc_ref,
                 s0_ref, s1_ref, ml0_ref, ml1_ref, bt_ref, *, lambda_init):
    h = pl.program_id(1)
    qi = pl.program_id(2)
    tq = ATT_TQ
    lam = scal_ref[0]
    cfar = scal_ref[1 + h]

    q = q_ref[0]
    lane = lax.broadcasted_iota(jnp.int32, q.shape, 1)
    zero = jnp.zeros_like(q)
    qs = jnp.concatenate([jnp.where(lane < DIFF_HEAD_DIM, q, zero),
                          jnp.where(lane >= DIFF_HEAD_DIM, q, zero)], axis=0)

    m_ref[...] = jnp.full(m_ref.shape, NEG_BIG, F32)
    l_ref[...] = jnp.zeros(l_ref.shape, F32)
    acc_ref[...] = jnp.zeros(acc_ref.shape, F32)

    @pl.when(qi == 0)
    def _():
        kj = lax.broadcasted_iota(jnp.int32, (tq, tq), 0)
        qq = lax.broadcasted_iota(jnp.int32, (tq, tq), 1)
        allowed = (kj // CHUNK) <= (qq // CHUNK)
        for d in range(2):
            rows = jnp.broadcast_to(bias_ref[0, d], (tq, 2 * tq))
            tile = pltpu.roll(rows, tq + 1, 1, stride=1, stride_axis=0)[:, :tq]
            if d == 1:
                tile = jnp.where(allowed, tile, NEG_BIG)
            bt_ref[d] = tile

    bufs = ((s0_ref, ml0_ref), (s1_ref, ml1_ref))

    def score(ki, bias, slot):
        s_ref, ml_ref = bufs[slot]
        start = pl.multiple_of(ki * tq, tq)
        kt = k_ref[0, pl.ds(start, tq), :]
        s = lax.dot_general(kt, qs, (((1,), (1,)), ((), ())), preferred_element_type=F32)
        if bias is not None:
            s = s + jnp.concatenate([bias, bias], axis=1)
        s_ref[...] = s
        ml_ref[...] = jnp.max(s, axis=0, keepdims=True)

    def accumulate(ki, shift, slot):
        s_ref, ml_ref = bufs[slot]
        start = pl.multiple_of(ki * tq, tq)
        vt = v_ref[0, pl.ds(start, tq), :]
        m_old = m_ref[...]
        m_new = jnp.maximum(m_old, ml_ref[...] + shift)
        alpha = jnp.exp2(m_old - m_new)
        p = jnp.exp2(s_ref[...] - (m_new - shift))
        l_ref[...] = alpha * l_ref[...] + jnp.sum(p, axis=0, keepdims=True)
        pv = lax.dot_general(vt, p.astype(BF16), (((0,), (0,)), ((), ())), preferred_element_type=F32)
        acc_ref[...] = alpha * acc_ref[...] + pv
        m_ref[...] = m_new

    n_far = qi - 1
    score(qi, bt_ref[1], 0)

    @pl.when(qi == 0)
    def _():
        accumulate(qi, 0.0, 0)

    @pl.when(qi >= 1)
    def _():
        accumulate(qi, 0.0, 0)
        score(qi - 1, bt_ref[0], 1)

    @pl.when(qi == 1)
    def _():
        accumulate(qi - 1, 0.0, 1)

    @pl.when(qi >= 2)
    def _():
        accumulate(qi - 1, 0.0, 1)
        score(0, None, 0)
        trips = (n_far - 1) // 2

        def pair(j, c):
            accumulate(2 * j, cfar, 0)
            score(2 * j + 1, None, 1)
            accumulate(2 * j + 1, cfar, 1)
            score(2 * j + 2, None, 0)
            return c

        lax.fori_loop(0, trips, pair, 0)
        last = 2 * trips

        @pl.when(n_far - last == 2)
        def _():
            accumulate(last, cfar, 0)
            score(last + 1, None, 1)
            accumulate(last + 1, cfar, 1)

        @pl.when(n_far - last == 1)
        def _():
            accumulate(last, cfar, 0)

    acc = acc_ref[...] * (1.0 / l_ref[...])
    o_t = acc[:, 0:tq] - lam * acc[:, tq:2 * tq]
    ms = jnp.mean(o_t * o_t, axis=0, keepdims=True)
    y = (o_t * lax.rsqrt(ms + EPS)).T * (g_ref[...] * (1.0 - lambda_init))
    o_ref[0] = y.astype(o_ref.dtype)


def _diff_attention(proj3, scal, bias_vecs, gnorm, lambda_init):
    B, S, _ = proj3.shape
    H = N_DIFF_HEADS
    tq = ATT_TQ
    kern = functools.partial(_attn_kernel, lambda_init=lambda_init)
    return pl.pallas_call(
        kern,
        grid=(B, H, S // tq),
        in_specs=[
            pl.BlockSpec(memory_space=pltpu.SMEM),
            pl.BlockSpec((1, tq, LANES), lambda b, h, i: (b, i, h)),
            pl.BlockSpec((1, S, LANES), lambda b, h, i: (b, 0, H + h)),
            pl.BlockSpec((1, S, LANES), lambda b, h, i: (b, 0, 2 * H + h)),
            pl.BlockSpec((1, 2, 1, 2 * tq), lambda b, h, i: (h, 0, 0, 0)),
            pl.BlockSpec((1, LANES), lambda b, h, i: (0, h)),
        ],
        out_specs=pl.BlockSpec((1, tq, LANES), lambda b, h, i: (b, i, h)),
        out_shape=jax.ShapeDtypeStruct((B, S, H * LANES), BF16),
        scratch_shapes=[
            pltpu.VMEM((1, 2 * tq), F32),
            pltpu.VMEM((1, 2 * tq), F32),
            pltpu.VMEM((LANES, 2 * tq), F32),
            pltpu.VMEM((tq, 2 * tq), F32),
            pltpu.VMEM((tq, 2 * tq), F32),
            pltpu.VMEM((1, 2 * tq), F32),
            pltpu.VMEM((1, 2 * tq), F32),
            pltpu.VMEM((2, tq, tq), F32),
        ],
        compiler_params=_cparams(("parallel", "parallel", "arbitrary")),
        name="diff_attention",
    )(scal, proj3, proj3, proj3, bias_vecs, gnorm)


def _log_sigmoid(x):
    return jnp.minimum(x, 0.0) - jnp.log(1.0 + jnp.exp(-jnp.abs(x)))


def _sigmoid(x):
    return 1.0 / (1.0 + jnp.exp(-x))


def _mlstm_kernel(q_ref, k_ref, v_ref, o_ref, gi_ref, gf_ref, cw_ref, cb_ref, bi_ref, bf_ref, gn_ref,
                  out_ref, qext_ref, kext_ref, ct_ref, n_ref, m_ref):
    sb = pl.program_id(1)
    L = CHUNK
    dh = MLSTM_HEAD_DIM
    H = N_MLSTM_HEADS
    ts = MLSTM_TS
    pad = 8

    @pl.when(sb == 0)
    def _():
        qext_ref[0:pad, :] = jnp.zeros((pad, H * dh), F32)
        kext_ref[0:pad, :] = jnp.zeros((pad, H * dh), F32)
        ct_ref[...] = jnp.zeros(ct_ref.shape, F32)
        n_ref[...] = jnp.zeros(n_ref.shape, F32)
        m_ref[...] = jnp.zeros(m_ref.shape, F32)

    qext_ref[pad:pad + ts, :] = q_ref[0].astype(F32)
    kext_ref[pad:pad + ts, :] = k_ref[0].astype(F32)

    row = lax.broadcasted_iota(jnp.int32, (L, L), 0)
    col = lax.broadcasted_iota(jnp.int32, (L, L), 1)
    tril = col <= row
    ltri = tril.astype(F32)

    def conv_silu(ext_ref, base, h, off):
        win = ext_ref[pl.ds(base, L + pad), h * dh:(h + 1) * dh]
        w = cw_ref[:, off + h * dh:off + (h + 1) * dh]
        y = cb_ref[:, off + h * dh:off + (h + 1) * dh]
        for j in range(CONV_WIDTH):
            lo = pad - (CONV_WIDTH - 1) + j
            y = y + w[j:j + 1, :] * win[lo:lo + L, :]
        return y * _sigmoid(y)

    def chunk_body(c, carry):
        base = pl.multiple_of(c * L, L)
        li = gi_ref[0, pl.ds(base, L), :] + bi_ref[...]
        logf = _log_sigmoid(gf_ref[0, pl.ds(base, L), :] + bf_ref[...])
        b = jnp.dot(ltri, logf, preferred_element_type=F32, precision=lax.Precision.HIGHEST)
        a = li - b
        g_row = b[L - 1:L, :]
        m_row = m_ref[...]
        m_new_row = g_row + jnp.maximum(m_row, jnp.max(a, axis=0, keepdims=True))
        a_t = a.T

        for h in range(H):
            qc = conv_silu(qext_ref, base, h, 0)
            kc = conv_silu(kext_ref, base, h, H * dh) * (dh ** -0.5)
            qb = qc.astype(BF16)
            kb = kc.astype(BF16)
            vb = v_ref[0, pl.ds(base, L), h * dh:(h + 1) * dh]

            a_row = a_t[h:h + 1, :]
            a_col = a[:, h:h + 1]
            b_col = b[:, h:h + 1]
            m_prev = m_row[:, h:h + 1]
            m_next = m_new_row[:, h:h + 1]
            g_h = g_row[:, h:h + 1]

            amat = jnp.where(tril, a_row, NEG_BIG)
            mcol = jnp.maximum(jnp.max(amat, axis=-1, keepdims=True), m_prev)
            wts = jnp.exp(amat - mcol)
            inter = jnp.exp(m_prev - mcol)

            s = lax.dot_general(qb, kb, (((1,), (1,)), ((), ())), preferred_element_type=F32)
            sqk = s * wts
            ct = ct_ref[h]
            nrow = n_ref[h:h + 1, :]
            num = (jnp.dot(sqk.astype(BF16), vb, preferred_element_type=F32)
                   + inter * jnp.dot(qb, ct.astype(BF16), preferred_element_type=F32))
            den = (jnp.sum(sqk, axis=-1, keepdims=True)
                   + inter * jnp.sum(qb.astype(F32) * nrow, axis=-1, keepdims=True))
            hv = num / jnp.maximum(jnp.abs(den), jnp.exp(-(b_col + mcol)))

            wt = jnp.exp(g_h + a_col - m_next)
            decay = jnp.exp(g_h + m_prev - m_next)
            wv = (wt * vb.astype(F32)).astype(BF16)
            ct_ref[h] = decay * ct + lax.dot_general(kb, wv, (((0,), (0,)), ((), ())),
                                                     preferred_element_type=F32)
            n_ref[h:h + 1, :] = decay * nrow + jnp.sum(wt * kb.astype(F32), axis=0, keepdims=True)

            ms = jnp.mean(hv * hv, axis=-1, keepdims=True)
            y = hv * lax.rsqrt(ms + EPS) * gn_ref[:, h * dh:(h + 1) * dh]
            og = o_ref[0, pl.ds(base, L), h * dh:(h + 1) * dh].astype(F32)
            out_ref[0, pl.ds(base, L), h * dh:(h + 1) * dh] = (y * _sigmoid(og)).astype(out_ref.dtype)

        m_ref[...] = m_new_row
        return carry

    lax.fori_loop(0, ts // L, chunk_body, 0, unroll=4)

    qext_ref[0:pad, :] = qext_ref[ts:ts + pad, :]
    kext_ref[0:pad, :] = kext_ref[ts:ts + pad, :]


def _mlstm(proj3, gates3, conv_w, conv_b, bi_row, bf_row, gnorm):
    B, S, _ = proj3.shape
    W = N_MLSTM_HEADS * MLSTM_HEAD_DIM
    ts = MLSTM_TS
    first = 3
    blk = lambda j: pl.BlockSpec((1, ts, W), lambda b, s: (b, s, j))
    full = lambda shape: pl.BlockSpec(shape, lambda b, s: (0,) * len(shape))
    return pl.pallas_call(
        _mlstm_kernel,
        grid=(B, S // ts),
        in_specs=[
            blk(first), blk(first + 1), blk(first + 2), blk(first + 3),
            pl.BlockSpec((1, ts, LANES), lambda b, s: (b, s, 0)),
            pl.BlockSpec((1, ts, LANES), lambda b, s: (b, s, 1)),
            full((CONV_WIDTH, 2 * W)), full((1, 2 * W)),
            full((1, LANES)), full((1, LANES)), full((1, W)),
        ],
        out_specs=pl.BlockSpec((1, ts, W), lambda b, s: (b, s, 0)),
        out_shape=jax.ShapeDtypeStruct((B, S, W), BF16),
        scratch_shapes=[
            pltpu.VMEM((ts + 8, W), F32),
            pltpu.VMEM((ts + 8, W), F32),
            pltpu.VMEM((N_MLSTM_HEADS, MLSTM_HEAD_DIM, MLSTM_HEAD_DIM), F32),
            pltpu.VMEM((N_MLSTM_HEADS, MLSTM_HEAD_DIM), F32),
            pltpu.VMEM((1, LANES), F32),
        ],
        compiler_params=_cparams(("parallel", "arbitrary")),
        name="mlstm",
    )(proj3, proj3, proj3, proj3, gates3, gates3, conv_w, conv_b, bi_row, bf_row, gnorm)


def _out_kernel(x_ref, a_ref, hm_ref, wa_ref, wm_ref, g_ref, wr_ref, x1_ref, h2_ref, lg_ref):
    y = (jnp.dot(a_ref[...], wa_ref[...].astype(BF16), preferred_element_type=F32)
         + jnp.dot(hm_ref[...], wm_ref[...].astype(BF16), preferred_element_type=F32))
    x1 = x_ref[...] + y
    x1_ref[...] = x1
    ms = jnp.mean(x1 * x1, axis=-1, keepdims=True)
    h2 = x1 * lax.rsqrt(ms + EPS) * g_ref[...]
    h2_ref[...] = _pack_bf16_pairs(h2)
    lg_ref[...] = jnp.dot(h2.astype(BF16), wr_ref[...], preferred_element_type=F32)


def _out_proj(x2, a2, hm2, wa, wm, g, wr):
    T, D = x2.shape
    W = a2.shape[1]
    tm = OUT_TM
    const = lambda shape: pl.BlockSpec(shape, lambda m: (0, 0), pipeline_mode=pl.Buffered(1))
    return pl.pallas_call(
        _out_kernel,
        grid=(T // tm,),
        in_specs=[
            pl.BlockSpec((tm, D), lambda m: (m, 0)),
            pl.BlockSpec((tm, W), lambda m: (m, 0)),
            pl.BlockSpec((tm, W), lambda m: (m, 0)),
            const((W, D)),
            pl.BlockSpec((W, D), lambda m: (1, 0), pipeline_mode=pl.Buffered(1)),
            const((1, D)), const((D, LANES)),
        ],
        out_specs=[
            pl.BlockSpec((tm, D), lambda m: (m, 0)),
            pl.BlockSpec((tm, D // 2), lambda m: (m, 0)),
            pl.BlockSpec((tm, LANES), lambda m: (m, 0)),
        ],
        out_shape=[
            jax.ShapeDtypeStruct((T, D), F32),
            jax.ShapeDtypeStruct((T, D // 2), jnp.uint32),
            jax.ShapeDtypeStruct((T, LANES), F32),
        ],
        compiler_params=_cparams(("parallel",)),
        name="out_proj_router",
    )(x2, a2, hm2, wa, wm, g, wr)


def _moe_kernel(te_ref, nv_ref, first_ref, slot_ref, nxt_ref, rows_ref, xs_ref, wg_hbm, wu_hbm, wd_hbm,
                ys_ref, wg_buf, wu_buf, wd_buf, sem):
    j = pl.program_id(0)
    valid = j < nv_ref[0]

    half_f = wd_buf.shape[1] // 2

    def weight_copies(e, s):
        lo, hi = pl.ds(0, half_f), pl.ds(half_f, half_f)
        return ((pltpu.make_async_copy(wg_hbm.at[e], wg_buf.at[s], sem.at[s, 0]), 0),
                (pltpu.make_async_copy(wu_hbm.at[e], wu_buf.at[s], sem.at[s, 1]), 1),
                (pltpu.make_async_copy(wd_hbm.at[e, lo], wd_buf.at[s, lo], sem.at[s, 2]), 0),
                (pltpu.make_async_copy(wd_hbm.at[e, hi], wd_buf.at[s, hi], sem.at[s, 3]), 1))

    @pl.when(j == 0)
    def _():
        for c, prio in weight_copies(te_ref[0], 0):
            c.start(priority=prio)

    @pl.when(jnp.logical_and(valid, first_ref[j] == 1))
    def _():
        for c, _ in weight_copies(te_ref[j], slot_ref[j]):
            c.wait()

        @pl.when(nxt_ref[j] >= 0)
        def _():
            for c, prio in weight_copies(nxt_ref[j], 1 - slot_ref[j]):
                c.start(priority=prio)

    @pl.when(valid)
    def _():
        s = slot_ref[j]
        row = lax.broadcasted_iota(jnp.int32, xs_ref.shape, 0)
        lo, hi = _unpack_bf16_pairs(jnp.where(row < rows_ref[j], xs_ref[...], jnp.uint32(0)))
        xs = jnp.concatenate([lo.astype(BF16), hi.astype(BF16)], axis=1)
        gt = jnp.dot(xs, wg_buf[s].astype(BF16), preferred_element_type=F32)
        up = jnp.dot(xs, wu_buf[s].astype(BF16), preferred_element_type=F32)
        hid = (gt * _sigmoid(gt) * up).astype(BF16)
        ys_ref[...] = _pack_bf16_pairs(jnp.dot(hid, wd_buf[s].astype(BF16), preferred_element_type=F32))

    @pl.when(jnp.logical_not(valid))
    def _():
        ys_ref[...] = jnp.zeros(ys_ref.shape, ys_ref.dtype)


def _moe(tile_expert, n_valid, tile_first, tile_slot, tile_next, tile_rows, xs, wg, wu, wd):
    R, Dw = xs.shape
    D, F = wg.shape[1], wg.shape[2]
    tm = MOE_TM
    hbm = pl.BlockSpec(memory_space=pl.ANY)
    grid_spec = pltpu.PrefetchScalarGridSpec(
        num_scalar_prefetch=6,
        grid=(R // tm,),
        in_specs=[pl.BlockSpec((tm, Dw), lambda j, *_: (j, 0)), hbm, hbm, hbm],
        out_specs=pl.BlockSpec((tm, Dw), lambda j, *_: (j, 0)),
        scratch_shapes=[
            pltpu.VMEM((2, D, F), wg.dtype),
            pltpu.VMEM((2, D, F), wu.dtype),
            pltpu.VMEM((2, F, D), wd.dtype),
            pltpu.SemaphoreType.DMA((2, 4)),
        ],
    )
    return pl.pallas_call(
        _moe_kernel,
        grid_spec=grid_spec,
        out_shape=jax.ShapeDtypeStruct((R, Dw), jnp.uint32),
        compiler_params=_cparams(("arbitrary",)),
        name="moe_experts",
    )(tile_expert, n_valid, tile_first, tile_slot, tile_next, tile_rows, xs, wg, wu, wd)


def _final_kernel(x1_ref, y0_ref, y1_ref, cw_ref, g_ref, o_ref):
    cw = cw_ref[...]
    lo0, hi0 = _unpack_bf16_pairs(y0_ref[...])
    lo1, hi1 = _unpack_bf16_pairs(y1_ref[...])
    w0, w1 = cw[:, 0:1], cw[:, 1:2]
    y = jnp.concatenate([w0 * lo0 + w1 * lo1, w0 * hi0 + w1 * hi1], axis=1)
    x = x1_ref[...] + y
    ms = jnp.mean(x * x, axis=-1, keepdims=True)
    o_ref[...] = x * lax.rsqrt(ms + EPS) * g_ref[...]


def _final(x1, yw, cw, g):
    T, D = x1.shape
    tm = FIN_TM
    row = lambda w: pl.BlockSpec((tm, w), lambda m: (m, 0))
    slot1 = pl.BlockSpec((tm, D // 2), lambda m: (m + T // tm, 0))
    return pl.pallas_call(
        _final_kernel,
        grid=(T // tm,),
        in_specs=[row(D), row(D // 2), slot1, row(LANES), pl.BlockSpec((1, D), lambda m: (0, 0))],
        out_specs=row(D),
        out_shape=jax.ShapeDtypeStruct((T, D), F32),
        compiler_params=_cparams(("parallel",)),
        name="combine_final_norm",
    )(x1, yw, yw, cw, g)


SC_CORES, SC_SUBCORES = 2, 16
SC_CHUNK = 32


def _sc_gather_rows(table, idx):
    V, Dw = table.shape
    R = idx.shape[0]
    n_workers = SC_CORES * SC_SUBCORES
    ch = SC_CHUNK
    per_w = R // n_workers
    n_chunks = per_w // ch
    assert per_w * n_workers == R and n_chunks * ch == per_w and n_chunks % 2 == 0
    idx3 = idx.reshape(n_workers, n_chunks, ch)
    mesh = plsc.VectorSubcoreMesh(core_axis_name="c", subcore_axis_name="s")

    def body(table_hbm, idx_hbm, out_hbm, idx_v, rows_v, gsem, osem):
        wid = lax.axis_index("s") * SC_CORES + lax.axis_index("c")
        base = wid * per_w
        pltpu.sync_copy(idx_hbm.at[wid], idx_v)

        def gather(c, slot):
            return pltpu.make_async_copy(table_hbm.at[idx_v.at[c]], rows_v.at[slot], gsem.at[slot])

        def put(c, slot):
            return pltpu.make_async_copy(rows_v.at[slot], out_hbm.at[pl.ds(base + c * ch, ch)],
                                         osem.at[slot])

        gather(0, 0).start()

        @pl.loop(0, n_chunks, step=2)
        def _(c):
            @pl.when(c > 0)
            def _():
                put(c - 1, 1).wait()

            gather(c + 1, 1).start()
            gather(c, 0).wait()
            put(c, 0).start()
            put(c, 0).wait()

            @pl.when(c + 2 < n_chunks)
            def _():
                gather(c + 2, 0).start()

            gather(c + 1, 1).wait()
            put(c + 1, 1).start()

        put(n_chunks - 1, 1).wait()

    return pl.kernel(
        body,
        out_type=jax.ShapeDtypeStruct((R, Dw), table.dtype),
        mesh=mesh,
        scratch_types=[
            pltpu.VMEM((n_chunks, ch), jnp.int32),
            pltpu.VMEM((2, ch, Dw), table.dtype),
            pltpu.SemaphoreType.DMA((2,)),
            pltpu.SemaphoreType.DMA((2,)),
        ],
        name="sc_gather_rows",
    )(table, idx3)


def _sc_scatter_rows(table, idx, n_rows_out):
    V, Dw = table.shape
    K = idx.shape[0]
    n_workers = SC_CORES * SC_SUBCORES
    ch = SC_CHUNK
    per_w = V // n_workers
    n_chunks = per_w // ch
    assert K == 2 and per_w * n_workers == V and n_chunks * ch == per_w and n_chunks % 2 == 0
    idx4 = jnp.transpose(idx.reshape(K, n_workers, n_chunks, ch), (1, 0, 2, 3))
    mesh = plsc.VectorSubcoreMesh(core_axis_name="c", subcore_axis_name="s")

    def body(table_hbm, idx_hbm, out_hbm, idx_v, rows_v, lsem, ssem):
        wid = lax.axis_index("s") * SC_CORES + lax.axis_index("c")
        base = wid * per_w
        pltpu.sync_copy(idx_hbm.at[wid], idx_v)

        def load(c, slot):
            return pltpu.make_async_copy(table_hbm.at[pl.ds(base + c * ch, ch)], rows_v.at[slot],
                                         lsem.at[slot])

        def scatter(c, slot, k):
            return pltpu.make_async_copy(rows_v.at[slot], out_hbm.at[idx_v.at[k, c]], ssem.at[slot, k])

        load(0, 0).start()

        @pl.loop(0, n_chunks, step=2)
        def _(c):
            @pl.when(c > 0)
            def _():
                scatter(c - 1, 1, 0).wait()
                scatter(c - 1, 1, 1).wait()

            load(c + 1, 1).start()
            load(c, 0).wait()
            scatter(c, 0, 0).start()
            scatter(c, 0, 1).start()
            scatter(c, 0, 0).wait()
            scatter(c, 0, 1).wait()

            @pl.when(c + 2 < n_chunks)
            def _():
                load(c + 2, 0).start()

            load(c + 1, 1).wait()
            scatter(c + 1, 1, 0).start()
            scatter(c + 1, 1, 1).start()

        scatter(n_chunks - 1, 1, 0).wait()
        scatter(n_chunks - 1, 1, 1).wait()

    return pl.kernel(
        body,
        out_type=jax.ShapeDtypeStruct((n_rows_out, Dw), table.dtype),
        mesh=mesh,
        scratch_types=[
            pltpu.VMEM((K, n_chunks, ch), jnp.int32),
            pltpu.VMEM((2, ch, Dw), table.dtype),
            pltpu.SemaphoreType.DMA((2,)),
            pltpu.SemaphoreType.DMA((2, K)),
        ],
        name="sc_scatter_rows",
    )(table, idx4)


ROUTE_BLK = 256


def _route_kernel(lg_ref, bias_ref, pos_ref, cw_ref, cnt_ref, oh_ref, pre_ref, *, tm):
    T = lg_ref.shape[0]
    G, E, NE = N_GROUPS, EXPERTS_PER_GROUP, N_EXPERTS
    blk = ROUTE_BLK
    lane = lax.broadcasted_iota(jnp.int32, (blk, LANES), 1).astype(F32)
    lane1 = lax.broadcasted_iota(jnp.int32, (1, LANES), 1)
    r_i = lax.broadcasted_iota(jnp.int32, (blk, blk), 0)
    c_i = lax.broadcasted_iota(jnp.int32, (blk, blk), 1)
    ltri = (c_i < r_i).astype(BF16)

    def first_argmax(v):
        mx = jnp.max(v, axis=1, keepdims=True)
        return mx, jnp.min(jnp.where(v == mx, lane, float(LANES)), axis=1, keepdims=True)

    def phase1(b, carry):
        rows = pl.ds(pl.multiple_of(b * blk, blk), blk)
        x = lg_ref[rows, :] + bias_ref[...]
        gl = jnp.where(lane < G, x, NEG_BIG)
        gmax, gsel = first_argmax(gl)
        gw = 1.0 / jnp.sum(jnp.exp(gl - gmax), axis=1, keepdims=True)
        lo = G + E * gsel
        el = jnp.where(jnp.logical_and(lane >= lo, lane < lo + E), x, NEG_BIG)
        v1, i1 = first_argmax(el)
        v2, i2 = first_argmax(jnp.where(lane == i1, NEG_BIG, el))
        e21 = jnp.exp(v2 - v1)
        w1 = 1.0 / (1.0 + e21)
        cw_ref[rows, :] = jnp.where(lane == 0, gw * w1, jnp.where(lane == 1, gw * (e21 * w1), 0.0))
        oh = jnp.where(jnp.logical_or(lane == i1 - G, lane == i2 - G + NE), 1.0, 0.0)
        oh_ref[rows, :] = oh
        pre_ref[rows, :] = jnp.dot(ltri, oh.astype(BF16), preferred_element_type=F32) + carry
        return carry + jnp.sum(oh, axis=0, keepdims=True)

    counts = lax.fori_loop(0, T // blk, phase1, jnp.zeros((1, LANES), F32))

    in_e = lane1 < NE
    c0 = jnp.where(in_e, counts, 0.0)
    ctot = jnp.where(in_e, counts + pltpu.roll(counts, LANES - NE, 1), 0.0)
    tiles = jnp.floor((ctot + (tm - 1)) * (1.0 / tm))
    u_r = lax.broadcasted_iota(jnp.int32, (LANES, LANES), 0)
    u_c = lax.broadcasted_iota(jnp.int32, (LANES, LANES), 1)
    utri = (u_r <= u_c).astype(BF16)
    tile_end = jnp.dot(jnp.broadcast_to(tiles, (8, LANES)).astype(BF16), utri,
                       preferred_element_type=F32)[0:1, :]
    row_start = (tile_end - tiles) * tm
    base = jnp.where(in_e, row_start, pltpu.roll(row_start + c0, NE, 1))
    cnt_ref[...] = jnp.concatenate([ctot, tiles, tile_end, jnp.zeros((5, LANES), F32)], axis=0)

    def phase2(b, carry):
        rows = pl.ds(pl.multiple_of(b * blk, blk), blk)
        t = oh_ref[rows, :] * (pre_ref[rows, :] + base)
        p0 = jnp.sum(jnp.where(lane < NE, t, 0.0), axis=1, keepdims=True)
        p1 = jnp.sum(jnp.where(lane >= NE, t, 0.0), axis=1, keepdims=True)
        pos_ref[rows, :] = jnp.where(lane == 0, p0, jnp.where(lane == 1, p1, 0.0)).astype(jnp.int32)
        return carry

    lax.fori_loop(0, T // blk, phase2, 0)


def _route(logits, bias_row, tm):
    T = logits.shape[0]
    whole = lambda shape: pl.BlockSpec(shape, lambda: (0,) * len(shape))
    return pl.pallas_call(
        functools.partial(_route_kernel, tm=tm),
        in_specs=[whole((T, LANES)), whole((1, LANES))],
        out_specs=[whole((T, LANES)), whole((T, LANES)), whole((8, LANES))],
        out_shape=[
            jax.ShapeDtypeStruct((T, LANES), jnp.int32),
            jax.ShapeDtypeStruct((T, LANES), F32),
            jax.ShapeDtypeStruct((8, LANES), F32),
        ],
        scratch_shapes=[pltpu.VMEM((T, LANES), F32), pltpu.VMEM((T, LANES), F32)],
        compiler_params=pltpu.CompilerParams(vmem_limit_bytes=VMEM_LIMIT),
        name="route_positions",
    )(logits, bias_row)


def _tile_plan(cnt, tm, n_tiles):
    counts = cnt[0, :N_EXPERTS].astype(jnp.int32)
    tiles_per_e = cnt[1, :N_EXPERTS].astype(jnp.int32)
    tile_end = cnt[2, :N_EXPERTS].astype(jnp.int32)
    n_valid = tile_end[-1]
    tile_ids = jnp.arange(n_tiles, dtype=jnp.int32)
    experts = jnp.arange(N_EXPERTS, dtype=jnp.int32)
    valid = tile_ids < n_valid
    jc = jnp.minimum(tile_ids, n_valid - 1)
    tile_expert = jnp.minimum(jnp.sum((tile_end[None, :] <= jc[:, None]).astype(jnp.int32), axis=1),
                              N_EXPERTS - 1)
    of_tile = (tile_expert[:, None] == experts[None, :]).astype(jnp.int32)
    pick = lambda per_expert: jnp.sum(of_tile * per_expert[None, :], axis=1)
    tile_start = tile_end - tiles_per_e
    nonempty = tiles_per_e > 0
    group_idx = jnp.sum(jnp.logical_and(nonempty[None, :], experts[None, :] <= experts[:, None]).astype(jnp.int32),
                        axis=1) - 1
    later = jnp.logical_and(nonempty[None, :], experts[None, :] > experts[:, None])
    next_e = jnp.min(jnp.where(later, experts[None, :], N_EXPERTS), axis=1)
    next_e = jnp.where(next_e == N_EXPERTS, -1, next_e)
    tile_first = jnp.logical_and(valid, tile_ids == pick(tile_start)).astype(jnp.int32)
    tile_slot = (pick(group_idx) % 2).astype(jnp.int32)
    tile_next = pick(next_e).astype(jnp.int32)
    tile_rows = jnp.clip(pick(counts) - (tile_ids - pick(tile_start)) * tm, 0, tm)
    tile_rows = jnp.where(valid, tile_rows, 0).astype(jnp.int32)
    return (tile_expert.astype(jnp.int32), n_valid.reshape(1), tile_first, tile_slot, tile_next, tile_rows)


def kernel(x, rel_bias, ln_mix_g, w_in, conv_w, conv_b, b_i, b_f, lam_q1, lam_k1, lam_q2, lam_k2,
           diff_norm_g, mlstm_norm_g, w_out, ln_ffn_g, w_group, b_group, w_router, b_router,
           w_gate, w_up, w_down, ln_f_g):
    B, S, D = x.shape
    T = B * S
    depth = w_in.shape[0]
    assert depth == 1, "the final rmsnorm is fused into the single layer's combine kernel"
    Hm = N_MLSTM_HEADS
    n_main = w_in.shape[2] - 2 * Hm
    n_diff = N_DIFF_HEADS * 2 * DIFF_HEAD_DIM
    xf = x.reshape(T, D)

    for l in range(depth):
        lambda_init = 0.8 - 0.6 * math.exp(-0.3 * l)
        w_main = w_in[l, :, :n_main].astype(BF16)
        wgt = w_in[l, :, n_main:]
        w_gates = jnp.zeros((D, 2 * LANES), F32).at[:, :Hm].set(wgt[:, :Hm]).at[:, LANES:LANES + Hm].set(
            wgt[:, Hm:]).astype(BF16)
        bi_row = jnp.zeros((1, LANES), F32).at[0, :Hm].set(b_i[l].astype(F32))
        bf_row = jnp.zeros((1, LANES), F32).at[0, :Hm].set(b_f[l].astype(F32))
        lam = (jnp.exp(jnp.sum(lam_q1[l].astype(F32) * lam_k1[l].astype(F32)))
               - jnp.exp(jnp.sum(lam_q2[l].astype(F32) * lam_k2[l].astype(F32))) + lambda_init)
        tq = ATT_TQ
        assert tq >= MAX_DISTANCE and tq % CHUNK == 0
        rb = rel_bias.astype(F32)
        log2e = math.log2(math.e)
        xx = jnp.arange(2 * tq, dtype=jnp.int32)
        rel_vec = jnp.stack([-tq + tq - 1 - xx, tq - 1 - xx], axis=0)
        bias_vecs = jnp.take(rb, _t5_bucket(rel_vec), axis=0) * log2e
        bias_vecs = jnp.transpose(bias_vecs, (2, 0, 1))[:, :, None, :]
        cfar = rb[N_BUCKETS // 2 - 1] * log2e
        scal = jnp.concatenate([lam.reshape(1), cfar]).astype(F32)
        col_scale = jnp.ones((1, n_main), F32).at[:, :n_diff].set(DIFF_HEAD_DIM ** -0.5 * log2e)

        proj, gates = _proj(xf, ln_mix_g[l].reshape(1, D).astype(F32), w_main, col_scale, w_gates)
        proj3 = proj.reshape(B, S, n_main)
        a = _diff_attention(proj3, scal, bias_vecs, diff_norm_g[l].reshape(1, n_diff).astype(F32),
                            lambda_init)
        hm = _mlstm(proj3, gates.reshape(B, S, 2 * LANES), conv_w[l].astype(F32),
                    conv_b[l].reshape(1, -1).astype(F32), bi_row, bf_row,
                    mlstm_norm_g[l].reshape(1, -1).astype(F32))

        wo = w_out[l]
        G, E = N_GROUPS, EXPERTS_PER_GROUP
        wr = jnp.zeros((D, LANES), F32).at[:, :G].set(w_group[l].astype(F32)).at[:, G:G + G * E].set(
            jnp.transpose(w_router[l].astype(F32), (1, 0, 2)).reshape(D, G * E)).astype(BF16)
        x1, h2, logits = _out_proj(xf, a.reshape(T, n_diff), hm.reshape(T, -1), wo, wo,
                                   ln_ffn_g[l].reshape(1, D).astype(F32), wr)

        route_bias = jnp.concatenate([b_group[l].astype(F32).reshape(-1), b_router[l].astype(F32).reshape(-1),
                                      jnp.zeros((LANES - G - G * E,), F32)]).reshape(1, LANES)
        n_tiles = (T * TOP_K_INNER) // MOE_TM + N_EXPERTS
        pos128, cwp, cnt = _route(logits, route_bias, MOE_TM)
        pos_t = pos128[:, :TOP_K_INNER].T
        tiles = _tile_plan(cnt, MOE_TM, n_tiles)
        xs = _sc_scatter_rows(h2, pos_t, n_tiles * MOE_TM)
        Fe = w_gate.shape[-1]
        ys = _moe(*tiles, xs, w_gate[l].reshape(N_EXPERTS, D, Fe),
                  w_up[l].reshape(N_EXPERTS, D, Fe), w_down[l].reshape(N_EXPERTS, Fe, D))
        yw = _sc_gather_rows(ys, pos_t.reshape(-1))
        xf = _final(x1, yw, cwp, ln_f_g.reshape(1, D).astype(F32))
    return xf.reshape(B, S, D)
```

```python
import functools
import math

import jax
import jax.numpy as jnp
from jax import lax
from jax.experimental import pallas as pl
from jax.experimental.pallas import tpu as pltpu
from jax.experimental.pallas import tpu_sc as plsc

F32 = jnp.float32
BF16 = jnp.bfloat16

EPS = 1e-6
CHUNK = 64
DIFF_HEAD_DIM = 64
N_DIFF_HEADS = 8
MLSTM_HEAD_DIM = 128
N_MLSTM_HEADS = 8
CONV_WIDTH = 4
N_BUCKETS = 32
MAX_DISTANCE = 128
N_GROUPS = 4
EXPERTS_PER_GROUP = 8
N_EXPERTS = N_GROUPS * EXPERTS_PER_GROUP
TOP_K_INNER = 2
LANES = 128
NEG_BIG = -1e30

VMEM_LIMIT = 56 * 1024 * 1024

PROJ_TM, PROJ_TN = 1024, 1024
ATT_TQ = 512
MLSTM_TS = 1024
OUT_TM = 512
MOE_TM = 256
FIN_TM = 512


def _cparams(sem):
    return pltpu.CompilerParams(dimension_semantics=sem, vmem_limit_bytes=VMEM_LIMIT)


_HI_MASK = 0xFFFF0000


def _pack_bf16_pairs(x):
    half = x.shape[-1] // 2
    xb = x.astype(BF16).astype(F32)
    lo = pltpu.bitcast(xb[:, :half], jnp.uint32)
    hi = pltpu.bitcast(xb[:, half:], jnp.uint32)
    return (hi & jnp.uint32(_HI_MASK)) | (lo >> 16)


def _unpack_bf16_pairs(w):
    lo = pltpu.bitcast(w << 16, F32)
    hi = pltpu.bitcast(w & jnp.uint32(_HI_MASK), F32)
    return lo, hi


def _proj_kernel(x_ref, g_ref, w_ref, cs_ref, wg_ref, o_ref, og_ref, h_ref):
    @pl.when(pl.program_id(1) == 0)
    def _():
        x = x_ref[...]
        ms = jnp.mean(x * x, axis=-1, keepdims=True)
        h = (x * lax.rsqrt(ms + EPS) * g_ref[...]).astype(BF16)
        h_ref[...] = h
        og_ref[...] = jnp.dot(h, wg_ref[...], preferred_element_type=F32)

    o_ref[...] = (jnp.dot(h_ref[...], w_ref[...], preferred_element_type=F32) * cs_ref[...]).astype(o_ref.dtype)


def _proj(x2, g, w_main, col_scale, w_gates):
    T, D = x2.shape
    N = w_main.shape[1]
    NG = w_gates.shape[1]
    return pl.pallas_call(
        _proj_kernel,
        grid=(T // PROJ_TM, N // PROJ_TN),
        in_specs=[
            pl.BlockSpec((PROJ_TM, D), lambda m, n: (m, 0)),
            pl.BlockSpec((1, D), lambda m, n: (0, 0)),
            pl.BlockSpec((D, PROJ_TN), lambda m, n: (0, n)),
            pl.BlockSpec((1, PROJ_TN), lambda m, n: (0, n)),
            pl.BlockSpec((D, NG), lambda m, n: (0, 0)),
        ],
        out_specs=[
            pl.BlockSpec((PROJ_TM, PROJ_TN), lambda m, n: (m, n)),
            pl.BlockSpec((PROJ_TM, NG), lambda m, n: (m, 0)),
        ],
        out_shape=[
            jax.ShapeDtypeStruct((T, N), BF16),
            jax.ShapeDtypeStruct((T, NG), F32),
        ],
        scratch_shapes=[pltpu.VMEM((PROJ_TM, D), BF16)],
        compiler_params=_cparams(("parallel", "arbitrary")),
        name="rms_in_proj",
    )(x2, g, w_main, col_scale, w_gates)


def _t5_bucket(rel):
    half = N_BUCKETS // 2
    max_exact = half // 2
    ret = jnp.where(rel > 0, half, 0)
    n = jnp.abs(rel)
    nf = jnp.maximum(n, 1).astype(F32)
    large = max_exact + (jnp.log(nf / max_exact) / math.log(MAX_DISTANCE / max_exact)
                         * (half - max_exact)).astype(jnp.int32)
    large = jnp.minimum(large, half - 1)
    return ret + jnp.where(n < max_exact, n, large)


def _attn_kernel(scal_ref, q_ref, k_ref, v_ref, bias_ref, g_ref, o_ref, m_ref, l_ref, acc_ref,
                 s0_ref, s1_ref, s2_ref, ml0_ref, ml1_ref, ml2_ref, bt_ref, *, lambda_init):
    h = pl.program_id(1)
    qi = pl.program_id(2)
    nq = pl.num_programs(2)
    tq = ATT_TQ
    lam = scal_ref[0]
    cfar = scal_ref[1 + h]

    def stacked_queries(tile):
        q = q_ref[0, pl.ds(pl.multiple_of(tile * tq, tq), tq), :]
        lane = lax.broadcasted_iota(jnp.int32, q.shape, 1)
        zero = jnp.zeros_like(q)
        return jnp.concatenate([jnp.where(lane < DIFF_HEAD_DIM, q, zero),
                                jnp.where(lane >= DIFF_HEAD_DIM, q, zero)], axis=0)

    qs = stacked_queries(qi)
    nxt = jnp.minimum(qi + 1, nq - 1)
    qs_next = stacked_queries(nxt)

    m_ref[...] = jnp.full(m_ref.shape, NEG_BIG, F32)
    l_ref[...] = jnp.zeros(l_ref.shape, F32)
    acc_ref[...] = jnp.zeros(acc_ref.shape, F32)

    @pl.when(qi == 0)
    def _():
        kj = lax.broadcasted_iota(jnp.int32, (tq, tq), 0)
        qq = lax.broadcasted_iota(jnp.int32, (tq, tq), 1)
        allowed = (kj // CHUNK) <= (qq // CHUNK)
        for d in range(2):
            rows = jnp.broadcast_to(bias_ref[0, d], (tq, 2 * tq))
            tile = pltpu.roll(rows, tq + 1, 1, stride=1, stride_axis=0)[:, :tq]
            if d == 1:
                tile = jnp.where(allowed, tile, NEG_BIG)
            bt_ref[d] = tile

    bufs = ((s0_ref, ml0_ref), (s1_ref, ml1_ref), (s2_ref, ml2_ref))

    def score(ki, bias, slot, queries=None):
        s_ref, ml_ref = bufs[slot]
        start = pl.multiple_of(ki * tq, tq)
        kt = k_ref[0, pl.ds(start, tq), :]
        s = lax.dot_general(kt, qs if queries is None else queries, (((1,), (1,)), ((), ())),
                            preferred_element_type=F32)
        if bias is not None:
            s = s + jnp.concatenate([bias, bias], axis=1)
        s_ref[...] = s
        ml_ref[...] = jnp.max(s, axis=0, keepdims=True)

    def accumulate(ki, shift, slot):
        s_ref, ml_ref = bufs[slot]
        start = pl.multiple_of(ki * tq, tq)
        vt = v_ref[0, pl.ds(start, tq), :]
        m_old = m_ref[...]
        m_new = jnp.maximum(m_old, ml_ref[...] + shift)
        alpha = jnp.exp2(m_old - m_new)
        p = jnp.exp2(s_ref[...] - (m_new - shift))
        l_ref[...] = alpha * l_ref[...] + jnp.sum(p, axis=0, keepdims=True)
        pv = lax.dot_general(vt, p.astype(BF16), (((0,), (0,)), ((), ())), preferred_element_type=F32)
        acc_ref[...] = alpha * acc_ref[...] + pv
        m_ref[...] = m_new

    n_far = qi - 1

    def score_next_diagonal():
        score(nxt, bt_ref[1], 2, qs_next)

    @pl.when(qi == 0)
    def _():
        score(qi, bt_ref[1], 2)
        accumulate(qi, 0.0, 2)
        score_next_diagonal()

    @pl.when(qi >= 1)
    def _():
        accumulate(qi, 0.0, 2)
        score(qi - 1, bt_ref[0], 1)

    @pl.when(qi == 1)
    def _():
        accumulate(qi - 1, 0.0, 1)
        score_next_diagonal()

    @pl.when(qi >= 2)
    def _():
        accumulate(qi - 1, 0.0, 1)
        score(0, None, 0)
        trips = (n_far - 1) // 2

        def pair(j, c):
            accumulate(2 * j, cfar, 0)
            score(2 * j + 1, None, 1)
            accumulate(2 * j + 1, cfar, 1)
            score(2 * j + 2, None, 0)
            return c

        lax.fori_loop(0, trips, pair, 0)
        last = 2 * trips

        @pl.when(n_far - last == 2)
        def _():
            accumulate(last, cfar, 0)
            score(last + 1, None, 1)
            accumulate(last + 1, cfar, 1)
            score_next_diagonal()

        @pl.when(n_far - last == 1)
        def _():
            accumulate(last, cfar, 0)
            score_next_diagonal()

    acc = acc_ref[...] * (1.0 / l_ref[...])
    o_t = acc[:, 0:tq] - lam * acc[:, tq:2 * tq]
    ms = jnp.mean(o_t * o_t, axis=0, keepdims=True)
    y = (o_t * lax.rsqrt(ms + EPS)).T * (g_ref[...] * (1.0 - lambda_init))
    o_ref[0] = y.astype(o_ref.dtype)


def _diff_attention(proj3, scal, bias_vecs, gnorm, lambda_init):
    B, S, _ = proj3.shape
    H = N_DIFF_HEADS
    tq = ATT_TQ
    kern = functools.partial(_attn_kernel, lambda_init=lambda_init)
    return pl.pallas_call(
        kern,
        grid=(B, H, S // tq),
        in_specs=[
            pl.BlockSpec(memory_space=pltpu.SMEM),
            pl.BlockSpec((1, S, LANES), lambda b, h, i: (b, 0, h)),
            pl.BlockSpec((1, S, LANES), lambda b, h, i: (b, 0, H + h)),
            pl.BlockSpec((1, S, LANES), lambda b, h, i: (b, 0, 2 * H + h)),
            pl.BlockSpec((1, 2, 1, 2 * tq), lambda b, h, i: (h, 0, 0, 0)),
            pl.BlockSpec((1, LANES), lambda b, h, i: (0, h)),
        ],
        out_specs=pl.BlockSpec((1, tq, LANES), lambda b, h, i: (b, i, h)),
        out_shape=jax.ShapeDtypeStruct((B, S, H * LANES), BF16),
        scratch_shapes=[
            pltpu.VMEM((1, 2 * tq), F32),
            pltpu.VMEM((1, 2 * tq), F32),
            pltpu.VMEM((LANES, 2 * tq), F32),
            pltpu.VMEM((tq, 2 * tq), F32),
            pltpu.VMEM((tq, 2 * tq), F32),
            pltpu.VMEM((tq, 2 * tq), F32),
            pltpu.VMEM((1, 2 * tq), F32),
            pltpu.VMEM((1, 2 * tq), F32),
            pltpu.VMEM((1, 2 * tq), F32),
            pltpu.VMEM((2, tq, tq), F32),
        ],
        compiler_params=_cparams(("parallel", "parallel", "arbitrary")),
        name="diff_attention",
    )(scal, proj3, proj3, proj3, bias_vecs, gnorm)


def _log_sigmoid(x):
    return jnp.minimum(x, 0.0) - jnp.log(1.0 + jnp.exp(-jnp.abs(x)))


def _sigmoid(x):
    return 1.0 / (1.0 + jnp.exp(-x))


def _mlstm_kernel(q_ref, k_ref, v_ref, o_ref, gi_ref, gf_ref, cw_ref, cb_ref, bi_ref, bf_ref, gn_ref,
                  out_ref, qext_ref, kext_ref, ct_ref, n_ref, m_ref):
    sb = pl.program_id(1)
    L = CHUNK
    dh = MLSTM_HEAD_DIM
    H = N_MLSTM_HEADS
    ts = MLSTM_TS
    pad = 8

    @pl.when(sb == 0)
    def _():
        qext_ref[0:pad, :] = jnp.zeros((pad, H * dh), F32)
        kext_ref[0:pad, :] = jnp.zeros((pad, H * dh), F32)
        ct_ref[...] = jnp.zeros(ct_ref.shape, F32)
        n_ref[...] = jnp.zeros(n_ref.shape, F32)
        m_ref[...] = jnp.zeros(m_ref.shape, F32)

    qext_ref[pad:pad + ts, :] = q_ref[0].astype(F32)
    kext_ref[pad:pad + ts, :] = k_ref[0].astype(F32)

    row = lax.broadcasted_iota(jnp.int32, (L, L), 0)
    col = lax.broadcasted_iota(jnp.int32, (L, L), 1)
    tril = col <= row
    ltri = tril.astype(F32)

    def conv_silu(ext_ref, base, h, off):
        win = ext_ref[pl.ds(base, L + pad), h * dh:(h + 1) * dh]
        w = cw_ref[:, off + h * dh:off + (h + 1) * dh]
        y = cb_ref[:, off + h * dh:off + (h + 1) * dh]
        for j in range(CONV_WIDTH):
            lo = pad - (CONV_WIDTH - 1) + j
            y = y + w[j:j + 1, :] * win[lo:lo + L, :]
        return y * _sigmoid(y)

    def chunk_body(c, carry):
        base = pl.multiple_of(c * L, L)
        li = gi_ref[0, pl.ds(base, L), :] + bi_ref[...]
        logf = _log_sigmoid(gf_ref[0, pl.ds(base, L), :] + bf_ref[...])
        b = jnp.dot(ltri, logf, preferred_element_type=F32, precision=lax.Precision.HIGHEST)
        a = li - b
        g_row = b[L - 1:L, :]
        m_row = m_ref[...]
        m_new_row = g_row + jnp.maximum(m_row, jnp.max(a, axis=0, keepdims=True))
        a_t = a.T

        for h in range(H):
            qc = conv_silu(qext_ref, base, h, 0)
            kc = conv_silu(kext_ref, base, h, H * dh) * (dh ** -0.5)
            qb = qc.astype(BF16)
            kb = kc.astype(BF16)
            vb = v_ref[0, pl.ds(base, L), h * dh:(h + 1) * dh]

            a_row = a_t[h:h + 1, :]
            a_col = a[:, h:h + 1]
            b_col = b[:, h:h + 1]
            m_prev = m_row[:, h:h + 1]
            m_next = m_new_row[:, h:h + 1]
            g_h = g_row[:, h:h + 1]

            amat = jnp.where(tril, a_row, NEG_BIG)
            mcol = jnp.maximum(jnp.max(amat, axis=-1, keepdims=True), m_prev)
            wts = jnp.exp(amat - mcol)
            inter = jnp.exp(m_prev - mcol)

            s = lax.dot_general(qb, kb, (((1,), (1,)), ((), ())), preferred_element_type=F32)
            sqk = s * wts
            ct = ct_ref[h]
            nrow = n_ref[h:h + 1, :]
            num = (jnp.dot(sqk.astype(BF16), vb, preferred_element_type=F32)
                   + inter * jnp.dot(qb, ct.astype(BF16), preferred_element_type=F32))
            den = (jnp.sum(sqk, axis=-1, keepdims=True)
                   + inter * jnp.sum(qb.astype(F32) * nrow, axis=-1, keepdims=True))
            hv = num / jnp.maximum(jnp.abs(den), jnp.exp(-(b_col + mcol)))

            wt = jnp.exp(g_h + a_col - m_next)
            decay = jnp.exp(g_h + m_prev - m_next)
            wv = (wt * vb.astype(F32)).astype(BF16)
            ct_ref[h] = decay * ct + lax.dot_general(kb, wv, (((0,), (0,)), ((), ())),
                                                     preferred_element_type=F32)
            n_ref[h:h + 1, :] = decay * nrow + jnp.sum(wt * kb.astype(F32), axis=0, keepdims=True)

            ms = jnp.mean(hv * hv, axis=-1, keepdims=True)
            y = hv * lax.rsqrt(ms + EPS) * gn_ref[:, h * dh:(h + 1) * dh]
            og = o_ref[0, pl.ds(base, L), h * dh:(h + 1) * dh].astype(F32)
            out_ref[0, pl.ds(base, L), h * dh:(h + 1) * dh] = (y * _sigmoid(og)).astype(out_ref.dtype)

        m_ref[...] = m_new_row
        return carry

    lax.fori_loop(0, ts // L, chunk_body, 0, unroll=4)

    qext_ref[0:pad, :] = qext_ref[ts:ts + pad, :]
    kext_ref[0:pad, :] = kext_ref[ts:ts + pad, :]


def _mlstm(proj3, gates3, conv_w, conv_b, bi_row, bf_row, gnorm):
    B, S, _ = proj3.shape
    W = N_MLSTM_HEADS * MLSTM_HEAD_DIM
    ts = MLSTM_TS
    first = 3
    blk = lambda j: pl.BlockSpec((1, ts, W), lambda b, s: (b, s, j))
    full = lambda shape: pl.BlockSpec(shape, lambda b, s: (0,) * len(shape))
    return pl.pallas_call(
        _mlstm_kernel,
        grid=(B, S // ts),
        in_specs=[
            blk(first), blk(first + 1), blk(first + 2), blk(first + 3),
            pl.BlockSpec((1, ts, LANES), lambda b, s: (b, s, 0)),
            pl.BlockSpec((1, ts, LANES), lambda b, s: (b, s, 1)),
            full((CONV_WIDTH, 2 * W)), full((1, 2 * W)),
            full((1, LANES)), full((1, LANES)), full((1, W)),
        ],
        out_specs=pl.BlockSpec((1, ts, W), lambda b, s: (b, s, 0)),
        out_shape=jax.ShapeDtypeStruct((B, S, W), BF16),
        scratch_shapes=[
            pltpu.VMEM((ts + 8, W), F32),
            pltpu.VMEM((ts + 8, W), F32),
            pltpu.VMEM((N_MLSTM_HEADS, MLSTM_HEAD_DIM, MLSTM_HEAD_DIM), F32),
            pltpu.VMEM((N_MLSTM_HEADS, MLSTM_HEAD_DIM), F32),
            pltpu.VMEM((1, LANES), F32),
        ],
        compiler_params=_cparams(("parallel", "arbitrary")),
        name="mlstm",
    )(proj3, proj3, proj3, proj3, gates3, gates3, conv_w, conv_b, bi_row, bf_row, gnorm)


def _out_kernel(x_ref, a_ref, hm_ref, wa_ref, wm_ref, g_ref, wr_ref, x1_ref, h2_ref, lg_ref):
    y = (jnp.dot(a_ref[...], wa_ref[...].astype(BF16), preferred_element_type=F32)
         + jnp.dot(hm_ref[...], wm_ref[...].astype(BF16), preferred_element_type=F32))
    x1 = x_ref[...] + y
    x1_ref[...] = x1
    ms = jnp.mean(x1 * x1, axis=-1, keepdims=True)
    h2 = x1 * lax.rsqrt(ms + EPS) * g_ref[...]
    h2_ref[...] = _pack_bf16_pairs(h2)
    lg_ref[...] = jnp.dot(h2.astype(BF16), wr_ref[...], preferred_element_type=F32)


def _out_proj(x2, a2, hm2, wa, wm, g, wr):
    T, D = x2.shape
    W = a2.shape[1]
    tm = OUT_TM
    const = lambda shape: pl.BlockSpec(shape, lambda m: (0, 0), pipeline_mode=pl.Buffered(1))
    return pl.pallas_call(
        _out_kernel,
        grid=(T // tm,),
        in_specs=[
            pl.BlockSpec((tm, D), lambda m: (m, 0)),
            pl.BlockSpec((tm, W), lambda m: (m, 0)),
            pl.BlockSpec((tm, W), lambda m: (m, 0)),
            const((W, D)),
            pl.BlockSpec((W, D), lambda m: (1, 0), pipeline_mode=pl.Buffered(1)),
            const((1, D)), const((D, LANES)),
        ],
        out_specs=[
            pl.BlockSpec((tm, D), lambda m: (m, 0)),
            pl.BlockSpec((tm, D // 2), lambda m: (m, 0)),
            pl.BlockSpec((tm, LANES), lambda m: (m, 0)),
        ],
        out_shape=[
            jax.ShapeDtypeStruct((T, D), F32),
            jax.ShapeDtypeStruct((T, D // 2), jnp.uint32),
            jax.ShapeDtypeStruct((T, LANES), F32),
        ],
        compiler_params=_cparams(("parallel",)),
        name="out_proj_router",
    )(x2, a2, hm2, wa, wm, g, wr)


def _moe_kernel(te_ref, nv_ref, first_ref, slot_ref, nxt_ref, rows_ref, xs_ref, wg_hbm, wu_hbm, wd_hbm,
                ys_ref, wg_buf, wu_buf, wd_buf, sem):
    j = pl.program_id(0)
    valid = j < nv_ref[0]

    half_f = wd_buf.shape[1] // 2

    def weight_copies(e, s):
        lo, hi = pl.ds(0, half_f), pl.ds(half_f, half_f)
        return ((pltpu.make_async_copy(wg_hbm.at[e], wg_buf.at[s], sem.at[s, 0]), 0),
                (pltpu.make_async_copy(wu_hbm.at[e], wu_buf.at[s], sem.at[s, 1]), 1),
                (pltpu.make_async_copy(wd_hbm.at[e, lo], wd_buf.at[s, lo], sem.at[s, 2]), 0),
                (pltpu.make_async_copy(wd_hbm.at[e, hi], wd_buf.at[s, hi], sem.at[s, 3]), 1))

    @pl.when(j == 0)
    def _():
        for c, prio in weight_copies(te_ref[0], 0):
            c.start(priority=prio)

    @pl.when(jnp.logical_and(valid, first_ref[j] == 1))
    def _():
        for c, _ in weight_copies(te_ref[j], slot_ref[j]):
            c.wait()

        @pl.when(nxt_ref[j] >= 0)
        def _():
            for c, prio in weight_copies(nxt_ref[j], 1 - slot_ref[j]):
                c.start(priority=prio)

    @pl.when(valid)
    def _():
        s = slot_ref[j]
        row = lax.broadcasted_iota(jnp.int32, xs_ref.shape, 0)
        lo, hi = _unpack_bf16_pairs(jnp.where(row < rows_ref[j], xs_ref[...], jnp.uint32(0)))
        xs = jnp.concatenate([lo.astype(BF16), hi.astype(BF16)], axis=1)
        gt = jnp.dot(xs, wg_buf[s].astype(BF16), preferred_element_type=F32)
        up = jnp.dot(xs, wu_buf[s].astype(BF16), preferred_element_type=F32)
        hid = (gt * _sigmoid(gt) * up).astype(BF16)
        ys_ref[...] = _pack_bf16_pairs(jnp.dot(hid, wd_buf[s].astype(BF16), preferred_element_type=F32))

    @pl.when(jnp.logical_not(valid))
    def _():
        ys_ref[...] = jnp.zeros(ys_ref.shape, ys_ref.dtype)


def _moe(tile_expert, n_valid, tile_first, tile_slot, tile_next, tile_rows, xs, wg, wu, wd):
    R, Dw = xs.shape
    D, F = wg.shape[1], wg.shape[2]
    tm = MOE_TM
    hbm = pl.BlockSpec(memory_space=pl.ANY)
    grid_spec = pltpu.PrefetchScalarGridSpec(
        num_scalar_prefetch=6,
        grid=(R // tm,),
        in_specs=[pl.BlockSpec((tm, Dw), lambda j, *_: (j, 0)), hbm, hbm, hbm],
        out_specs=pl.BlockSpec((tm, Dw), lambda j, *_: (j, 0)),
        scratch_shapes=[
            pltpu.VMEM((2, D, F), wg.dtype),
            pltpu.VMEM((2, D, F), wu.dtype),
            pltpu.VMEM((2, F, D), wd.dtype),
            pltpu.SemaphoreType.DMA((2, 4)),
        ],
    )
    return pl.pallas_call(
        _moe_kernel,
        grid_spec=grid_spec,
        out_shape=jax.ShapeDtypeStruct((R, Dw), jnp.uint32),
        compiler_params=_cparams(("arbitrary",)),
        name="moe_experts",
    )(tile_expert, n_valid, tile_first, tile_slot, tile_next, tile_rows, xs, wg, wu, wd)


def _final_kernel(x1_ref, y0_ref, y1_ref, cw_ref, g_ref, o_ref):
    cw = cw_ref[...]
    lo0, hi0 = _unpack_bf16_pairs(y0_ref[...])
    lo1, hi1 = _unpack_bf16_pairs(y1_ref[...])
    w0, w1 = cw[:, 0:1], cw[:, 1:2]
    y = jnp.concatenate([w0 * lo0 + w1 * lo1, w0 * hi0 + w1 * hi1], axis=1)
    x = x1_ref[...] + y
    ms = jnp.mean(x * x, axis=-1, keepdims=True)
    o_ref[...] = x * lax.rsqrt(ms + EPS) * g_ref[...]


def _final(x1, yw, cw, g):
    T, D = x1.shape
    tm = FIN_TM
    row = lambda w: pl.BlockSpec((tm, w), lambda m: (m, 0))
    slot1 = pl.BlockSpec((tm, D // 2), lambda m: (m + T // tm, 0))
    return pl.pallas_call(
        _final_kernel,
        grid=(T // tm,),
        in_specs=[row(D), row(D // 2), slot1, row(LANES), pl.BlockSpec((1, D), lambda m: (0, 0))],
        out_specs=row(D),
        out_shape=jax.ShapeDtypeStruct((T, D), F32),
        compiler_params=_cparams(("parallel",)),
        name="combine_final_norm",
    )(x1, yw, yw, cw, g)


SC_CORES, SC_SUBCORES = 2, 16
SC_CHUNK = 32


def _sc_gather_rows(table, idx):
    V, Dw = table.shape
    R = idx.shape[0]
    n_workers = SC_CORES * SC_SUBCORES
    ch = SC_CHUNK
    per_w = R // n_workers
    n_chunks = per_w // ch
    assert per_w * n_workers == R and n_chunks * ch == per_w and n_chunks % 2 == 0
    idx3 = idx.reshape(n_workers, n_chunks, ch)
    mesh = plsc.VectorSubcoreMesh(core_axis_name="c", subcore_axis_name="s")

    def body(table_hbm, idx_hbm, out_hbm, idx_v, rows_v, gsem, osem):
        wid = lax.axis_index("s") * SC_CORES + lax.axis_index("c")
        base = wid * per_w
        pltpu.sync_copy(idx_hbm.at[wid], idx_v)

        def gather(c, slot):
            return pltpu.make_async_copy(table_hbm.at[idx_v.at[c]], rows_v.at[slot], gsem.at[slot])

        def put(c, slot):
            return pltpu.make_async_copy(rows_v.at[slot], out_hbm.at[pl.ds(base + c * ch, ch)],
                                         osem.at[slot])

        gather(0, 0).start()

        @pl.loop(0, n_chunks, step=2)
        def _(c):
            @pl.when(c > 0)
            def _():
                put(c - 1, 1).wait()

            gather(c + 1, 1).start()
            gather(c, 0).wait()
            put(c, 0).start()
            put(c, 0).wait()

            @pl.when(c + 2 < n_chunks)
            def _():
                gather(c + 2, 0).start()

            gather(c + 1, 1).wait()
            put(c + 1, 1).start()

        put(n_chunks - 1, 1).wait()

    return pl.kernel(
        body,
        out_type=jax.ShapeDtypeStruct((R, Dw), table.dtype),
        mesh=mesh,
        scratch_types=[
            pltpu.VMEM((n_chunks, ch), jnp.int32),
            pltpu.VMEM((2, ch, Dw), table.dtype),
            pltpu.SemaphoreType.DMA((2,)),
            pltpu.SemaphoreType.DMA((2,)),
        ],
        name="sc_gather_rows",
    )(table, idx3)


def _sc_scatter_rows(table, idx, n_rows_out):
    V, Dw = table.shape
    K = idx.shape[0]
    n_workers = SC_CORES * SC_SUBCORES
    ch = SC_CHUNK
    per_w = V // n_workers
    n_chunks = per_w // ch
    assert K == 2 and per_w * n_workers == V and n_chunks * ch == per_w and n_chunks % 2 == 0
    idx4 = jnp.transpose(idx.reshape(K, n_workers, n_chunks, ch), (1, 0, 2, 3))
    mesh = plsc.VectorSubcoreMesh(core_axis_name="c", subcore_axis_name="s")

    def body(table_hbm, idx_hbm, out_hbm, idx_v, rows_v, lsem, ssem):
        wid = lax.axis_index("s") * SC_CORES + lax.axis_index("c")
        base = wid * per_w
        pltpu.sync_copy(idx_hbm.at[wid], idx_v)

        def load(c, slot):
            return pltpu.make_async_copy(table_hbm.at[pl.ds(base + c * ch, ch)], rows_v.at[slot],
                                         lsem.at[slot])

        def scatter(c, slot, k):
            return pltpu.make_async_copy(rows_v.at[slot], out_hbm.at[idx_v.at[k, c]], ssem.at[slot, k])

        load(0, 0).start()

        @pl.loop(0, n_chunks, step=2)
        def _(c):
            @pl.when(c > 0)
            def _():
                scatter(c - 1, 1, 0).wait()
                scatter(c - 1, 1, 1).wait()

            load(c + 1, 1).start()
            load(c, 0).wait()
            scatter(c, 0, 0).start()
            scatter(c, 0, 1).start()
            scatter(c, 0, 0).wait()
            scatter(c, 0, 1).wait()

            @pl.when(c + 2 < n_chunks)
            def _():
                load(c + 2, 0).start()

            load(c + 1, 1).wait()
            scatter(c + 1, 1, 0).start()
            scatter(c + 1, 1, 1).start()

        scatter(n_chunks - 1, 1, 0).wait()
        scatter(n_chunks - 1, 1, 1).wait()

    return pl.kernel(
        body,
        out_type=jax.ShapeDtypeStruct((n_rows_out, Dw), table.dtype),
        mesh=mesh,
        scratch_types=[
            pltpu.VMEM((K, n_chunks, ch), jnp.int32),
            pltpu.VMEM((2, ch, Dw), table.dtype),
            pltpu.SemaphoreType.DMA((2,)),
            pltpu.SemaphoreType.DMA((2, K)),
        ],
        name="sc_scatter_rows",
    )(table, idx4)


ROUTE_BLK = 256


def _route_kernel(lg_ref, bias_ref, pos_ref, cw_ref, cnt_ref, oh_ref, pre_ref, *, tm):
    T = lg_ref.shape[0]
    G, E, NE = N_GROUPS, EXPERTS_PER_GROUP, N_EXPERTS
    blk = ROUTE_BLK
    lane = lax.broadcasted_iota(jnp.int32, (blk, LANES), 1).astype(F32)
    lane1 = lax.broadcasted_iota(jnp.int32, (1, LANES), 1)
    r_i = lax.broadcasted_iota(jnp.int32, (blk, blk), 0)
    c_i = lax.broadcasted_iota(jnp.int32, (blk, blk), 1)
    ltri = (c_i < r_i).astype(BF16)

    def first_argmax(v):
        mx = jnp.max(v, axis=1, keepdims=True)
        return mx, jnp.min(jnp.where(v == mx, lane, float(LANES)), axis=1, keepdims=True)

    def phase1(b, carry):
        rows = pl.ds(pl.multiple_of(b * blk, blk), blk)
        x = lg_ref[rows, :] + bias_ref[...]
        gl = jnp.where(lane < G, x, NEG_BIG)
        gmax, gsel = first_argmax(gl)
        gw = 1.0 / jnp.sum(jnp.exp(gl - gmax), axis=1, keepdims=True)
        lo = G + E * gsel
        el = jnp.where(jnp.logical_and(lane >= lo, lane < lo + E), x, NEG_BIG)
        v1, i1 = first_argmax(el)
        v2, i2 = first_argmax(jnp.where(lane == i1, NEG_BIG, el))
        e21 = jnp.exp(v2 - v1)
        w1 = 1.0 / (1.0 + e21)
        cw_ref[rows, :] = jnp.where(lane == 0, gw * w1, jnp.where(lane == 1, gw * (e21 * w1), 0.0))
        oh = jnp.where(jnp.logical_or(lane == i1 - G, lane == i2 - G + NE), 1.0, 0.0)
        oh_ref[rows, :] = oh
        pre_ref[rows, :] = jnp.dot(ltri, oh.astype(BF16), preferred_element_type=F32) + carry
        return carry + jnp.sum(oh, axis=0, keepdims=True)

    counts = lax.fori_loop(0, T // blk, phase1, jnp.zeros((1, LANES), F32))

    in_e = lane1 < NE
    c0 = jnp.where(in_e, counts, 0.0)
    ctot = jnp.where(in_e, counts + pltpu.roll(counts, LANES - NE, 1), 0.0)
    tiles = jnp.floor((ctot + (tm - 1)) * (1.0 / tm))
    u_r = lax.broadcasted_iota(jnp.int32, (LANES, LANES), 0)
    u_c = lax.broadcasted_iota(jnp.int32, (LANES, LANES), 1)
    utri = (u_r <= u_c).astype(BF16)
    tile_end = jnp.dot(jnp.broadcast_to(tiles, (8, LANES)).astype(BF16), utri,
                       preferred_element_type=F32)[0:1, :]
    row_start = (tile_end - tiles) * tm
    base = jnp.where(in_e, row_start, pltpu.roll(row_start + c0, NE, 1))
    cnt_ref[...] = jnp.concatenate([ctot, tiles, tile_end, jnp.zeros((5, LANES), F32)], axis=0)

    def phase2(b, carry):
        rows = pl.ds(pl.multiple_of(b * blk, blk), blk)
        t = oh_ref[rows, :] * (pre_ref[rows, :] + base)
        p0 = jnp.sum(jnp.where(lane < NE, t, 0.0), axis=1, keepdims=True)
        p1 = jnp.sum(jnp.where(lane >= NE, t, 0.0), axis=1, keepdims=True)
        pos_ref[rows, :] = jnp.where(lane == 0, p0, jnp.where(lane == 1, p1, 0.0)).astype(jnp.int32)
        return carry

    lax.fori_loop(0, T // blk, phase2, 0)


def _route(logits, bias_row, tm):
    T = logits.shape[0]
    whole = lambda shape: pl.BlockSpec(shape, lambda: (0,) * len(shape))
    return pl.pallas_call(
        functools.partial(_route_kernel, tm=tm),
        in_specs=[whole((T, LANES)), whole((1, LANES))],
        out_specs=[whole((T, LANES)), whole((T, LANES)), whole((8, LANES))],
        out_shape=[
            jax.ShapeDtypeStruct((T, LANES), jnp.int32),
            jax.ShapeDtypeStruct((T, LANES), F32),
            jax.ShapeDtypeStruct((8, LANES), F32),
        ],
        scratch_shapes=[pltpu.VMEM((T, LANES), F32), pltpu.VMEM((T, LANES), F32)],
        compiler_params=pltpu.CompilerParams(vmem_limit_bytes=VMEM_LIMIT),
        name="route_positions",
    )(logits, bias_row)


def _tile_plan(cnt, tm, n_tiles):
    counts = cnt[0, :N_EXPERTS].astype(jnp.int32)
    tiles_per_e = cnt[1, :N_EXPERTS].astype(jnp.int32)
    tile_end = cnt[2, :N_EXPERTS].astype(jnp.int32)
    n_valid = tile_end[-1]
    tile_ids = jnp.arange(n_tiles, dtype=jnp.int32)
    experts = jnp.arange(N_EXPERTS, dtype=jnp.int32)
    valid = tile_ids < n_valid
    jc = jnp.minimum(tile_ids, n_valid - 1)
    tile_expert = jnp.minimum(jnp.sum((tile_end[None, :] <= jc[:, None]).astype(jnp.int32), axis=1),
                              N_EXPERTS - 1)
    of_tile = (tile_expert[:, None] == experts[None, :]).astype(jnp.int32)
    pick = lambda per_expert: jnp.sum(of_tile * per_expert[None, :], axis=1)
    tile_start = tile_end - tiles_per_e
    nonempty = tiles_per_e > 0
    group_idx = jnp.sum(jnp.logical_and(nonempty[None, :], experts[None, :] <= experts[:, None]).astype(jnp.int32),
                        axis=1) - 1
    later = jnp.logical_and(nonempty[None, :], experts[None, :] > experts[:, None])
    next_e = jnp.min(jnp.where(later, experts[None, :], N_EXPERTS), axis=1)
    next_e = jnp.where(next_e == N_EXPERTS, -1, next_e)
    tile_first = jnp.logical_and(valid, tile_ids == pick(tile_start)).astype(jnp.int32)
    tile_slot = (pick(group_idx) % 2).astype(jnp.int32)
    tile_next = pick(next_e).astype(jnp.int32)
    tile_rows = jnp.clip(pick(counts) - (tile_ids - pick(tile_start)) * tm, 0, tm)
    tile_rows = jnp.where(valid, tile_rows, 0).astype(jnp.int32)
    return (tile_expert.astype(jnp.int32), n_valid.reshape(1), tile_first, tile_slot, tile_next, tile_rows)


def kernel(x, rel_bias, ln_mix_g, w_in, conv_w, conv_b, b_i, b_f, lam_q1, lam_k1, lam_q2, lam_k2,
           diff_norm_g, mlstm_norm_g, w_out, ln_ffn_g, w_group, b_group, w_router, b_router,
           w_gate, w_up, w_down, ln_f_g):
    B, S, D = x.shape
    T = B * S
    depth = w_in.shape[0]
    assert depth == 1, "the final rmsnorm is fused into the single layer's combine kernel"
    Hm = N_MLSTM_HEADS
    n_main = w_in.shape[2] - 2 * Hm
    n_diff = N_DIFF_HEADS * 2 * DIFF_HEAD_DIM
    xf = x.reshape(T, D)

    for l in range(depth):
        lambda_init = 0.8 - 0.6 * math.exp(-0.3 * l)
        w_main = w_in[l, :, :n_main].astype(BF16)
        wgt = w_in[l, :, n_main:]
        w_gates = jnp.zeros((D, 2 * LANES), F32).at[:, :Hm].set(wgt[:, :Hm]).at[:, LANES:LANES + Hm].set(
            wgt[:, Hm:]).astype(BF16)
        bi_row = jnp.zeros((1, LANES), F32).at[0, :Hm].set(b_i[l].astype(F32))
        bf_row = jnp.zeros((1, LANES), F32).at[0, :Hm].set(b_f[l].astype(F32))
        lam = (jnp.exp(jnp.sum(lam_q1[l].astype(F32) * lam_k1[l].astype(F32)))
               - jnp.exp(jnp.sum(lam_q2[l].astype(F32) * lam_k2[l].astype(F32))) + lambda_init)
        tq = ATT_TQ
        assert tq >= MAX_DISTANCE and tq % CHUNK == 0
        rb = rel_bias.astype(F32)
        log2e = math.log2(math.e)
        xx = jnp.arange(2 * tq, dtype=jnp.int32)
        rel_vec = jnp.stack([-tq + tq - 1 - xx, tq - 1 - xx], axis=0)
        bias_vecs = jnp.take(rb, _t5_bucket(rel_vec), axis=0) * log2e
        bias_vecs = jnp.transpose(bias_vecs, (2, 0, 1))[:, :, None, :]
        cfar = rb[N_BUCKETS // 2 - 1] * log2e
        scal = jnp.concatenate([lam.reshape(1), cfar]).astype(F32)
        col_scale = jnp.ones((1, n_main), F32).at[:, :n_diff].set(DIFF_HEAD_DIM ** -0.5 * log2e)

        proj, gates = _proj(xf, ln_mix_g[l].reshape(1, D).astype(F32), w_main, col_scale, w_gates)
        proj3 = proj.reshape(B, S, n_main)
        a = _diff_attention(proj3, scal, bias_vecs, diff_norm_g[l].reshape(1, n_diff).astype(F32),
                            lambda_init)
        hm = _mlstm(proj3, gates.reshape(B, S, 2 * LANES), conv_w[l].astype(F32),
                    conv_b[l].reshape(1, -1).astype(F32), bi_row, bf_row,
                    mlstm_norm_g[l].reshape(1, -1).astype(F32))

        wo = w_out[l]
        G, E = N_GROUPS, EXPERTS_PER_GROUP
        wr = jnp.zeros((D, LANES), F32).at[:, :G].set(w_group[l].astype(F32)).at[:, G:G + G * E].set(
            jnp.transpose(w_router[l].astype(F32), (1, 0, 2)).reshape(D, G * E)).astype(BF16)
        x1, h2, logits = _out_proj(xf, a.reshape(T, n_diff), hm.reshape(T, -1), wo, wo,
                                   ln_ffn_g[l].reshape(1, D).astype(F32), wr)

        route_bias = jnp.concatenate([b_group[l].astype(F32).reshape(-1), b_router[l].astype(F32).reshape(-1),
                                      jnp.zeros((LANES - G - G * E,), F32)]).reshape(1, LANES)
        n_tiles = (T * TOP_K_INNER) // MOE_TM + N_EXPERTS
        pos128, cwp, cnt = _route(logits, route_bias, MOE_TM)
        pos_t = pos128[:, :TOP_K_INNER].T
        tiles = _tile_plan(cnt, MOE_TM, n_tiles)
        xs = _sc_scatter_rows(h2, pos_t, n_tiles * MOE_TM)
        Fe = w_gate.shape[-1]
        ys = _moe(*tiles, xs, w_gate[l].reshape(N_EXPERTS, D, Fe),
                  w_up[l].reshape(N_EXPERTS, D, Fe), w_down[l].reshape(N_EXPERTS, Fe, D))
        yw = _sc_gather_rows(ys, pos_t.reshape(-1))
        xf = _final(x1, yw, cwp, ln_f_g.reshape(1, D).astype(F32))
    return xf.reshape(B, S, D)
```

```python
import functools
import math

import jax
import jax.numpy as jnp
from jax import lax
from jax.experimental import pallas as pl
from jax.experimental.pallas import tpu as pltpu
from jax.experimental.pallas import tpu_sc as plsc

F32 = jnp.float32
BF16 = jnp.bfloat16

EPS = 1e-6
CHUNK = 64
DIFF_HEAD_DIM = 64
N_DIFF_HEADS = 8
MLSTM_HEAD_DIM = 128
N_MLSTM_HEADS = 8
CONV_WIDTH = 4
N_BUCKETS = 32
MAX_DISTANCE = 128
N_GROUPS = 4
EXPERTS_PER_GROUP = 8
N_EXPERTS = N_GROUPS * EXPERTS_PER_GROUP
TOP_K_INNER = 2
LANES = 128
NEG_BIG = -1e30

VMEM_LIMIT = 56 * 1024 * 1024

PROJ_TM, PROJ_TN = 1024, 1024
ATT_TQ = 512
MLSTM_TS = 1024
OUT_TM = 512
MOE_TM = 256
FIN_TM = 512


def _cparams(sem):
    return pltpu.CompilerParams(dimension_semantics=sem, vmem_limit_bytes=VMEM_LIMIT)


_HI_MASK = 0xFFFF0000


def _pack_bf16_pairs(x):
    half = x.shape[-1] // 2
    xb = x.astype(BF16).astype(F32)
    lo = pltpu.bitcast(xb[:, :half], jnp.uint32)
    hi = pltpu.bitcast(xb[:, half:], jnp.uint32)
    return (hi & jnp.uint32(_HI_MASK)) | (lo >> 16)


def _unpack_bf16_pairs(w):
    lo = pltpu.bitcast(w << 16, F32)
    hi = pltpu.bitcast(w & jnp.uint32(_HI_MASK), F32)
    return lo, hi


def _proj_kernel(x_ref, g_ref, w_ref, cs_ref, wg_ref, o_ref, og_ref, h_ref):
    @pl.when(pl.program_id(1) == 0)
    def _():
        x = x_ref[...]
        ms = jnp.mean(x * x, axis=-1, keepdims=True)
        h = (x * lax.rsqrt(ms + EPS) * g_ref[...]).astype(BF16)
        h_ref[...] = h
        og_ref[...] = jnp.dot(h, wg_ref[...], preferred_element_type=F32)

    o_ref[...] = (jnp.dot(h_ref[...], w_ref[...], preferred_element_type=F32) * cs_ref[...]).astype(o_ref.dtype)


def _proj(x2, g, w_main, col_scale, w_gates):
    T, D = x2.shape
    N = w_main.shape[1]
    NG = w_gates.shape[1]
    return pl.pallas_call(
        _proj_kernel,
        grid=(T // PROJ_TM, N // PROJ_TN),
        in_specs=[
            pl.BlockSpec((PROJ_TM, D), lambda m, n: (m, 0)),
            pl.BlockSpec((1, D), lambda m, n: (0, 0)),
            pl.BlockSpec((D, PROJ_TN), lambda m, n: (0, n)),
            pl.BlockSpec((1, PROJ_TN), lambda m, n: (0, n)),
            pl.BlockSpec((D, NG), lambda m, n: (0, 0)),
        ],
        out_specs=[
            pl.BlockSpec((PROJ_TM, PROJ_TN), lambda m, n: (m, n)),
            pl.BlockSpec((PROJ_TM, NG), lambda m, n: (m, 0)),
        ],
        out_shape=[
            jax.ShapeDtypeStruct((T, N), BF16),
            jax.ShapeDtypeStruct((T, NG), F32),
        ],
        scratch_shapes=[pltpu.VMEM((PROJ_TM, D), BF16)],
        compiler_params=_cparams(("parallel", "arbitrary")),
        name="rms_in_proj",
    )(x2, g, w_main, col_scale, w_gates)


def _t5_bucket(rel):
    half = N_BUCKETS // 2
    max_exact = half // 2
    ret = jnp.where(rel > 0, half, 0)
    n = jnp.abs(rel)
    nf = jnp.maximum(n, 1).astype(F32)
    large = max_exact + (jnp.log(nf / max_exact) / math.log(MAX_DISTANCE / max_exact)
                         * (half - max_exact)).astype(jnp.int32)
    large = jnp.minimum(large, half - 1)
    return ret + jnp.where(n < max_exact, n, large)


def _attn_kernel(scal_ref, q_ref, k_ref, v_ref, bias_ref, g_ref, o_ref, m_ref, l_ref, acc_ref,
                 s0_ref, s1_ref, s2_ref, ml0_ref, ml1_ref, ml2_ref, bt_ref, *, lambda_init):
    h = pl.program_id(1)
    qi = pl.program_id(2)
    nq = pl.num_programs(2)
    tq = ATT_TQ
    lam = scal_ref[0]
    cfar = scal_ref[1 + h]

    def stacked_queries(tile):
        q = q_ref[0, pl.ds(pl.multiple_of(tile * tq, tq), tq), :]
        lane = lax.broadcasted_iota(jnp.int32, q.shape, 1)
        zero = jnp.zeros_like(q)
        return jnp.concatenate([jnp.where(lane < DIFF_HEAD_DIM, q, zero),
                                jnp.where(lane >= DIFF_HEAD_DIM, q, zero)], axis=0)

    qs = stacked_queries(qi)
    nxt = jnp.minimum(qi + 1, nq - 1)
    qs_next = stacked_queries(nxt)

    m_ref[...] = jnp.full(m_ref.shape, NEG_BIG, F32)
    l_ref[...] = jnp.zeros(l_ref.shape, F32)
    acc_ref[...] = jnp.zeros(acc_ref.shape, F32)

    @pl.when(qi == 0)
    def _():
        kj = lax.broadcasted_iota(jnp.int32, (tq, tq), 0)
        qq = lax.broadcasted_iota(jnp.int32, (tq, tq), 1)
        allowed = (kj // CHUNK) <= (qq // CHUNK)
        for d in range(2):
            rows = jnp.broadcast_to(bias_ref[0, d], (tq, 2 * tq))
            tile = pltpu.roll(rows, tq + 1, 1, stride=1, stride_axis=0)[:, :tq]
            if d == 1:
                tile = jnp.where(allowed, tile, NEG_BIG)
            bt_ref[d] = tile

    bufs = ((s0_ref, ml0_ref), (s1_ref, ml1_ref), (s2_ref, ml2_ref))

    def score(ki, bias, slot, queries=None):
        s_ref, ml_ref = bufs[slot]
        start = pl.multiple_of(ki * tq, tq)
        kt = k_ref[0, pl.ds(start, tq), :]
        s = lax.dot_general(kt, qs if queries is None else queries, (((1,), (1,)), ((), ())),
                            preferred_element_type=F32)
        if bias is not None:
            s = s + jnp.concatenate([bias, bias], axis=1)
        s_ref[...] = s
        ml_ref[...] = jnp.max(s, axis=0, keepdims=True)

    def accumulate(ki, shift, slot):
        s_ref, ml_ref = bufs[slot]
        start = pl.multiple_of(ki * tq, tq)
        vt = v_ref[0, pl.ds(start, tq), :]
        m_old = m_ref[...]
        m_new = jnp.maximum(m_old, ml_ref[...] + shift)
        alpha = jnp.exp2(m_old - m_new)
        p = jnp.exp2(s_ref[...] - (m_new - shift))
        l_ref[...] = alpha * l_ref[...] + jnp.sum(p, axis=0, keepdims=True)
        pv = lax.dot_general(vt, p.astype(BF16), (((0,), (0,)), ((), ())), preferred_element_type=F32)
        acc_ref[...] = alpha * acc_ref[...] + pv
        m_ref[...] = m_new

    n_far = qi - 1

    def score_next_diagonal():
        score(nxt, bt_ref[1], 2, qs_next)

    @pl.when(qi == 0)
    def _():
        score(qi, bt_ref[1], 2)
        accumulate(qi, 0.0, 2)
        score_next_diagonal()

    @pl.when(qi >= 1)
    def _():
        accumulate(qi, 0.0, 2)
        score(qi - 1, bt_ref[0], 1)

    @pl.when(qi == 1)
    def _():
        accumulate(qi - 1, 0.0, 1)
        score_next_diagonal()

    @pl.when(qi >= 2)
    def _():
        accumulate(qi - 1, 0.0, 1)
        score(0, None, 0)
        trips = (n_far - 1) // 2

        def pair(j, c):
            accumulate(2 * j, cfar, 0)
            score(2 * j + 1, None, 1)
            accumulate(2 * j + 1, cfar, 1)
            score(2 * j + 2, None, 0)
            return c

        lax.fori_loop(0, trips, pair, 0)
        last = 2 * trips

        @pl.when(n_far - last == 2)
        def _():
            accumulate(last, cfar, 0)
            score(last + 1, None, 1)
            accumulate(last + 1, cfar, 1)
            score_next_diagonal()

        @pl.when(n_far - last == 1)
        def _():
            accumulate(last, cfar, 0)
            score_next_diagonal()

    acc = acc_ref[...] * (1.0 / l_ref[...])
    o_t = acc[:, 0:tq] - lam * acc[:, tq:2 * tq]
    ms = jnp.mean(o_t * o_t, axis=0, keepdims=True)
    y = (o_t * lax.rsqrt(ms + EPS)).T * (g_ref[...] * (1.0 - lambda_init))
    o_ref[0] = y.astype(o_ref.dtype)


def _diff_attention(proj3, scal, bias_vecs, gnorm, lambda_init):
    B, S, _ = proj3.shape
    H = N_DIFF_HEADS
    tq = ATT_TQ
    kern = functools.partial(_attn_kernel, lambda_init=lambda_init)
    return pl.pallas_call(
        kern,
        grid=(B, H, S // tq),
        in_specs=[
            pl.BlockSpec(memory_space=pltpu.SMEM),
            pl.BlockSpec((1, S, LANES), lambda b, h, i: (b, 0, h)),
            pl.BlockSpec((1, S, LANES), lambda b, h, i: (b, 0, H + h)),
            pl.BlockSpec((1, S, LANES), lambda b, h, i: (b, 0, 2 * H + h)),
            pl.BlockSpec((1, 2, 1, 2 * tq), lambda b, h, i: (h, 0, 0, 0)),
            pl.BlockSpec((1, LANES), lambda b, h, i: (0, h)),
        ],
        out_specs=pl.BlockSpec((1, tq, LANES), lambda b, h, i: (b, i, h)),
        out_shape=jax.ShapeDtypeStruct((B, S, H * LANES), BF16),
        scratch_shapes=[
            pltpu.VMEM((1, 2 * tq), F32),
            pltpu.VMEM((1, 2 * tq), F32),
            pltpu.VMEM((LANES, 2 * tq), F32),
            pltpu.VMEM((tq, 2 * tq), F32),
            pltpu.VMEM((tq, 2 * tq), F32),
            pltpu.VMEM((tq, 2 * tq), F32),
            pltpu.VMEM((1, 2 * tq), F32),
            pltpu.VMEM((1, 2 * tq), F32),
            pltpu.VMEM((1, 2 * tq), F32),
            pltpu.VMEM((2, tq, tq), F32),
        ],
        compiler_params=_cparams(("parallel", "parallel", "arbitrary")),
        name="diff_attention",
    )(scal, proj3, proj3, proj3, bias_vecs, gnorm)


def _log_sigmoid(x):
    return jnp.minimum(x, 0.0) - jnp.log(1.0 + jnp.exp(-jnp.abs(x)))


def _sigmoid(x):
    return 1.0 / (1.0 + jnp.exp(-x))


def _mlstm_kernel(q_ref, k_ref, v_ref, o_ref, gi_ref, gf_ref, cw_ref, cb_ref, bi_ref, bf_ref, gn_ref,
                  out_ref, qext_ref, kext_ref, ct_ref, n_ref, m_ref):
    sb = pl.program_id(1)
    L = CHUNK
    dh = MLSTM_HEAD_DIM
    H = N_MLSTM_HEADS
    ts = MLSTM_TS
    pad = 8

    @pl.when(sb == 0)
    def _():
        qext_ref[0:pad, :] = jnp.zeros((pad, H * dh), F32)
        kext_ref[0:pad, :] = jnp.zeros((pad, H * dh), F32)
        ct_ref[...] = jnp.zeros(ct_ref.shape, F32)
        n_ref[...] = jnp.zeros(n_ref.shape, F32)
        m_ref[...] = jnp.zeros(m_ref.shape, F32)

    qext_ref[pad:pad + ts, :] = q_ref[0].astype(F32)
    kext_ref[pad:pad + ts, :] = k_ref[0].astype(F32)

    row = lax.broadcasted_iota(jnp.int32, (L, L), 0)
    col = lax.broadcasted_iota(jnp.int32, (L, L), 1)
    tril = col <= row
    ltri = tril.astype(F32)

    def conv_silu(ext_ref, base, h, off):
        win = ext_ref[pl.ds(base, L + pad), h * dh:(h + 1) * dh]
        w = cw_ref[:, off + h * dh:off + (h + 1) * dh]
        y = cb_ref[:, off + h * dh:off + (h + 1) * dh]
        for j in range(CONV_WIDTH):
            lo = pad - (CONV_WIDTH - 1) + j
            y = y + w[j:j + 1, :] * win[lo:lo + L, :]
        return y * _sigmoid(y)

    def chunk_body(c, carry):
        base = pl.multiple_of(c * L, L)
        li = gi_ref[0, pl.ds(base, L), :] + bi_ref[...]
        logf = _log_sigmoid(gf_ref[0, pl.ds(base, L), :] + bf_ref[...])
        b = jnp.dot(ltri, logf, preferred_element_type=F32, precision=lax.Precision.HIGHEST)
        a = li - b
        g_row = b[L - 1:L, :]
        m_row = m_ref[...]
        m_new_row = g_row + jnp.maximum(m_row, jnp.max(a, axis=0, keepdims=True))
        a_t = a.T

        for h in range(H):
            qc = conv_silu(qext_ref, base, h, 0)
            kc = conv_silu(kext_ref, base, h, H * dh) * (dh ** -0.5)
            qb = qc.astype(BF16)
            kb = kc.astype(BF16)
            vb = v_ref[0, pl.ds(base, L), h * dh:(h + 1) * dh]

            a_row = a_t[h:h + 1, :]
            a_col = a[:, h:h + 1]
            b_col = b[:, h:h + 1]
            m_prev = m_row[:, h:h + 1]
            m_next = m_new_row[:, h:h + 1]
            g_h = g_row[:, h:h + 1]

            amat = jnp.where(tril, a_row, NEG_BIG)
            mcol = jnp.maximum(jnp.max(amat, axis=-1, keepdims=True), m_prev)
            wts = jnp.exp(amat - mcol)
            inter = jnp.exp(m_prev - mcol)

            s = lax.dot_general(qb, kb, (((1,), (1,)), ((), ())), preferred_element_type=F32)
            sqk = s * wts
            ct = ct_ref[h]
            nrow = n_ref[h:h + 1, :]
            num = (jnp.dot(sqk.astype(BF16), vb, preferred_element_type=F32)
                   + inter * jnp.dot(qb, ct.astype(BF16), preferred_element_type=F32))
            den = (jnp.sum(sqk, axis=-1, keepdims=True)
                   + inter * jnp.sum(qb.astype(F32) * nrow, axis=-1, keepdims=True))
            hv = num / jnp.maximum(jnp.abs(den), jnp.exp(-(b_col + mcol)))

            wt = jnp.exp(g_h + a_col - m_next)
            decay = jnp.exp(g_h + m_prev - m_next)
            wv = (wt * vb.astype(F32)).astype(BF16)
            ct_ref[h] = decay * ct + lax.dot_general(kb, wv, (((0,), (0,)), ((), ())),
                                                     preferred_element_type=F32)
            n_ref[h:h + 1, :] = decay * nrow + jnp.sum(wt * kb.astype(F32), axis=0, keepdims=True)

            ms = jnp.mean(hv * hv, axis=-1, keepdims=True)
            y = hv * lax.rsqrt(ms + EPS) * gn_ref[:, h * dh:(h + 1) * dh]
            og = o_ref[0, pl.ds(base, L), h * dh:(h + 1) * dh].astype(F32)
            out_ref[0, pl.ds(base, L), h * dh:(h + 1) * dh] = (y * _sigmoid(og)).astype(out_ref.dtype)

        m_ref[...] = m_new_row
        return carry

    lax.fori_loop(0, ts // L, chunk_body, 0, unroll=4)

    qext_ref[0:pad, :] = qext_ref[ts:ts + pad, :]
    kext_ref[0:pad, :] = kext_ref[ts:ts + pad, :]


def _mlstm(proj3, gates3, conv_w, conv_b, bi_row, bf_row, gnorm):
    B, S, _ = proj3.shape
    W = N_MLSTM_HEADS * MLSTM_HEAD_DIM
    ts = MLSTM_TS
    first = 3
    blk = lambda j: pl.BlockSpec((1, ts, W), lambda b, s: (b, s, j))
    full = lambda shape: pl.BlockSpec(shape, lambda b, s: (0,) * len(shape))
    return pl.pallas_call(
        _mlstm_kernel,
        grid=(B, S // ts),
        in_specs=[
            blk(first), blk(first + 1), blk(first + 2), blk(first + 3),
            pl.BlockSpec((1, ts, LANES), lambda b, s: (b, s, 0)),
            pl.BlockSpec((1, ts, LANES), lambda b, s: (b, s, 1)),
            full((CONV_WIDTH, 2 * W)), full((1, 2 * W)),
            full((1, LANES)), full((1, LANES)), full((1, W)),
        ],
        out_specs=pl.BlockSpec((1, ts, W), lambda b, s: (b, s, 0)),
        out_shape=jax.ShapeDtypeStruct((B, S, W), BF16),
        scratch_shapes=[
            pltpu.VMEM((ts + 8, W), F32),
            pltpu.VMEM((ts + 8, W), F32),
            pltpu.VMEM((N_MLSTM_HEADS, MLSTM_HEAD_DIM, MLSTM_HEAD_DIM), F32),
            pltpu.VMEM((N_MLSTM_HEADS, MLSTM_HEAD_DIM), F32),
            pltpu.VMEM((1, LANES), F32),
        ],
        compiler_params=_cparams(("parallel", "arbitrary")),
        name="mlstm",
    )(proj3, proj3, proj3, proj3, gates3, gates3, conv_w, conv_b, bi_row, bf_row, gnorm)


def _out_kernel(x_ref, a_ref, hm_ref, wa_ref, wm_ref, g_ref, wr_ref, x1_ref, h2_ref, lg_ref):
    y = (jnp.dot(a_ref[...], wa_ref[...].astype(BF16), preferred_element_type=F32)
         + jnp.dot(hm_ref[...], wm_ref[...].astype(BF16), preferred_element_type=F32))
    x1 = x_ref[...] + y
    x1_ref[...] = x1
    ms = jnp.mean(x1 * x1, axis=-1, keepdims=True)
    h2 = x1 * lax.rsqrt(ms + EPS) * g_ref[...]
    h2_ref[...] = _pack_bf16_pairs(h2)
    lg_ref[...] = jnp.dot(h2.astype(BF16), wr_ref[...], preferred_element_type=F32)


def _out_proj(x2, a2, hm2, wa, wm, g, wr):
    T, D = x2.shape
    W = a2.shape[1]
    tm = OUT_TM
    const = lambda shape: pl.BlockSpec(shape, lambda m: (0, 0), pipeline_mode=pl.Buffered(1))
    return pl.pallas_call(
        _out_kernel,
        grid=(T // tm,),
        in_specs=[
            pl.BlockSpec((tm, D), lambda m: (m, 0)),
            pl.BlockSpec((tm, W), lambda m: (m, 0)),
            pl.BlockSpec((tm, W), lambda m: (m, 0)),
            const((W, D)),
            pl.BlockSpec((W, D), lambda m: (1, 0), pipeline_mode=pl.Buffered(1)),
            const((1, D)), const((D, LANES)),
        ],
        out_specs=[
            pl.BlockSpec((tm, D), lambda m: (m, 0)),
            pl.BlockSpec((tm, D // 2), lambda m: (m, 0)),
            pl.BlockSpec((tm, LANES), lambda m: (m, 0)),
        ],
        out_shape=[
            jax.ShapeDtypeStruct((T, D), F32),
            jax.ShapeDtypeStruct((T, D // 2), jnp.uint32),
            jax.ShapeDtypeStruct((T, LANES), F32),
        ],
        compiler_params=_cparams(("parallel",)),
        name="out_proj_router",
    )(x2, a2, hm2, wa, wm, g, wr)


def _moe_kernel(te_ref, nv_ref, first_ref, slot_ref, nxt_ref, rows_ref, xs_ref, wg_hbm, wu_hbm, wd_hbm,
                ys_ref, wg_buf, wu_buf, wd_buf, sem):
    j = pl.program_id(0)
    valid = j < nv_ref[0]

    half_f = wd_buf.shape[1] // 2

    def weight_copies(e, s):
        lo, hi = pl.ds(0, half_f), pl.ds(half_f, half_f)
        return ((pltpu.make_async_copy(wg_hbm.at[e], wg_buf.at[s], sem.at[s, 0]), 0),
                (pltpu.make_async_copy(wu_hbm.at[e], wu_buf.at[s], sem.at[s, 1]), 1),
                (pltpu.make_async_copy(wd_hbm.at[e, lo], wd_buf.at[s, lo], sem.at[s, 2]), 0),
                (pltpu.make_async_copy(wd_hbm.at[e, hi], wd_buf.at[s, hi], sem.at[s, 3]), 1))

    @pl.when(j == 0)
    def _():
        for c, prio in weight_copies(te_ref[0], 0):
            c.start(priority=prio)

    @pl.when(jnp.logical_and(valid, first_ref[j] == 1))
    def _():
        for c, _ in weight_copies(te_ref[j], slot_ref[j]):
            c.wait()

        @pl.when(nxt_ref[j] >= 0)
        def _():
            for c, prio in weight_copies(nxt_ref[j], 1 - slot_ref[j]):
                c.start(priority=prio)

    @pl.when(valid)
    def _():
        s = slot_ref[j]
        row = lax.broadcasted_iota(jnp.int32, xs_ref.shape, 0)
        lo, hi = _unpack_bf16_pairs(jnp.where(row < rows_ref[j], xs_ref[...], jnp.uint32(0)))
        xs = jnp.concatenate([lo.astype(BF16), hi.astype(BF16)], axis=1)
        gt = jnp.dot(xs, wg_buf[s].astype(BF16), preferred_element_type=F32)
        up = jnp.dot(xs, wu_buf[s].astype(BF16), preferred_element_type=F32)
        hid = (gt * _sigmoid(gt) * up).astype(BF16)
        ys_ref[...] = _pack_bf16_pairs(jnp.dot(hid, wd_buf[s].astype(BF16), preferred_element_type=F32))

    @pl.when(jnp.logical_not(valid))
    def _():
        ys_ref[...] = jnp.zeros(ys_ref.shape, ys_ref.dtype)


def _moe(tile_expert, n_valid, tile_first, tile_slot, tile_next, tile_rows, xs, wg, wu, wd):
    R, Dw = xs.shape
    D, F = wg.shape[1], wg.shape[2]
    tm = MOE_TM
    hbm = pl.BlockSpec(memory_space=pl.ANY)
    grid_spec = pltpu.PrefetchScalarGridSpec(
        num_scalar_prefetch=6,
        grid=(R // tm,),
        in_specs=[pl.BlockSpec((tm, Dw), lambda j, *_: (j, 0)), hbm, hbm, hbm],
        out_specs=pl.BlockSpec((tm, Dw), lambda j, *_: (j, 0)),
        scratch_shapes=[
            pltpu.VMEM((2, D, F), wg.dtype),
            pltpu.VMEM((2, D, F), wu.dtype),
            pltpu.VMEM((2, F, D), wd.dtype),
            pltpu.SemaphoreType.DMA((2, 4)),
        ],
    )
    return pl.pallas_call(
        _moe_kernel,
        grid_spec=grid_spec,
        out_shape=jax.ShapeDtypeStruct((R, Dw), jnp.uint32),
        compiler_params=_cparams(("arbitrary",)),
        name="moe_experts",
    )(tile_expert, n_valid, tile_first, tile_slot, tile_next, tile_rows, xs, wg, wu, wd)


def _final_kernel(x1_ref, y0_ref, y1_ref, cw_ref, g_ref, o_ref):
    cw = cw_ref[...]
    lo0, hi0 = _unpack_bf16_pairs(y0_ref[...])
    lo1, hi1 = _unpack_bf16_pairs(y1_ref[...])
    w0, w1 = cw[:, 0:1], cw[:, 1:2]
    y = jnp.concatenate([w0 * lo0 + w1 * lo1, w0 * hi0 + w1 * hi1], axis=1)
    x = x1_ref[...] + y
    ms = jnp.mean(x * x, axis=-1, keepdims=True)
    o_ref[...] = x * lax.rsqrt(ms + EPS) * g_ref[...]


def _final(x1, yw, cw, g):
    T, D = x1.shape
    tm = FIN_TM
    row = lambda w: pl.BlockSpec((tm, w), lambda m: (m, 0))
    slot1 = pl.BlockSpec((tm, D // 2), lambda m: (m + T // tm, 0))
    return pl.pallas_call(
        _final_kernel,
        grid=(T // tm,),
        in_specs=[row(D), row(D // 2), slot1, row(LANES), pl.BlockSpec((1, D), lambda m: (0, 0))],
        out_specs=row(D),
        out_shape=jax.ShapeDtypeStruct((T, D), F32),
        compiler_params=_cparams(("parallel",)),
        name="combine_final_norm",
    )(x1, yw, yw, cw, g)


SC_CORES, SC_SUBCORES = 2, 16
SC_CHUNK = 32


def _sc_gather_rows(table, idx):
    V, Dw = table.shape
    R = idx.shape[0]
    n_workers = SC_CORES * SC_SUBCORES
    ch = SC_CHUNK
    per_w = R // n_workers
    n_chunks = per_w // ch
    assert per_w * n_workers == R and n_chunks * ch == per_w and n_chunks % 2 == 0
    idx3 = idx.reshape(n_workers, n_chunks, ch)
    mesh = plsc.VectorSubcoreMesh(core_axis_name="c", subcore_axis_name="s")

    def body(table_hbm, idx_hbm, out_hbm, idx_v, rows_v, gsem, osem):
        wid = lax.axis_index("s") * SC_CORES + lax.axis_index("c")
        base = wid * per_w
        pltpu.sync_copy(idx_hbm.at[wid], idx_v)

        def gather(c, slot):
            return pltpu.make_async_copy(table_hbm.at[idx_v.at[c]], rows_v.at[slot], gsem.at[slot])

        def put(c, slot):
            return pltpu.make_async_copy(rows_v.at[slot], out_hbm.at[pl.ds(base + c * ch, ch)],
                                         osem.at[slot])

        gather(0, 0).start()

        @pl.loop(0, n_chunks, step=2)
        def _(c):
            @pl.when(c > 0)
            def _():
                put(c - 1, 1).wait()

            gather(c + 1, 1).start()
            gather(c, 0).wait()
            put(c, 0).start()
            put(c, 0).wait()

            @pl.when(c + 2 < n_chunks)
            def _():
                gather(c + 2, 0).start()

            gather(c + 1, 1).wait()
            put(c + 1, 1).start()

        put(n_chunks - 1, 1).wait()

    return pl.kernel(
        body,
        out_type=jax.ShapeDtypeStruct((R, Dw), table.dtype),
        mesh=mesh,
        scratch_types=[
            pltpu.VMEM((n_chunks, ch), jnp.int32),
            pltpu.VMEM((2, ch, Dw), table.dtype),
            pltpu.SemaphoreType.DMA((2,)),
            pltpu.SemaphoreType.DMA((2,)),
        ],
        name="sc_gather_rows",
    )(table, idx3)


def _sc_scatter_rows(table, idx, n_rows_out):
    V, Dw = table.shape
    K = idx.shape[0]
    n_workers = SC_CORES * SC_SUBCORES
    ch = SC_CHUNK
    per_w = V // n_workers
    n_chunks = per_w // ch
    assert K == 2 and per_w * n_workers == V and n_chunks * ch == per_w and n_chunks % 2 == 0
    idx4 = jnp.transpose(idx.reshape(K, n_workers, n_chunks, ch), (1, 0, 2, 3))
    mesh = plsc.VectorSubcoreMesh(core_axis_name="c", subcore_axis_name="s")

    def body(table_hbm, idx_hbm, out_hbm, idx_v, rows_v, lsem, ssem):
        wid = lax.axis_index("s") * SC_CORES + lax.axis_index("c")
        base = wid * per_w
        pltpu.sync_copy(idx_hbm.at[wid], idx_v)

        def load(c, slot):
            return pltpu.make_async_copy(table_hbm.at[pl.ds(base + c * ch, ch)], rows_v.at[slot],
                                         lsem.at[slot])

        def scatter(c, slot, k):
            return pltpu.make_async_copy(rows_v.at[slot], out_hbm.at[idx_v.at[k, c]], ssem.at[slot, k])

        load(0, 0).start()

        @pl.loop(0, n_chunks, step=2)
        def _(c):
            @pl.when(c > 0)
            def _():
                scatter(c - 1, 1, 0).wait()
                scatter(c - 1, 1, 1).wait()

            load(c + 1, 1).start()
            load(c, 0).wait()
            scatter(c, 0, 0).start()
            scatter(c, 0, 1).start()
            scatter(c, 0, 0).wait()
            scatter(c, 0, 1).wait()

            @pl.when(c + 2 < n_chunks)
            def _():
                load(c + 2, 0).start()

            load(c + 1, 1).wait()
            scatter(c + 1, 1, 0).start()
            scatter(c + 1, 1, 1).start()

        scatter(n_chunks - 1, 1, 0).wait()
        scatter(n_chunks - 1, 1, 1).wait()

    return pl.kernel(
        body,
        out_type=jax.ShapeDtypeStruct((n_rows_out, Dw), table.dtype),
        mesh=mesh,
        scratch_types=[
            pltpu.VMEM((K, n_chunks, ch), jnp.int32),
            pltpu.VMEM((2, ch, Dw), table.dtype),
            pltpu.SemaphoreType.DMA((2,)),
            pltpu.SemaphoreType.DMA((2, K)),
        ],
        name="sc_scatter_rows",
    )(table, idx4)


ROUTE_BLK = 256


def _route_kernel(lg_ref, bias_ref, pos_ref, cw_ref, cnt_ref, oh_ref, pre_ref, *, tm):
    T = lg_ref.shape[0]
    G, E, NE = N_GROUPS, EXPERTS_PER_GROUP, N_EXPERTS
    blk = ROUTE_BLK
    sub = lax.broadcasted_iota(jnp.int32, (LANES, blk), 0).astype(F32)
    r_i = lax.broadcasted_iota(jnp.int32, (blk, blk), 0)
    c_i = lax.broadcasted_iota(jnp.int32, (blk, blk), 1)
    before = (r_i < c_i).astype(BF16)

    def first_argmax(v):
        mx = jnp.max(v, axis=0, keepdims=True)
        return mx, jnp.min(jnp.where(v == mx, sub, float(LANES)), axis=0, keepdims=True)

    def phase1(b, carry):
        cols = pl.ds(pl.multiple_of(b * blk, blk), blk)
        x = (lg_ref[cols, :] + bias_ref[...]).T
        gl = jnp.where(sub < G, x, NEG_BIG)
        gmax, gsel = first_argmax(gl)
        gw = 1.0 / jnp.sum(jnp.exp(gl - gmax), axis=0, keepdims=True)
        lo = G + E * gsel
        el = jnp.where(jnp.logical_and(sub >= lo, sub < lo + E), x, NEG_BIG)
        v1, i1 = first_argmax(el)
        v2, i2 = first_argmax(jnp.where(sub == i1, NEG_BIG, el))
        e21 = jnp.exp(v2 - v1)
        w1 = 1.0 / (1.0 + e21)
        cw_t = jnp.where(sub == 0, gw * w1, jnp.where(sub == 1, gw * (e21 * w1), 0.0))
        cw_ref[cols, :] = cw_t.T
        oh = jnp.where(jnp.logical_or(sub == i1 - G, sub == i2 - G + NE), 1.0, 0.0)
        oh_ref[:, cols] = oh
        pre_ref[:, cols] = jnp.dot(oh.astype(BF16), before, preferred_element_type=F32) + carry
        return carry + jnp.sum(oh, axis=1, keepdims=True)

    counts = lax.fori_loop(0, T // blk, phase1, jnp.zeros((LANES, 1), F32))

    c0 = counts[0:NE, :]
    ctot = c0 + counts[NE:2 * NE, :]
    tiles = jnp.floor((ctot + (tm - 1)) * (1.0 / tm))
    e_r = lax.broadcasted_iota(jnp.int32, (NE, NE), 0)
    e_c = lax.broadcasted_iota(jnp.int32, (NE, NE), 1)
    upto = (e_c <= e_r).astype(BF16)
    tile_end = jnp.dot(upto, jnp.broadcast_to(tiles, (NE, LANES)).astype(BF16),
                       preferred_element_type=F32)[:, 0:1]
    row_start = (tile_end - tiles) * tm
    base = jnp.concatenate([row_start, row_start + c0, jnp.zeros((LANES - 2 * NE, 1), F32)], axis=0)
    lane_e = lax.broadcasted_iota(jnp.int32, (NE, LANES), 1)
    cnt_ref[...] = jnp.concatenate(
        [jnp.where(lane_e == 0, ctot, jnp.where(lane_e == 1, tiles, jnp.where(lane_e == 2, tile_end, 0.0))),
         jnp.zeros((LANES - NE, LANES), F32)], axis=0)

    def phase2(b, carry):
        cols = pl.ds(pl.multiple_of(b * blk, blk), blk)
        t = oh_ref[:, cols] * (pre_ref[:, cols] + base)
        p0 = jnp.sum(t[0:NE, :], axis=0, keepdims=True)
        p1 = jnp.sum(t[NE:2 * NE, :], axis=0, keepdims=True)
        pos_ref[:, cols] = jnp.concatenate([p0, p1, jnp.zeros((6, blk), F32)], axis=0).astype(jnp.int32)
        return carry

    lax.fori_loop(0, T // blk, phase2, 0)


def _route(logits, bias_row, tm):
    T = logits.shape[0]
    whole = lambda shape: pl.BlockSpec(shape, lambda: (0,) * len(shape))
    return pl.pallas_call(
        functools.partial(_route_kernel, tm=tm),
        in_specs=[whole((T, LANES)), whole((1, LANES))],
        out_specs=[whole((8, T)), whole((T, LANES)), whole((LANES, LANES))],
        out_shape=[
            jax.ShapeDtypeStruct((8, T), jnp.int32),
            jax.ShapeDtypeStruct((T, LANES), F32),
            jax.ShapeDtypeStruct((LANES, LANES), F32),
        ],
        scratch_shapes=[pltpu.VMEM((LANES, T), F32), pltpu.VMEM((LANES, T), F32)],
        compiler_params=pltpu.CompilerParams(vmem_limit_bytes=VMEM_LIMIT),
        name="route_positions",
    )(logits, bias_row)


def _tile_plan(cnt, tm, n_tiles):
    counts = cnt[:N_EXPERTS, 0].astype(jnp.int32)
    tiles_per_e = cnt[:N_EXPERTS, 1].astype(jnp.int32)
    tile_end = cnt[:N_EXPERTS, 2].astype(jnp.int32)
    n_valid = tile_end[-1]
    tile_ids = jnp.arange(n_tiles, dtype=jnp.int32)
    experts = jnp.arange(N_EXPERTS, dtype=jnp.int32)
    valid = tile_ids < n_valid
    jc = jnp.minimum(tile_ids, n_valid - 1)
    tile_expert = jnp.minimum(jnp.sum((tile_end[None, :] <= jc[:, None]).astype(jnp.int32), axis=1),
                              N_EXPERTS - 1)
    of_tile = (tile_expert[:, None] == experts[None, :]).astype(jnp.int32)
    pick = lambda per_expert: jnp.sum(of_tile * per_expert[None, :], axis=1)
    tile_start = tile_end - tiles_per_e
    nonempty = tiles_per_e > 0
    group_idx = jnp.sum(jnp.logical_and(nonempty[None, :], experts[None, :] <= experts[:, None]).astype(jnp.int32),
                        axis=1) - 1
    later = jnp.logical_and(nonempty[None, :], experts[None, :] > experts[:, None])
    next_e = jnp.min(jnp.where(later, experts[None, :], N_EXPERTS), axis=1)
    next_e = jnp.where(next_e == N_EXPERTS, -1, next_e)
    tile_first = jnp.logical_and(valid, tile_ids == pick(tile_start)).astype(jnp.int32)
    tile_slot = (pick(group_idx) % 2).astype(jnp.int32)
    tile_next = pick(next_e).astype(jnp.int32)
    tile_rows = jnp.clip(pick(counts) - (tile_ids - pick(tile_start)) * tm, 0, tm)
    tile_rows = jnp.where(valid, tile_rows, 0).astype(jnp.int32)
    return (tile_expert.astype(jnp.int32), n_valid.reshape(1), tile_first, tile_slot, tile_next, tile_rows)


def kernel(x, rel_bias, ln_mix_g, w_in, conv_w, conv_b, b_i, b_f, lam_q1, lam_k1, lam_q2, lam_k2,
           diff_norm_g, mlstm_norm_g, w_out, ln_ffn_g, w_group, b_group, w_router, b_router,
           w_gate, w_up, w_down, ln_f_g):
    B, S, D = x.shape
    T = B * S
    depth = w_in.shape[0]
    assert depth == 1, "the final rmsnorm is fused into the single layer's combine kernel"
    Hm = N_MLSTM_HEADS
    n_main = w_in.shape[2] - 2 * Hm
    n_diff = N_DIFF_HEADS * 2 * DIFF_HEAD_DIM
    xf = x.reshape(T, D)

    for l in range(depth):
        lambda_init = 0.8 - 0.6 * math.exp(-0.3 * l)
        w_main = w_in[l, :, :n_main].astype(BF16)
        wgt = w_in[l, :, n_main:]
        w_gates = jnp.zeros((D, 2 * LANES), F32).at[:, :Hm].set(wgt[:, :Hm]).at[:, LANES:LANES + Hm].set(
            wgt[:, Hm:]).astype(BF16)
        bi_row = jnp.zeros((1, LANES), F32).at[0, :Hm].set(b_i[l].astype(F32))
        bf_row = jnp.zeros((1, LANES), F32).at[0, :Hm].set(b_f[l].astype(F32))
        lam = (jnp.exp(jnp.sum(lam_q1[l].astype(F32) * lam_k1[l].astype(F32)))
               - jnp.exp(jnp.sum(lam_q2[l].astype(F32) * lam_k2[l].astype(F32))) + lambda_init)
        tq = ATT_TQ
        assert tq >= MAX_DISTANCE and tq % CHUNK == 0
        rb = rel_bias.astype(F32)
        log2e = math.log2(math.e)
        xx = jnp.arange(2 * tq, dtype=jnp.int32)
        rel_vec = jnp.stack([-tq + tq - 1 - xx, tq - 1 - xx], axis=0)
        bias_vecs = jnp.take(rb, _t5_bucket(rel_vec), axis=0) * log2e
        bias_vecs = jnp.transpose(bias_vecs, (2, 0, 1))[:, :, None, :]
        cfar = rb[N_BUCKETS // 2 - 1] * log2e
        scal = jnp.concatenate([lam.reshape(1), cfar]).astype(F32)
        col_scale = jnp.ones((1, n_main), F32).at[:, :n_diff].set(DIFF_HEAD_DIM ** -0.5 * log2e)

        proj, gates = _proj(xf, ln_mix_g[l].reshape(1, D).astype(F32), w_main, col_scale, w_gates)
        proj3 = proj.reshape(B, S, n_main)
        a = _diff_attention(proj3, scal, bias_vecs, diff_norm_g[l].reshape(1, n_diff).astype(F32),
                            lambda_init)
        hm = _mlstm(proj3, gates.reshape(B, S, 2 * LANES), conv_w[l].astype(F32),
                    conv_b[l].reshape(1, -1).astype(F32), bi_row, bf_row,
                    mlstm_norm_g[l].reshape(1, -1).astype(F32))

        wo = w_out[l]
        G, E = N_GROUPS, EXPERTS_PER_GROUP
        wr = jnp.zeros((D, LANES), F32).at[:, :G].set(w_group[l].astype(F32)).at[:, G:G + G * E].set(
            jnp.transpose(w_router[l].astype(F32), (1, 0, 2)).reshape(D, G * E)).astype(BF16)
        x1, h2, logits = _out_proj(xf, a.reshape(T, n_diff), hm.reshape(T, -1), wo, wo,
                                   ln_ffn_g[l].reshape(1, D).astype(F32), wr)

        route_bias = jnp.concatenate([b_group[l].astype(F32).reshape(-1), b_router[l].astype(F32).reshape(-1),
                                      jnp.zeros((LANES - G - G * E,), F32)]).reshape(1, LANES)
        n_tiles = (T * TOP_K_INNER) // MOE_TM + N_EXPERTS
        pos8, cwp, cnt = _route(logits, route_bias, MOE_TM)
        pos_t = pos8[:TOP_K_INNER]
        tiles = _tile_plan(cnt, MOE_TM, n_tiles)
        xs = _sc_scatter_rows(h2, pos_t, n_tiles * MOE_TM)
        Fe = w_gate.shape[-1]
        ys = _moe(*tiles, xs, w_gate[l].reshape(N_EXPERTS, D, Fe),
                  w_up[l].reshape(N_EXPERTS, D, Fe), w_down[l].reshape(N_EXPERTS, Fe, D))
        yw = _sc_gather_rows(ys, pos_t.reshape(-1))
        xf = _final(x1, yw, cwp, ln_f_g.reshape(1, D).astype(F32))
    return xf.reshape(B, S, D)
```

```python
import functools
import math

import jax
import jax.numpy as jnp
from jax import lax
from jax.experimental import pallas as pl
from jax.experimental.pallas import tpu as pltpu
from jax.experimental.pallas import tpu_sc as plsc

F32 = jnp.float32
BF16 = jnp.bfloat16

EPS = 1e-6
CHUNK = 64
DIFF_HEAD_DIM = 64
N_DIFF_HEADS = 8
MLSTM_HEAD_DIM = 128
N_MLSTM_HEADS = 8
CONV_WIDTH = 4
N_BUCKETS = 32
MAX_DISTANCE = 128
N_GROUPS = 4
EXPERTS_PER_GROUP = 8
N_EXPERTS = N_GROUPS * EXPERTS_PER_GROUP
TOP_K_INNER = 2
LANES = 128
NEG_BIG = -1e30

VMEM_LIMIT = 56 * 1024 * 1024

PROJ_TM, PROJ_TN = 1024, 1024
ATT_TQ = 512
MLSTM_TS = 1024
OUT_TM = 512
MOE_TM = 256
MOE_WBUF = 3
FIN_TM = 512


def _cparams(sem):
    return pltpu.CompilerParams(dimension_semantics=sem, vmem_limit_bytes=VMEM_LIMIT)


_HI_MASK = 0xFFFF0000


def _pack_bf16_pairs(x):
    half = x.shape[-1] // 2
    xb = x.astype(BF16).astype(F32)
    lo = pltpu.bitcast(xb[:, :half], jnp.uint32)
    hi = pltpu.bitcast(xb[:, half:], jnp.uint32)
    return (hi & jnp.uint32(_HI_MASK)) | (lo >> 16)


def _unpack_bf16_pairs(w):
    lo = pltpu.bitcast(w << 16, F32)
    hi = pltpu.bitcast(w & jnp.uint32(_HI_MASK), F32)
    return lo, hi


def _proj_kernel(x_ref, g_ref, w_ref, cs_ref, wg_ref, o_ref, og_ref, h_ref):
    @pl.when(pl.program_id(1) == 0)
    def _():
        x = x_ref[...]
        ms = jnp.mean(x * x, axis=-1, keepdims=True)
        h = (x * lax.rsqrt(ms + EPS) * g_ref[...]).astype(BF16)
        h_ref[...] = h
        og_ref[...] = jnp.dot(h, wg_ref[...], preferred_element_type=F32)

    o_ref[...] = (jnp.dot(h_ref[...], w_ref[...], preferred_element_type=F32) * cs_ref[...]).astype(o_ref.dtype)


def _proj(x2, g, w_main, col_scale, w_gates):
    T, D = x2.shape
    N = w_main.shape[1]
    NG = w_gates.shape[1]
    return pl.pallas_call(
        _proj_kernel,
        grid=(T // PROJ_TM, N // PROJ_TN),
        in_specs=[
            pl.BlockSpec((PROJ_TM, D), lambda m, n: (m, 0)),
            pl.BlockSpec((1, D), lambda m, n: (0, 0)),
            pl.BlockSpec((D, PROJ_TN), lambda m, n: (0, n)),
            pl.BlockSpec((1, PROJ_TN), lambda m, n: (0, n)),
            pl.BlockSpec((D, NG), lambda m, n: (0, 0)),
        ],
        out_specs=[
            pl.BlockSpec((PROJ_TM, PROJ_TN), lambda m, n: (m, n)),
            pl.BlockSpec((PROJ_TM, NG), lambda m, n: (m, 0)),
        ],
        out_shape=[
            jax.ShapeDtypeStruct((T, N), BF16),
            jax.ShapeDtypeStruct((T, NG), F32),
        ],
        scratch_shapes=[pltpu.VMEM((PROJ_TM, D), BF16)],
        compiler_params=_cparams(("parallel", "arbitrary")),
        name="rms_in_proj",
    )(x2, g, w_main, col_scale, w_gates)


def _t5_bucket(rel):
    half = N_BUCKETS // 2
    max_exact = half // 2
    ret = jnp.where(rel > 0, half, 0)
    n = jnp.abs(rel)
    nf = jnp.maximum(n, 1).astype(F32)
    large = max_exact + (jnp.log(nf / max_exact) / math.log(MAX_DISTANCE / max_exact)
                         * (half - max_exact)).astype(jnp.int32)
    large = jnp.minimum(large, half - 1)
    return ret + jnp.where(n < max_exact, n, large)


def _attn_kernel(scal_ref, q_ref, k_ref, v_ref, bias_ref, g_ref, o_ref, m_ref, l_ref, acc_ref,
                 s0_ref, s1_ref, s2_ref, ml0_ref, ml1_ref, ml2_ref, bt_ref, *, lambda_init):
    h = pl.program_id(1)
    qi = pl.program_id(2)
    nq = pl.num_programs(2)
    tq = ATT_TQ
    lam = scal_ref[0]
    cfar = scal_ref[1 + h]

    def stacked_queries(tile):
        q = q_ref[0, pl.ds(pl.multiple_of(tile * tq, tq), tq), :]
        lane = lax.broadcasted_iota(jnp.int32, q.shape, 1)
        zero = jnp.zeros_like(q)
        return jnp.concatenate([jnp.where(lane < DIFF_HEAD_DIM, q, zero),
                                jnp.where(lane >= DIFF_HEAD_DIM, q, zero)], axis=0)

    qs = stacked_queries(qi)
    nxt = jnp.minimum(qi + 1, nq - 1)
    qs_next = stacked_queries(nxt)

    m_ref[...] = jnp.full(m_ref.shape, NEG_BIG, F32)
    l_ref[...] = jnp.zeros(l_ref.shape, F32)
    acc_ref[...] = jnp.zeros(acc_ref.shape, F32)

    @pl.when(qi == 0)
    def _():
        kj = lax.broadcasted_iota(jnp.int32, (tq, tq), 0)
        qq = lax.broadcasted_iota(jnp.int32, (tq, tq), 1)
        allowed = (kj // CHUNK) <= (qq // CHUNK)
        for d in range(2):
            rows = jnp.broadcast_to(bias_ref[0, d], (tq, 2 * tq))
            tile = pltpu.roll(rows, tq + 1, 1, stride=1, stride_axis=0)[:, :tq]
            if d == 1:
                tile = jnp.where(allowed, tile, NEG_BIG)
            bt_ref[d] = tile

    bufs = ((s0_ref, ml0_ref), (s1_ref, ml1_ref), (s2_ref, ml2_ref))

    def score(ki, bias, slot, queries=None):
        s_ref, ml_ref = bufs[slot]
        start = pl.multiple_of(ki * tq, tq)
        kt = k_ref[0, pl.ds(start, tq), :]
        s = lax.dot_general(kt, qs if queries is None else queries, (((1,), (1,)), ((), ())),
                            preferred_element_type=F32)
        if bias is not None:
            s = s + jnp.concatenate([bias, bias], axis=1)
        s_ref[...] = s
        ml_ref[...] = jnp.max(s, axis=0, keepdims=True)

    def accumulate(ki, shift, slot):
        s_ref, ml_ref = bufs[slot]
        start = pl.multiple_of(ki * tq, tq)
        vt = v_ref[0, pl.ds(start, tq), :]
        m_old = m_ref[...]
        m_new = jnp.maximum(m_old, ml_ref[...] + shift)
        alpha = jnp.exp2(m_old - m_new)
        p = jnp.exp2(s_ref[...] - (m_new - shift))
        l_ref[...] = alpha * l_ref[...] + jnp.sum(p, axis=0, keepdims=True)
        pv = lax.dot_general(vt, p.astype(BF16), (((0,), (0,)), ((), ())), preferred_element_type=F32)
        acc_ref[...] = alpha * acc_ref[...] + pv
        m_ref[...] = m_new

    n_far = qi - 1

    def score_next_diagonal():
        score(nxt, bt_ref[1], 2, qs_next)

    @pl.when(qi == 0)
    def _():
        score(qi, bt_ref[1], 2)
        accumulate(qi, 0.0, 2)
        score_next_diagonal()

    @pl.when(qi >= 1)
    def _():
        accumulate(qi, 0.0, 2)
        score(qi - 1, bt_ref[0], 1)

    @pl.when(qi == 1)
    def _():
        accumulate(qi - 1, 0.0, 1)
        score_next_diagonal()

    @pl.when(qi >= 2)
    def _():
        accumulate(qi - 1, 0.0, 1)
        score(0, None, 0)
        trips = (n_far - 1) // 2

        def pair(j, c):
            accumulate(2 * j, cfar, 0)
            score(2 * j + 1, None, 1)
            accumulate(2 * j + 1, cfar, 1)
            score(2 * j + 2, None, 0)
            return c

        lax.fori_loop(0, trips, pair, 0)
        last = 2 * trips

        @pl.when(n_far - last == 2)
        def _():
            accumulate(last, cfar, 0)
            score(last + 1, None, 1)
            accumulate(last + 1, cfar, 1)
            score_next_diagonal()

        @pl.when(n_far - last == 1)
        def _():
            accumulate(last, cfar, 0)
            score_next_diagonal()

    acc = acc_ref[...] * (1.0 / l_ref[...])
    o_t = acc[:, 0:tq] - lam * acc[:, tq:2 * tq]
    ms = jnp.mean(o_t * o_t, axis=0, keepdims=True)
    y = (o_t * lax.rsqrt(ms + EPS)).T * (g_ref[...] * (1.0 - lambda_init))
    o_ref[0] = y.astype(o_ref.dtype)


def _diff_attention(proj3, scal, bias_vecs, gnorm, lambda_init):
    B, S, _ = proj3.shape
    H = N_DIFF_HEADS
    tq = ATT_TQ
    kern = functools.partial(_attn_kernel, lambda_init=lambda_init)
    return pl.pallas_call(
        kern,
        grid=(B, H, S // tq),
        in_specs=[
            pl.BlockSpec(memory_space=pltpu.SMEM),
            pl.BlockSpec((1, S, LANES), lambda b, h, i: (b, 0, h)),
            pl.BlockSpec((1, S, LANES), lambda b, h, i: (b, 0, H + h)),
            pl.BlockSpec((1, S, LANES), lambda b, h, i: (b, 0, 2 * H + h)),
            pl.BlockSpec((1, 2, 1, 2 * tq), lambda b, h, i: (h, 0, 0, 0)),
            pl.BlockSpec((1, LANES), lambda b, h, i: (0, h)),
        ],
        out_specs=pl.BlockSpec((1, tq, LANES), lambda b, h, i: (b, i, h)),
        out_shape=jax.ShapeDtypeStruct((B, S, H * LANES), BF16),
        scratch_shapes=[
            pltpu.VMEM((1, 2 * tq), F32),
            pltpu.VMEM((1, 2 * tq), F32),
            pltpu.VMEM((LANES, 2 * tq), F32),
            pltpu.VMEM((tq, 2 * tq), F32),
            pltpu.VMEM((tq, 2 * tq), F32),
            pltpu.VMEM((tq, 2 * tq), F32),
            pltpu.VMEM((1, 2 * tq), F32),
            pltpu.VMEM((1, 2 * tq), F32),
            pltpu.VMEM((1, 2 * tq), F32),
            pltpu.VMEM((2, tq, tq), F32),
        ],
        compiler_params=_cparams(("parallel", "parallel", "arbitrary")),
        name="diff_attention",
    )(scal, proj3, proj3, proj3, bias_vecs, gnorm)


def _log_sigmoid(x):
    return jnp.minimum(x, 0.0) - jnp.log(1.0 + jnp.exp(-jnp.abs(x)))


def _sigmoid(x):
    return 1.0 / (1.0 + jnp.exp(-x))


def _mlstm_kernel(q_ref, k_ref, v_ref, o_ref, gi_ref, gf_ref, cw_ref, cb_ref, bi_ref, bf_ref, gn_ref,
                  out_ref, qext_ref, kext_ref, ct_ref, n_ref, m_ref):
    sb = pl.program_id(1)
    L = CHUNK
    dh = MLSTM_HEAD_DIM
    H = N_MLSTM_HEADS
    ts = MLSTM_TS
    pad = 8

    @pl.when(sb == 0)
    def _():
        qext_ref[0:pad, :] = jnp.zeros((pad, H * dh), F32)
        kext_ref[0:pad, :] = jnp.zeros((pad, H * dh), F32)
        ct_ref[...] = jnp.zeros(ct_ref.shape, F32)
        n_ref[...] = jnp.zeros(n_ref.shape, F32)
        m_ref[...] = jnp.zeros(m_ref.shape, F32)

    qext_ref[pad:pad + ts, :] = q_ref[0].astype(F32)
    kext_ref[pad:pad + ts, :] = k_ref[0].astype(F32)

    row = lax.broadcasted_iota(jnp.int32, (L, L), 0)
    col = lax.broadcasted_iota(jnp.int32, (L, L), 1)
    tril = col <= row
    ltri = tril.astype(F32)

    def conv_silu(ext_ref, base, h, off):
        win = ext_ref[pl.ds(base, L + pad), h * dh:(h + 1) * dh]
        w = cw_ref[:, off + h * dh:off + (h + 1) * dh]
        y = cb_ref[:, off + h * dh:off + (h + 1) * dh]
        for j in range(CONV_WIDTH):
            lo = pad - (CONV_WIDTH - 1) + j
            y = y + w[j:j + 1, :] * win[lo:lo + L, :]
        return y * _sigmoid(y)

    def chunk_body(c, carry):
        base = pl.multiple_of(c * L, L)
        li = gi_ref[0, pl.ds(base, L), :] + bi_ref[...]
        logf = _log_sigmoid(gf_ref[0, pl.ds(base, L), :] + bf_ref[...])
        b = jnp.dot(ltri, logf, preferred_element_type=F32, precision=lax.Precision.HIGHEST)
        a = li - b
        g_row = b[L - 1:L, :]
        m_row = m_ref[...]
        m_new_row = g_row + jnp.maximum(m_row, jnp.max(a, axis=0, keepdims=True))
        a_t = a.T

        for h in range(H):
            qc = conv_silu(qext_ref, base, h, 0)
            kc = conv_silu(kext_ref, base, h, H * dh) * (dh ** -0.5)
            qb = qc.astype(BF16)
            kb = kc.astype(BF16)
            vb = v_ref[0, pl.ds(base, L), h * dh:(h + 1) * dh]

            a_row = a_t[h:h + 1, :]
            a_col = a[:, h:h + 1]
            b_col = b[:, h:h + 1]
            m_prev = m_row[:, h:h + 1]
            m_next = m_new_row[:, h:h + 1]
            g_h = g_row[:, h:h + 1]

            amat = jnp.where(tril, a_row, NEG_BIG)
            mcol = jnp.maximum(jnp.max(amat, axis=-1, keepdims=True), m_prev)
            wts = jnp.exp(amat - mcol)
            inter = jnp.exp(m_prev - mcol)

            s = lax.dot_general(qb, kb, (((1,), (1,)), ((), ())), preferred_element_type=F32)
            sqk = s * wts
            ct = ct_ref[h]
            nrow = n_ref[h:h + 1, :]
            num = (jnp.dot(sqk.astype(BF16), vb, preferred_element_type=F32)
                   + inter * jnp.dot(qb, ct.astype(BF16), preferred_element_type=F32))
            den = (jnp.sum(sqk, axis=-1, keepdims=True)
                   + inter * jnp.sum(qb.astype(F32) * nrow, axis=-1, keepdims=True))
            hv = num / jnp.maximum(jnp.abs(den), jnp.exp(-(b_col + mcol)))

            wt = jnp.exp(g_h + a_col - m_next)
            decay = jnp.exp(g_h + m_prev - m_next)
            wv = (wt * vb.astype(F32)).astype(BF16)
            ct_ref[h] = decay * ct + lax.dot_general(kb, wv, (((0,), (0,)), ((), ())),
                                                     preferred_element_type=F32)
            n_ref[h:h + 1, :] = decay * nrow + jnp.sum(wt * kb.astype(F32), axis=0, keepdims=True)

            ms = jnp.mean(hv * hv, axis=-1, keepdims=True)
            y = hv * lax.rsqrt(ms + EPS) * gn_ref[:, h * dh:(h + 1) * dh]
            og = o_ref[0, pl.ds(base, L), h * dh:(h + 1) * dh].astype(F32)
            out_ref[0, pl.ds(base, L), h * dh:(h + 1) * dh] = (y * _sigmoid(og)).astype(out_ref.dtype)

        m_ref[...] = m_new_row
        return carry

    lax.fori_loop(0, ts // L, chunk_body, 0, unroll=4)

    qext_ref[0:pad, :] = qext_ref[ts:ts + pad, :]
    kext_ref[0:pad, :] = kext_ref[ts:ts + pad, :]


def _mlstm(proj3, gates3, conv_w, conv_b, bi_row, bf_row, gnorm):
    B, S, _ = proj3.shape
    W = N_MLSTM_HEADS * MLSTM_HEAD_DIM
    ts = MLSTM_TS
    first = 3
    blk = lambda j: pl.BlockSpec((1, ts, W), lambda b, s: (b, s, j))
    full = lambda shape: pl.BlockSpec(shape, lambda b, s: (0,) * len(shape))
    return pl.pallas_call(
        _mlstm_kernel,
        grid=(B, S // ts),
        in_specs=[
            blk(first), blk(first + 1), blk(first + 2), blk(first + 3),
            pl.BlockSpec((1, ts, LANES), lambda b, s: (b, s, 0)),
            pl.BlockSpec((1, ts, LANES), lambda b, s: (b, s, 1)),
            full((CONV_WIDTH, 2 * W)), full((1, 2 * W)),
            full((1, LANES)), full((1, LANES)), full((1, W)),
        ],
        out_specs=pl.BlockSpec((1, ts, W), lambda b, s: (b, s, 0)),
        out_shape=jax.ShapeDtypeStruct((B, S, W), BF16),
        scratch_shapes=[
            pltpu.VMEM((ts + 8, W), F32),
            pltpu.VMEM((ts + 8, W), F32),
            pltpu.VMEM((N_MLSTM_HEADS, MLSTM_HEAD_DIM, MLSTM_HEAD_DIM), F32),
            pltpu.VMEM((N_MLSTM_HEADS, MLSTM_HEAD_DIM), F32),
            pltpu.VMEM((1, LANES), F32),
        ],
        compiler_params=_cparams(("parallel", "arbitrary")),
        name="mlstm",
    )(proj3, proj3, proj3, proj3, gates3, gates3, conv_w, conv_b, bi_row, bf_row, gnorm)


def _out_kernel(x_ref, a_ref, hm_ref, wa_ref, wm_ref, g_ref, wr_ref, x1_ref, h2_ref, lg_ref):
    y = (jnp.dot(a_ref[...], wa_ref[...].astype(BF16), preferred_element_type=F32)
         + jnp.dot(hm_ref[...], wm_ref[...].astype(BF16), preferred_element_type=F32))
    x1 = x_ref[...] + y
    x1_ref[...] = x1
    ms = jnp.mean(x1 * x1, axis=-1, keepdims=True)
    h2 = x1 * lax.rsqrt(ms + EPS) * g_ref[...]
    h2_ref[...] = _pack_bf16_pairs(h2)
    lg_ref[...] = jnp.dot(h2.astype(BF16), wr_ref[...], preferred_element_type=F32)


def _out_proj(x2, a2, hm2, wa, wm, g, wr):
    T, D = x2.shape
    W = a2.shape[1]
    tm = OUT_TM
    const = lambda shape: pl.BlockSpec(shape, lambda m: (0, 0), pipeline_mode=pl.Buffered(1))
    return pl.pallas_call(
        _out_kernel,
        grid=(T // tm,),
        in_specs=[
            pl.BlockSpec((tm, D), lambda m: (m, 0)),
            pl.BlockSpec((tm, W), lambda m: (m, 0)),
            pl.BlockSpec((tm, W), lambda m: (m, 0)),
            const((W, D)),
            pl.BlockSpec((W, D), lambda m: (1, 0), pipeline_mode=pl.Buffered(1)),
            const((1, D)), const((D, LANES)),
        ],
        out_specs=[
            pl.BlockSpec((tm, D), lambda m: (m, 0)),
            pl.BlockSpec((tm, D // 2), lambda m: (m, 0)),
            pl.BlockSpec((tm, LANES), lambda m: (m, 0)),
        ],
        out_shape=[
            jax.ShapeDtypeStruct((T, D), F32),
            jax.ShapeDtypeStruct((T, D // 2), jnp.uint32),
            jax.ShapeDtypeStruct((T, LANES), F32),
        ],
        compiler_params=_cparams(("parallel",)),
        name="out_proj_router",
    )(x2, a2, hm2, wa, wm, g, wr)


def _moe_kernel(te_ref, nv_ref, first_ref, slot_ref, nxt_ref, nxt2_ref, rows_ref, xs_ref, wg_hbm, wu_hbm,
                wd_hbm, ys_ref, wg_buf, wu_buf, wd_buf, sem):
    j = pl.program_id(0)
    valid = j < nv_ref[0]
    nbuf = wg_buf.shape[0]

    half_f = wd_buf.shape[1] // 2

    def weight_copies(e, s):
        lo, hi = pl.ds(0, half_f), pl.ds(half_f, half_f)
        return ((pltpu.make_async_copy(wg_hbm.at[e], wg_buf.at[s], sem.at[s, 0]), 0),
                (pltpu.make_async_copy(wu_hbm.at[e], wu_buf.at[s], sem.at[s, 1]), 1),
                (pltpu.make_async_copy(wd_hbm.at[e, lo], wd_buf.at[s, lo], sem.at[s, 2]), 0),
                (pltpu.make_async_copy(wd_hbm.at[e, hi], wd_buf.at[s, hi], sem.at[s, 3]), 1))

    @pl.when(j == 0)
    def _():
        for c, prio in weight_copies(te_ref[0], 0):
            c.start(priority=prio)

        @pl.when(nxt_ref[0] >= 0)
        def _():
            for c, prio in weight_copies(nxt_ref[0], 1):
                c.start(priority=prio)

    @pl.when(jnp.logical_and(valid, first_ref[j] == 1))
    def _():
        for c, _ in weight_copies(te_ref[j], slot_ref[j]):
            c.wait()

        @pl.when(nxt2_ref[j] >= 0)
        def _():
            for c, prio in weight_copies(nxt2_ref[j], (slot_ref[j] + 2) % nbuf):
                c.start(priority=prio)

    @pl.when(valid)
    def _():
        s = slot_ref[j]
        row = lax.broadcasted_iota(jnp.int32, xs_ref.shape, 0)
        lo, hi = _unpack_bf16_pairs(jnp.where(row < rows_ref[j], xs_ref[...], jnp.uint32(0)))
        xs = jnp.concatenate([lo.astype(BF16), hi.astype(BF16)], axis=1)
        gt = jnp.dot(xs, wg_buf[s].astype(BF16), preferred_element_type=F32)
        up = jnp.dot(xs, wu_buf[s].astype(BF16), preferred_element_type=F32)
        hid = (gt * _sigmoid(gt) * up).astype(BF16)
        ys_ref[...] = _pack_bf16_pairs(jnp.dot(hid, wd_buf[s].astype(BF16), preferred_element_type=F32))

    @pl.when(jnp.logical_not(valid))
    def _():
        ys_ref[...] = jnp.zeros(ys_ref.shape, ys_ref.dtype)


def _moe(tile_expert, n_valid, tile_first, tile_slot, tile_next, tile_next2, tile_rows, xs, wg, wu, wd):
    R, Dw = xs.shape
    D, F = wg.shape[1], wg.shape[2]
    tm = MOE_TM
    hbm = pl.BlockSpec(memory_space=pl.ANY)
    grid_spec = pltpu.PrefetchScalarGridSpec(
        num_scalar_prefetch=7,
        grid=(R // tm,),
        in_specs=[pl.BlockSpec((tm, Dw), lambda j, *_: (j, 0)), hbm, hbm, hbm],
        out_specs=pl.BlockSpec((tm, Dw), lambda j, *_: (j, 0)),
        scratch_shapes=[
            pltpu.VMEM((MOE_WBUF, D, F), wg.dtype),
            pltpu.VMEM((MOE_WBUF, D, F), wu.dtype),
            pltpu.VMEM((MOE_WBUF, F, D), wd.dtype),
            pltpu.SemaphoreType.DMA((MOE_WBUF, 4)),
        ],
    )
    return pl.pallas_call(
        _moe_kernel,
        grid_spec=grid_spec,
        out_shape=jax.ShapeDtypeStruct((R, Dw), jnp.uint32),
        compiler_params=_cparams(("arbitrary",)),
        name="moe_experts",
    )(tile_expert, n_valid, tile_first, tile_slot, tile_next, tile_next2, tile_rows, xs, wg, wu, wd)


def _final_kernel(x1_ref, y0_ref, y1_ref, cw_ref, g_ref, o_ref):
    cw = cw_ref[...]
    lo0, hi0 = _unpack_bf16_pairs(y0_ref[...])
    lo1, hi1 = _unpack_bf16_pairs(y1_ref[...])
    w0, w1 = cw[:, 0:1], cw[:, 1:2]
    y = jnp.concatenate([w0 * lo0 + w1 * lo1, w0 * hi0 + w1 * hi1], axis=1)
    x = x1_ref[...] + y
    ms = jnp.mean(x * x, axis=-1, keepdims=True)
    o_ref[...] = x * lax.rsqrt(ms + EPS) * g_ref[...]


def _final(x1, yw, cw, g):
    T, D = x1.shape
    tm = FIN_TM
    row = lambda w: pl.BlockSpec((tm, w), lambda m: (m, 0))
    slot1 = pl.BlockSpec((tm, D // 2), lambda m: (m + T // tm, 0))
    return pl.pallas_call(
        _final_kernel,
        grid=(T // tm,),
        in_specs=[row(D), row(D // 2), slot1, row(LANES), pl.BlockSpec((1, D), lambda m: (0, 0))],
        out_specs=row(D),
        out_shape=jax.ShapeDtypeStruct((T, D), F32),
        compiler_params=_cparams(("parallel",)),
        name="combine_final_norm",
    )(x1, yw, yw, cw, g)


SC_CORES, SC_SUBCORES = 2, 16
SC_CHUNK = 32


def _sc_gather_rows(table, idx):
    V, Dw = table.shape
    R = idx.shape[0]
    n_workers = SC_CORES * SC_SUBCORES
    ch = SC_CHUNK
    per_w = R // n_workers
    n_chunks = per_w // ch
    assert per_w * n_workers == R and n_chunks * ch == per_w and n_chunks % 2 == 0
    idx3 = idx.reshape(n_workers, n_chunks, ch)
    mesh = plsc.VectorSubcoreMesh(core_axis_name="c", subcore_axis_name="s")

    def body(table_hbm, idx_hbm, out_hbm, idx_v, rows_v, gsem, osem):
        wid = lax.axis_index("s") * SC_CORES + lax.axis_index("c")
        base = wid * per_w
        pltpu.sync_copy(idx_hbm.at[wid], idx_v)

        def gather(c, slot):
            return pltpu.make_async_copy(table_hbm.at[idx_v.at[c]], rows_v.at[slot], gsem.at[slot])

        def put(c, slot):
            return pltpu.make_async_copy(rows_v.at[slot], out_hbm.at[pl.ds(base + c * ch, ch)],
                                         osem.at[slot])

        gather(0, 0).start()

        @pl.loop(0, n_chunks, step=2)
        def _(c):
            @pl.when(c > 0)
            def _():
                put(c - 1, 1).wait()

            gather(c + 1, 1).start()
            gather(c, 0).wait()
            put(c, 0).start()
            put(c, 0).wait()

            @pl.when(c + 2 < n_chunks)
            def _():
                gather(c + 2, 0).start()

            gather(c + 1, 1).wait()
            put(c + 1, 1).start()

        put(n_chunks - 1, 1).wait()

    return pl.kernel(
        body,
        out_type=jax.ShapeDtypeStruct((R, Dw), table.dtype),
        mesh=mesh,
        scratch_types=[
            pltpu.VMEM((n_chunks, ch), jnp.int32),
            pltpu.VMEM((2, ch, Dw), table.dtype),
            pltpu.SemaphoreType.DMA((2,)),
            pltpu.SemaphoreType.DMA((2,)),
        ],
        name="sc_gather_rows",
    )(table, idx3)


def _sc_scatter_rows(table, idx, n_rows_out):
    V, Dw = table.shape
    K = idx.shape[0]
    n_workers = SC_CORES * SC_SUBCORES
    ch = SC_CHUNK
    per_w = V // n_workers
    n_chunks = per_w // ch
    assert K == 2 and per_w * n_workers == V and n_chunks * ch == per_w and n_chunks % 2 == 0
    idx4 = jnp.transpose(idx.reshape(K, n_workers, n_chunks, ch), (1, 0, 2, 3))
    mesh = plsc.VectorSubcoreMesh(core_axis_name="c", subcore_axis_name="s")

    def body(table_hbm, idx_hbm, out_hbm, idx_v, rows_v, lsem, ssem):
        wid = lax.axis_index("s") * SC_CORES + lax.axis_index("c")
        base = wid * per_w
        pltpu.sync_copy(idx_hbm.at[wid], idx_v)

        def load(c, slot):
            return pltpu.make_async_copy(table_hbm.at[pl.ds(base + c * ch, ch)], rows_v.at[slot],
                                         lsem.at[slot])

        def scatter(c, slot, k):
            return pltpu.make_async_copy(rows_v.at[slot], out_hbm.at[idx_v.at[k, c]], ssem.at[slot, k])

        load(0, 0).start()

        @pl.loop(0, n_chunks, step=2)
        def _(c):
            @pl.when(c > 0)
            def _():
                scatter(c - 1, 1, 0).wait()
                scatter(c - 1, 1, 1).wait()

            load(c + 1, 1).start()
            load(c, 0).wait()
            scatter(c, 0, 0).start()
            scatter(c, 0, 1).start()
            scatter(c, 0, 0).wait()
            scatter(c, 0, 1).wait()

            @pl.when(c + 2 < n_chunks)
            def _():
                load(c + 2, 0).start()

            load(c + 1, 1).wait()
            scatter(c + 1, 1, 0).start()
            scatter(c + 1, 1, 1).start()

        scatter(n_chunks - 1, 1, 0).wait()
        scatter(n_chunks - 1, 1, 1).wait()

    return pl.kernel(
        body,
        out_type=jax.ShapeDtypeStruct((n_rows_out, Dw), table.dtype),
        mesh=mesh,
        scratch_types=[
            pltpu.VMEM((K, n_chunks, ch), jnp.int32),
            pltpu.VMEM((2, ch, Dw), table.dtype),
            pltpu.SemaphoreType.DMA((2,)),
            pltpu.SemaphoreType.DMA((2, K)),
        ],
        name="sc_scatter_rows",
    )(table, idx4)


ROUTE_BLK = 256


def _route_kernel(lg_ref, bias_ref, pos_ref, cw_ref, cnt_ref, oh_ref, pre_ref, *, tm):
    T = lg_ref.shape[0]
    G, E, NE = N_GROUPS, EXPERTS_PER_GROUP, N_EXPERTS
    blk = ROUTE_BLK
    sub = lax.broadcasted_iota(jnp.int32, (LANES, blk), 0).astype(F32)
    r_i = lax.broadcasted_iota(jnp.int32, (blk, blk), 0)
    c_i = lax.broadcasted_iota(jnp.int32, (blk, blk), 1)
    before = (r_i < c_i).astype(BF16)

    def first_argmax(v):
        mx = jnp.max(v, axis=0, keepdims=True)
        return mx, jnp.min(jnp.where(v == mx, sub, float(LANES)), axis=0, keepdims=True)

    def phase1(b, carry):
        cols = pl.ds(pl.multiple_of(b * blk, blk), blk)
        x = (lg_ref[cols, :] + bias_ref[...]).T
        gl = jnp.where(sub < G, x, NEG_BIG)
        gmax, gsel = first_argmax(gl)
        gw = 1.0 / jnp.sum(jnp.exp(gl - gmax), axis=0, keepdims=True)
        lo = G + E * gsel
        el = jnp.where(jnp.logical_and(sub >= lo, sub < lo + E), x, NEG_BIG)
        v1, i1 = first_argmax(el)
        v2, i2 = first_argmax(jnp.where(sub == i1, NEG_BIG, el))
        e21 = jnp.exp(v2 - v1)
        w1 = 1.0 / (1.0 + e21)
        cw_t = jnp.where(sub == 0, gw * w1, jnp.where(sub == 1, gw * (e21 * w1), 0.0))
        cw_ref[cols, :] = cw_t.T
        oh = jnp.where(jnp.logical_or(sub == i1 - G, sub == i2 - G + NE), 1.0, 0.0)
        oh_ref[:, cols] = oh
        pre_ref[:, cols] = jnp.dot(oh.astype(BF16), before, preferred_element_type=F32) + carry
        return carry + jnp.sum(oh, axis=1, keepdims=True)

    counts = lax.fori_loop(0, T // blk, phase1, jnp.zeros((LANES, 1), F32))

    c0 = counts[0:NE, :]
    ctot = c0 + counts[NE:2 * NE, :]
    tiles = jnp.floor((ctot + (tm - 1)) * (1.0 / tm))
    e_r = lax.broadcasted_iota(jnp.int32, (NE, NE), 0)
    e_c = lax.broadcasted_iota(jnp.int32, (NE, NE), 1)
    upto = (e_c <= e_r).astype(BF16)
    tile_end = jnp.dot(upto, jnp.broadcast_to(tiles, (NE, LANES)).astype(BF16),
                       preferred_element_type=F32)[:, 0:1]
    row_start = (tile_end - tiles) * tm
    base = jnp.concatenate([row_start, row_start + c0, jnp.zeros((LANES - 2 * NE, 1), F32)], axis=0)
    lane_e = lax.broadcasted_iota(jnp.int32, (NE, LANES), 1)
    cnt_ref[...] = jnp.concatenate(
        [jnp.where(lane_e == 0, ctot, jnp.where(lane_e == 1, tiles, jnp.where(lane_e == 2, tile_end, 0.0))),
         jnp.zeros((LANES - NE, LANES), F32)], axis=0)

    def phase2(b, carry):
        cols = pl.ds(pl.multiple_of(b * blk, blk), blk)
        t = oh_ref[:, cols] * (pre_ref[:, cols] + base)
        p0 = jnp.sum(t[0:NE, :], axis=0, keepdims=True)
        p1 = jnp.sum(t[NE:2 * NE, :], axis=0, keepdims=True)
        pos_ref[:, cols] = jnp.concatenate([p0, p1, jnp.zeros((6, blk), F32)], axis=0).astype(jnp.int32)
        return carry

    lax.fori_loop(0, T // blk, phase2, 0)


def _route(logits, bias_row, tm):
    T = logits.shape[0]
    whole = lambda shape: pl.BlockSpec(shape, lambda: (0,) * len(shape))
    return pl.pallas_call(
        functools.partial(_route_kernel, tm=tm),
        in_specs=[whole((T, LANES)), whole((1, LANES))],
        out_specs=[whole((8, T)), whole((T, LANES)), whole((LANES, LANES))],
        out_shape=[
            jax.ShapeDtypeStruct((8, T), jnp.int32),
            jax.ShapeDtypeStruct((T, LANES), F32),
            jax.ShapeDtypeStruct((LANES, LANES), F32),
        ],
        scratch_shapes=[pltpu.VMEM((LANES, T), F32), pltpu.VMEM((LANES, T), F32)],
        compiler_params=pltpu.CompilerParams(vmem_limit_bytes=VMEM_LIMIT),
        name="route_positions",
    )(logits, bias_row)


def _tile_plan(cnt, tm, n_tiles):
    counts = cnt[:N_EXPERTS, 0].astype(jnp.int32)
    tiles_per_e = cnt[:N_EXPERTS, 1].astype(jnp.int32)
    tile_end = cnt[:N_EXPERTS, 2].astype(jnp.int32)
    n_valid = tile_end[-1]
    tile_ids = jnp.arange(n_tiles, dtype=jnp.int32)
    experts = jnp.arange(N_EXPERTS, dtype=jnp.int32)
    valid = tile_ids < n_valid
    jc = jnp.minimum(tile_ids, n_valid - 1)
    tile_expert = jnp.minimum(jnp.sum((tile_end[None, :] <= jc[:, None]).astype(jnp.int32), axis=1),
                              N_EXPERTS - 1)
    of_tile = (tile_expert[:, None] == experts[None, :]).astype(jnp.int32)
    pick = lambda per_expert: jnp.sum(of_tile * per_expert[None, :], axis=1)
    tile_start = tile_end - tiles_per_e
    nonempty = tiles_per_e > 0
    group_idx = jnp.sum(jnp.logical_and(nonempty[None, :], experts[None, :] <= experts[:, None]).astype(jnp.int32),
                        axis=1) - 1
    later = jnp.logical_and(nonempty[None, :], experts[None, :] > experts[:, None])
    next_e = jnp.min(jnp.where(later, experts[None, :], N_EXPERTS), axis=1)
    next_e = jnp.where(next_e == N_EXPERTS, -1, next_e)
    is_next = (next_e[:, None] == experts[None, :]).astype(jnp.int32)
    next2_e = jnp.where(next_e >= 0, jnp.sum(is_next * next_e[None, :], axis=1), -1)
    tile_first = jnp.logical_and(valid, tile_ids == pick(tile_start)).astype(jnp.int32)
    tile_slot = (pick(group_idx) % MOE_WBUF).astype(jnp.int32)
    tile_next = pick(next_e).astype(jnp.int32)
    tile_next2 = pick(next2_e).astype(jnp.int32)
    tile_rows = jnp.clip(pick(counts) - (tile_ids - pick(tile_start)) * tm, 0, tm)
    tile_rows = jnp.where(valid, tile_rows, 0).astype(jnp.int32)
    return (tile_expert.astype(jnp.int32), n_valid.reshape(1), tile_first, tile_slot, tile_next, tile_next2,
            tile_rows)


def kernel(x, rel_bias, ln_mix_g, w_in, conv_w, conv_b, b_i, b_f, lam_q1, lam_k1, lam_q2, lam_k2,
           diff_norm_g, mlstm_norm_g, w_out, ln_ffn_g, w_group, b_group, w_router, b_router,
           w_gate, w_up, w_down, ln_f_g):
    B, S, D = x.shape
    T = B * S
    depth = w_in.shape[0]
    assert depth == 1, "the final rmsnorm is fused into the single layer's combine kernel"
    Hm = N_MLSTM_HEADS
    n_main = w_in.shape[2] - 2 * Hm
    n_diff = N_DIFF_HEADS * 2 * DIFF_HEAD_DIM
    xf = x.reshape(T, D)

    for l in range(depth):
        lambda_init = 0.8 - 0.6 * math.exp(-0.3 * l)
        w_main = w_in[l, :, :n_main].astype(BF16)
        wgt = w_in[l, :, n_main:]
        w_gates = jnp.zeros((D, 2 * LANES), F32).at[:, :Hm].set(wgt[:, :Hm]).at[:, LANES:LANES + Hm].set(
            wgt[:, Hm:]).astype(BF16)
        bi_row = jnp.zeros((1, LANES), F32).at[0, :Hm].set(b_i[l].astype(F32))
        bf_row = jnp.zeros((1, LANES), F32).at[0, :Hm].set(b_f[l].astype(F32))
        lam = (jnp.exp(jnp.sum(lam_q1[l].astype(F32) * lam_k1[l].astype(F32)))
               - jnp.exp(jnp.sum(lam_q2[l].astype(F32) * lam_k2[l].astype(F32))) + lambda_init)
        tq = ATT_TQ
        assert tq >= MAX_DISTANCE and tq % CHUNK == 0
        rb = rel_bias.astype(F32)
        log2e = math.log2(math.e)
        xx = jnp.arange(2 * tq, dtype=jnp.int32)
        rel_vec = jnp.stack([-tq + tq - 1 - xx, tq - 1 - xx], axis=0)
        bias_vecs = jnp.take(rb, _t5_bucket(rel_vec), axis=0) * log2e
        bias_vecs = jnp.transpose(bias_vecs, (2, 0, 1))[:, :, None, :]
        cfar = rb[N_BUCKETS // 2 - 1] * log2e
        scal = jnp.concatenate([lam.reshape(1), cfar]).astype(F32)
        col_scale = jnp.ones((1, n_main), F32).at[:, :n_diff].set(DIFF_HEAD_DIM ** -0.5 * log2e)

        proj, gates = _proj(xf, ln_mix_g[l].reshape(1, D).astype(F32), w_main, col_scale, w_gates)
        proj3 = proj.reshape(B, S, n_main)
        a = _diff_attention(proj3, scal, bias_vecs, diff_norm_g[l].reshape(1, n_diff).astype(F32),
                            lambda_init)
        hm = _mlstm(proj3, gates.reshape(B, S, 2 * LANES), conv_w[l].astype(F32),
                    conv_b[l].reshape(1, -1).astype(F32), bi_row, bf_row,
                    mlstm_norm_g[l].reshape(1, -1).astype(F32))

        wo = w_out[l]
        G, E = N_GROUPS, EXPERTS_PER_GROUP
        wr = jnp.zeros((D, LANES), F32).at[:, :G].set(w_group[l].astype(F32)).at[:, G:G + G * E].set(
            jnp.transpose(w_router[l].astype(F32), (1, 0, 2)).reshape(D, G * E)).astype(BF16)
        x1, h2, logits = _out_proj(xf, a.reshape(T, n_diff), hm.reshape(T, -1), wo, wo,
                                   ln_ffn_g[l].reshape(1, D).astype(F32), wr)

        route_bias = jnp.concatenate([b_group[l].astype(F32).reshape(-1), b_router[l].astype(F32).reshape(-1),
                                      jnp.zeros((LANES - G - G * E,), F32)]).reshape(1, LANES)
        n_tiles = (T * TOP_K_INNER) // MOE_TM + N_EXPERTS
        pos8, cwp, cnt = _route(logits, route_bias, MOE_TM)
        pos_t = pos8[:TOP_K_INNER]
        tiles = _tile_plan(cnt, MOE_TM, n_tiles)
        xs = _sc_scatter_rows(h2, pos_t, n_tiles * MOE_TM)
        Fe = w_gate.shape[-1]
        ys = _moe(*tiles, xs, w_gate[l].reshape(N_EXPERTS, D, Fe),
                  w_up[l].reshape(N_EXPERTS, D, Fe), w_down[l].reshape(N_EXPERTS, Fe, D))
        yw = _sc_gather_rows(ys, pos_t.reshape(-1))
        xf = _final(x1, yw, cwp, ln_f_g.reshape(1, D).astype(F32))
    return xf.reshape(B, S, D)
```

```python
import functools
import math

import jax
import jax.numpy as jnp
from jax import lax
from jax.experimental import pallas as pl
from jax.experimental.pallas import tpu as pltpu
from jax.experimental.pallas import tpu_sc as plsc

F32 = jnp.float32
BF16 = jnp.bfloat16

EPS = 1e-6
CHUNK = 64
DIFF_HEAD_DIM = 64
N_DIFF_HEADS = 8
MLSTM_HEAD_DIM = 128
N_MLSTM_HEADS = 8
CONV_WIDTH = 4
N_BUCKETS = 32
MAX_DISTANCE = 128
N_GROUPS = 4
EXPERTS_PER_GROUP = 8
N_EXPERTS = N_GROUPS * EXPERTS_PER_GROUP
TOP_K_INNER = 2
LANES = 128
NEG_BIG = -1e30

VMEM_LIMIT = 56 * 1024 * 1024

PROJ_TM, PROJ_TN = 1024, 1024
ATT_TQ = 512
MLSTM_TS = 1024
OUT_TM = 512
MOE_TM = 256
MOE_WBUF = 3
FIN_TM = 512


def _cparams(sem):
    return pltpu.CompilerParams(dimension_semantics=sem, vmem_limit_bytes=VMEM_LIMIT)


_HI_MASK = 0xFFFF0000


def _pack_bf16_pairs(x):
    half = x.shape[-1] // 2
    xb = x.astype(BF16).astype(F32)
    lo = pltpu.bitcast(xb[:, :half], jnp.uint32)
    hi = pltpu.bitcast(xb[:, half:], jnp.uint32)
    return (hi & jnp.uint32(_HI_MASK)) | (lo >> 16)


def _unpack_bf16_pairs(w):
    lo = pltpu.bitcast(w << 16, F32)
    hi = pltpu.bitcast(w & jnp.uint32(_HI_MASK), F32)
    return lo, hi


def _proj_kernel(x_ref, g_ref, w_ref, cs_ref, wg_ref, o_ref, og_ref, h_ref):
    @pl.when(pl.program_id(1) == 0)
    def _():
        x = x_ref[...]
        ms = jnp.mean(x * x, axis=-1, keepdims=True)
        h = (x * lax.rsqrt(ms + EPS) * g_ref[...]).astype(BF16)
        h_ref[...] = h
        og_ref[...] = jnp.dot(h, wg_ref[...], preferred_element_type=F32)

    o_ref[...] = (jnp.dot(h_ref[...], w_ref[...], preferred_element_type=F32) * cs_ref[...]).astype(o_ref.dtype)


def _proj(x2, g, w_main, col_scale, w_gates):
    T, D = x2.shape
    N = w_main.shape[1]
    NG = w_gates.shape[1]
    return pl.pallas_call(
        _proj_kernel,
        grid=(T // PROJ_TM, N // PROJ_TN),
        in_specs=[
            pl.BlockSpec((PROJ_TM, D), lambda m, n: (m, 0)),
            pl.BlockSpec((1, D), lambda m, n: (0, 0)),
            pl.BlockSpec((D, PROJ_TN), lambda m, n: (0, n)),
            pl.BlockSpec((1, PROJ_TN), lambda m, n: (0, n)),
            pl.BlockSpec((D, NG), lambda m, n: (0, 0)),
        ],
        out_specs=[
            pl.BlockSpec((PROJ_TM, PROJ_TN), lambda m, n: (m, n)),
            pl.BlockSpec((PROJ_TM, NG), lambda m, n: (m, 0)),
        ],
        out_shape=[
            jax.ShapeDtypeStruct((T, N), BF16),
            jax.ShapeDtypeStruct((T, NG), F32),
        ],
        scratch_shapes=[pltpu.VMEM((PROJ_TM, D), BF16)],
        compiler_params=_cparams(("parallel", "arbitrary")),
        name="rms_in_proj",
    )(x2, g, w_main, col_scale, w_gates)


def _t5_bucket(rel):
    half = N_BUCKETS // 2
    max_exact = half // 2
    ret = jnp.where(rel > 0, half, 0)
    n = jnp.abs(rel)
    nf = jnp.maximum(n, 1).astype(F32)
    large = max_exact + (jnp.log(nf / max_exact) / math.log(MAX_DISTANCE / max_exact)
                         * (half - max_exact)).astype(jnp.int32)
    large = jnp.minimum(large, half - 1)
    return ret + jnp.where(n < max_exact, n, large)


def _attn_kernel(scal_ref, q_ref, k_ref, v_ref, bias_ref, g_ref, o_ref, m_ref, l_ref, acc_ref,
                 s0_ref, s1_ref, s2_ref, ml0_ref, ml1_ref, ml2_ref, bt_ref, *, lambda_init):
    h = pl.program_id(1)
    qi = pl.program_id(2)
    nq = pl.num_programs(2)
    tq = ATT_TQ
    lam = scal_ref[0]
    cfar = scal_ref[1 + h]

    hq = tq // 2

    def stacked_queries(tile):
        q = q_ref[0, pl.ds(pl.multiple_of(tile * tq, tq), tq), :]
        lane = lax.broadcasted_iota(jnp.int32, q.shape, 1)
        zero = jnp.zeros_like(q)
        q1 = jnp.where(lane < DIFF_HEAD_DIM, q, zero)
        q2 = jnp.where(lane >= DIFF_HEAD_DIM, q, zero)
        return jnp.concatenate([q1[:hq], q2[:hq], q1[hq:], q2[hq:]], axis=0)

    def stacked_bias(bias):
        return jnp.concatenate([bias[:, :hq], bias[:, :hq], bias[:, hq:], bias[:, hq:]], axis=1)

    qs = stacked_queries(qi)
    nxt = jnp.minimum(qi + 1, nq - 1)
    qs_next = stacked_queries(nxt)

    m_ref[...] = jnp.full(m_ref.shape, NEG_BIG, F32)
    l_ref[...] = jnp.zeros(l_ref.shape, F32)
    acc_ref[...] = jnp.zeros(acc_ref.shape, F32)

    @pl.when(qi == 0)
    def _():
        kj = lax.broadcasted_iota(jnp.int32, (tq, tq), 0)
        qq = lax.broadcasted_iota(jnp.int32, (tq, tq), 1)
        allowed = (kj // CHUNK) <= (qq // CHUNK)
        for d in range(2):
            rows = jnp.broadcast_to(bias_ref[0, d], (tq, 2 * tq))
            tile = pltpu.roll(rows, tq + 1, 1, stride=1, stride_axis=0)[:, :tq]
            if d == 1:
                tile = jnp.where(allowed, tile, NEG_BIG)
            bt_ref[d] = tile

    bufs = ((s0_ref, ml0_ref), (s1_ref, ml1_ref), (s2_ref, ml2_ref))

    def score(ki, bias, slot, queries=None):
        s_ref, ml_ref = bufs[slot]
        start = pl.multiple_of(ki * tq, tq)
        kt = k_ref[0, pl.ds(start, tq), :]
        s = lax.dot_general(kt, qs if queries is None else queries, (((1,), (1,)), ((), ())),
                            preferred_element_type=F32)
        if bias is not None:
            s = s + stacked_bias(bias)
        s_ref[...] = s
        ml_ref[...] = jnp.max(s, axis=0, keepdims=True)

    def score_diagonal(ki, queries):
        start = pl.multiple_of(ki * tq, tq)
        bias = stacked_bias(bt_ref[1])
        s_a = lax.dot_general(k_ref[0, pl.ds(start, hq), :], queries, (((1,), (1,)), ((), ())),
                              preferred_element_type=F32) + bias[:hq, :]
        s_b = lax.dot_general(k_ref[0, pl.ds(start + hq, hq), :], queries[tq:, :], (((1,), (1,)), ((), ())),
                              preferred_element_type=F32) + bias[hq:, tq:]
        s2_ref[0:hq, :] = s_a
        s2_ref[hq:tq, tq:2 * tq] = s_b
        ml_a = jnp.max(s_a, axis=0, keepdims=True)
        ml2_ref[:, 0:tq] = ml_a[:, :tq]
        ml2_ref[:, tq:2 * tq] = jnp.maximum(ml_a[:, tq:], jnp.max(s_b, axis=0, keepdims=True))

    def accumulate_diagonal(ki):
        start = pl.multiple_of(ki * tq, tq)
        m_old = m_ref[...]
        m_new = jnp.maximum(m_old, ml2_ref[...])
        alpha = jnp.exp2(m_old - m_new)
        p_a = jnp.exp2(s2_ref[0:hq, :] - m_new)
        p_b = jnp.exp2(s2_ref[hq:tq, tq:2 * tq] - m_new[:, tq:])
        l_ref[...] = alpha * l_ref[...] + jnp.sum(p_a, axis=0, keepdims=True)
        l_ref[:, tq:2 * tq] = l_ref[:, tq:2 * tq] + jnp.sum(p_b, axis=0, keepdims=True)
        dims = (((0,), (0,)), ((), ()))
        pv_a = lax.dot_general(v_ref[0, pl.ds(start, hq), :], p_a.astype(BF16), dims, preferred_element_type=F32)
        pv_b = lax.dot_general(v_ref[0, pl.ds(start + hq, hq), :], p_b.astype(BF16), dims,
                               preferred_element_type=F32)
        acc_ref[...] = alpha * acc_ref[...] + pv_a
        acc_ref[:, tq:2 * tq] = acc_ref[:, tq:2 * tq] + pv_b
        m_ref[...] = m_new

    def accumulate(ki, shift, slot):
        s_ref, ml_ref = bufs[slot]
        start = pl.multiple_of(ki * tq, tq)
        vt = v_ref[0, pl.ds(start, tq), :]
        m_old = m_ref[...]
        m_new = jnp.maximum(m_old, ml_ref[...] + shift)
        alpha = jnp.exp2(m_old - m_new)
        p = jnp.exp2(s_ref[...] - (m_new - shift))
        l_ref[...] = alpha * l_ref[...] + jnp.sum(p, axis=0, keepdims=True)
        pv = lax.dot_general(vt, p.astype(BF16), (((0,), (0,)), ((), ())), preferred_element_type=F32)
        acc_ref[...] = alpha * acc_ref[...] + pv
        m_ref[...] = m_new

    n_far = qi - 1

    def score_next_diagonal():
        score_diagonal(nxt, qs_next)

    @pl.when(qi == 0)
    def _():
        score_diagonal(qi, qs)
        accumulate_diagonal(qi)
        score_next_diagonal()

    @pl.when(qi >= 1)
    def _():
        accumulate_diagonal(qi)
        score(qi - 1, bt_ref[0], 1)

    @pl.when(qi == 1)
    def _():
        accumulate(qi - 1, 0.0, 1)
        score_next_diagonal()

    @pl.when(qi >= 2)
    def _():
        accumulate(qi - 1, 0.0, 1)
        score(0, None, 0)
        trips = (n_far - 1) // 2

        def pair(j, c):
            accumulate(2 * j, cfar, 0)
            score(2 * j + 1, None, 1)
            accumulate(2 * j + 1, cfar, 1)
            score(2 * j + 2, None, 0)
            return c

        lax.fori_loop(0, trips, pair, 0)
        last = 2 * trips

        @pl.when(n_far - last == 2)
        def _():
            accumulate(last, cfar, 0)
            score(last + 1, None, 1)
            accumulate(last + 1, cfar, 1)
            score_next_diagonal()

        @pl.when(n_far - last == 1)
        def _():
            accumulate(last, cfar, 0)
            score_next_diagonal()

    acc = acc_ref[...] * (1.0 / l_ref[...])
    o_t = jnp.concatenate([acc[:, 0:hq] - lam * acc[:, hq:tq],
                           acc[:, tq:tq + hq] - lam * acc[:, tq + hq:2 * tq]], axis=1)
    ms = jnp.mean(o_t * o_t, axis=0, keepdims=True)
    y = (o_t * lax.rsqrt(ms + EPS)).T * (g_ref[...] * (1.0 - lambda_init))
    o_ref[0] = y.astype(o_ref.dtype)


def _diff_attention(proj3, scal, bias_vecs, gnorm, lambda_init):
    B, S, _ = proj3.shape
    H = N_DIFF_HEADS
    tq = ATT_TQ
    kern = functools.partial(_attn_kernel, lambda_init=lambda_init)
    return pl.pallas_call(
        kern,
        grid=(B, H, S // tq),
        in_specs=[
            pl.BlockSpec(memory_space=pltpu.SMEM),
            pl.BlockSpec((1, S, LANES), lambda b, h, i: (b, 0, h)),
            pl.BlockSpec((1, S, LANES), lambda b, h, i: (b, 0, H + h)),
            pl.BlockSpec((1, S, LANES), lambda b, h, i: (b, 0, 2 * H + h)),
            pl.BlockSpec((1, 2, 1, 2 * tq), lambda b, h, i: (h, 0, 0, 0)),
            pl.BlockSpec((1, LANES), lambda b, h, i: (0, h)),
        ],
        out_specs=pl.BlockSpec((1, tq, LANES), lambda b, h, i: (b, i, h)),
        out_shape=jax.ShapeDtypeStruct((B, S, H * LANES), BF16),
        scratch_shapes=[
            pltpu.VMEM((1, 2 * tq), F32),
            pltpu.VMEM((1, 2 * tq), F32),
            pltpu.VMEM((LANES, 2 * tq), F32),
            pltpu.VMEM((tq, 2 * tq), F32),
            pltpu.VMEM((tq, 2 * tq), F32),
            pltpu.VMEM((tq, 2 * tq), F32),
            pltpu.VMEM((1, 2 * tq), F32),
            pltpu.VMEM((1, 2 * tq), F32),
            pltpu.VMEM((1, 2 * tq), F32),
            pltpu.VMEM((2, tq, tq), F32),
        ],
        compiler_params=_cparams(("parallel", "parallel", "arbitrary")),
        name="diff_attention",
    )(scal, proj3, proj3, proj3, bias_vecs, gnorm)


def _log_sigmoid(x):
    return jnp.minimum(x, 0.0) - jnp.log(1.0 + jnp.exp(-jnp.abs(x)))


def _sigmoid(x):
    return 1.0 / (1.0 + jnp.exp(-x))


def _mlstm_kernel(q_ref, k_ref, v_ref, o_ref, gi_ref, gf_ref, cw_ref, cb_ref, bi_ref, bf_ref, gn_ref,
                  out_ref, qext_ref, kext_ref, ct_ref, n_ref, m_ref):
    sb = pl.program_id(1)
    L = CHUNK
    dh = MLSTM_HEAD_DIM
    H = N_MLSTM_HEADS
    ts = MLSTM_TS
    pad = 8

    @pl.when(sb == 0)
    def _():
        qext_ref[0:pad, :] = jnp.zeros((pad, H * dh), F32)
        kext_ref[0:pad, :] = jnp.zeros((pad, H * dh), F32)
        ct_ref[...] = jnp.zeros(ct_ref.shape, F32)
        n_ref[...] = jnp.zeros(n_ref.shape, F32)
        m_ref[...] = jnp.zeros(m_ref.shape, F32)

    qext_ref[pad:pad + ts, :] = q_ref[0].astype(F32)
    kext_ref[pad:pad + ts, :] = k_ref[0].astype(F32)

    row = lax.broadcasted_iota(jnp.int32, (L, L), 0)
    col = lax.broadcasted_iota(jnp.int32, (L, L), 1)
    tril = col <= row
    ltri = tril.astype(F32)

    def conv_silu(ext_ref, base, h, off):
        win = ext_ref[pl.ds(base, L + pad), h * dh:(h + 1) * dh]
        w = cw_ref[:, off + h * dh:off + (h + 1) * dh]
        y = cb_ref[:, off + h * dh:off + (h + 1) * dh]
        for j in range(CONV_WIDTH):
            lo = pad - (CONV_WIDTH - 1) + j
            y = y + w[j:j + 1, :] * win[lo:lo + L, :]
        return y * _sigmoid(y)

    def chunk_body(c, carry):
        base = pl.multiple_of(c * L, L)
        li = gi_ref[0, pl.ds(base, L), :] + bi_ref[...]
        logf = _log_sigmoid(gf_ref[0, pl.ds(base, L), :] + bf_ref[...])
        b = jnp.dot(ltri, logf, preferred_element_type=F32, precision=lax.Precision.HIGHEST)
        a = li - b
        g_row = b[L - 1:L, :]
        m_row = m_ref[...]
        m_new_row = g_row + jnp.maximum(m_row, jnp.max(a, axis=0, keepdims=True))
        a_t = a.T

        for h in range(H):
            qc = conv_silu(qext_ref, base, h, 0)
            kc = conv_silu(kext_ref, base, h, H * dh) * (dh ** -0.5)
            qb = qc.astype(BF16)
            kb = kc.astype(BF16)
            vb = v_ref[0, pl.ds(base, L), h * dh:(h + 1) * dh]

            a_row = a_t[h:h + 1, :]
            a_col = a[:, h:h + 1]
            b_col = b[:, h:h + 1]
            m_prev = m_row[:, h:h + 1]
            m_next = m_new_row[:, h:h + 1]
            g_h = g_row[:, h:h + 1]

            amat = jnp.where(tril, a_row, NEG_BIG)
            mcol = jnp.maximum(jnp.max(amat, axis=-1, keepdims=True), m_prev)
            wts = jnp.exp(amat - mcol)
            inter = jnp.exp(m_prev - mcol)

            s = lax.dot_general(qb, kb, (((1,), (1,)), ((), ())), preferred_element_type=F32)
            sqk = s * wts
            ct = ct_ref[h]
            nrow = n_ref[h:h + 1, :]
            num = (jnp.dot(sqk.astype(BF16), vb, preferred_element_type=F32)
                   + inter * jnp.dot(qb, ct.astype(BF16), preferred_element_type=F32))
            den = (jnp.sum(sqk, axis=-1, keepdims=True)
                   + inter * jnp.sum(qb.astype(F32) * nrow, axis=-1, keepdims=True))
            hv = num / jnp.maximum(jnp.abs(den), jnp.exp(-(b_col + mcol)))

            wt = jnp.exp(g_h + a_col - m_next)
            decay = jnp.exp(g_h + m_prev - m_next)
            wv = (wt * vb.astype(F32)).astype(BF16)
            ct_ref[h] = decay * ct + lax.dot_general(kb, wv, (((0,), (0,)), ((), ())),
                                                     preferred_element_type=F32)
            n_ref[h:h + 1, :] = decay * nrow + jnp.sum(wt * kb.astype(F32), axis=0, keepdims=True)

            ms = jnp.mean(hv * hv, axis=-1, keepdims=True)
            y = hv * lax.rsqrt(ms + EPS) * gn_ref[:, h * dh:(h + 1) * dh]
            og = o_ref[0, pl.ds(base, L), h * dh:(h + 1) * dh].astype(F32)
            out_ref[0, pl.ds(base, L), h * dh:(h + 1) * dh] = (y * _sigmoid(og)).astype(out_ref.dtype)

        m_ref[...] = m_new_row
        return carry

    lax.fori_loop(0, ts // L, chunk_body, 0, unroll=4)

    qext_ref[0:pad, :] = qext_ref[ts:ts + pad, :]
    kext_ref[0:pad, :] = kext_ref[ts:ts + pad, :]


def _mlstm(proj3, gates3, conv_w, conv_b, bi_row, bf_row, gnorm):
    B, S, _ = proj3.shape
    W = N_MLSTM_HEADS * MLSTM_HEAD_DIM
    ts = MLSTM_TS
    first = 3
    blk = lambda j: pl.BlockSpec((1, ts, W), lambda b, s: (b, s, j))
    full = lambda shape: pl.BlockSpec(shape, lambda b, s: (0,) * len(shape))
    return pl.pallas_call(
        _mlstm_kernel,
        grid=(B, S // ts),
        in_specs=[
            blk(first), blk(first + 1), blk(first + 2), blk(first + 3),
            pl.BlockSpec((1, ts, LANES), lambda b, s: (b, s, 0)),
            pl.BlockSpec((1, ts, LANES), lambda b, s: (b, s, 1)),
            full((CONV_WIDTH, 2 * W)), full((1, 2 * W)),
            full((1, LANES)), full((1, LANES)), full((1, W)),
        ],
        out_specs=pl.BlockSpec((1, ts, W), lambda b, s: (b, s, 0)),
        out_shape=jax.ShapeDtypeStruct((B, S, W), BF16),
        scratch_shapes=[
            pltpu.VMEM((ts + 8, W), F32),
            pltpu.VMEM((ts + 8, W), F32),
            pltpu.VMEM((N_MLSTM_HEADS, MLSTM_HEAD_DIM, MLSTM_HEAD_DIM), F32),
            pltpu.VMEM((N_MLSTM_HEADS, MLSTM_HEAD_DIM), F32),
            pltpu.VMEM((1, LANES), F32),
        ],
        compiler_params=_cparams(("parallel", "arbitrary")),
        name="mlstm",
    )(proj3, proj3, proj3, proj3, gates3, gates3, conv_w, conv_b, bi_row, bf_row, gnorm)


def _out_kernel(x_ref, a_ref, hm_ref, wa_ref, wm_ref, g_ref, wr_ref, x1_ref, h2_ref, lg_ref):
    y = (jnp.dot(a_ref[...], wa_ref[...].astype(BF16), preferred_element_type=F32)
         + jnp.dot(hm_ref[...], wm_ref[...].astype(BF16), preferred_element_type=F32))
    x1 = x_ref[...] + y
    x1_ref[...] = x1
    ms = jnp.mean(x1 * x1, axis=-1, keepdims=True)
    h2 = x1 * lax.rsqrt(ms + EPS) * g_ref[...]
    h2_ref[...] = _pack_bf16_pairs(h2)
    lg_ref[...] = jnp.dot(h2.astype(BF16), wr_ref[...], preferred_element_type=F32)


def _out_proj(x2, a2, hm2, wa, wm, g, wr):
    T, D = x2.shape
    W = a2.shape[1]
    tm = OUT_TM
    const = lambda shape: pl.BlockSpec(shape, lambda m: (0, 0), pipeline_mode=pl.Buffered(1))
    return pl.pallas_call(
        _out_kernel,
        grid=(T // tm,),
        in_specs=[
            pl.BlockSpec((tm, D), lambda m: (m, 0)),
            pl.BlockSpec((tm, W), lambda m: (m, 0)),
            pl.BlockSpec((tm, W), lambda m: (m, 0)),
            const((W, D)),
            pl.BlockSpec((W, D), lambda m: (1, 0), pipeline_mode=pl.Buffered(1)),
            const((1, D)), const((D, LANES)),
        ],
        out_specs=[
            pl.BlockSpec((tm, D), lambda m: (m, 0)),
            pl.BlockSpec((tm, D // 2), lambda m: (m, 0)),
            pl.BlockSpec((tm, LANES), lambda m: (m, 0)),
        ],
        out_shape=[
            jax.ShapeDtypeStruct((T, D), F32),
            jax.ShapeDtypeStruct((T, D // 2), jnp.uint32),
            jax.ShapeDtypeStruct((T, LANES), F32),
        ],
        compiler_params=_cparams(("parallel",)),
        name="out_proj_router",
    )(x2, a2, hm2, wa, wm, g, wr)


def _moe_kernel(te_ref, nv_ref, first_ref, slot_ref, nxt_ref, nxt2_ref, rows_ref, xs_ref, wg_hbm, wu_hbm,
                wd_hbm, ys_ref, wg_buf, wu_buf, wd_buf, sem):
    j = pl.program_id(0)
    valid = j < nv_ref[0]
    nbuf = wg_buf.shape[0]

    half_f = wd_buf.shape[1] // 2

    def weight_copies(e, s):
        lo, hi = pl.ds(0, half_f), pl.ds(half_f, half_f)
        return ((pltpu.make_async_copy(wg_hbm.at[e], wg_buf.at[s], sem.at[s, 0]), 0),
                (pltpu.make_async_copy(wu_hbm.at[e], wu_buf.at[s], sem.at[s, 1]), 1),
                (pltpu.make_async_copy(wd_hbm.at[e, lo], wd_buf.at[s, lo], sem.at[s, 2]), 0),
                (pltpu.make_async_copy(wd_hbm.at[e, hi], wd_buf.at[s, hi], sem.at[s, 3]), 1))

    @pl.when(j == 0)
    def _():
        for c, prio in weight_copies(te_ref[0], 0):
            c.start(priority=prio)

        @pl.when(nxt_ref[0] >= 0)
        def _():
            for c, prio in weight_copies(nxt_ref[0], 1):
                c.start(priority=prio)

    @pl.when(jnp.logical_and(valid, first_ref[j] == 1))
    def _():
        for c, _ in weight_copies(te_ref[j], slot_ref[j]):
            c.wait()

        @pl.when(nxt2_ref[j] >= 0)
        def _():
            for c, prio in weight_copies(nxt2_ref[j], (slot_ref[j] + 2) % nbuf):
                c.start(priority=prio)

    @pl.when(valid)
    def _():
        s = slot_ref[j]
        row = lax.broadcasted_iota(jnp.int32, xs_ref.shape, 0)
        lo, hi = _unpack_bf16_pairs(jnp.where(row < rows_ref[j], xs_ref[...], jnp.uint32(0)))
        xs = jnp.concatenate([lo.astype(BF16), hi.astype(BF16)], axis=1)
        gt = jnp.dot(xs, wg_buf[s].astype(BF16), preferred_element_type=F32)
        up = jnp.dot(xs, wu_buf[s].astype(BF16), preferred_element_type=F32)
        hid = (gt * _sigmoid(gt) * up).astype(BF16)
        ys_ref[...] = _pack_bf16_pairs(jnp.dot(hid, wd_buf[s].astype(BF16), preferred_element_type=F32))

    @pl.when(jnp.logical_not(valid))
    def _():
        ys_ref[...] = jnp.zeros(ys_ref.shape, ys_ref.dtype)


def _moe(tile_expert, n_valid, tile_first, tile_slot, tile_next, tile_next2, tile_rows, xs, wg, wu, wd):
    R, Dw = xs.shape
    D, F = wg.shape[1], wg.shape[2]
    tm = MOE_TM
    hbm = pl.BlockSpec(memory_space=pl.ANY)
    grid_spec = pltpu.PrefetchScalarGridSpec(
        num_scalar_prefetch=7,
        grid=(R // tm,),
        in_specs=[pl.BlockSpec((tm, Dw), lambda j, *_: (j, 0)), hbm, hbm, hbm],
        out_specs=pl.BlockSpec((tm, Dw), lambda j, *_: (j, 0)),
        scratch_shapes=[
            pltpu.VMEM((MOE_WBUF, D, F), wg.dtype),
            pltpu.VMEM((MOE_WBUF, D, F), wu.dtype),
            pltpu.VMEM((MOE_WBUF, F, D), wd.dtype),
            pltpu.SemaphoreType.DMA((MOE_WBUF, 4)),
        ],
    )
    return pl.pallas_call(
        _moe_kernel,
        grid_spec=grid_spec,
        out_shape=jax.ShapeDtypeStruct((R, Dw), jnp.uint32),
        compiler_params=_cparams(("arbitrary",)),
        name="moe_experts",
    )(tile_expert, n_valid, tile_first, tile_slot, tile_next, tile_next2, tile_rows, xs, wg, wu, wd)


def _final_kernel(x1_ref, y0_ref, y1_ref, cw_ref, g_ref, o_ref):
    cw = cw_ref[...]
    lo0, hi0 = _unpack_bf16_pairs(y0_ref[...])
    lo1, hi1 = _unpack_bf16_pairs(y1_ref[...])
    w0, w1 = cw[:, 0:1], cw[:, 1:2]
    y = jnp.concatenate([w0 * lo0 + w1 * lo1, w0 * hi0 + w1 * hi1], axis=1)
    x = x1_ref[...] + y
    ms = jnp.mean(x * x, axis=-1, keepdims=True)
    o_ref[...] = x * lax.rsqrt(ms + EPS) * g_ref[...]


def _final(x1, yw, cw, g):
    T, D = x1.shape
    tm = FIN_TM
    row = lambda w: pl.BlockSpec((tm, w), lambda m: (m, 0))
    slot1 = pl.BlockSpec((tm, D // 2), lambda m: (m + T // tm, 0))
    return pl.pallas_call(
        _final_kernel,
        grid=(T // tm,),
        in_specs=[row(D), row(D // 2), slot1, row(LANES), pl.BlockSpec((1, D), lambda m: (0, 0))],
        out_specs=row(D),
        out_shape=jax.ShapeDtypeStruct((T, D), F32),
        compiler_params=_cparams(("parallel",)),
        name="combine_final_norm",
    )(x1, yw, yw, cw, g)


SC_CORES, SC_SUBCORES = 2, 16
SC_CHUNK = 32


def _sc_gather_rows(table, idx):
    V, Dw = table.shape
    R = idx.shape[0]
    n_workers = SC_CORES * SC_SUBCORES
    ch = SC_CHUNK
    per_w = R // n_workers
    n_chunks = per_w // ch
    assert per_w * n_workers == R and n_chunks * ch == per_w and n_chunks % 2 == 0
    idx3 = idx.reshape(n_workers, n_chunks, ch)
    mesh = plsc.VectorSubcoreMesh(core_axis_name="c", subcore_axis_name="s")

    def body(table_hbm, idx_hbm, out_hbm, idx_v, rows_v, gsem, osem):
        wid = lax.axis_index("s") * SC_CORES + lax.axis_index("c")
        base = wid * per_w
        pltpu.sync_copy(idx_hbm.at[wid], idx_v)

        def gather(c, slot):
            return pltpu.make_async_copy(table_hbm.at[idx_v.at[c]], rows_v.at[slot], gsem.at[slot])

        def put(c, slot):
            return pltpu.make_async_copy(rows_v.at[slot], out_hbm.at[pl.ds(base + c * ch, ch)],
                                         osem.at[slot])

        gather(0, 0).start()

        @pl.loop(0, n_chunks, step=2)
        def _(c):
            @pl.when(c > 0)
            def _():
                put(c - 1, 1).wait()

            gather(c + 1, 1).start()
            gather(c, 0).wait()
            put(c, 0).start()
            put(c, 0).wait()

            @pl.when(c + 2 < n_chunks)
            def _():
                gather(c + 2, 0).start()

            gather(c + 1, 1).wait()
            put(c + 1, 1).start()

        put(n_chunks - 1, 1).wait()

    return pl.kernel(
        body,
        out_type=jax.ShapeDtypeStruct((R, Dw), table.dtype),
        mesh=mesh,
        scratch_types=[
            pltpu.VMEM((n_chunks, ch), jnp.int32),
            pltpu.VMEM((2, ch, Dw), table.dtype),
            pltpu.SemaphoreType.DMA((2,)),
            pltpu.SemaphoreType.DMA((2,)),
        ],
        name="sc_gather_rows",
    )(table, idx3)


def _sc_scatter_rows(table, idx, n_rows_out):
    V, Dw = table.shape
    K = idx.shape[0]
    n_workers = SC_CORES * SC_SUBCORES
    ch = SC_CHUNK
    per_w = V // n_workers
    n_chunks = per_w // ch
    assert K == 2 and per_w * n_workers == V and n_chunks * ch == per_w and n_chunks % 2 == 0
    idx4 = jnp.transpose(idx.reshape(K, n_workers, n_chunks, ch), (1, 0, 2, 3))
    mesh = plsc.VectorSubcoreMesh(core_axis_name="c", subcore_axis_name="s")

    def body(table_hbm, idx_hbm, out_hbm, idx_v, rows_v, lsem, ssem):
        wid = lax.axis_index("s") * SC_CORES + lax.axis_index("c")
        base = wid * per_w
        pltpu.sync_copy(idx_hbm.at[wid], idx_v)

        def load(c, slot):
            return pltpu.make_async_copy(table_hbm.at[pl.ds(base + c * ch, ch)], rows_v.at[slot],
                                         lsem.at[slot])

        def scatter(c, slot, k):
            return pltpu.make_async_copy(rows_v.at[slot], out_hbm.at[idx_v.at[k, c]], ssem.at[slot, k])

        load(0, 0).start()

        @pl.loop(0, n_chunks, step=2)
        def _(c):
            @pl.when(c > 0)
            def _():
                scatter(c - 1, 1, 0).wait()
                scatter(c - 1, 1, 1).wait()

            load(c + 1, 1).start()
            load(c, 0).wait()
            scatter(c, 0, 0).start()
            scatter(c, 0, 1).start()
            scatter(c, 0, 0).wait()
            scatter(c, 0, 1).wait()

            @pl.when(c + 2 < n_chunks)
            def _():
                load(c + 2, 0).start()

            load(c + 1, 1).wait()
            scatter(c + 1, 1, 0).start()
            scatter(c + 1, 1, 1).start()

        scatter(n_chunks - 1, 1, 0).wait()
        scatter(n_chunks - 1, 1, 1).wait()

    return pl.kernel(
        body,
        out_type=jax.ShapeDtypeStruct((n_rows_out, Dw), table.dtype),
        mesh=mesh,
        scratch_types=[
            pltpu.VMEM((K, n_chunks, ch), jnp.int32),
            pltpu.VMEM((2, ch, Dw), table.dtype),
            pltpu.SemaphoreType.DMA((2,)),
            pltpu.SemaphoreType.DMA((2, K)),
        ],
        name="sc_scatter_rows",
    )(table, idx4)


ROUTE_BLK = 256


def _route_kernel(lg_ref, bias_ref, pos_ref, cw_ref, cnt_ref, oh_ref, pre_ref, *, tm):
    T = lg_ref.shape[0]
    G, E, NE = N_GROUPS, EXPERTS_PER_GROUP, N_EXPERTS
    blk = ROUTE_BLK
    sub = lax.broadcasted_iota(jnp.int32, (LANES, blk), 0).astype(F32)
    r_i = lax.broadcasted_iota(jnp.int32, (blk, blk), 0)
    c_i = lax.broadcasted_iota(jnp.int32, (blk, blk), 1)
    before = (r_i < c_i).astype(BF16)

    def first_argmax(v):
        mx = jnp.max(v, axis=0, keepdims=True)
        return mx, jnp.min(jnp.where(v == mx, sub, float(LANES)), axis=0, keepdims=True)

    def phase1(b, carry):
        cols = pl.ds(pl.multiple_of(b * blk, blk), blk)
        x = (lg_ref[cols, :] + bias_ref[...]).T
        gl = jnp.where(sub < G, x, NEG_BIG)
        gmax, gsel = first_argmax(gl)
        gw = 1.0 / jnp.sum(jnp.exp(gl - gmax), axis=0, keepdims=True)
        lo = G + E * gsel
        el = jnp.where(jnp.logical_and(sub >= lo, sub < lo + E), x, NEG_BIG)
        v1, i1 = first_argmax(el)
        v2, i2 = first_argmax(jnp.where(sub == i1, NEG_BIG, el))
        e21 = jnp.exp(v2 - v1)
        w1 = 1.0 / (1.0 + e21)
        cw_t = jnp.where(sub == 0, gw * w1, jnp.where(sub == 1, gw * (e21 * w1), 0.0))
        cw_ref[cols, :] = cw_t.T
        oh = jnp.where(jnp.logical_or(sub == i1 - G, sub == i2 - G + NE), 1.0, 0.0)
        oh_ref[:, cols] = oh
        pre_ref[:, cols] = jnp.dot(oh.astype(BF16), before, preferred_element_type=F32) + carry
        return carry + jnp.sum(oh, axis=1, keepdims=True)

    counts = lax.fori_loop(0, T // blk, phase1, jnp.zeros((LANES, 1), F32))

    c0 = counts[0:NE, :]
    ctot = c0 + counts[NE:2 * NE, :]
    tiles = jnp.floor((ctot + (tm - 1)) * (1.0 / tm))
    e_r = lax.broadcasted_iota(jnp.int32, (NE, NE), 0)
    e_c = lax.broadcasted_iota(jnp.int32, (NE, NE), 1)
    upto = (e_c <= e_r).astype(BF16)
    tile_end = jnp.dot(upto, jnp.broadcast_to(tiles, (NE, LANES)).astype(BF16),
                       preferred_element_type=F32)[:, 0:1]
    row_start = (tile_end - tiles) * tm
    base = jnp.concatenate([row_start, row_start + c0, jnp.zeros((LANES - 2 * NE, 1), F32)], axis=0)
    lane_e = lax.broadcasted_iota(jnp.int32, (NE, LANES), 1)
    cnt_ref[...] = jnp.concatenate(
        [jnp.where(lane_e == 0, ctot, jnp.where(lane_e == 1, tiles, jnp.where(lane_e == 2, tile_end, 0.0))),
         jnp.zeros((LANES - NE, LANES), F32)], axis=0)

    def phase2(b, carry):
        cols = pl.ds(pl.multiple_of(b * blk, blk), blk)
        t = oh_ref[:, cols] * (pre_ref[:, cols] + base)
        p0 = jnp.sum(t[0:NE, :], axis=0, keepdims=True)
        p1 = jnp.sum(t[NE:2 * NE, :], axis=0, keepdims=True)
        pos_ref[:, cols] = jnp.concatenate([p0, p1, jnp.zeros((6, blk), F32)], axis=0).astype(jnp.int32)
        return carry

    lax.fori_loop(0, T // blk, phase2, 0)


def _route(logits, bias_row, tm):
    T = logits.shape[0]
    whole = lambda shape: pl.BlockSpec(shape, lambda: (0,) * len(shape))
    return pl.pallas_call(
        functools.partial(_route_kernel, tm=tm),
        in_specs=[whole((T, LANES)), whole((1, LANES))],
        out_specs=[whole((8, T)), whole((T, LANES)), whole((LANES, LANES))],
        out_shape=[
            jax.ShapeDtypeStruct((8, T), jnp.int32),
            jax.ShapeDtypeStruct((T, LANES), F32),
            jax.ShapeDtypeStruct((LANES, LANES), F32),
        ],
        scratch_shapes=[pltpu.VMEM((LANES, T), F32), pltpu.VMEM((LANES, T), F32)],
        compiler_params=pltpu.CompilerParams(vmem_limit_bytes=VMEM_LIMIT),
        name="route_positions",
    )(logits, bias_row)


def _tile_plan(cnt, tm, n_tiles):
    counts = cnt[:N_EXPERTS, 0].astype(jnp.int32)
    tiles_per_e = cnt[:N_EXPERTS, 1].astype(jnp.int32)
    tile_end = cnt[:N_EXPERTS, 2].astype(jnp.int32)
    n_valid = tile_end[-1]
    tile_ids = jnp.arange(n_tiles, dtype=jnp.int32)
    experts = jnp.arange(N_EXPERTS, dtype=jnp.int32)
    valid = tile_ids < n_valid
    jc = jnp.minimum(tile_ids, n_valid - 1)
    tile_expert = jnp.minimum(jnp.sum((tile_end[None, :] <= jc[:, None]).astype(jnp.int32), axis=1),
                              N_EXPERTS - 1)
    of_tile = (tile_expert[:, None] == experts[None, :]).astype(jnp.int32)
    pick = lambda per_expert: jnp.sum(of_tile * per_expert[None, :], axis=1)
    tile_start = tile_end - tiles_per_e
    nonempty = tiles_per_e > 0
    group_idx = jnp.sum(jnp.logical_and(nonempty[None, :], experts[None, :] <= experts[:, None]).astype(jnp.int32),
                        axis=1) - 1
    later = jnp.logical_and(nonempty[None, :], experts[None, :] > experts[:, None])
    next_e = jnp.min(jnp.where(later, experts[None, :], N_EXPERTS), axis=1)
    next_e = jnp.where(next_e == N_EXPERTS, -1, next_e)
    is_next = (next_e[:, None] == experts[None, :]).astype(jnp.int32)
    next2_e = jnp.where(next_e >= 0, jnp.sum(is_next * next_e[None, :], axis=1), -1)
    tile_first = jnp.logical_and(valid, tile_ids == pick(tile_start)).astype(jnp.int32)
    tile_slot = (pick(group_idx) % MOE_WBUF).astype(jnp.int32)
    tile_next = pick(next_e).astype(jnp.int32)
    tile_next2 = pick(next2_e).astype(jnp.int32)
    tile_rows = jnp.clip(pick(counts) - (tile_ids - pick(tile_start)) * tm, 0, tm)
    tile_rows = jnp.where(valid, tile_rows, 0).astype(jnp.int32)
    return (tile_expert.astype(jnp.int32), n_valid.reshape(1), tile_first, tile_slot, tile_next, tile_next2,
            tile_rows)


def kernel(x, rel_bias, ln_mix_g, w_in, conv_w, conv_b, b_i, b_f, lam_q1, lam_k1, lam_q2, lam_k2,
           diff_norm_g, mlstm_norm_g, w_out, ln_ffn_g, w_group, b_group, w_router, b_router,
           w_gate, w_up, w_down, ln_f_g):
    B, S, D = x.shape
    T = B * S
    depth = w_in.shape[0]
    assert depth == 1, "the final rmsnorm is fused into the single layer's combine kernel"
    Hm = N_MLSTM_HEADS
    n_main = w_in.shape[2] - 2 * Hm
    n_diff = N_DIFF_HEADS * 2 * DIFF_HEAD_DIM
    xf = x.reshape(T, D)

    for l in range(depth):
        lambda_init = 0.8 - 0.6 * math.exp(-0.3 * l)
        w_main = w_in[l, :, :n_main].astype(BF16)
        wgt = w_in[l, :, n_main:]
        w_gates = jnp.zeros((D, 2 * LANES), F32).at[:, :Hm].set(wgt[:, :Hm]).at[:, LANES:LANES + Hm].set(
            wgt[:, Hm:]).astype(BF16)
        bi_row = jnp.zeros((1, LANES), F32).at[0, :Hm].set(b_i[l].astype(F32))
        bf_row = jnp.zeros((1, LANES), F32).at[0, :Hm].set(b_f[l].astype(F32))
        lam = (jnp.exp(jnp.sum(lam_q1[l].astype(F32) * lam_k1[l].astype(F32)))
               - jnp.exp(jnp.sum(lam_q2[l].astype(F32) * lam_k2[l].astype(F32))) + lambda_init)
        tq = ATT_TQ
        assert tq >= MAX_DISTANCE and tq % CHUNK == 0
        rb = rel_bias.astype(F32)
        log2e = math.log2(math.e)
        xx = jnp.arange(2 * tq, dtype=jnp.int32)
        rel_vec = jnp.stack([-tq + tq - 1 - xx, tq - 1 - xx], axis=0)
        bias_vecs = jnp.take(rb, _t5_bucket(rel_vec), axis=0) * log2e
        bias_vecs = jnp.transpose(bias_vecs, (2, 0, 1))[:, :, None, :]
        cfar = rb[N_BUCKETS // 2 - 1] * log2e
        scal = jnp.concatenate([lam.reshape(1), cfar]).astype(F32)
        col_scale = jnp.ones((1, n_main), F32).at[:, :n_diff].set(DIFF_HEAD_DIM ** -0.5 * log2e)

        proj, gates = _proj(xf, ln_mix_g[l].reshape(1, D).astype(F32), w_main, col_scale, w_gates)
        proj3 = proj.reshape(B, S, n_main)
        a = _diff_attention(proj3, scal, bias_vecs, diff_norm_g[l].reshape(1, n_diff).astype(F32),
                            lambda_init)
        hm = _mlstm(proj3, gates.reshape(B, S, 2 * LANES), conv_w[l].astype(F32),
                    conv_b[l].reshape(1, -1).astype(F32), bi_row, bf_row,
                    mlstm_norm_g[l].reshape(1, -1).astype(F32))

        wo = w_out[l]
        G, E = N_GROUPS, EXPERTS_PER_GROUP
        wr = jnp.zeros((D, LANES), F32).at[:, :G].set(w_group[l].astype(F32)).at[:, G:G + G * E].set(
            jnp.transpose(w_router[l].astype(F32), (1, 0, 2)).reshape(D, G * E)).astype(BF16)
        x1, h2, logits = _out_proj(xf, a.reshape(T, n_diff), hm.reshape(T, -1), wo, wo,
                                   ln_ffn_g[l].reshape(1, D).astype(F32), wr)

        route_bias = jnp.concatenate([b_group[l].astype(F32).reshape(-1), b_router[l].astype(F32).reshape(-1),
                                      jnp.zeros((LANES - G - G * E,), F32)]).reshape(1, LANES)
        n_tiles = (T * TOP_K_INNER) // MOE_TM + N_EXPERTS
        pos8, cwp, cnt = _route(logits, route_bias, MOE_TM)
        pos_t = pos8[:TOP_K_INNER]
        tiles = _tile_plan(cnt, MOE_TM, n_tiles)
        xs = _sc_scatter_rows(h2, pos_t, n_tiles * MOE_TM)
        Fe = w_gate.shape[-1]
        ys = _moe(*tiles, xs, w_gate[l].reshape(N_EXPERTS, D, Fe),
                  w_up[l].reshape(N_EXPERTS, D, Fe), w_down[l].reshape(N_EXPERTS, Fe, D))
        yw = _sc_gather_rows(ys, pos_t.reshape(-1))
        xf = _final(x1, yw, cwp, ln_f_g.reshape(1, D).astype(F32))
    return xf.reshape(B, S, D)
```

```python
import functools
import math

import jax
import jax.numpy as jnp
from jax import lax
from jax.experimental import pallas as pl
from jax.experimental.pallas import tpu as pltpu
from jax.experimental.pallas import tpu_sc as plsc

F32 = jnp.float32
BF16 = jnp.bfloat16

EPS = 1e-6
CHUNK = 64
DIFF_HEAD_DIM = 64
N_DIFF_HEADS = 8
MLSTM_HEAD_DIM = 128
N_MLSTM_HEADS = 8
CONV_WIDTH = 4
N_BUCKETS = 32
MAX_DISTANCE = 128
N_GROUPS = 4
EXPERTS_PER_GROUP = 8
N_EXPERTS = N_GROUPS * EXPERTS_PER_GROUP
TOP_K_INNER = 2
LANES = 128
NEG_BIG = -1e30

VMEM_LIMIT = 56 * 1024 * 1024

PROJ_TM, PROJ_TN = 1024, 1792
ATT_TQ = 512
MLSTM_TS = 1024
OUT_TM = 512
MOE_TM = 256
MOE_WBUF = 3
FIN_TM = 512


def _cparams(sem):
    return pltpu.CompilerParams(dimension_semantics=sem, vmem_limit_bytes=VMEM_LIMIT)


_HI_MASK = 0xFFFF0000


def _pack_bf16_pairs(x):
    half = x.shape[-1] // 2
    xb = x.astype(BF16).astype(F32)
    lo = pltpu.bitcast(xb[:, :half], jnp.uint32)
    hi = pltpu.bitcast(xb[:, half:], jnp.uint32)
    return (hi & jnp.uint32(_HI_MASK)) | (lo >> 16)


def _unpack_bf16_pairs(w):
    lo = pltpu.bitcast(w << 16, F32)
    hi = pltpu.bitcast(w & jnp.uint32(_HI_MASK), F32)
    return lo, hi


def _proj_kernel(x_ref, g_ref, w_ref, cs_ref, wg_ref, o_ref, og_ref, h_ref):
    @pl.when(pl.program_id(1) == 0)
    def _():
        x = x_ref[...]
        ms = jnp.mean(x * x, axis=-1, keepdims=True)
        h = (x * lax.rsqrt(ms + EPS) * g_ref[...]).astype(BF16)
        h_ref[...] = h
        og_ref[...] = jnp.dot(h, wg_ref[...], preferred_element_type=F32)

    o_ref[...] = (jnp.dot(h_ref[...], w_ref[...], preferred_element_type=F32) * cs_ref[...]).astype(o_ref.dtype)


def _proj(x2, g, w_main, col_scale, w_gates):
    T, D = x2.shape
    N = w_main.shape[1]
    NG = w_gates.shape[1]
    return pl.pallas_call(
        _proj_kernel,
        grid=(T // PROJ_TM, N // PROJ_TN),
        in_specs=[
            pl.BlockSpec((PROJ_TM, D), lambda m, n: (m, 0)),
            pl.BlockSpec((1, D), lambda m, n: (0, 0)),
            pl.BlockSpec((D, PROJ_TN), lambda m, n: (0, n)),
            pl.BlockSpec((1, PROJ_TN), lambda m, n: (0, n)),
            pl.BlockSpec((D, NG), lambda m, n: (0, 0)),
        ],
        out_specs=[
            pl.BlockSpec((PROJ_TM, PROJ_TN), lambda m, n: (m, n)),
            pl.BlockSpec((PROJ_TM, NG), lambda m, n: (m, 0)),
        ],
        out_shape=[
            jax.ShapeDtypeStruct((T, N), BF16),
            jax.ShapeDtypeStruct((T, NG), F32),
        ],
        scratch_shapes=[pltpu.VMEM((PROJ_TM, D), BF16)],
        compiler_params=_cparams(("parallel", "arbitrary")),
        name="rms_in_proj",
    )(x2, g, w_main, col_scale, w_gates)


def _t5_bucket(rel):
    half = N_BUCKETS // 2
    max_exact = half // 2
    ret = jnp.where(rel > 0, half, 0)
    n = jnp.abs(rel)
    nf = jnp.maximum(n, 1).astype(F32)
    large = max_exact + (jnp.log(nf / max_exact) / math.log(MAX_DISTANCE / max_exact)
                         * (half - max_exact)).astype(jnp.int32)
    large = jnp.minimum(large, half - 1)
    return ret + jnp.where(n < max_exact, n, large)


def _attn_kernel(scal_ref, q_ref, k_ref, v_ref, bias_ref, g_ref, o_ref, m_ref, l_ref, acc_ref,
                 s0_ref, s1_ref, s2_ref, ml0_ref, ml1_ref, ml2_ref, bt_ref, *, lambda_init):
    h = pl.program_id(1)
    qi = pl.program_id(2)
    nq = pl.num_programs(2)
    tq = ATT_TQ
    lam = scal_ref[0]
    cfar = scal_ref[1 + h]

    hq = tq // 2

    def stacked_queries(tile):
        q = q_ref[0, pl.ds(pl.multiple_of(tile * tq, tq), tq), :]
        lane = lax.broadcasted_iota(jnp.int32, q.shape, 1)
        zero = jnp.zeros_like(q)
        q1 = jnp.where(lane < DIFF_HEAD_DIM, q, zero)
        q2 = jnp.where(lane >= DIFF_HEAD_DIM, q, zero)
        return jnp.concatenate([q1[:hq], q2[:hq], q1[hq:], q2[hq:]], axis=0)

    def stacked_bias(bias):
        return jnp.concatenate([bias[:, :hq], bias[:, :hq], bias[:, hq:], bias[:, hq:]], axis=1)

    qs = stacked_queries(qi)
    nxt = jnp.minimum(qi + 1, nq - 1)
    qs_next = stacked_queries(nxt)

    m_ref[...] = jnp.full(m_ref.shape, NEG_BIG, F32)
    l_ref[...] = jnp.zeros(l_ref.shape, F32)
    acc_ref[...] = jnp.zeros(acc_ref.shape, F32)

    @pl.when(qi == 0)
    def _():
        kj = lax.broadcasted_iota(jnp.int32, (tq, tq), 0)
        qq = lax.broadcasted_iota(jnp.int32, (tq, tq), 1)
        allowed = (kj // CHUNK) <= (qq // CHUNK)
        for d in range(2):
            rows = jnp.broadcast_to(bias_ref[0, d], (tq, 2 * tq))
            tile = pltpu.roll(rows, tq + 1, 1, stride=1, stride_axis=0)[:, :tq]
            if d == 1:
                tile = jnp.where(allowed, tile, NEG_BIG)
            bt_ref[d] = tile

    bufs = ((s0_ref, ml0_ref), (s1_ref, ml1_ref), (s2_ref, ml2_ref))

    def score(ki, bias, slot, queries=None):
        s_ref, ml_ref = bufs[slot]
        start = pl.multiple_of(ki * tq, tq)
        kt = k_ref[0, pl.ds(start, tq), :]
        s = lax.dot_general(kt, qs if queries is None else queries, (((1,), (1,)), ((), ())),
                            preferred_element_type=F32)
        if bias is not None:
            s = s + stacked_bias(bias)
        s_ref[...] = s
        ml_ref[...] = jnp.max(s, axis=0, keepdims=True)

    def score_diagonal(ki, queries):
        start = pl.multiple_of(ki * tq, tq)
        bias = stacked_bias(bt_ref[1])
        s_a = lax.dot_general(k_ref[0, pl.ds(start, hq), :], queries, (((1,), (1,)), ((), ())),
                              preferred_element_type=F32) + bias[:hq, :]
        s_b = lax.dot_general(k_ref[0, pl.ds(start + hq, hq), :], queries[tq:, :], (((1,), (1,)), ((), ())),
                              preferred_element_type=F32) + bias[hq:, tq:]
        s2_ref[0:hq, :] = s_a
        s2_ref[hq:tq, tq:2 * tq] = s_b
        ml_a = jnp.max(s_a, axis=0, keepdims=True)
        ml2_ref[:, 0:tq] = ml_a[:, :tq]
        ml2_ref[:, tq:2 * tq] = jnp.maximum(ml_a[:, tq:], jnp.max(s_b, axis=0, keepdims=True))

    def accumulate_diagonal(ki):
        start = pl.multiple_of(ki * tq, tq)
        m_old = m_ref[...]
        m_new = jnp.maximum(m_old, ml2_ref[...])
        alpha = jnp.exp2(m_old - m_new)
        p_a = jnp.exp2(s2_ref[0:hq, :] - m_new)
        p_b = jnp.exp2(s2_ref[hq:tq, tq:2 * tq] - m_new[:, tq:])
        l_ref[...] = alpha * l_ref[...] + jnp.sum(p_a, axis=0, keepdims=True)
        l_ref[:, tq:2 * tq] = l_ref[:, tq:2 * tq] + jnp.sum(p_b, axis=0, keepdims=True)
        dims = (((0,), (0,)), ((), ()))
        pv_a = lax.dot_general(v_ref[0, pl.ds(start, hq), :], p_a.astype(BF16), dims, preferred_element_type=F32)
        pv_b = lax.dot_general(v_ref[0, pl.ds(start + hq, hq), :], p_b.astype(BF16), dims,
                               preferred_element_type=F32)
        acc_ref[...] = alpha * acc_ref[...] + pv_a
        acc_ref[:, tq:2 * tq] = acc_ref[:, tq:2 * tq] + pv_b
        m_ref[...] = m_new

    def accumulate(ki, shift, slot):
        s_ref, ml_ref = bufs[slot]
        start = pl.multiple_of(ki * tq, tq)
        vt = v_ref[0, pl.ds(start, tq), :]
        m_old = m_ref[...]
        m_new = jnp.maximum(m_old, ml_ref[...] + shift)
        alpha = jnp.exp2(m_old - m_new)
        p = jnp.exp2(s_ref[...] - (m_new - shift))
        l_ref[...] = alpha * l_ref[...] + jnp.sum(p, axis=0, keepdims=True)
        pv = lax.dot_general(vt, p.astype(BF16), (((0,), (0,)), ((), ())), preferred_element_type=F32)
        acc_ref[...] = alpha * acc_ref[...] + pv
        m_ref[...] = m_new

    n_far = qi - 1

    def score_next_diagonal():
        score_diagonal(nxt, qs_next)

    @pl.when(qi == 0)
    def _():
        score_diagonal(qi, qs)
        accumulate_diagonal(qi)
        score_next_diagonal()

    @pl.when(qi >= 1)
    def _():
        accumulate_diagonal(qi)
        score(qi - 1, bt_ref[0], 1)

    @pl.when(qi == 1)
    def _():
        accumulate(qi - 1, 0.0, 1)
        score_next_diagonal()

    @pl.when(qi >= 2)
    def _():
        accumulate(qi - 1, 0.0, 1)
        score(0, None, 0)
        trips = (n_far - 1) // 2

        def pair(j, c):
            accumulate(2 * j, cfar, 0)
            score(2 * j + 1, None, 1)
            accumulate(2 * j + 1, cfar, 1)
            score(2 * j + 2, None, 0)
            return c

        lax.fori_loop(0, trips, pair, 0)
        last = 2 * trips

        @pl.when(n_far - last == 2)
        def _():
            accumulate(last, cfar, 0)
            score(last + 1, None, 1)
            accumulate(last + 1, cfar, 1)
            score_next_diagonal()

        @pl.when(n_far - last == 1)
        def _():
            accumulate(last, cfar, 0)
            score_next_diagonal()

    acc = acc_ref[...] * (1.0 / l_ref[...])
    o_t = jnp.concatenate([acc[:, 0:hq] - lam * acc[:, hq:tq],
                           acc[:, tq:tq + hq] - lam * acc[:, tq + hq:2 * tq]], axis=1)
    ms = jnp.mean(o_t * o_t, axis=0, keepdims=True)
    y = (o_t * lax.rsqrt(ms + EPS)).T * (g_ref[...] * (1.0 - lambda_init))
    o_ref[0] = y.astype(o_ref.dtype)


def _diff_attention(proj3, scal, bias_vecs, gnorm, lambda_init):
    B, S, _ = proj3.shape
    H = N_DIFF_HEADS
    tq = ATT_TQ
    kern = functools.partial(_attn_kernel, lambda_init=lambda_init)
    return pl.pallas_call(
        kern,
        grid=(B, H, S // tq),
        in_specs=[
            pl.BlockSpec(memory_space=pltpu.SMEM),
            pl.BlockSpec((1, S, LANES), lambda b, h, i: (b, 0, h)),
            pl.BlockSpec((1, S, LANES), lambda b, h, i: (b, 0, H + h)),
            pl.BlockSpec((1, S, LANES), lambda b, h, i: (b, 0, 2 * H + h)),
            pl.BlockSpec((1, 2, 1, 2 * tq), lambda b, h, i: (h, 0, 0, 0)),
            pl.BlockSpec((1, LANES), lambda b, h, i: (0, h)),
        ],
        out_specs=pl.BlockSpec((1, tq, LANES), lambda b, h, i: (b, i, h)),
        out_shape=jax.ShapeDtypeStruct((B, S, H * LANES), BF16),
        scratch_shapes=[
            pltpu.VMEM((1, 2 * tq), F32),
            pltpu.VMEM((1, 2 * tq), F32),
            pltpu.VMEM((LANES, 2 * tq), F32),
            pltpu.VMEM((tq, 2 * tq), F32),
            pltpu.VMEM((tq, 2 * tq), F32),
            pltpu.VMEM((tq, 2 * tq), F32),
            pltpu.VMEM((1, 2 * tq), F32),
            pltpu.VMEM((1, 2 * tq), F32),
            pltpu.VMEM((1, 2 * tq), F32),
            pltpu.VMEM((2, tq, tq), F32),
        ],
        compiler_params=_cparams(("parallel", "parallel", "arbitrary")),
        name="diff_attention",
    )(scal, proj3, proj3, proj3, bias_vecs, gnorm)


def _log_sigmoid(x):
    return jnp.minimum(x, 0.0) - jnp.log(1.0 + jnp.exp(-jnp.abs(x)))


def _sigmoid(x):
    return 1.0 / (1.0 + jnp.exp(-x))


def _mlstm_kernel(q_ref, k_ref, v_ref, o_ref, gi_ref, gf_ref, cw_ref, cb_ref, bi_ref, bf_ref, gn_ref,
                  out_ref, qext_ref, kext_ref, ct_ref, n_ref, m_ref):
    sb = pl.program_id(1)
    L = CHUNK
    dh = MLSTM_HEAD_DIM
    H = N_MLSTM_HEADS
    ts = MLSTM_TS
    pad = 8

    @pl.when(sb == 0)
    def _():
        qext_ref[0:pad, :] = jnp.zeros((pad, H * dh), F32)
        kext_ref[0:pad, :] = jnp.zeros((pad, H * dh), F32)
        ct_ref[...] = jnp.zeros(ct_ref.shape, F32)
        n_ref[...] = jnp.zeros(n_ref.shape, F32)
        m_ref[...] = jnp.zeros(m_ref.shape, F32)

    qext_ref[pad:pad + ts, :] = q_ref[0].astype(F32)
    kext_ref[pad:pad + ts, :] = k_ref[0].astype(F32)

    row = lax.broadcasted_iota(jnp.int32, (L, L), 0)
    col = lax.broadcasted_iota(jnp.int32, (L, L), 1)
    tril = col <= row
    ltri = tril.astype(F32)

    def conv_silu(ext_ref, base, h, off):
        win = ext_ref[pl.ds(base, L + pad), h * dh:(h + 1) * dh]
        w = cw_ref[:, off + h * dh:off + (h + 1) * dh]
        y = cb_ref[:, off + h * dh:off + (h + 1) * dh]
        for j in range(CONV_WIDTH):
            lo = pad - (CONV_WIDTH - 1) + j
            y = y + w[j:j + 1, :] * win[lo:lo + L, :]
        return y * _sigmoid(y)

    def chunk_body(c, carry):
        base = pl.multiple_of(c * L, L)
        li = gi_ref[0, pl.ds(base, L), :] + bi_ref[...]
        logf = _log_sigmoid(gf_ref[0, pl.ds(base, L), :] + bf_ref[...])
        b = jnp.dot(ltri, logf, preferred_element_type=F32, precision=lax.Precision.HIGHEST)
        a = li - b
        g_row = b[L - 1:L, :]
        m_row = m_ref[...]
        m_new_row = g_row + jnp.maximum(m_row, jnp.max(a, axis=0, keepdims=True))
        a_t = a.T

        for h in range(H):
            qc = conv_silu(qext_ref, base, h, 0)
            kc = conv_silu(kext_ref, base, h, H * dh) * (dh ** -0.5)
            qb = qc.astype(BF16)
            kb = kc.astype(BF16)
            vb = v_ref[0, pl.ds(base, L), h * dh:(h + 1) * dh]

            a_row = a_t[h:h + 1, :]
            a_col = a[:, h:h + 1]
            b_col = b[:, h:h + 1]
            m_prev = m_row[:, h:h + 1]
            m_next = m_new_row[:, h:h + 1]
            g_h = g_row[:, h:h + 1]

            amat = jnp.where(tril, a_row, NEG_BIG)
            mcol = jnp.maximum(jnp.max(amat, axis=-1, keepdims=True), m_prev)
            wts = jnp.exp(amat - mcol)
            inter = jnp.exp(m_prev - mcol)

            s = lax.dot_general(qb, kb, (((1,), (1,)), ((), ())), preferred_element_type=F32)
            sqk = s * wts
            ct = ct_ref[h]
            nrow = n_ref[h:h + 1, :]
            num = (jnp.dot(sqk.astype(BF16), vb, preferred_element_type=F32)
                   + inter * jnp.dot(qb, ct.astype(BF16), preferred_element_type=F32))
            den = (jnp.sum(sqk, axis=-1, keepdims=True)
                   + inter * jnp.sum(qb.astype(F32) * nrow, axis=-1, keepdims=True))
            hv = num / jnp.maximum(jnp.abs(den), jnp.exp(-(b_col + mcol)))

            wt = jnp.exp(g_h + a_col - m_next)
            decay = jnp.exp(g_h + m_prev - m_next)
            wv = (wt * vb.astype(F32)).astype(BF16)
            ct_ref[h] = decay * ct + lax.dot_general(kb, wv, (((0,), (0,)), ((), ())),
                                                     preferred_element_type=F32)
            n_ref[h:h + 1, :] = decay * nrow + jnp.sum(wt * kb.astype(F32), axis=0, keepdims=True)

            ms = jnp.mean(hv * hv, axis=-1, keepdims=True)
            y = hv * lax.rsqrt(ms + EPS) * gn_ref[:, h * dh:(h + 1) * dh]
            og = o_ref[0, pl.ds(base, L), h * dh:(h + 1) * dh].astype(F32)
            out_ref[0, pl.ds(base, L), h * dh:(h + 1) * dh] = (y * _sigmoid(og)).astype(out_ref.dtype)

        m_ref[...] = m_new_row
        return carry

    lax.fori_loop(0, ts // L, chunk_body, 0, unroll=4)

    qext_ref[0:pad, :] = qext_ref[ts:ts + pad, :]
    kext_ref[0:pad, :] = kext_ref[ts:ts + pad, :]


def _mlstm(proj3, gates3, conv_w, conv_b, bi_row, bf_row, gnorm):
    B, S, _ = proj3.shape
    W = N_MLSTM_HEADS * MLSTM_HEAD_DIM
    ts = MLSTM_TS
    first = 3
    blk = lambda j: pl.BlockSpec((1, ts, W), lambda b, s: (b, s, j))
    full = lambda shape: pl.BlockSpec(shape, lambda b, s: (0,) * len(shape))
    return pl.pallas_call(
        _mlstm_kernel,
        grid=(B, S // ts),
        in_specs=[
            blk(first), blk(first + 1), blk(first + 2), blk(first + 3),
            pl.BlockSpec((1, ts, LANES), lambda b, s: (b, s, 0)),
            pl.BlockSpec((1, ts, LANES), lambda b, s: (b, s, 1)),
            full((CONV_WIDTH, 2 * W)), full((1, 2 * W)),
            full((1, LANES)), full((1, LANES)), full((1, W)),
        ],
        out_specs=pl.BlockSpec((1, ts, W), lambda b, s: (b, s, 0)),
        out_shape=jax.ShapeDtypeStruct((B, S, W), BF16),
        scratch_shapes=[
            pltpu.VMEM((ts + 8, W), F32),
            pltpu.VMEM((ts + 8, W), F32),
            pltpu.VMEM((N_MLSTM_HEADS, MLSTM_HEAD_DIM, MLSTM_HEAD_DIM), F32),
            pltpu.VMEM((N_MLSTM_HEADS, MLSTM_HEAD_DIM), F32),
            pltpu.VMEM((1, LANES), F32),
        ],
        compiler_params=_cparams(("parallel", "arbitrary")),
        name="mlstm",
    )(proj3, proj3, proj3, proj3, gates3, gates3, conv_w, conv_b, bi_row, bf_row, gnorm)


def _out_kernel(x_ref, a_ref, hm_ref, wa_ref, wm_ref, g_ref, wr_ref, x1_ref, h2_ref, lg_ref):
    y = (jnp.dot(a_ref[...], wa_ref[...].astype(BF16), preferred_element_type=F32)
         + jnp.dot(hm_ref[...], wm_ref[...].astype(BF16), preferred_element_type=F32))
    x1 = x_ref[...] + y
    x1_ref[...] = x1
    ms = jnp.mean(x1 * x1, axis=-1, keepdims=True)
    h2 = x1 * lax.rsqrt(ms + EPS) * g_ref[...]
    h2_ref[...] = _pack_bf16_pairs(h2)
    lg_ref[...] = jnp.dot(h2.astype(BF16), wr_ref[...], preferred_element_type=F32)


def _out_proj(x2, a2, hm2, wa, wm, g, wr):
    T, D = x2.shape
    W = a2.shape[1]
    tm = OUT_TM
    const = lambda shape: pl.BlockSpec(shape, lambda m: (0, 0), pipeline_mode=pl.Buffered(1))
    return pl.pallas_call(
        _out_kernel,
        grid=(T // tm,),
        in_specs=[
            pl.BlockSpec((tm, D), lambda m: (m, 0)),
            pl.BlockSpec((tm, W), lambda m: (m, 0)),
            pl.BlockSpec((tm, W), lambda m: (m, 0)),
            const((W, D)),
            pl.BlockSpec((W, D), lambda m: (1, 0), pipeline_mode=pl.Buffered(1)),
            const((1, D)), const((D, LANES)),
        ],
        out_specs=[
            pl.BlockSpec((tm, D), lambda m: (m, 0)),
            pl.BlockSpec((tm, D // 2), lambda m: (m, 0)),
            pl.BlockSpec((tm, LANES), lambda m: (m, 0)),
        ],
        out_shape=[
            jax.ShapeDtypeStruct((T, D), F32),
            jax.ShapeDtypeStruct((T, D // 2), jnp.uint32),
            jax.ShapeDtypeStruct((T, LANES), F32),
        ],
        compiler_params=_cparams(("parallel",)),
        name="out_proj_router",
    )(x2, a2, hm2, wa, wm, g, wr)


def _moe_kernel(te_ref, nv_ref, first_ref, slot_ref, nxt_ref, nxt2_ref, rows_ref, xs_ref, wg_hbm, wu_hbm,
                wd_hbm, ys_ref, wg_buf, wu_buf, wd_buf, sem):
    j = pl.program_id(0)
    valid = j < nv_ref[0]
    nbuf = wg_buf.shape[0]

    half_f = wd_buf.shape[1] // 2

    def weight_copies(e, s):
        lo, hi = pl.ds(0, half_f), pl.ds(half_f, half_f)
        return ((pltpu.make_async_copy(wg_hbm.at[e], wg_buf.at[s], sem.at[s, 0]), 0),
                (pltpu.make_async_copy(wu_hbm.at[e], wu_buf.at[s], sem.at[s, 1]), 1),
                (pltpu.make_async_copy(wd_hbm.at[e, lo], wd_buf.at[s, lo], sem.at[s, 2]), 0),
                (pltpu.make_async_copy(wd_hbm.at[e, hi], wd_buf.at[s, hi], sem.at[s, 3]), 1))

    @pl.when(j == 0)
    def _():
        for c, prio in weight_copies(te_ref[0], 0):
            c.start(priority=prio)

        @pl.when(nxt_ref[0] >= 0)
        def _():
            for c, prio in weight_copies(nxt_ref[0], 1):
                c.start(priority=prio)

    @pl.when(jnp.logical_and(valid, first_ref[j] == 1))
    def _():
        for c, _ in weight_copies(te_ref[j], slot_ref[j]):
            c.wait()

        @pl.when(nxt2_ref[j] >= 0)
        def _():
            for c, prio in weight_copies(nxt2_ref[j], (slot_ref[j] + 2) % nbuf):
                c.start(priority=prio)

    @pl.when(valid)
    def _():
        s = slot_ref[j]
        row = lax.broadcasted_iota(jnp.int32, xs_ref.shape, 0)
        lo, hi = _unpack_bf16_pairs(jnp.where(row < rows_ref[j], xs_ref[...], jnp.uint32(0)))
        xs = jnp.concatenate([lo.astype(BF16), hi.astype(BF16)], axis=1)
        gt = jnp.dot(xs, wg_buf[s].astype(BF16), preferred_element_type=F32)
        up = jnp.dot(xs, wu_buf[s].astype(BF16), preferred_element_type=F32)
        hid = (gt * _sigmoid(gt) * up).astype(BF16)
        ys_ref[...] = _pack_bf16_pairs(jnp.dot(hid, wd_buf[s].astype(BF16), preferred_element_type=F32))

    @pl.when(jnp.logical_not(valid))
    def _():
        ys_ref[...] = jnp.zeros(ys_ref.shape, ys_ref.dtype)


def _moe(tile_expert, n_valid, tile_first, tile_slot, tile_next, tile_next2, tile_rows, xs, wg, wu, wd):
    R, Dw = xs.shape
    D, F = wg.shape[1], wg.shape[2]
    tm = MOE_TM
    hbm = pl.BlockSpec(memory_space=pl.ANY)
    grid_spec = pltpu.PrefetchScalarGridSpec(
        num_scalar_prefetch=7,
        grid=(R // tm,),
        in_specs=[pl.BlockSpec((tm, Dw), lambda j, *_: (j, 0)), hbm, hbm, hbm],
        out_specs=pl.BlockSpec((tm, Dw), lambda j, *_: (j, 0)),
        scratch_shapes=[
            pltpu.VMEM((MOE_WBUF, D, F), wg.dtype),
            pltpu.VMEM((MOE_WBUF, D, F), wu.dtype),
            pltpu.VMEM((MOE_WBUF, F, D), wd.dtype),
            pltpu.SemaphoreType.DMA((MOE_WBUF, 4)),
        ],
    )
    return pl.pallas_call(
        _moe_kernel,
        grid_spec=grid_spec,
        out_shape=jax.ShapeDtypeStruct((R, Dw), jnp.uint32),
        compiler_params=_cparams(("arbitrary",)),
        name="moe_experts",
    )(tile_expert, n_valid, tile_first, tile_slot, tile_next, tile_next2, tile_rows, xs, wg, wu, wd)


def _final_kernel(x1_ref, y0_ref, y1_ref, cw_ref, g_ref, o_ref):
    cw = cw_ref[...]
    lo0, hi0 = _unpack_bf16_pairs(y0_ref[...])
    lo1, hi1 = _unpack_bf16_pairs(y1_ref[...])
    w0, w1 = cw[:, 0:1], cw[:, 1:2]
    y = jnp.concatenate([w0 * lo0 + w1 * lo1, w0 * hi0 + w1 * hi1], axis=1)
    x = x1_ref[...] + y
    ms = jnp.mean(x * x, axis=-1, keepdims=True)
    o_ref[...] = x * lax.rsqrt(ms + EPS) * g_ref[...]


def _final(x1, yw, cw, g):
    T, D = x1.shape
    tm = FIN_TM
    row = lambda w: pl.BlockSpec((tm, w), lambda m: (m, 0))
    slot1 = pl.BlockSpec((tm, D // 2), lambda m: (m + T // tm, 0))
    return pl.pallas_call(
        _final_kernel,
        grid=(T // tm,),
        in_specs=[row(D), row(D // 2), slot1, row(LANES), pl.BlockSpec((1, D), lambda m: (0, 0))],
        out_specs=row(D),
        out_shape=jax.ShapeDtypeStruct((T, D), F32),
        compiler_params=_cparams(("parallel",)),
        name="combine_final_norm",
    )(x1, yw, yw, cw, g)


SC_CORES, SC_SUBCORES = 2, 16
SC_CHUNK = 32


def _sc_gather_rows(table, idx):
    V, Dw = table.shape
    R = idx.shape[0]
    n_workers = SC_CORES * SC_SUBCORES
    ch = SC_CHUNK
    per_w = R // n_workers
    n_chunks = per_w // ch
    assert per_w * n_workers == R and n_chunks * ch == per_w and n_chunks % 2 == 0
    idx3 = idx.reshape(n_workers, n_chunks, ch)
    mesh = plsc.VectorSubcoreMesh(core_axis_name="c", subcore_axis_name="s")

    def body(table_hbm, idx_hbm, out_hbm, idx_v, rows_v, gsem, osem):
        wid = lax.axis_index("s") * SC_CORES + lax.axis_index("c")
        base = wid * per_w
        pltpu.sync_copy(idx_hbm.at[wid], idx_v)

        def gather(c, slot):
            return pltpu.make_async_copy(table_hbm.at[idx_v.at[c]], rows_v.at[slot], gsem.at[slot])

        def put(c, slot):
            return pltpu.make_async_copy(rows_v.at[slot], out_hbm.at[pl.ds(base + c * ch, ch)],
                                         osem.at[slot])

        gather(0, 0).start()

        @pl.loop(0, n_chunks, step=2)
        def _(c):
            @pl.when(c > 0)
            def _():
                put(c - 1, 1).wait()

            gather(c + 1, 1).start()
            gather(c, 0).wait()
            put(c, 0).start()
            put(c, 0).wait()

            @pl.when(c + 2 < n_chunks)
            def _():
                gather(c + 2, 0).start()

            gather(c + 1, 1).wait()
            put(c + 1, 1).start()

        put(n_chunks - 1, 1).wait()

    return pl.kernel(
        body,
        out_type=jax.ShapeDtypeStruct((R, Dw), table.dtype),
        mesh=mesh,
        scratch_types=[
            pltpu.VMEM((n_chunks, ch), jnp.int32),
            pltpu.VMEM((2, ch, Dw), table.dtype),
            pltpu.SemaphoreType.DMA((2,)),
            pltpu.SemaphoreType.DMA((2,)),
        ],
        name="sc_gather_rows",
    )(table, idx3)


def _sc_scatter_rows(table, idx, n_rows_out):
    V, Dw = table.shape
    K = idx.shape[0]
    n_workers = SC_CORES * SC_SUBCORES
    ch = SC_CHUNK
    per_w = V // n_workers
    n_chunks = per_w // ch
    assert K == 2 and per_w * n_workers == V and n_chunks * ch == per_w and n_chunks % 2 == 0
    idx4 = jnp.transpose(idx.reshape(K, n_workers, n_chunks, ch), (1, 0, 2, 3))
    mesh = plsc.VectorSubcoreMesh(core_axis_name="c", subcore_axis_name="s")

    def body(table_hbm, idx_hbm, out_hbm, idx_v, rows_v, lsem, ssem):
        wid = lax.axis_index("s") * SC_CORES + lax.axis_index("c")
        base = wid * per_w
        pltpu.sync_copy(idx_hbm.at[wid], idx_v)

        def load(c, slot):
            return pltpu.make_async_copy(table_hbm.at[pl.ds(base + c * ch, ch)], rows_v.at[slot],
                                         lsem.at[slot])

        def scatter(c, slot, k):
            return pltpu.make_async_copy(rows_v.at[slot], out_hbm.at[idx_v.at[k, c]], ssem.at[slot, k])

        load(0, 0).start()

        @pl.loop(0, n_chunks, step=2)
        def _(c):
            @pl.when(c > 0)
            def _():
                scatter(c - 1, 1, 0).wait()
                scatter(c - 1, 1, 1).wait()

            load(c + 1, 1).start()
            load(c, 0).wait()
            scatter(c, 0, 0).start()
            scatter(c, 0, 1).start()
            scatter(c, 0, 0).wait()
            scatter(c, 0, 1).wait()

            @pl.when(c + 2 < n_chunks)
            def _():
                load(c + 2, 0).start()

            load(c + 1, 1).wait()
            scatter(c + 1, 1, 0).start()
            scatter(c + 1, 1, 1).start()

        scatter(n_chunks - 1, 1, 0).wait()
        scatter(n_chunks - 1, 1, 1).wait()

    return pl.kernel(
        body,
        out_type=jax.ShapeDtypeStruct((n_rows_out, Dw), table.dtype),
        mesh=mesh,
        scratch_types=[
            pltpu.VMEM((K, n_chunks, ch), jnp.int32),
            pltpu.VMEM((2, ch, Dw), table.dtype),
            pltpu.SemaphoreType.DMA((2,)),
            pltpu.SemaphoreType.DMA((2, K)),
        ],
        name="sc_scatter_rows",
    )(table, idx4)


ROUTE_BLK = 256


def _route_kernel(lg_ref, bias_ref, pos_ref, cw_ref, cnt_ref, oh_ref, pre_ref, *, tm):
    T = lg_ref.shape[0]
    G, E, NE = N_GROUPS, EXPERTS_PER_GROUP, N_EXPERTS
    blk = ROUTE_BLK
    sub = lax.broadcasted_iota(jnp.int32, (LANES, blk), 0).astype(F32)
    r_i = lax.broadcasted_iota(jnp.int32, (blk, blk), 0)
    c_i = lax.broadcasted_iota(jnp.int32, (blk, blk), 1)
    before = (r_i < c_i).astype(BF16)

    def first_argmax(v):
        mx = jnp.max(v, axis=0, keepdims=True)
        return mx, jnp.min(jnp.where(v == mx, sub, float(LANES)), axis=0, keepdims=True)

    def phase1(b, carry):
        cols = pl.ds(pl.multiple_of(b * blk, blk), blk)
        x = (lg_ref[cols, :] + bias_ref[...]).T
        gl = jnp.where(sub < G, x, NEG_BIG)
        gmax, gsel = first_argmax(gl)
        gw = 1.0 / jnp.sum(jnp.exp(gl - gmax), axis=0, keepdims=True)
        lo = G + E * gsel
        el = jnp.where(jnp.logical_and(sub >= lo, sub < lo + E), x, NEG_BIG)
        v1, i1 = first_argmax(el)
        v2, i2 = first_argmax(jnp.where(sub == i1, NEG_BIG, el))
        e21 = jnp.exp(v2 - v1)
        w1 = 1.0 / (1.0 + e21)
        cw_t = jnp.where(sub == 0, gw * w1, jnp.where(sub == 1, gw * (e21 * w1), 0.0))
        cw_ref[cols, :] = cw_t.T
        oh = jnp.where(jnp.logical_or(sub == i1 - G, sub == i2 - G + NE), 1.0, 0.0)
        oh_ref[:, cols] = oh
        pre_ref[:, cols] = jnp.dot(oh.astype(BF16), before, preferred_element_type=F32) + carry
        return carry + jnp.sum(oh, axis=1, keepdims=True)

    counts = lax.fori_loop(0, T // blk, phase1, jnp.zeros((LANES, 1), F32))

    c0 = counts[0:NE, :]
    ctot = c0 + counts[NE:2 * NE, :]
    tiles = jnp.floor((ctot + (tm - 1)) * (1.0 / tm))
    e_r = lax.broadcasted_iota(jnp.int32, (NE, NE), 0)
    e_c = lax.broadcasted_iota(jnp.int32, (NE, NE), 1)
    upto = (e_c <= e_r).astype(BF16)
    tile_end = jnp.dot(upto, jnp.broadcast_to(tiles, (NE, LANES)).astype(BF16),
                       preferred_element_type=F32)[:, 0:1]
    row_start = (tile_end - tiles) * tm
    base = jnp.concatenate([row_start, row_start + c0, jnp.zeros((LANES - 2 * NE, 1), F32)], axis=0)
    lane_e = lax.broadcasted_iota(jnp.int32, (NE, LANES), 1)
    cnt_ref[...] = jnp.concatenate(
        [jnp.where(lane_e == 0, ctot, jnp.where(lane_e == 1, tiles, jnp.where(lane_e == 2, tile_end, 0.0))),
         jnp.zeros((LANES - NE, LANES), F32)], axis=0)

    def phase2(b, carry):
        cols = pl.ds(pl.multiple_of(b * blk, blk), blk)
        t = oh_ref[:, cols] * (pre_ref[:, cols] + base)
        p0 = jnp.sum(t[0:NE, :], axis=0, keepdims=True)
        p1 = jnp.sum(t[NE:2 * NE, :], axis=0, keepdims=True)
        pos_ref[:, cols] = jnp.concatenate([p0, p1, jnp.zeros((6, blk), F32)], axis=0).astype(jnp.int32)
        return carry

    lax.fori_loop(0, T // blk, phase2, 0)


def _route(logits, bias_row, tm):
    T = logits.shape[0]
    whole = lambda shape: pl.BlockSpec(shape, lambda: (0,) * len(shape))
    return pl.pallas_call(
        functools.partial(_route_kernel, tm=tm),
        in_specs=[whole((T, LANES)), whole((1, LANES))],
        out_specs=[whole((8, T)), whole((T, LANES)), whole((LANES, LANES))],
        out_shape=[
            jax.ShapeDtypeStruct((8, T), jnp.int32),
            jax.ShapeDtypeStruct((T, LANES), F32),
            jax.ShapeDtypeStruct((LANES, LANES), F32),
        ],
        scratch_shapes=[pltpu.VMEM((LANES, T), F32), pltpu.VMEM((LANES, T), F32)],
        compiler_params=pltpu.CompilerParams(vmem_limit_bytes=VMEM_LIMIT),
        name="route_positions",
    )(logits, bias_row)


def _tile_plan(cnt, tm, n_tiles):
    counts = cnt[:N_EXPERTS, 0].astype(jnp.int32)
    tiles_per_e = cnt[:N_EXPERTS, 1].astype(jnp.int32)
    tile_end = cnt[:N_EXPERTS, 2].astype(jnp.int32)
    n_valid = tile_end[-1]
    tile_ids = jnp.arange(n_tiles, dtype=jnp.int32)
    experts = jnp.arange(N_EXPERTS, dtype=jnp.int32)
    valid = tile_ids < n_valid
    jc = jnp.minimum(tile_ids, n_valid - 1)
    tile_expert = jnp.minimum(jnp.sum((tile_end[None, :] <= jc[:, None]).astype(jnp.int32), axis=1),
                              N_EXPERTS - 1)
    of_tile = (tile_expert[:, None] == experts[None, :]).astype(jnp.int32)
    pick = lambda per_expert: jnp.sum(of_tile * per_expert[None, :], axis=1)
    tile_start = tile_end - tiles_per_e
    nonempty = tiles_per_e > 0
    group_idx = jnp.sum(jnp.logical_and(nonempty[None, :], experts[None, :] <= experts[:, None]).astype(jnp.int32),
                        axis=1) - 1
    later = jnp.logical_and(nonempty[None, :], experts[None, :] > experts[:, None])
    next_e = jnp.min(jnp.where(later, experts[None, :], N_EXPERTS), axis=1)
    next_e = jnp.where(next_e == N_EXPERTS, -1, next_e)
    is_next = (next_e[:, None] == experts[None, :]).astype(jnp.int32)
    next2_e = jnp.where(next_e >= 0, jnp.sum(is_next * next_e[None, :], axis=1), -1)
    tile_first = jnp.logical_and(valid, tile_ids == pick(tile_start)).astype(jnp.int32)
    tile_slot = (pick(group_idx) % MOE_WBUF).astype(jnp.int32)
    tile_next = pick(next_e).astype(jnp.int32)
    tile_next2 = pick(next2_e).astype(jnp.int32)
    tile_rows = jnp.clip(pick(counts) - (tile_ids - pick(tile_start)) * tm, 0, tm)
    tile_rows = jnp.where(valid, tile_rows, 0).astype(jnp.int32)
    return (tile_expert.astype(jnp.int32), n_valid.reshape(1), tile_first, tile_slot, tile_next, tile_next2,
            tile_rows)


def kernel(x, rel_bias, ln_mix_g, w_in, conv_w, conv_b, b_i, b_f, lam_q1, lam_k1, lam_q2, lam_k2,
           diff_norm_g, mlstm_norm_g, w_out, ln_ffn_g, w_group, b_group, w_router, b_router,
           w_gate, w_up, w_down, ln_f_g):
    B, S, D = x.shape
    T = B * S
    depth = w_in.shape[0]
    assert depth == 1, "the final rmsnorm is fused into the single layer's combine kernel"
    Hm = N_MLSTM_HEADS
    n_main = w_in.shape[2] - 2 * Hm
    n_diff = N_DIFF_HEADS * 2 * DIFF_HEAD_DIM
    xf = x.reshape(T, D)

    for l in range(depth):
        lambda_init = 0.8 - 0.6 * math.exp(-0.3 * l)
        w_main = w_in[l, :, :n_main].astype(BF16)
        wgt = w_in[l, :, n_main:]
        w_gates = jnp.zeros((D, 2 * LANES), F32).at[:, :Hm].set(wgt[:, :Hm]).at[:, LANES:LANES + Hm].set(
            wgt[:, Hm:]).astype(BF16)
        bi_row = jnp.zeros((1, LANES), F32).at[0, :Hm].set(b_i[l].astype(F32))
        bf_row = jnp.zeros((1, LANES), F32).at[0, :Hm].set(b_f[l].astype(F32))
        lam = (jnp.exp(jnp.sum(lam_q1[l].astype(F32) * lam_k1[l].astype(F32)))
               - jnp.exp(jnp.sum(lam_q2[l].astype(F32) * lam_k2[l].astype(F32))) + lambda_init)
        tq = ATT_TQ
        assert tq >= MAX_DISTANCE and tq % CHUNK == 0
        rb = rel_bias.astype(F32)
        log2e = math.log2(math.e)
        xx = jnp.arange(2 * tq, dtype=jnp.int32)
        rel_vec = jnp.stack([-tq + tq - 1 - xx, tq - 1 - xx], axis=0)
        bias_vecs = jnp.take(rb, _t5_bucket(rel_vec), axis=0) * log2e
        bias_vecs = jnp.transpose(bias_vecs, (2, 0, 1))[:, :, None, :]
        cfar = rb[N_BUCKETS // 2 - 1] * log2e
        scal = jnp.concatenate([lam.reshape(1), cfar]).astype(F32)
        col_scale = jnp.ones((1, n_main), F32).at[:, :n_diff].set(DIFF_HEAD_DIM ** -0.5 * log2e)

        proj, gates = _proj(xf, ln_mix_g[l].reshape(1, D).astype(F32), w_main, col_scale, w_gates)
        proj3 = proj.reshape(B, S, n_main)
        a = _diff_attention(proj3, scal, bias_vecs, diff_norm_g[l].reshape(1, n_diff).astype(F32),
                            lambda_init)
        hm = _mlstm(proj3, gates.reshape(B, S, 2 * LANES), conv_w[l].astype(F32),
                    conv_b[l].reshape(1, -1).astype(F32), bi_row, bf_row,
                    mlstm_norm_g[l].reshape(1, -1).astype(F32))

        wo = w_out[l]
        G, E = N_GROUPS, EXPERTS_PER_GROUP
        wr = jnp.zeros((D, LANES), F32).at[:, :G].set(w_group[l].astype(F32)).at[:, G:G + G * E].set(
            jnp.transpose(w_router[l].astype(F32), (1, 0, 2)).reshape(D, G * E)).astype(BF16)
        x1, h2, logits = _out_proj(xf, a.reshape(T, n_diff), hm.reshape(T, -1), wo, wo,
                                   ln_ffn_g[l].reshape(1, D).astype(F32), wr)

        route_bias = jnp.concatenate([b_group[l].astype(F32).reshape(-1), b_router[l].astype(F32).reshape(-1),
                                      jnp.zeros((LANES - G - G * E,), F32)]).reshape(1, LANES)
        n_tiles = (T * TOP_K_INNER) // MOE_TM + N_EXPERTS
        pos8, cwp, cnt = _route(logits, route_bias, MOE_TM)
        pos_t = pos8[:TOP_K_INNER]
        tiles = _tile_plan(cnt, MOE_TM, n_tiles)
        xs = _sc_scatter_rows(h2, pos_t, n_tiles * MOE_TM)
        Fe = w_gate.shape[-1]
        ys = _moe(*tiles, xs, w_gate[l].reshape(N_EXPERTS, D, Fe),
                  w_up[l].reshape(N_EXPERTS, D, Fe), w_down[l].reshape(N_EXPERTS, Fe, D))
        yw = _sc_gather_rows(ys, pos_t.reshape(-1))
        xf = _final(x1, yw, cwp, ln_f_g.reshape(1, D).astype(F32))
    return xf.reshape(B, S, D)
```

```python
import functools
import math

import jax
import jax.numpy as jnp
from jax import lax
from jax.experimental import pallas as pl
from jax.experimental.pallas import tpu as pltpu
from jax.experimental.pallas import tpu_sc as plsc

F32 = jnp.float32
BF16 = jnp.bfloat16

EPS = 1e-6
CHUNK = 64
DIFF_HEAD_DIM = 64
N_DIFF_HEADS = 8
MLSTM_HEAD_DIM = 128
N_MLSTM_HEADS = 8
CONV_WIDTH = 4
N_BUCKETS = 32
MAX_DISTANCE = 128
N_GROUPS = 4
EXPERTS_PER_GROUP = 8
N_EXPERTS = N_GROUPS * EXPERTS_PER_GROUP
TOP_K_INNER = 2
LANES = 128
NEG_BIG = -1e30

VMEM_LIMIT = 56 * 1024 * 1024

PROJ_TM, PROJ_TN = 1024, 1792
ATT_TQ = 512
MLSTM_TS = 1024
OUT_TM = 512
MOE_TM = 256
MOE_WBUF = 3
FIN_TM = 512


def _cparams(sem):
    return pltpu.CompilerParams(dimension_semantics=sem, vmem_limit_bytes=VMEM_LIMIT)


_HI_MASK = 0xFFFF0000


def _pack_bf16_pairs(x):
    half = x.shape[-1] // 2
    xb = x.astype(BF16).astype(F32)
    lo = pltpu.bitcast(xb[:, :half], jnp.uint32)
    hi = pltpu.bitcast(xb[:, half:], jnp.uint32)
    return (hi & jnp.uint32(_HI_MASK)) | (lo >> 16)


def _unpack_bf16_pairs(w):
    lo = pltpu.bitcast(w << 16, F32)
    hi = pltpu.bitcast(w & jnp.uint32(_HI_MASK), F32)
    return lo, hi


def _proj_kernel(x_ref, g_ref, w_ref, cs_ref, wg_ref, o_ref, og_ref, h_ref):
    @pl.when(pl.program_id(1) == 0)
    def _():
        x = x_ref[...]
        ms = jnp.mean(x * x, axis=-1, keepdims=True)
        h = (x * lax.rsqrt(ms + EPS) * g_ref[...]).astype(BF16)
        h_ref[...] = h
        og_ref[...] = jnp.dot(h, wg_ref[...], preferred_element_type=F32)

    o_ref[...] = (jnp.dot(h_ref[...], w_ref[...], preferred_element_type=F32) * cs_ref[...]).astype(o_ref.dtype)


def _proj(x2, g, w_main, col_scale, w_gates):
    T, D = x2.shape
    N = w_main.shape[1]
    NG = w_gates.shape[1]
    return pl.pallas_call(
        _proj_kernel,
        grid=(T // PROJ_TM, N // PROJ_TN),
        in_specs=[
            pl.BlockSpec((PROJ_TM, D), lambda m, n: (m, 0)),
            pl.BlockSpec((1, D), lambda m, n: (0, 0)),
            pl.BlockSpec((D, PROJ_TN), lambda m, n: (0, n)),
            pl.BlockSpec((1, PROJ_TN), lambda m, n: (0, n)),
            pl.BlockSpec((D, NG), lambda m, n: (0, 0)),
        ],
        out_specs=[
            pl.BlockSpec((PROJ_TM, PROJ_TN), lambda m, n: (m, n)),
            pl.BlockSpec((PROJ_TM, NG), lambda m, n: (m, 0)),
        ],
        out_shape=[
            jax.ShapeDtypeStruct((T, N), BF16),
            jax.ShapeDtypeStruct((T, NG), F32),
        ],
        scratch_shapes=[pltpu.VMEM((PROJ_TM, D), BF16)],
        compiler_params=_cparams(("parallel", "arbitrary")),
        name="rms_in_proj",
    )(x2, g, w_main, col_scale, w_gates)


def _t5_bucket(rel):
    half = N_BUCKETS // 2
    max_exact = half // 2
    ret = jnp.where(rel > 0, half, 0)
    n = jnp.abs(rel)
    nf = jnp.maximum(n, 1).astype(F32)
    large = max_exact + (jnp.log(nf / max_exact) / math.log(MAX_DISTANCE / max_exact)
                         * (half - max_exact)).astype(jnp.int32)
    large = jnp.minimum(large, half - 1)
    return ret + jnp.where(n < max_exact, n, large)


def _attn_kernel(scal_ref, q_ref, k_ref, v_ref, bias_ref, g_ref, o_ref, m_ref, l_ref, acc_ref,
                 s0_ref, s1_ref, s2_ref, ml0_ref, ml1_ref, ml2_ref, bt_ref, *, lambda_init):
    h = pl.program_id(1)
    qi = pl.program_id(2)
    nq = pl.num_programs(2)
    tq = ATT_TQ
    lam = scal_ref[0]
    cfar = scal_ref[1 + h]

    hq = tq // 2

    def stacked_queries(tile):
        q = q_ref[0, pl.ds(pl.multiple_of(tile * tq, tq), tq), :]
        lane = lax.broadcasted_iota(jnp.int32, q.shape, 1)
        zero = jnp.zeros_like(q)
        q1 = jnp.where(lane < DIFF_HEAD_DIM, q, zero)
        q2 = jnp.where(lane >= DIFF_HEAD_DIM, q, zero)
        return jnp.concatenate([q1[:hq], q2[:hq], q1[hq:], q2[hq:]], axis=0)

    def stacked_bias(bias):
        return jnp.concatenate([bias[:, :hq], bias[:, :hq], bias[:, hq:], bias[:, hq:]], axis=1)

    qs = stacked_queries(qi)
    nxt = jnp.minimum(qi + 1, nq - 1)
    qs_next = stacked_queries(nxt)

    m_ref[...] = jnp.full(m_ref.shape, NEG_BIG, F32)
    l_ref[...] = jnp.zeros(l_ref.shape, F32)
    acc_ref[...] = jnp.zeros(acc_ref.shape, F32)

    @pl.when(qi == 0)
    def _():
        kj = lax.broadcasted_iota(jnp.int32, (tq, tq), 0)
        qq = lax.broadcasted_iota(jnp.int32, (tq, tq), 1)
        allowed = (kj // CHUNK) <= (qq // CHUNK)
        for d in range(2):
            rows = jnp.broadcast_to(bias_ref[0, d], (tq, 2 * tq))
            tile = pltpu.roll(rows, tq + 1, 1, stride=1, stride_axis=0)[:, :tq]
            if d == 1:
                tile = jnp.where(allowed, tile, NEG_BIG)
            bt_ref[d] = tile

    bufs = ((s0_ref, ml0_ref), (s1_ref, ml1_ref), (s2_ref, ml2_ref))

    def score(ki, bias, slot, queries=None):
        s_ref, ml_ref = bufs[slot]
        start = pl.multiple_of(ki * tq, tq)
        kt = k_ref[0, pl.ds(start, tq), :]
        s = lax.dot_general(kt, qs if queries is None else queries, (((1,), (1,)), ((), ())),
                            preferred_element_type=F32)
        if bias is not None:
            s = s + stacked_bias(bias)
        s_ref[...] = s
        ml_ref[...] = jnp.max(s, axis=0, keepdims=True)

    def score_diagonal(ki, queries):
        start = pl.multiple_of(ki * tq, tq)
        bias = stacked_bias(bt_ref[1])
        s_a = lax.dot_general(k_ref[0, pl.ds(start, hq), :], queries, (((1,), (1,)), ((), ())),
                              preferred_element_type=F32) + bias[:hq, :]
        s_b = lax.dot_general(k_ref[0, pl.ds(start + hq, hq), :], queries[tq:, :], (((1,), (1,)), ((), ())),
                              preferred_element_type=F32) + bias[hq:, tq:]
        s2_ref[0:hq, :] = s_a
        s2_ref[hq:tq, tq:2 * tq] = s_b
        ml_a = jnp.max(s_a, axis=0, keepdims=True)
        ml2_ref[:, 0:tq] = ml_a[:, :tq]
        ml2_ref[:, tq:2 * tq] = jnp.maximum(ml_a[:, tq:], jnp.max(s_b, axis=0, keepdims=True))

    def accumulate_diagonal(ki):
        start = pl.multiple_of(ki * tq, tq)
        m_old = m_ref[...]
        m_new = jnp.maximum(m_old, ml2_ref[...])
        alpha = jnp.exp2(m_old - m_new)
        p_a = jnp.exp2(s2_ref[0:hq, :] - m_new)
        p_b = jnp.exp2(s2_ref[hq:tq, tq:2 * tq] - m_new[:, tq:])
        l_ref[...] = alpha * l_ref[...] + jnp.sum(p_a, axis=0, keepdims=True)
        l_ref[:, tq:2 * tq] = l_ref[:, tq:2 * tq] + jnp.sum(p_b, axis=0, keepdims=True)
        dims = (((0,), (0,)), ((), ()))
        pv_a = lax.dot_general(v_ref[0, pl.ds(start, hq), :], p_a.astype(BF16), dims, preferred_element_type=F32)
        pv_b = lax.dot_general(v_ref[0, pl.ds(start + hq, hq), :], p_b.astype(BF16), dims,
                               preferred_element_type=F32)
        acc_ref[...] = alpha * acc_ref[...] + pv_a
        acc_ref[:, tq:2 * tq] = acc_ref[:, tq:2 * tq] + pv_b
        m_ref[...] = m_new

    def accumulate(ki, shift, slot):
        s_ref, ml_ref = bufs[slot]
        start = pl.multiple_of(ki * tq, tq)
        vt = v_ref[0, pl.ds(start, tq), :]
        m_old = m_ref[...]
        m_new = jnp.maximum(m_old, ml_ref[...] + shift)
        alpha = jnp.exp2(m_old - m_new)
        p = jnp.exp2(s_ref[...] - (m_new - shift))
        l_ref[...] = alpha * l_ref[...] + jnp.sum(p, axis=0, keepdims=True)
        pv = lax.dot_general(vt, p.astype(BF16), (((0,), (0,)), ((), ())), preferred_element_type=F32)
        acc_ref[...] = alpha * acc_ref[...] + pv
        m_ref[...] = m_new

    n_far = qi - 1

    def score_next_diagonal():
        score_diagonal(nxt, qs_next)

    @pl.when(qi == 0)
    def _():
        score_diagonal(qi, qs)
        accumulate_diagonal(qi)
        score_next_diagonal()

    @pl.when(qi >= 1)
    def _():
        accumulate_diagonal(qi)
        score(qi - 1, bt_ref[0], 1)

    @pl.when(qi == 1)
    def _():
        accumulate(qi - 1, 0.0, 1)
        score_next_diagonal()

    @pl.when(qi >= 2)
    def _():
        accumulate(qi - 1, 0.0, 1)
        score(0, None, 0)
        trips = (n_far - 1) // 2

        def pair(j, c):
            accumulate(2 * j, cfar, 0)
            score(2 * j + 1, None, 1)
            accumulate(2 * j + 1, cfar, 1)
            score(2 * j + 2, None, 0)
            return c

        lax.fori_loop(0, trips, pair, 0)
        last = 2 * trips

        @pl.when(n_far - last == 2)
        def _():
            accumulate(last, cfar, 0)
            score(last + 1, None, 1)
            accumulate(last + 1, cfar, 1)
            score_next_diagonal()

        @pl.when(n_far - last == 1)
        def _():
            accumulate(last, cfar, 0)
            score_next_diagonal()

    acc = acc_ref[...] * (1.0 / l_ref[...])
    o_t = jnp.concatenate([acc[:, 0:hq] - lam * acc[:, hq:tq],
                           acc[:, tq:tq + hq] - lam * acc[:, tq + hq:2 * tq]], axis=1)
    ms = jnp.mean(o_t * o_t, axis=0, keepdims=True)
    y = (o_t * lax.rsqrt(ms + EPS)).T * (g_ref[...] * (1.0 - lambda_init))
    o_ref[0] = y.astype(o_ref.dtype)


def _diff_attention(proj3, scal, bias_vecs, gnorm, lambda_init):
    B, S, _ = proj3.shape
    H = N_DIFF_HEADS
    tq = ATT_TQ
    kern = functools.partial(_attn_kernel, lambda_init=lambda_init)
    return pl.pallas_call(
        kern,
        grid=(B, H, S // tq),
        in_specs=[
            pl.BlockSpec(memory_space=pltpu.SMEM),
            pl.BlockSpec((1, S, LANES), lambda b, h, i: (b, 0, h)),
            pl.BlockSpec((1, S, LANES), lambda b, h, i: (b, 0, H + h)),
            pl.BlockSpec((1, S, LANES), lambda b, h, i: (b, 0, 2 * H + h)),
            pl.BlockSpec((1, 2, 1, 2 * tq), lambda b, h, i: (h, 0, 0, 0)),
            pl.BlockSpec((1, LANES), lambda b, h, i: (0, h)),
        ],
        out_specs=pl.BlockSpec((1, tq, LANES), lambda b, h, i: (b, i, h)),
        out_shape=jax.ShapeDtypeStruct((B, S, H * LANES), BF16),
        scratch_shapes=[
            pltpu.VMEM((1, 2 * tq), F32),
            pltpu.VMEM((1, 2 * tq), F32),
            pltpu.VMEM((LANES, 2 * tq), F32),
            pltpu.VMEM((tq, 2 * tq), F32),
            pltpu.VMEM((tq, 2 * tq), F32),
            pltpu.VMEM((tq, 2 * tq), F32),
            pltpu.VMEM((1, 2 * tq), F32),
            pltpu.VMEM((1, 2 * tq), F32),
            pltpu.VMEM((1, 2 * tq), F32),
            pltpu.VMEM((2, tq, tq), F32),
        ],
        compiler_params=_cparams(("parallel", "parallel", "arbitrary")),
        name="diff_attention",
    )(scal, proj3, proj3, proj3, bias_vecs, gnorm)


def _log_sigmoid(x):
    return jnp.minimum(x, 0.0) - jnp.log(1.0 + jnp.exp(-jnp.abs(x)))


def _sigmoid(x):
    return 1.0 / (1.0 + jnp.exp(-x))


def _mlstm_kernel(q_ref, k_ref, v_ref, o_ref, gi_ref, gf_ref, cw_ref, cb_ref, bi_ref, bf_ref, gn_ref,
                  out_ref, qext_ref, kext_ref, ct_ref, n_ref, m_ref):
    sb = pl.program_id(1)
    L = CHUNK
    dh = MLSTM_HEAD_DIM
    H = N_MLSTM_HEADS
    ts = MLSTM_TS
    pad = 8

    @pl.when(sb == 0)
    def _():
        qext_ref[0:pad, :] = jnp.zeros((pad, H * dh), F32)
        kext_ref[0:pad, :] = jnp.zeros((pad, H * dh), F32)
        ct_ref[...] = jnp.zeros(ct_ref.shape, F32)
        n_ref[...] = jnp.zeros(n_ref.shape, F32)
        m_ref[...] = jnp.zeros(m_ref.shape, F32)

    qext_ref[pad:pad + ts, :] = q_ref[0].astype(F32)
    kext_ref[pad:pad + ts, :] = k_ref[0].astype(F32)

    row = lax.broadcasted_iota(jnp.int32, (L, L), 0)
    col = lax.broadcasted_iota(jnp.int32, (L, L), 1)
    tril = col <= row
    ltri = tril.astype(F32)

    def conv_silu(ext_ref, base, h, off):
        win = ext_ref[pl.ds(base, L + pad), h * dh:(h + 1) * dh]
        w = cw_ref[:, off + h * dh:off + (h + 1) * dh]
        y = cb_ref[:, off + h * dh:off + (h + 1) * dh]
        for j in range(CONV_WIDTH):
            lo = pad - (CONV_WIDTH - 1) + j
            y = y + w[j:j + 1, :] * win[lo:lo + L, :]
        return y * _sigmoid(y)

    def chunk_body(c, carry):
        base = pl.multiple_of(c * L, L)
        li = gi_ref[0, pl.ds(base, L), :] + bi_ref[...]
        logf = _log_sigmoid(gf_ref[0, pl.ds(base, L), :] + bf_ref[...])
        b = jnp.dot(ltri, logf, preferred_element_type=F32, precision=lax.Precision.HIGHEST)
        a = li - b
        g_row = b[L - 1:L, :]
        m_row = m_ref[...]
        m_new_row = g_row + jnp.maximum(m_row, jnp.max(a, axis=0, keepdims=True))
        a_t = a.T

        lane_h = lax.broadcasted_iota(jnp.int32, (L, LANES), 1)
        cmax_all = jnp.full((L, LANES), NEG_BIG, F32)
        for h in range(H):
            cmax_h = jnp.max(jnp.where(tril, a_t[h:h + 1, :], NEG_BIG), axis=-1, keepdims=True)
            cmax_all = jnp.where(lane_h == h, cmax_h, cmax_all)
        mcol_all = jnp.maximum(cmax_all, m_row)
        inter_all = jnp.exp(m_row - mcol_all)
        floor_all = jnp.exp(-(b + mcol_all))
        wt_all = jnp.exp(g_row + a - m_new_row)
        decay_row = jnp.exp(g_row + m_row - m_new_row)

        for h in range(H):
            qc = conv_silu(qext_ref, base, h, 0)
            kc = conv_silu(kext_ref, base, h, H * dh) * (dh ** -0.5)
            qb = qc.astype(BF16)
            kb = kc.astype(BF16)
            vb = v_ref[0, pl.ds(base, L), h * dh:(h + 1) * dh]

            a_row = a_t[h:h + 1, :]
            amat = jnp.where(tril, a_row, NEG_BIG)
            wts = jnp.exp(amat - mcol_all[:, h:h + 1])
            inter = inter_all[:, h:h + 1]

            s = lax.dot_general(qb, kb, (((1,), (1,)), ((), ())), preferred_element_type=F32)
            sqk = s * wts
            ct = ct_ref[h]
            nrow = n_ref[h:h + 1, :]
            num = (jnp.dot(sqk.astype(BF16), vb, preferred_element_type=F32)
                   + inter * jnp.dot(qb, ct.astype(BF16), preferred_element_type=F32))
            den = (jnp.sum(sqk, axis=-1, keepdims=True)
                   + inter * jnp.sum(qc * nrow, axis=-1, keepdims=True))
            hv = num / jnp.maximum(jnp.abs(den), floor_all[:, h:h + 1])

            wt = wt_all[:, h:h + 1]
            decay = decay_row[:, h:h + 1]
            wv = (wt * vb.astype(F32)).astype(BF16)
            ct_ref[h] = decay * ct + lax.dot_general(kb, wv, (((0,), (0,)), ((), ())),
                                                     preferred_element_type=F32)
            n_ref[h:h + 1, :] = decay * nrow + jnp.sum(wt * kc, axis=0, keepdims=True)

            ms = jnp.mean(hv * hv, axis=-1, keepdims=True)
            y = hv * lax.rsqrt(ms + EPS) * gn_ref[:, h * dh:(h + 1) * dh]
            og = o_ref[0, pl.ds(base, L), h * dh:(h + 1) * dh].astype(F32)
            out_ref[0, pl.ds(base, L), h * dh:(h + 1) * dh] = (y * _sigmoid(og)).astype(out_ref.dtype)

        m_ref[...] = m_new_row
        return carry

    lax.fori_loop(0, ts // L, chunk_body, 0, unroll=4)

    qext_ref[0:pad, :] = qext_ref[ts:ts + pad, :]
    kext_ref[0:pad, :] = kext_ref[ts:ts + pad, :]


def _mlstm(proj3, gates3, conv_w, conv_b, bi_row, bf_row, gnorm):
    B, S, _ = proj3.shape
    W = N_MLSTM_HEADS * MLSTM_HEAD_DIM
    ts = MLSTM_TS
    first = 3
    blk = lambda j: pl.BlockSpec((1, ts, W), lambda b, s: (b, s, j))
    full = lambda shape: pl.BlockSpec(shape, lambda b, s: (0,) * len(shape))
    return pl.pallas_call(
        _mlstm_kernel,
        grid=(B, S // ts),
        in_specs=[
            blk(first), blk(first + 1), blk(first + 2), blk(first + 3),
            pl.BlockSpec((1, ts, LANES), lambda b, s: (b, s, 0)),
            pl.BlockSpec((1, ts, LANES), lambda b, s: (b, s, 1)),
            full((CONV_WIDTH, 2 * W)), full((1, 2 * W)),
            full((1, LANES)), full((1, LANES)), full((1, W)),
        ],
        out_specs=pl.BlockSpec((1, ts, W), lambda b, s: (b, s, 0)),
        out_shape=jax.ShapeDtypeStruct((B, S, W), BF16),
        scratch_shapes=[
            pltpu.VMEM((ts + 8, W), F32),
            pltpu.VMEM((ts + 8, W), F32),
            pltpu.VMEM((N_MLSTM_HEADS, MLSTM_HEAD_DIM, MLSTM_HEAD_DIM), F32),
            pltpu.VMEM((N_MLSTM_HEADS, MLSTM_HEAD_DIM), F32),
            pltpu.VMEM((1, LANES), F32),
        ],
        compiler_params=_cparams(("parallel", "arbitrary")),
        name="mlstm",
    )(proj3, proj3, proj3, proj3, gates3, gates3, conv_w, conv_b, bi_row, bf_row, gnorm)


def _out_kernel(x_ref, a_ref, hm_ref, wa_ref, wm_ref, g_ref, wr_ref, x1_ref, h2_ref, lg_ref):
    y = (jnp.dot(a_ref[...], wa_ref[...].astype(BF16), preferred_element_type=F32)
         + jnp.dot(hm_ref[...], wm_ref[...].astype(BF16), preferred_element_type=F32))
    x1 = x_ref[...] + y
    x1_ref[...] = x1
    ms = jnp.mean(x1 * x1, axis=-1, keepdims=True)
    h2 = x1 * lax.rsqrt(ms + EPS) * g_ref[...]
    h2_ref[...] = _pack_bf16_pairs(h2)
    lg_ref[...] = jnp.dot(h2.astype(BF16), wr_ref[...], preferred_element_type=F32)


def _out_proj(x2, a2, hm2, wa, wm, g, wr):
    T, D = x2.shape
    W = a2.shape[1]
    tm = OUT_TM
    const = lambda shape: pl.BlockSpec(shape, lambda m: (0, 0), pipeline_mode=pl.Buffered(1))
    return pl.pallas_call(
        _out_kernel,
        grid=(T // tm,),
        in_specs=[
            pl.BlockSpec((tm, D), lambda m: (m, 0)),
            pl.BlockSpec((tm, W), lambda m: (m, 0)),
            pl.BlockSpec((tm, W), lambda m: (m, 0)),
            const((W, D)),
            pl.BlockSpec((W, D), lambda m: (1, 0), pipeline_mode=pl.Buffered(1)),
            const((1, D)), const((D, LANES)),
        ],
        out_specs=[
            pl.BlockSpec((tm, D), lambda m: (m, 0)),
            pl.BlockSpec((tm, D // 2), lambda m: (m, 0)),
            pl.BlockSpec((tm, LANES), lambda m: (m, 0)),
        ],
        out_shape=[
            jax.ShapeDtypeStruct((T, D), F32),
            jax.ShapeDtypeStruct((T, D // 2), jnp.uint32),
            jax.ShapeDtypeStruct((T, LANES), F32),
        ],
        compiler_params=_cparams(("parallel",)),
        name="out_proj_router",
    )(x2, a2, hm2, wa, wm, g, wr)


def _moe_kernel(te_ref, nv_ref, first_ref, slot_ref, nxt_ref, nxt2_ref, rows_ref, xs_ref, wg_hbm, wu_hbm,
                wd_hbm, ys_ref, wg_buf, wu_buf, wd_buf, sem):
    j = pl.program_id(0)
    valid = j < nv_ref[0]
    nbuf = wg_buf.shape[0]

    half_f = wd_buf.shape[1] // 2

    def weight_copies(e, s):
        lo, hi = pl.ds(0, half_f), pl.ds(half_f, half_f)
        return ((pltpu.make_async_copy(wg_hbm.at[e], wg_buf.at[s], sem.at[s, 0]), 0),
                (pltpu.make_async_copy(wu_hbm.at[e], wu_buf.at[s], sem.at[s, 1]), 1),
                (pltpu.make_async_copy(wd_hbm.at[e, lo], wd_buf.at[s, lo], sem.at[s, 2]), 0),
                (pltpu.make_async_copy(wd_hbm.at[e, hi], wd_buf.at[s, hi], sem.at[s, 3]), 1))

    @pl.when(j == 0)
    def _():
        for c, prio in weight_copies(te_ref[0], 0):
            c.start(priority=prio)

        @pl.when(nxt_ref[0] >= 0)
        def _():
            for c, prio in weight_copies(nxt_ref[0], 1):
                c.start(priority=prio)

    @pl.when(jnp.logical_and(valid, first_ref[j] == 1))
    def _():
        for c, _ in weight_copies(te_ref[j], slot_ref[j]):
            c.wait()

        @pl.when(nxt2_ref[j] >= 0)
        def _():
            for c, prio in weight_copies(nxt2_ref[j], (slot_ref[j] + 2) % nbuf):
                c.start(priority=prio)

    @pl.when(valid)
    def _():
        s = slot_ref[j]
        row = lax.broadcasted_iota(jnp.int32, xs_ref.shape, 0)
        lo, hi = _unpack_bf16_pairs(jnp.where(row < rows_ref[j], xs_ref[...], jnp.uint32(0)))
        xs = jnp.concatenate([lo.astype(BF16), hi.astype(BF16)], axis=1)
        gt = jnp.dot(xs, wg_buf[s].astype(BF16), preferred_element_type=F32)
        up = jnp.dot(xs, wu_buf[s].astype(BF16), preferred_element_type=F32)
        hid = (gt * _sigmoid(gt) * up).astype(BF16)
        ys_ref[...] = _pack_bf16_pairs(jnp.dot(hid, wd_buf[s].astype(BF16), preferred_element_type=F32))

    @pl.when(jnp.logical_not(valid))
    def _():
        ys_ref[...] = jnp.zeros(ys_ref.shape, ys_ref.dtype)


def _moe(tile_expert, n_valid, tile_first, tile_slot, tile_next, tile_next2, tile_rows, xs, wg, wu, wd):
    R, Dw = xs.shape
    D, F = wg.shape[1], wg.shape[2]
    tm = MOE_TM
    hbm = pl.BlockSpec(memory_space=pl.ANY)
    grid_spec = pltpu.PrefetchScalarGridSpec(
        num_scalar_prefetch=7,
        grid=(R // tm,),
        in_specs=[pl.BlockSpec((tm, Dw), lambda j, *_: (j, 0)), hbm, hbm, hbm],
        out_specs=pl.BlockSpec((tm, Dw), lambda j, *_: (j, 0)),
        scratch_shapes=[
            pltpu.VMEM((MOE_WBUF, D, F), wg.dtype),
            pltpu.VMEM((MOE_WBUF, D, F), wu.dtype),
            pltpu.VMEM((MOE_WBUF, F, D), wd.dtype),
            pltpu.SemaphoreType.DMA((MOE_WBUF, 4)),
        ],
    )
    return pl.pallas_call(
        _moe_kernel,
        grid_spec=grid_spec,
        out_shape=jax.ShapeDtypeStruct((R, Dw), jnp.uint32),
        compiler_params=_cparams(("arbitrary",)),
        name="moe_experts",
    )(tile_expert, n_valid, tile_first, tile_slot, tile_next, tile_next2, tile_rows, xs, wg, wu, wd)


def _final_kernel(x1_ref, y0_ref, y1_ref, cw_ref, g_ref, o_ref):
    cw = cw_ref[...]
    lo0, hi0 = _unpack_bf16_pairs(y0_ref[...])
    lo1, hi1 = _unpack_bf16_pairs(y1_ref[...])
    w0, w1 = cw[:, 0:1], cw[:, 1:2]
    y = jnp.concatenate([w0 * lo0 + w1 * lo1, w0 * hi0 + w1 * hi1], axis=1)
    x = x1_ref[...] + y
    ms = jnp.mean(x * x, axis=-1, keepdims=True)
    o_ref[...] = x * lax.rsqrt(ms + EPS) * g_ref[...]


def _final(x1, yw, cw, g):
    T, D = x1.shape
    tm = FIN_TM
    row = lambda w: pl.BlockSpec((tm, w), lambda m: (m, 0))
    slot1 = pl.BlockSpec((tm, D // 2), lambda m: (m + T // tm, 0))
    return pl.pallas_call(
        _final_kernel,
        grid=(T // tm,),
        in_specs=[row(D), row(D // 2), slot1, row(LANES), pl.BlockSpec((1, D), lambda m: (0, 0))],
        out_specs=row(D),
        out_shape=jax.ShapeDtypeStruct((T, D), F32),
        compiler_params=_cparams(("parallel",)),
        name="combine_final_norm",
    )(x1, yw, yw, cw, g)


SC_CORES, SC_SUBCORES = 2, 16
SC_CHUNK = 32


def _sc_gather_rows(table, idx):
    V, Dw = table.shape
    R = idx.shape[0]
    n_workers = SC_CORES * SC_SUBCORES
    ch = SC_CHUNK
    per_w = R // n_workers
    n_chunks = per_w // ch
    assert per_w * n_workers == R and n_chunks * ch == per_w and n_chunks % 2 == 0
    idx3 = idx.reshape(n_workers, n_chunks, ch)
    mesh = plsc.VectorSubcoreMesh(core_axis_name="c", subcore_axis_name="s")

    def body(table_hbm, idx_hbm, out_hbm, idx_v, rows_v, gsem, osem):
        wid = lax.axis_index("s") * SC_CORES + lax.axis_index("c")
        base = wid * per_w
        pltpu.sync_copy(idx_hbm.at[wid], idx_v)

        def gather(c, slot):
            return pltpu.make_async_copy(table_hbm.at[idx_v.at[c]], rows_v.at[slot], gsem.at[slot])

        def put(c, slot):
            return pltpu.make_async_copy(rows_v.at[slot], out_hbm.at[pl.ds(base + c * ch, ch)],
                                         osem.at[slot])

        gather(0, 0).start()

        @pl.loop(0, n_chunks, step=2)
        def _(c):
            @pl.when(c > 0)
            def _():
                put(c - 1, 1).wait()

            gather(c + 1, 1).start()
            gather(c, 0).wait()
            put(c, 0).start()
            put(c, 0).wait()

            @pl.when(c + 2 < n_chunks)
            def _():
                gather(c + 2, 0).start()

            gather(c + 1, 1).wait()
            put(c + 1, 1).start()

        put(n_chunks - 1, 1).wait()

    return pl.kernel(
        body,
        out_type=jax.ShapeDtypeStruct((R, Dw), table.dtype),
        mesh=mesh,
        scratch_types=[
            pltpu.VMEM((n_chunks, ch), jnp.int32),
            pltpu.VMEM((2, ch, Dw), table.dtype),
            pltpu.SemaphoreType.DMA((2,)),
            pltpu.SemaphoreType.DMA((2,)),
        ],
        name="sc_gather_rows",
    )(table, idx3)


def _sc_scatter_rows(table, idx, n_rows_out):
    V, Dw = table.shape
    K = idx.shape[0]
    n_workers = SC_CORES * SC_SUBCORES
    ch = SC_CHUNK
    per_w = V // n_workers
    n_chunks = per_w // ch
    assert K == 2 and per_w * n_workers == V and n_chunks * ch == per_w and n_chunks % 2 == 0
    idx4 = jnp.transpose(idx.reshape(K, n_workers, n_chunks, ch), (1, 0, 2, 3))
    mesh = plsc.VectorSubcoreMesh(core_axis_name="c", subcore_axis_name="s")

    def body(table_hbm, idx_hbm, out_hbm, idx_v, rows_v, lsem, ssem):
        wid = lax.axis_index("s") * SC_CORES + lax.axis_index("c")
        base = wid * per_w
        pltpu.sync_copy(idx_hbm.at[wid], idx_v)

        def load(c, slot):
            return pltpu.make_async_copy(table_hbm.at[pl.ds(base + c * ch, ch)], rows_v.at[slot],
                                         lsem.at[slot])

        def scatter(c, slot, k):
            return pltpu.make_async_copy(rows_v.at[slot], out_hbm.at[idx_v.at[k, c]], ssem.at[slot, k])

        load(0, 0).start()

        @pl.loop(0, n_chunks, step=2)
        def _(c):
            @pl.when(c > 0)
            def _():
                scatter(c - 1, 1, 0).wait()
                scatter(c - 1, 1, 1).wait()

            load(c + 1, 1).start()
            load(c, 0).wait()
            scatter(c, 0, 0).start()
            scatter(c, 0, 1).start()
            scatter(c, 0, 0).wait()
            scatter(c, 0, 1).wait()

            @pl.when(c + 2 < n_chunks)
            def _():
                load(c + 2, 0).start()

            load(c + 1, 1).wait()
            scatter(c + 1, 1, 0).start()
            scatter(c + 1, 1, 1).start()

        scatter(n_chunks - 1, 1, 0).wait()
        scatter(n_chunks - 1, 1, 1).wait()

    return pl.kernel(
        body,
        out_type=jax.ShapeDtypeStruct((n_rows_out, Dw), table.dtype),
        mesh=mesh,
        scratch_types=[
            pltpu.VMEM((K, n_chunks, ch), jnp.int32),
            pltpu.VMEM((2, ch, Dw), table.dtype),
            pltpu.SemaphoreType.DMA((2,)),
            pltpu.SemaphoreType.DMA((2, K)),
        ],
        name="sc_scatter_rows",
    )(table, idx4)


ROUTE_BLK = 256


def _route_kernel(lg_ref, bias_ref, pos_ref, cw_ref, cnt_ref, oh_ref, pre_ref, *, tm):
    T = lg_ref.shape[0]
    G, E, NE = N_GROUPS, EXPERTS_PER_GROUP, N_EXPERTS
    blk = ROUTE_BLK
    sub = lax.broadcasted_iota(jnp.int32, (LANES, blk), 0).astype(F32)
    r_i = lax.broadcasted_iota(jnp.int32, (blk, blk), 0)
    c_i = lax.broadcasted_iota(jnp.int32, (blk, blk), 1)
    before = (r_i < c_i).astype(BF16)

    def first_argmax(v):
        mx = jnp.max(v, axis=0, keepdims=True)
        return mx, jnp.min(jnp.where(v == mx, sub, float(LANES)), axis=0, keepdims=True)

    def phase1(b, carry):
        cols = pl.ds(pl.multiple_of(b * blk, blk), blk)
        x = (lg_ref[cols, :] + bias_ref[...]).T
        gl = jnp.where(sub < G, x, NEG_BIG)
        gmax, gsel = first_argmax(gl)
        gw = 1.0 / jnp.sum(jnp.exp(gl - gmax), axis=0, keepdims=True)
        lo = G + E * gsel
        el = jnp.where(jnp.logical_and(sub >= lo, sub < lo + E), x, NEG_BIG)
        v1, i1 = first_argmax(el)
        v2, i2 = first_argmax(jnp.where(sub == i1, NEG_BIG, el))
        e21 = jnp.exp(v2 - v1)
        w1 = 1.0 / (1.0 + e21)
        cw_t = jnp.where(sub == 0, gw * w1, jnp.where(sub == 1, gw * (e21 * w1), 0.0))
        cw_ref[cols, :] = cw_t.T
        oh = jnp.where(jnp.logical_or(sub == i1 - G, sub == i2 - G + NE), 1.0, 0.0)
        oh_ref[:, cols] = oh
        pre_ref[:, cols] = jnp.dot(oh.astype(BF16), before, preferred_element_type=F32) + carry
        return carry + jnp.sum(oh, axis=1, keepdims=True)

    counts = lax.fori_loop(0, T // blk, phase1, jnp.zeros((LANES, 1), F32))

    c0 = counts[0:NE, :]
    ctot = c0 + counts[NE:2 * NE, :]
    tiles = jnp.floor((ctot + (tm - 1)) * (1.0 / tm))
    e_r = lax.broadcasted_iota(jnp.int32, (NE, NE), 0)
    e_c = lax.broadcasted_iota(jnp.int32, (NE, NE), 1)
    upto = (e_c <= e_r).astype(BF16)
    tile_end = jnp.dot(upto, jnp.broadcast_to(tiles, (NE, LANES)).astype(BF16),
                       preferred_element_type=F32)[:, 0:1]
    row_start = (tile_end - tiles) * tm
    base = jnp.concatenate([row_start, row_start + c0, jnp.zeros((LANES - 2 * NE, 1), F32)], axis=0)
    lane_e = lax.broadcasted_iota(jnp.int32, (NE, LANES), 1)
    cnt_ref[...] = jnp.concatenate(
        [jnp.where(lane_e == 0, ctot, jnp.where(lane_e == 1, tiles, jnp.where(lane_e == 2, tile_end, 0.0))),
         jnp.zeros((LANES - NE, LANES), F32)], axis=0)

    def phase2(b, carry):
        cols = pl.ds(pl.multiple_of(b * blk, blk), blk)
        t = oh_ref[:, cols] * (pre_ref[:, cols] + base)
        p0 = jnp.sum(t[0:NE, :], axis=0, keepdims=True)
        p1 = jnp.sum(t[NE:2 * NE, :], axis=0, keepdims=True)
        pos_ref[:, cols] = jnp.concatenate([p0, p1, jnp.zeros((6, blk), F32)], axis=0).astype(jnp.int32)
        return carry

    lax.fori_loop(0, T // blk, phase2, 0)


def _route(logits, bias_row, tm):
    T = logits.shape[0]
    whole = lambda shape: pl.BlockSpec(shape, lambda: (0,) * len(shape))
    return pl.pallas_call(
        functools.partial(_route_kernel, tm=tm),
        in_specs=[whole((T, LANES)), whole((1, LANES))],
        out_specs=[whole((8, T)), whole((T, LANES)), whole((LANES, LANES))],
        out_shape=[
            jax.ShapeDtypeStruct((8, T), jnp.int32),
            jax.ShapeDtypeStruct((T, LANES), F32),
            jax.ShapeDtypeStruct((LANES, LANES), F32),
        ],
        scratch_shapes=[pltpu.VMEM((LANES, T), F32), pltpu.VMEM((LANES, T), F32)],
        compiler_params=pltpu.CompilerParams(vmem_limit_bytes=VMEM_LIMIT),
        name="route_positions",
    )(logits, bias_row)


def _tile_plan(cnt, tm, n_tiles):
    counts = cnt[:N_EXPERTS, 0].astype(jnp.int32)
    tiles_per_e = cnt[:N_EXPERTS, 1].astype(jnp.int32)
    tile_end = cnt[:N_EXPERTS, 2].astype(jnp.int32)
    n_valid = tile_end[-1]
    tile_ids = jnp.arange(n_tiles, dtype=jnp.int32)
    experts = jnp.arange(N_EXPERTS, dtype=jnp.int32)
    valid = tile_ids < n_valid
    jc = jnp.minimum(tile_ids, n_valid - 1)
    tile_expert = jnp.minimum(jnp.sum((tile_end[None, :] <= jc[:, None]).astype(jnp.int32), axis=1),
                              N_EXPERTS - 1)
    of_tile = (tile_expert[:, None] == experts[None, :]).astype(jnp.int32)
    pick = lambda per_expert: jnp.sum(of_tile * per_expert[None, :], axis=1)
    tile_start = tile_end - tiles_per_e
    nonempty = tiles_per_e > 0
    group_idx = jnp.sum(jnp.logical_and(nonempty[None, :], experts[None, :] <= experts[:, None]).astype(jnp.int32),
                        axis=1) - 1
    later = jnp.logical_and(nonempty[None, :], experts[None, :] > experts[:, None])
    next_e = jnp.min(jnp.where(later, experts[None, :], N_EXPERTS), axis=1)
    next_e = jnp.where(next_e == N_EXPERTS, -1, next_e)
    is_next = (next_e[:, None] == experts[None, :]).astype(jnp.int32)
    next2_e = jnp.where(next_e >= 0, jnp.sum(is_next * next_e[None, :], axis=1), -1)
    tile_first = jnp.logical_and(valid, tile_ids == pick(tile_start)).astype(jnp.int32)
    tile_slot = (pick(group_idx) % MOE_WBUF).astype(jnp.int32)
    tile_next = pick(next_e).astype(jnp.int32)
    tile_next2 = pick(next2_e).astype(jnp.int32)
    tile_rows = jnp.clip(pick(counts) - (tile_ids - pick(tile_start)) * tm, 0, tm)
    tile_rows = jnp.where(valid, tile_rows, 0).astype(jnp.int32)
    return (tile_expert.astype(jnp.int32), n_valid.reshape(1), tile_first, tile_slot, tile_next, tile_next2,
            tile_rows)


def kernel(x, rel_bias, ln_mix_g, w_in, conv_w, conv_b, b_i, b_f, lam_q1, lam_k1, lam_q2, lam_k2,
           diff_norm_g, mlstm_norm_g, w_out, ln_ffn_g, w_group, b_group, w_router, b_router,
           w_gate, w_up, w_down, ln_f_g):
    B, S, D = x.shape
    T = B * S
    depth = w_in.shape[0]
    assert depth == 1, "the final rmsnorm is fused into the single layer's combine kernel"
    Hm = N_MLSTM_HEADS
    n_main = w_in.shape[2] - 2 * Hm
    n_diff = N_DIFF_HEADS * 2 * DIFF_HEAD_DIM
    xf = x.reshape(T, D)

    for l in range(depth):
        lambda_init = 0.8 - 0.6 * math.exp(-0.3 * l)
        w_main = w_in[l, :, :n_main].astype(BF16)
        wgt = w_in[l, :, n_main:]
        w_gates = jnp.zeros((D, 2 * LANES), F32).at[:, :Hm].set(wgt[:, :Hm]).at[:, LANES:LANES + Hm].set(
            wgt[:, Hm:]).astype(BF16)
        bi_row = jnp.zeros((1, LANES), F32).at[0, :Hm].set(b_i[l].astype(F32))
        bf_row = jnp.zeros((1, LANES), F32).at[0, :Hm].set(b_f[l].astype(F32))
        lam = (jnp.exp(jnp.sum(lam_q1[l].astype(F32) * lam_k1[l].astype(F32)))
               - jnp.exp(jnp.sum(lam_q2[l].astype(F32) * lam_k2[l].astype(F32))) + lambda_init)
        tq = ATT_TQ
        assert tq >= MAX_DISTANCE and tq % CHUNK == 0
        rb = rel_bias.astype(F32)
        log2e = math.log2(math.e)
        xx = jnp.arange(2 * tq, dtype=jnp.int32)
        rel_vec = jnp.stack([-tq + tq - 1 - xx, tq - 1 - xx], axis=0)
        bias_vecs = jnp.take(rb, _t5_bucket(rel_vec), axis=0) * log2e
        bias_vecs = jnp.transpose(bias_vecs, (2, 0, 1))[:, :, None, :]
        cfar = rb[N_BUCKETS // 2 - 1] * log2e
        scal = jnp.concatenate([lam.reshape(1), cfar]).astype(F32)
        col_scale = jnp.ones((1, n_main), F32).at[:, :n_diff].set(DIFF_HEAD_DIM ** -0.5 * log2e)

        proj, gates = _proj(xf, ln_mix_g[l].reshape(1, D).astype(F32), w_main, col_scale, w_gates)
        proj3 = proj.reshape(B, S, n_main)
        a = _diff_attention(proj3, scal, bias_vecs, diff_norm_g[l].reshape(1, n_diff).astype(F32),
                            lambda_init)
        hm = _mlstm(proj3, gates.reshape(B, S, 2 * LANES), conv_w[l].astype(F32),
                    conv_b[l].reshape(1, -1).astype(F32), bi_row, bf_row,
                    mlstm_norm_g[l].reshape(1, -1).astype(F32))

        wo = w_out[l]
        G, E = N_GROUPS, EXPERTS_PER_GROUP
        wr = jnp.zeros((D, LANES), F32).at[:, :G].set(w_group[l].astype(F32)).at[:, G:G + G * E].set(
            jnp.transpose(w_router[l].astype(F32), (1, 0, 2)).reshape(D, G * E)).astype(BF16)
        x1, h2, logits = _out_proj(xf, a.reshape(T, n_diff), hm.reshape(T, -1), wo, wo,
                                   ln_ffn_g[l].reshape(1, D).astype(F32), wr)

        route_bias = jnp.concatenate([b_group[l].astype(F32).reshape(-1), b_router[l].astype(F32).reshape(-1),
                                      jnp.zeros((LANES - G - G * E,), F32)]).reshape(1, LANES)
        n_tiles = (T * TOP_K_INNER) // MOE_TM + N_EXPERTS
        pos8, cwp, cnt = _route(logits, route_bias, MOE_TM)
        pos_t = pos8[:TOP_K_INNER]
        tiles = _tile_plan(cnt, MOE_TM, n_tiles)
        xs = _sc_scatter_rows(h2, pos_t, n_tiles * MOE_TM)
        Fe = w_gate.shape[-1]
        ys = _moe(*tiles, xs, w_gate[l].reshape(N_EXPERTS, D, Fe),
                  w_up[l].reshape(N_EXPERTS, D, Fe), w_down[l].reshape(N_EXPERTS, Fe, D))
        yw = _sc_gather_rows(ys, pos_t.reshape(-1))
        xf = _final(x1, yw, cwp, ln_f_g.reshape(1, D).astype(F32))
    return xf.reshape(B, S, D)
```

```python
import functools
import math

import jax
import jax.numpy as jnp
from jax import lax
from jax.experimental import pallas as pl
from jax.experimental.pallas import tpu as pltpu
from jax.experimental.pallas import tpu_sc as plsc

F32 = jnp.float32
BF16 = jnp.bfloat16

EPS = 1e-6
CHUNK = 64
DIFF_HEAD_DIM = 64
N_DIFF_HEADS = 8
MLSTM_HEAD_DIM = 128
N_MLSTM_HEADS = 8
CONV_WIDTH = 4
N_BUCKETS = 32
MAX_DISTANCE = 128
N_GROUPS = 4
EXPERTS_PER_GROUP = 8
N_EXPERTS = N_GROUPS * EXPERTS_PER_GROUP
TOP_K_INNER = 2
LANES = 128
NEG_BIG = -1e30

VMEM_LIMIT = 56 * 1024 * 1024

PROJ_TM, PROJ_TN = 1024, 1792
ATT_TQ = 512
MLSTM_TS = 1024
OUT_TM = 512
MOE_TM = 256
MOE_WBUF = 3
FIN_TM = 512


def _cparams(sem):
    return pltpu.CompilerParams(dimension_semantics=sem, vmem_limit_bytes=VMEM_LIMIT)


_HI_MASK = 0xFFFF0000


def _pack_bf16_pairs(x):
    half = x.shape[-1] // 2
    xb = x.astype(BF16).astype(F32)
    lo = pltpu.bitcast(xb[:, :half], jnp.uint32)
    hi = pltpu.bitcast(xb[:, half:], jnp.uint32)
    return (hi & jnp.uint32(_HI_MASK)) | (lo >> 16)


def _unpack_bf16_pairs(w):
    lo = pltpu.bitcast(w << 16, F32)
    hi = pltpu.bitcast(w & jnp.uint32(_HI_MASK), F32)
    return lo, hi


def _proj_kernel(x_ref, g_ref, w_ref, cs_ref, wg_ref, o_ref, og_ref, h_ref):
    @pl.when(pl.program_id(1) == 0)
    def _():
        x = x_ref[...]
        ms = jnp.mean(x * x, axis=-1, keepdims=True)
        h = (x * lax.rsqrt(ms + EPS) * g_ref[...]).astype(BF16)
        h_ref[...] = h
        og_ref[...] = jnp.dot(h, wg_ref[...], preferred_element_type=F32)

    o_ref[...] = (jnp.dot(h_ref[...], w_ref[...], preferred_element_type=F32) * cs_ref[...]).astype(o_ref.dtype)


def _proj(x2, g, w_main, col_scale, w_gates):
    T, D = x2.shape
    N = w_main.shape[1]
    NG = w_gates.shape[1]
    return pl.pallas_call(
        _proj_kernel,
        grid=(T // PROJ_TM, N // PROJ_TN),
        in_specs=[
            pl.BlockSpec((PROJ_TM, D), lambda m, n: (m, 0)),
            pl.BlockSpec((1, D), lambda m, n: (0, 0)),
            pl.BlockSpec((D, PROJ_TN), lambda m, n: (0, n)),
            pl.BlockSpec((1, PROJ_TN), lambda m, n: (0, n)),
            pl.BlockSpec((D, NG), lambda m, n: (0, 0)),
        ],
        out_specs=[
            pl.BlockSpec((PROJ_TM, PROJ_TN), lambda m, n: (m, n)),
            pl.BlockSpec((PROJ_TM, NG), lambda m, n: (m, 0)),
        ],
        out_shape=[
            jax.ShapeDtypeStruct((T, N), BF16),
            jax.ShapeDtypeStruct((T, NG), F32),
        ],
        scratch_shapes=[pltpu.VMEM((PROJ_TM, D), BF16)],
        compiler_params=_cparams(("parallel", "arbitrary")),
        name="rms_in_proj",
    )(x2, g, w_main, col_scale, w_gates)


def _t5_bucket(rel):
    half = N_BUCKETS // 2
    max_exact = half // 2
    ret = jnp.where(rel > 0, half, 0)
    n = jnp.abs(rel)
    nf = jnp.maximum(n, 1).astype(F32)
    large = max_exact + (jnp.log(nf / max_exact) / math.log(MAX_DISTANCE / max_exact)
                         * (half - max_exact)).astype(jnp.int32)
    large = jnp.minimum(large, half - 1)
    return ret + jnp.where(n < max_exact, n, large)


def _attn_kernel(scal_ref, q_ref, k_ref, v_ref, bias_ref, g_ref, o_ref, m_ref, l_ref, acc_ref,
                 s0_ref, s1_ref, s2_ref, ml0_ref, ml1_ref, ml2_ref, bt_ref, *, lambda_init):
    h = pl.program_id(1)
    qi = pl.program_id(2)
    nq = pl.num_programs(2)
    tq = ATT_TQ
    lam = scal_ref[0]
    cfar = scal_ref[1 + h]

    hq = tq // 2

    def stacked_queries(tile):
        q = q_ref[0, pl.ds(pl.multiple_of(tile * tq, tq), tq), :]
        lane = lax.broadcasted_iota(jnp.int32, q.shape, 1)
        zero = jnp.zeros_like(q)
        q1 = jnp.where(lane < DIFF_HEAD_DIM, q, zero)
        q2 = jnp.where(lane >= DIFF_HEAD_DIM, q, zero)
        return jnp.concatenate([q1[:hq], q2[:hq], q1[hq:], q2[hq:]], axis=0)

    def stacked_bias(bias):
        return jnp.concatenate([bias[:, :hq], bias[:, :hq], bias[:, hq:], bias[:, hq:]], axis=1)

    qs = stacked_queries(qi)
    nxt = jnp.minimum(qi + 1, nq - 1)
    qs_next = stacked_queries(nxt)

    m_ref[...] = jnp.full(m_ref.shape, NEG_BIG, F32)
    l_ref[...] = jnp.zeros(l_ref.shape, F32)
    acc_ref[...] = jnp.zeros(acc_ref.shape, F32)

    @pl.when(qi == 0)
    def _():
        kj = lax.broadcasted_iota(jnp.int32, (tq, tq), 0)
        qq = lax.broadcasted_iota(jnp.int32, (tq, tq), 1)
        allowed = (kj // CHUNK) <= (qq // CHUNK)
        for d in range(2):
            rows = jnp.broadcast_to(bias_ref[0, d], (tq, 2 * tq))
            tile = pltpu.roll(rows, tq + 1, 1, stride=1, stride_axis=0)[:, :tq]
            if d == 1:
                tile = jnp.where(allowed, tile, NEG_BIG)
            bt_ref[d] = tile

    bufs = ((s0_ref, ml0_ref), (s1_ref, ml1_ref), (s2_ref, ml2_ref))

    def score(ki, bias, slot, queries=None):
        s_ref, ml_ref = bufs[slot]
        start = pl.multiple_of(ki * tq, tq)
        kt = k_ref[0, pl.ds(start, tq), :]
        s = lax.dot_general(kt, qs if queries is None else queries, (((1,), (1,)), ((), ())),
                            preferred_element_type=F32)
        if bias is not None:
            s = s + stacked_bias(bias)
        s_ref[...] = s
        ml_ref[...] = jnp.max(s, axis=0, keepdims=True)

    def score_diagonal(ki, queries):
        start = pl.multiple_of(ki * tq, tq)
        bias = stacked_bias(bt_ref[1])
        s_a = lax.dot_general(k_ref[0, pl.ds(start, hq), :], queries, (((1,), (1,)), ((), ())),
                              preferred_element_type=F32) + bias[:hq, :]
        s_b = lax.dot_general(k_ref[0, pl.ds(start + hq, hq), :], queries[tq:, :], (((1,), (1,)), ((), ())),
                              preferred_element_type=F32) + bias[hq:, tq:]
        s2_ref[0:hq, :] = s_a
        s2_ref[hq:tq, tq:2 * tq] = s_b
        ml_a = jnp.max(s_a, axis=0, keepdims=True)
        ml2_ref[:, 0:tq] = ml_a[:, :tq]
        ml2_ref[:, tq:2 * tq] = jnp.maximum(ml_a[:, tq:], jnp.max(s_b, axis=0, keepdims=True))

    def accumulate_diagonal(ki):
        start = pl.multiple_of(ki * tq, tq)
        m_old = m_ref[...]
        m_new = jnp.maximum(m_old, ml2_ref[...])
        alpha = jnp.exp2(m_old - m_new)
        p_a = jnp.exp2(s2_ref[0:hq, :] - m_new)
        p_b = jnp.exp2(s2_ref[hq:tq, tq:2 * tq] - m_new[:, tq:])
        l_ref[...] = alpha * l_ref[...] + jnp.sum(p_a, axis=0, keepdims=True)
        l_ref[:, tq:2 * tq] = l_ref[:, tq:2 * tq] + jnp.sum(p_b, axis=0, keepdims=True)
        dims = (((0,), (0,)), ((), ()))
        pv_a = lax.dot_general(v_ref[0, pl.ds(start, hq), :], p_a.astype(BF16), dims, preferred_element_type=F32)
        pv_b = lax.dot_general(v_ref[0, pl.ds(start + hq, hq), :], p_b.astype(BF16), dims,
                               preferred_element_type=F32)
        acc_ref[...] = alpha * acc_ref[...] + pv_a
        acc_ref[:, tq:2 * tq] = acc_ref[:, tq:2 * tq] + pv_b
        m_ref[...] = m_new

    def accumulate(ki, shift, slot):
        s_ref, ml_ref = bufs[slot]
        start = pl.multiple_of(ki * tq, tq)
        vt = v_ref[0, pl.ds(start, tq), :]
        m_old = m_ref[...]
        m_new = jnp.maximum(m_old, ml_ref[...] + shift)
        alpha = jnp.exp2(m_old - m_new)
        p = jnp.exp2(s_ref[...] - (m_new - shift))
        l_ref[...] = alpha * l_ref[...] + jnp.sum(p, axis=0, keepdims=True)
        pv = lax.dot_general(vt, p.astype(BF16), (((0,), (0,)), ((), ())), preferred_element_type=F32)
        acc_ref[...] = alpha * acc_ref[...] + pv
        m_ref[...] = m_new

    n_far = qi - 1

    def score_next_diagonal():
        score_diagonal(nxt, qs_next)

    @pl.when(qi == 0)
    def _():
        score_diagonal(qi, qs)
        accumulate_diagonal(qi)
        score_next_diagonal()

    @pl.when(qi >= 1)
    def _():
        accumulate_diagonal(qi)
        score(qi - 1, bt_ref[0], 1)

    @pl.when(qi == 1)
    def _():
        accumulate(qi - 1, 0.0, 1)
        score_next_diagonal()

    @pl.when(qi >= 2)
    def _():
        accumulate(qi - 1, 0.0, 1)
        score(0, None, 0)
        trips = (n_far - 1) // 2

        def pair(j, c):
            accumulate(2 * j, cfar, 0)
            score(2 * j + 1, None, 1)
            accumulate(2 * j + 1, cfar, 1)
            score(2 * j + 2, None, 0)
            return c

        lax.fori_loop(0, trips, pair, 0)
        last = 2 * trips

        @pl.when(n_far - last == 2)
        def _():
            accumulate(last, cfar, 0)
            score(last + 1, None, 1)
            accumulate(last + 1, cfar, 1)
            score_next_diagonal()

        @pl.when(n_far - last == 1)
        def _():
            accumulate(last, cfar, 0)
            score_next_diagonal()

    acc = acc_ref[...] * (1.0 / l_ref[...])
    o_t = jnp.concatenate([acc[:, 0:hq] - lam * acc[:, hq:tq],
                           acc[:, tq:tq + hq] - lam * acc[:, tq + hq:2 * tq]], axis=1)
    ms = jnp.mean(o_t * o_t, axis=0, keepdims=True)
    y = (o_t * lax.rsqrt(ms + EPS)).T * (g_ref[...] * (1.0 - lambda_init))
    o_ref[0] = y.astype(o_ref.dtype)


def _diff_attention(proj3, scal, bias_vecs, gnorm, lambda_init):
    B, S, _ = proj3.shape
    H = N_DIFF_HEADS
    tq = ATT_TQ
    kern = functools.partial(_attn_kernel, lambda_init=lambda_init)
    return pl.pallas_call(
        kern,
        grid=(B, H, S // tq),
        in_specs=[
            pl.BlockSpec(memory_space=pltpu.SMEM),
            pl.BlockSpec((1, S, LANES), lambda b, h, i: (b, 0, h)),
            pl.BlockSpec((1, S, LANES), lambda b, h, i: (b, 0, H + h)),
            pl.BlockSpec((1, S, LANES), lambda b, h, i: (b, 0, 2 * H + h)),
            pl.BlockSpec((1, 2, 1, 2 * tq), lambda b, h, i: (h, 0, 0, 0)),
            pl.BlockSpec((1, LANES), lambda b, h, i: (0, h)),
        ],
        out_specs=pl.BlockSpec((1, tq, LANES), lambda b, h, i: (b, i, h)),
        out_shape=jax.ShapeDtypeStruct((B, S, H * LANES), BF16),
        scratch_shapes=[
            pltpu.VMEM((1, 2 * tq), F32),
            pltpu.VMEM((1, 2 * tq), F32),
            pltpu.VMEM((LANES, 2 * tq), F32),
            pltpu.VMEM((tq, 2 * tq), F32),
            pltpu.VMEM((tq, 2 * tq), F32),
            pltpu.VMEM((tq, 2 * tq), F32),
            pltpu.VMEM((1, 2 * tq), F32),
            pltpu.VMEM((1, 2 * tq), F32),
            pltpu.VMEM((1, 2 * tq), F32),
            pltpu.VMEM((2, tq, tq), F32),
        ],
        compiler_params=_cparams(("parallel", "parallel", "arbitrary")),
        name="diff_attention",
    )(scal, proj3, proj3, proj3, bias_vecs, gnorm)


def _log_sigmoid(x):
    return jnp.minimum(x, 0.0) - jnp.log(1.0 + jnp.exp(-jnp.abs(x)))


def _sigmoid(x):
    return 1.0 / (1.0 + jnp.exp(-x))


def _mlstm_kernel(q_ref, k_ref, v_ref, o_ref, gi_ref, gf_ref, cw_ref, cb_ref, bi_ref, bf_ref, gn_ref,
                  out_ref, qext_ref, kext_ref, ct_ref, n_ref, m_ref):
    sb = pl.program_id(1)
    L = CHUNK
    dh = MLSTM_HEAD_DIM
    H = N_MLSTM_HEADS
    ts = MLSTM_TS
    pad = 8

    @pl.when(sb == 0)
    def _():
        qext_ref[0:pad, :] = jnp.zeros((pad, H * dh), F32)
        kext_ref[0:pad, :] = jnp.zeros((pad, H * dh), F32)
        ct_ref[...] = jnp.zeros(ct_ref.shape, F32)
        n_ref[...] = jnp.zeros(n_ref.shape, F32)
        m_ref[...] = jnp.zeros(m_ref.shape, F32)

    qext_ref[pad:pad + ts, :] = q_ref[0].astype(F32)
    kext_ref[pad:pad + ts, :] = k_ref[0].astype(F32)

    row = lax.broadcasted_iota(jnp.int32, (L, L), 0)
    col = lax.broadcasted_iota(jnp.int32, (L, L), 1)
    tril = col <= row
    ltri = tril.astype(F32)

    def conv_silu(ext_ref, base, h, off):
        win = ext_ref[pl.ds(base, L + pad), h * dh:(h + 1) * dh]
        w = cw_ref[:, off + h * dh:off + (h + 1) * dh]
        y = cb_ref[:, off + h * dh:off + (h + 1) * dh]
        for j in range(CONV_WIDTH):
            lo = pad - (CONV_WIDTH - 1) + j
            y = y + w[j:j + 1, :] * win[lo:lo + L, :]
        return y * _sigmoid(y)

    def chunk_body(c, carry):
        base = pl.multiple_of(c * L, L)
        li = gi_ref[0, pl.ds(base, L), :] + bi_ref[...]
        logf = _log_sigmoid(gf_ref[0, pl.ds(base, L), :] + bf_ref[...])
        b = jnp.dot(ltri, logf, preferred_element_type=F32, precision=lax.Precision.HIGHEST)
        a = li - b
        g_row = b[L - 1:L, :]
        m_row = m_ref[...]
        m_new_row = g_row + jnp.maximum(m_row, jnp.max(a, axis=0, keepdims=True))
        a_t = a.T

        lane_h = lax.broadcasted_iota(jnp.int32, (L, LANES), 1)
        cmax_all = jnp.full((L, LANES), NEG_BIG, F32)
        for h in range(H):
            cmax_h = jnp.max(jnp.where(tril, a_t[h:h + 1, :], NEG_BIG), axis=-1, keepdims=True)
            cmax_all = jnp.where(lane_h == h, cmax_h, cmax_all)
        mcol_all = jnp.maximum(cmax_all, m_row)
        inter_all = jnp.exp(m_row - mcol_all)
        floor_all = jnp.exp(-(b + mcol_all))
        wt_all = jnp.exp(g_row + a - m_new_row)
        decay_row = jnp.exp(g_row + m_row - m_new_row)

        for h in range(H):
            qc = conv_silu(qext_ref, base, h, 0)
            kc = conv_silu(kext_ref, base, h, H * dh) * (dh ** -0.5)
            qb = qc.astype(BF16)
            kb = kc.astype(BF16)
            vb = v_ref[0, pl.ds(base, L), h * dh:(h + 1) * dh]

            a_row = a_t[h:h + 1, :]
            amat = jnp.where(tril, a_row, NEG_BIG)
            wts = jnp.exp(amat - mcol_all[:, h:h + 1])
            inter = inter_all[:, h:h + 1]

            s = lax.dot_general(qb, kb, (((1,), (1,)), ((), ())), preferred_element_type=F32)
            sqk = s * wts
            ct = ct_ref[h]
            nrow = n_ref[h:h + 1, :]
            num = (jnp.dot(sqk.astype(BF16), vb, preferred_element_type=F32)
                   + inter * jnp.dot(qb, ct.astype(BF16), preferred_element_type=F32))
            den = (jnp.sum(sqk, axis=-1, keepdims=True)
                   + inter * jnp.sum(qc * nrow, axis=-1, keepdims=True))
            hv = num / jnp.maximum(jnp.abs(den), floor_all[:, h:h + 1])

            wt = wt_all[:, h:h + 1]
            decay = decay_row[:, h:h + 1]
            wv = (wt * vb.astype(F32)).astype(BF16)
            ct_ref[h] = decay * ct + lax.dot_general(kb, wv, (((0,), (0,)), ((), ())),
                                                     preferred_element_type=F32)
            n_ref[h:h + 1, :] = decay * nrow + jnp.sum(wt * kc, axis=0, keepdims=True)

            ms = jnp.mean(hv * hv, axis=-1, keepdims=True)
            y = hv * lax.rsqrt(ms + EPS) * gn_ref[:, h * dh:(h + 1) * dh]
            og = o_ref[0, pl.ds(base, L), h * dh:(h + 1) * dh].astype(F32)
            out_ref[0, pl.ds(base, L), h * dh:(h + 1) * dh] = (y * _sigmoid(og)).astype(out_ref.dtype)

        m_ref[...] = m_new_row
        return carry

    lax.fori_loop(0, ts // L, chunk_body, 0, unroll=4)

    qext_ref[0:pad, :] = qext_ref[ts:ts + pad, :]
    kext_ref[0:pad, :] = kext_ref[ts:ts + pad, :]


def _mlstm(proj3, gates3, conv_w, conv_b, bi_row, bf_row, gnorm):
    B, S, _ = proj3.shape
    W = N_MLSTM_HEADS * MLSTM_HEAD_DIM
    ts = MLSTM_TS
    first = 3
    blk = lambda j: pl.BlockSpec((1, ts, W), lambda b, s: (b, s, j))
    full = lambda shape: pl.BlockSpec(shape, lambda b, s: (0,) * len(shape))
    return pl.pallas_call(
        _mlstm_kernel,
        grid=(B, S // ts),
        in_specs=[
            blk(first), blk(first + 1), blk(first + 2), blk(first + 3),
            pl.BlockSpec((1, ts, LANES), lambda b, s: (b, s, 0)),
            pl.BlockSpec((1, ts, LANES), lambda b, s: (b, s, 1)),
            full((CONV_WIDTH, 2 * W)), full((1, 2 * W)),
            full((1, LANES)), full((1, LANES)), full((1, W)),
        ],
        out_specs=pl.BlockSpec((1, ts, W), lambda b, s: (b, s, 0)),
        out_shape=jax.ShapeDtypeStruct((B, S, W), BF16),
        scratch_shapes=[
            pltpu.VMEM((ts + 8, W), F32),
            pltpu.VMEM((ts + 8, W), F32),
            pltpu.VMEM((N_MLSTM_HEADS, MLSTM_HEAD_DIM, MLSTM_HEAD_DIM), F32),
            pltpu.VMEM((N_MLSTM_HEADS, MLSTM_HEAD_DIM), F32),
            pltpu.VMEM((1, LANES), F32),
        ],
        compiler_params=_cparams(("parallel", "arbitrary")),
        name="mlstm",
    )(proj3, proj3, proj3, proj3, gates3, gates3, conv_w, conv_b, bi_row, bf_row, gnorm)


def _out_kernel(x_ref, a_ref, hm_ref, wa_ref, wm_ref, g_ref, wr_ref, x1_ref, h2_ref, lg_ref):
    y = (jnp.dot(a_ref[...], wa_ref[...].astype(BF16), preferred_element_type=F32)
         + jnp.dot(hm_ref[...], wm_ref[...].astype(BF16), preferred_element_type=F32))
    x1 = x_ref[...] + y
    x1_ref[...] = x1
    ms = jnp.mean(x1 * x1, axis=-1, keepdims=True)
    h2 = x1 * lax.rsqrt(ms + EPS) * g_ref[...]
    h2_ref[...] = _pack_bf16_pairs(h2)
    lg_ref[...] = jnp.dot(h2.astype(BF16), wr_ref[...], preferred_element_type=F32)


def _out_proj(x2, a2, hm2, wa, wm, g, wr):
    T, D = x2.shape
    W = a2.shape[1]
    tm = OUT_TM
    const = lambda shape: pl.BlockSpec(shape, lambda m: (0, 0), pipeline_mode=pl.Buffered(1))
    return pl.pallas_call(
        _out_kernel,
        grid=(T // tm,),
        in_specs=[
            pl.BlockSpec((tm, D), lambda m: (m, 0)),
            pl.BlockSpec((tm, W), lambda m: (m, 0)),
            pl.BlockSpec((tm, W), lambda m: (m, 0)),
            const((W, D)),
            pl.BlockSpec((W, D), lambda m: (1, 0), pipeline_mode=pl.Buffered(1)),
            const((1, D)), const((D, LANES)),
        ],
        out_specs=[
            pl.BlockSpec((tm, D), lambda m: (m, 0)),
            pl.BlockSpec((tm, D // 2), lambda m: (m, 0)),
            pl.BlockSpec((tm, LANES), lambda m: (m, 0)),
        ],
        out_shape=[
            jax.ShapeDtypeStruct((T, D), F32),
            jax.ShapeDtypeStruct((T, D // 2), jnp.uint32),
            jax.ShapeDtypeStruct((T, LANES), F32),
        ],
        compiler_params=_cparams(("parallel",)),
        name="out_proj_router",
    )(x2, a2, hm2, wa, wm, g, wr)


def _moe_kernel(te_ref, nv_ref, first_ref, slot_ref, nxt_ref, nxt2_ref, rows_ref, xs_ref, wg_hbm, wu_hbm,
                wd_hbm, ys_ref, wg_buf, wu_buf, wd_buf, sem):
    j = pl.program_id(0)
    valid = j < nv_ref[0]
    nbuf = wg_buf.shape[0]

    half_f = wd_buf.shape[1] // 2

    def weight_copies(e, s):
        lo, hi = pl.ds(0, half_f), pl.ds(half_f, half_f)
        return ((pltpu.make_async_copy(wg_hbm.at[e], wg_buf.at[s], sem.at[s, 0]), 0),
                (pltpu.make_async_copy(wu_hbm.at[e], wu_buf.at[s], sem.at[s, 1]), 1),
                (pltpu.make_async_copy(wd_hbm.at[e, lo], wd_buf.at[s, lo], sem.at[s, 2]), 0),
                (pltpu.make_async_copy(wd_hbm.at[e, hi], wd_buf.at[s, hi], sem.at[s, 3]), 1))

    @pl.when(j == 0)
    def _():
        for c, prio in weight_copies(te_ref[0], 0):
            c.start(priority=prio)

        @pl.when(nxt_ref[0] >= 0)
        def _():
            for c, prio in weight_copies(nxt_ref[0], 1):
                c.start(priority=prio)

    @pl.when(jnp.logical_and(valid, first_ref[j] == 1))
    def _():
        for c, _ in weight_copies(te_ref[j], slot_ref[j]):
            c.wait()

        @pl.when(nxt2_ref[j] >= 0)
        def _():
            for c, prio in weight_copies(nxt2_ref[j], (slot_ref[j] + 2) % nbuf):
                c.start(priority=prio)

    @pl.when(valid)
    def _():
        s = slot_ref[j]
        row = lax.broadcasted_iota(jnp.int32, xs_ref.shape, 0)
        lo, hi = _unpack_bf16_pairs(jnp.where(row < rows_ref[j], xs_ref[...], jnp.uint32(0)))
        xs = jnp.concatenate([lo.astype(BF16), hi.astype(BF16)], axis=1)
        gt = jnp.dot(xs, wg_buf[s].astype(BF16), preferred_element_type=F32)
        up = jnp.dot(xs, wu_buf[s].astype(BF16), preferred_element_type=F32)
        hid = (gt * _sigmoid(gt) * up).astype(BF16)
        ys_ref[...] = _pack_bf16_pairs(jnp.dot(hid, wd_buf[s].astype(BF16), preferred_element_type=F32))

    @pl.when(jnp.logical_not(valid))
    def _():
        ys_ref[...] = jnp.zeros(ys_ref.shape, ys_ref.dtype)


def _moe(tile_expert, n_valid, tile_first, tile_slot, tile_next, tile_next2, tile_rows, xs, wg, wu, wd):
    R, Dw = xs.shape
    D, F = wg.shape[1], wg.shape[2]
    tm = MOE_TM
    hbm = pl.BlockSpec(memory_space=pl.ANY)
    grid_spec = pltpu.PrefetchScalarGridSpec(
        num_scalar_prefetch=7,
        grid=(R // tm,),
        in_specs=[pl.BlockSpec((tm, Dw), lambda j, *_: (j, 0)), hbm, hbm, hbm],
        out_specs=pl.BlockSpec((tm, Dw), lambda j, *_: (j, 0)),
        scratch_shapes=[
            pltpu.VMEM((MOE_WBUF, D, F), wg.dtype),
            pltpu.VMEM((MOE_WBUF, D, F), wu.dtype),
            pltpu.VMEM((MOE_WBUF, F, D), wd.dtype),
            pltpu.SemaphoreType.DMA((MOE_WBUF, 4)),
        ],
    )
    return pl.pallas_call(
        _moe_kernel,
        grid_spec=grid_spec,
        out_shape=jax.ShapeDtypeStruct((R, Dw), jnp.uint32),
        compiler_params=_cparams(("arbitrary",)),
        name="moe_experts",
    )(tile_expert, n_valid, tile_first, tile_slot, tile_next, tile_next2, tile_rows, xs, wg, wu, wd)


def _final_kernel(x1_ref, y0_ref, y1_ref, cw_ref, g_ref, o_ref):
    cw = cw_ref[...]
    lo0, hi0 = _unpack_bf16_pairs(y0_ref[...])
    lo1, hi1 = _unpack_bf16_pairs(y1_ref[...])
    w0, w1 = cw[:, 0:1], cw[:, 1:2]
    y = jnp.concatenate([w0 * lo0 + w1 * lo1, w0 * hi0 + w1 * hi1], axis=1)
    x = x1_ref[...] + y
    ms = jnp.mean(x * x, axis=-1, keepdims=True)
    o_ref[...] = x * lax.rsqrt(ms + EPS) * g_ref[...]


def _final(x1, yw, cw, g):
    T, D = x1.shape
    tm = FIN_TM
    row = lambda w: pl.BlockSpec((tm, w), lambda m: (m, 0))
    slot1 = pl.BlockSpec((tm, D // 2), lambda m: (m + T // tm, 0))
    return pl.pallas_call(
        _final_kernel,
        grid=(T // tm,),
        in_specs=[row(D), row(D // 2), slot1, row(LANES), pl.BlockSpec((1, D), lambda m: (0, 0))],
        out_specs=row(D),
        out_shape=jax.ShapeDtypeStruct((T, D), F32),
        compiler_params=_cparams(("parallel",)),
        name="combine_final_norm",
    )(x1, yw, yw, cw, g)


SC_CORES, SC_SUBCORES = 2, 16
SC_CHUNK = 32


def _sc_gather_rows(table, idx):
    V, Dw = table.shape
    R = idx.shape[0]
    n_workers = SC_CORES * SC_SUBCORES
    ch = SC_CHUNK
    per_w = R // n_workers
    n_chunks = per_w // ch
    assert per_w * n_workers == R and n_chunks * ch == per_w and n_chunks % 2 == 0
    idx3 = idx.reshape(n_workers, n_chunks, ch)
    mesh = plsc.VectorSubcoreMesh(core_axis_name="c", subcore_axis_name="s")

    def body(table_hbm, idx_hbm, out_hbm, idx_v, rows_v, gsem, osem):
        wid = lax.axis_index("s") * SC_CORES + lax.axis_index("c")
        base = wid * per_w
        pltpu.sync_copy(idx_hbm.at[wid], idx_v)

        def gather(c, slot):
            return pltpu.make_async_copy(table_hbm.at[idx_v.at[c]], rows_v.at[slot], gsem.at[slot])

        def put(c, slot):
            return pltpu.make_async_copy(rows_v.at[slot], out_hbm.at[pl.ds(base + c * ch, ch)],
                                         osem.at[slot])

        gather(0, 0).start()

        @pl.loop(0, n_chunks, step=2)
        def _(c):
            @pl.when(c > 0)
            def _():
                put(c - 1, 1).wait()

            gather(c + 1, 1).start()
            gather(c, 0).wait()
            put(c, 0).start()
            put(c, 0).wait()

            @pl.when(c + 2 < n_chunks)
            def _():
                gather(c + 2, 0).start()

            gather(c + 1, 1).wait()
            put(c + 1, 1).start()

        put(n_chunks - 1, 1).wait()

    return pl.kernel(
        body,
        out_type=jax.ShapeDtypeStruct((R, Dw), table.dtype),
        mesh=mesh,
        scratch_types=[
            pltpu.VMEM((n_chunks, ch), jnp.int32),
            pltpu.VMEM((2, ch, Dw), table.dtype),
            pltpu.SemaphoreType.DMA((2,)),
            pltpu.SemaphoreType.DMA((2,)),
        ],
        name="sc_gather_rows",
    )(table, idx3)


def _sc_scatter_rows(table, idx, n_rows_out):
    V, Dw = table.shape
    K = idx.shape[0]
    n_workers = SC_CORES * SC_SUBCORES
    ch = SC_CHUNK
    per_w = V // n_workers
    n_chunks = per_w // ch
    assert K == 2 and per_w * n_workers == V and n_chunks * ch == per_w and n_chunks % 2 == 0
    idx4 = jnp.transpose(idx.reshape(K, n_workers, n_chunks, ch), (1, 0, 2, 3))
    mesh = plsc.VectorSubcoreMesh(core_axis_name="c", subcore_axis_name="s")

    def body(table_hbm, idx_hbm, out_hbm, idx_v, rows_v, lsem, ssem):
        wid = lax.axis_index("s") * SC_CORES + lax.axis_index("c")
        base = wid * per_w
        pltpu.sync_copy(idx_hbm.at[wid], idx_v)

        def load(c, slot):
            return pltpu.make_async_copy(table_hbm.at[pl.ds(base + c * ch, ch)], rows_v.at[slot],
                                         lsem.at[slot])

        def scatter(c, slot, k):
            return pltpu.make_async_copy(rows_v.at[slot], out_hbm.at[idx_v.at[k, c]], ssem.at[slot, k])

        load(0, 0).start()

        @pl.loop(0, n_chunks, step=2)
        def _(c):
            @pl.when(c > 0)
            def _():
                scatter(c - 1, 1, 0).wait()
                scatter(c - 1, 1, 1).wait()

            load(c + 1, 1).start()
            load(c, 0).wait()
            scatter(c, 0, 0).start()
            scatter(c, 0, 1).start()
            scatter(c, 0, 0).wait()
            scatter(c, 0, 1).wait()

            @pl.when(c + 2 < n_chunks)
            def _():
                load(c + 2, 0).start()

            load(c + 1, 1).wait()
            scatter(c + 1, 1, 0).start()
            scatter(c + 1, 1, 1).start()

        scatter(n_chunks - 1, 1, 0).wait()
        scatter(n_chunks - 1, 1, 1).wait()

    return pl.kernel(
        body,
        out_type=jax.ShapeDtypeStruct((n_rows_out, Dw), table.dtype),
        mesh=mesh,
        scratch_types=[
            pltpu.VMEM((K, n_chunks, ch), jnp.int32),
            pltpu.VMEM((2, ch, Dw), table.dtype),
            pltpu.SemaphoreType.DMA((2,)),
            pltpu.SemaphoreType.DMA((2, K)),
        ],
        name="sc_scatter_rows",
    )(table, idx4)


ROUTE_BLK = 1024


def _route_kernel(lg_ref, bias_ref, pos_ref, cw_ref, cnt_ref, oh_ref, pre_ref, *, tm):
    T = lg_ref.shape[0]
    G, E, NE = N_GROUPS, EXPERTS_PER_GROUP, N_EXPERTS
    blk = ROUTE_BLK
    sub = lax.broadcasted_iota(jnp.int32, (LANES, blk), 0).astype(F32)
    r_i = lax.broadcasted_iota(jnp.int32, (blk, blk), 0)
    c_i = lax.broadcasted_iota(jnp.int32, (blk, blk), 1)
    before = (r_i < c_i).astype(BF16)

    def first_argmax(v):
        mx = jnp.max(v, axis=0, keepdims=True)
        return mx, jnp.min(jnp.where(v == mx, sub, float(LANES)), axis=0, keepdims=True)

    def phase1(b, carry):
        cols = pl.ds(pl.multiple_of(b * blk, blk), blk)
        x = (lg_ref[cols, :] + bias_ref[...]).T
        gl = jnp.where(sub < G, x, NEG_BIG)
        gmax, gsel = first_argmax(gl)
        gw = 1.0 / jnp.sum(jnp.exp(gl - gmax), axis=0, keepdims=True)
        lo = G + E * gsel
        el = jnp.where(jnp.logical_and(sub >= lo, sub < lo + E), x, NEG_BIG)
        v1, i1 = first_argmax(el)
        v2, i2 = first_argmax(jnp.where(sub == i1, NEG_BIG, el))
        e21 = jnp.exp(v2 - v1)
        w1 = 1.0 / (1.0 + e21)
        cw_t = jnp.where(sub == 0, gw * w1, jnp.where(sub == 1, gw * (e21 * w1), 0.0))
        cw_ref[cols, :] = cw_t.T
        oh = jnp.where(jnp.logical_or(sub == i1 - G, sub == i2 - G + NE), 1.0, 0.0)
        oh_ref[:, cols] = oh
        pre_ref[:, cols] = jnp.dot(oh.astype(BF16), before, preferred_element_type=F32) + carry
        return carry + jnp.sum(oh, axis=1, keepdims=True)

    counts = lax.fori_loop(0, T // blk, phase1, jnp.zeros((LANES, 1), F32))

    c0 = counts[0:NE, :]
    ctot = c0 + counts[NE:2 * NE, :]
    tiles = jnp.floor((ctot + (tm - 1)) * (1.0 / tm))
    e_r = lax.broadcasted_iota(jnp.int32, (NE, NE), 0)
    e_c = lax.broadcasted_iota(jnp.int32, (NE, NE), 1)
    upto = (e_c <= e_r).astype(BF16)
    tile_end = jnp.dot(upto, jnp.broadcast_to(tiles, (NE, LANES)).astype(BF16),
                       preferred_element_type=F32)[:, 0:1]
    row_start = (tile_end - tiles) * tm
    base = jnp.concatenate([row_start, row_start + c0, jnp.zeros((LANES - 2 * NE, 1), F32)], axis=0)
    lane_e = lax.broadcasted_iota(jnp.int32, (NE, LANES), 1)
    cnt_ref[...] = jnp.concatenate(
        [jnp.where(lane_e == 0, ctot, jnp.where(lane_e == 1, tiles, jnp.where(lane_e == 2, tile_end, 0.0))),
         jnp.zeros((LANES - NE, LANES), F32)], axis=0)

    def phase2(b, carry):
        cols = pl.ds(pl.multiple_of(b * blk, blk), blk)
        t = oh_ref[:, cols] * (pre_ref[:, cols] + base)
        p0 = jnp.sum(t[0:NE, :], axis=0, keepdims=True)
        p1 = jnp.sum(t[NE:2 * NE, :], axis=0, keepdims=True)
        pos_ref[:, cols] = jnp.concatenate([p0, p1, jnp.zeros((6, blk), F32)], axis=0).astype(jnp.int32)
        return carry

    lax.fori_loop(0, T // blk, phase2, 0)


def _route(logits, bias_row, tm):
    T = logits.shape[0]
    whole = lambda shape: pl.BlockSpec(shape, lambda: (0,) * len(shape))
    return pl.pallas_call(
        functools.partial(_route_kernel, tm=tm),
        in_specs=[whole((T, LANES)), whole((1, LANES))],
        out_specs=[whole((8, T)), whole((T, LANES)), whole((LANES, LANES))],
        out_shape=[
            jax.ShapeDtypeStruct((8, T), jnp.int32),
            jax.ShapeDtypeStruct((T, LANES), F32),
            jax.ShapeDtypeStruct((LANES, LANES), F32),
        ],
        scratch_shapes=[pltpu.VMEM((LANES, T), F32), pltpu.VMEM((LANES, T), F32)],
        compiler_params=pltpu.CompilerParams(vmem_limit_bytes=VMEM_LIMIT),
        name="route_positions",
    )(logits, bias_row)


def _tile_plan(cnt, tm, n_tiles):
    counts = cnt[:N_EXPERTS, 0].astype(jnp.int32)
    tiles_per_e = cnt[:N_EXPERTS, 1].astype(jnp.int32)
    tile_end = cnt[:N_EXPERTS, 2].astype(jnp.int32)
    n_valid = tile_end[-1]
    tile_ids = jnp.arange(n_tiles, dtype=jnp.int32)
    experts = jnp.arange(N_EXPERTS, dtype=jnp.int32)
    valid = tile_ids < n_valid
    jc = jnp.minimum(tile_ids, n_valid - 1)
    tile_expert = jnp.minimum(jnp.sum((tile_end[None, :] <= jc[:, None]).astype(jnp.int32), axis=1),
                              N_EXPERTS - 1)
    of_tile = (tile_expert[:, None] == experts[None, :]).astype(jnp.int32)
    pick = lambda per_expert: jnp.sum(of_tile * per_expert[None, :], axis=1)
    tile_start = tile_end - tiles_per_e
    nonempty = tiles_per_e > 0
    group_idx = jnp.sum(jnp.logical_and(nonempty[None, :], experts[None, :] <= experts[:, None]).astype(jnp.int32),
                        axis=1) - 1
    later = jnp.logical_and(nonempty[None, :], experts[None, :] > experts[:, None])
    next_e = jnp.min(jnp.where(later, experts[None, :], N_EXPERTS), axis=1)
    next_e = jnp.where(next_e == N_EXPERTS, -1, next_e)
    is_next = (next_e[:, None] == experts[None, :]).astype(jnp.int32)
    next2_e = jnp.where(next_e >= 0, jnp.sum(is_next * next_e[None, :], axis=1), -1)
    tile_first = jnp.logical_and(valid, tile_ids == pick(tile_start)).astype(jnp.int32)
    tile_slot = (pick(group_idx) % MOE_WBUF).astype(jnp.int32)
    tile_next = pick(next_e).astype(jnp.int32)
    tile_next2 = pick(next2_e).astype(jnp.int32)
    tile_rows = jnp.clip(pick(counts) - (tile_ids - pick(tile_start)) * tm, 0, tm)
    tile_rows = jnp.where(valid, tile_rows, 0).astype(jnp.int32)
    return (tile_expert.astype(jnp.int32), n_valid.reshape(1), tile_first, tile_slot, tile_next, tile_next2,
            tile_rows)


def kernel(x, rel_bias, ln_mix_g, w_in, conv_w, conv_b, b_i, b_f, lam_q1, lam_k1, lam_q2, lam_k2,
           diff_norm_g, mlstm_norm_g, w_out, ln_ffn_g, w_group, b_group, w_router, b_router,
           w_gate, w_up, w_down, ln_f_g):
    B, S, D = x.shape
    T = B * S
    depth = w_in.shape[0]
    assert depth == 1, "the final rmsnorm is fused into the single layer's combine kernel"
    Hm = N_MLSTM_HEADS
    n_main = w_in.shape[2] - 2 * Hm
    n_diff = N_DIFF_HEADS * 2 * DIFF_HEAD_DIM
    xf = x.reshape(T, D)

    for l in range(depth):
        lambda_init = 0.8 - 0.6 * math.exp(-0.3 * l)
        w_main = w_in[l, :, :n_main].astype(BF16)
        wgt = w_in[l, :, n_main:]
        w_gates = jnp.zeros((D, 2 * LANES), F32).at[:, :Hm].set(wgt[:, :Hm]).at[:, LANES:LANES + Hm].set(
            wgt[:, Hm:]).astype(BF16)
        bi_row = jnp.zeros((1, LANES), F32).at[0, :Hm].set(b_i[l].astype(F32))
        bf_row = jnp.zeros((1, LANES), F32).at[0, :Hm].set(b_f[l].astype(F32))
        lam = (jnp.exp(jnp.sum(lam_q1[l].astype(F32) * lam_k1[l].astype(F32)))
               - jnp.exp(jnp.sum(lam_q2[l].astype(F32) * lam_k2[l].astype(F32))) + lambda_init)
        tq = ATT_TQ
        assert tq >= MAX_DISTANCE and tq % CHUNK == 0
        rb = rel_bias.astype(F32)
        log2e = math.log2(math.e)
        xx = jnp.arange(2 * tq, dtype=jnp.int32)
        rel_vec = jnp.stack([-tq + tq - 1 - xx, tq - 1 - xx], axis=0)
        bias_vecs = jnp.take(rb, _t5_bucket(rel_vec), axis=0) * log2e
        bias_vecs = jnp.transpose(bias_vecs, (2, 0, 1))[:, :, None, :]
        cfar = rb[N_BUCKETS // 2 - 1] * log2e
        scal = jnp.concatenate([lam.reshape(1), cfar]).astype(F32)
        col_scale = jnp.ones((1, n_main), F32).at[:, :n_diff].set(DIFF_HEAD_DIM ** -0.5 * log2e)

        proj, gates = _proj(xf, ln_mix_g[l].reshape(1, D).astype(F32), w_main, col_scale, w_gates)
        proj3 = proj.reshape(B, S, n_main)
        a = _diff_attention(proj3, scal, bias_vecs, diff_norm_g[l].reshape(1, n_diff).astype(F32),
                            lambda_init)
        hm = _mlstm(proj3, gates.reshape(B, S, 2 * LANES), conv_w[l].astype(F32),
                    conv_b[l].reshape(1, -1).astype(F32), bi_row, bf_row,
                    mlstm_norm_g[l].reshape(1, -1).astype(F32))

        wo = w_out[l]
        G, E = N_GROUPS, EXPERTS_PER_GROUP
        wr = jnp.zeros((D, LANES), F32).at[:, :G].set(w_group[l].astype(F32)).at[:, G:G + G * E].set(
            jnp.transpose(w_router[l].astype(F32), (1, 0, 2)).reshape(D, G * E)).astype(BF16)
        x1, h2, logits = _out_proj(xf, a.reshape(T, n_diff), hm.reshape(T, -1), wo, wo,
                                   ln_ffn_g[l].reshape(1, D).astype(F32), wr)

        route_bias = jnp.concatenate([b_group[l].astype(F32).reshape(-1), b_router[l].astype(F32).reshape(-1),
                                      jnp.zeros((LANES - G - G * E,), F32)]).reshape(1, LANES)
        n_tiles = (T * TOP_K_INNER) // MOE_TM + N_EXPERTS
        pos8, cwp, cnt = _route(logits, route_bias, MOE_TM)
        pos_t = pos8[:TOP_K_INNER]
        tiles = _tile_plan(cnt, MOE_TM, n_tiles)
        xs = _sc_scatter_rows(h2, pos_t, n_tiles * MOE_TM)
        Fe = w_gate.shape[-1]
        ys = _moe(*tiles, xs, w_gate[l].reshape(N_EXPERTS, D, Fe),
                  w_up[l].reshape(N_EXPERTS, D, Fe), w_down[l].reshape(N_EXPERTS, Fe, D))
        yw = _sc_gather_rows(ys, pos_t.reshape(-1))
        xf = _final(x1, yw, cwp, ln_f_g.reshape(1, D).astype(F32))
    return xf.reshape(B, S, D)
```

```python
import functools
import math

import jax
import jax.numpy as jnp
from jax import lax
from jax.experimental import pallas as pl
from jax.experimental.pallas import tpu as pltpu
from jax.experimental.pallas import tpu_sc as plsc

F32 = jnp.float32
BF16 = jnp.bfloat16

EPS = 1e-6
CHUNK = 64
DIFF_HEAD_DIM = 64
N_DIFF_HEADS = 8
MLSTM_HEAD_DIM = 128
N_MLSTM_HEADS = 8
CONV_WIDTH = 4
N_BUCKETS = 32
MAX_DISTANCE = 128
N_GROUPS = 4
EXPERTS_PER_GROUP = 8
N_EXPERTS = N_GROUPS * EXPERTS_PER_GROUP
TOP_K_INNER = 2
LANES = 128
NEG_BIG = -1e30

VMEM_LIMIT = 56 * 1024 * 1024

PROJ_TM, PROJ_TN = 1024, 1792
ATT_TQ = 512
MLSTM_TS = 1024
OUT_TM = 512
MOE_TM = 256
MOE_WBUF = 3
FIN_TM = 512


def _cparams(sem):
    return pltpu.CompilerParams(dimension_semantics=sem, vmem_limit_bytes=VMEM_LIMIT)


_HI_MASK = 0xFFFF0000


def _pack_bf16_pairs(x):
    half = x.shape[-1] // 2
    xb = x.astype(BF16).astype(F32)
    lo = pltpu.bitcast(xb[:, :half], jnp.uint32)
    hi = pltpu.bitcast(xb[:, half:], jnp.uint32)
    return (hi & jnp.uint32(_HI_MASK)) | (lo >> 16)


def _unpack_bf16_pairs(w):
    lo = pltpu.bitcast(w << 16, F32)
    hi = pltpu.bitcast(w & jnp.uint32(_HI_MASK), F32)
    return lo, hi


def _proj_kernel(x_ref, g_ref, w_ref, cs_ref, wg_ref, o_ref, og_ref, h_ref):
    @pl.when(pl.program_id(1) == 0)
    def _():
        x = x_ref[...]
        ms = jnp.mean(x * x, axis=-1, keepdims=True)
        h = (x * lax.rsqrt(ms + EPS) * g_ref[...]).astype(BF16)
        h_ref[...] = h
        og_ref[...] = jnp.dot(h, wg_ref[...], preferred_element_type=F32)

    o_ref[...] = (jnp.dot(h_ref[...], w_ref[...], preferred_element_type=F32) * cs_ref[...]).astype(o_ref.dtype)


def _proj(x2, g, w_main, col_scale, w_gates):
    T, D = x2.shape
    N = w_main.shape[1]
    NG = w_gates.shape[1]
    return pl.pallas_call(
        _proj_kernel,
        grid=(T // PROJ_TM, N // PROJ_TN),
        in_specs=[
            pl.BlockSpec((PROJ_TM, D), lambda m, n: (m, 0)),
            pl.BlockSpec((1, D), lambda m, n: (0, 0)),
            pl.BlockSpec((D, PROJ_TN), lambda m, n: (0, n)),
            pl.BlockSpec((1, PROJ_TN), lambda m, n: (0, n)),
            pl.BlockSpec((D, NG), lambda m, n: (0, 0)),
        ],
        out_specs=[
            pl.BlockSpec((PROJ_TM, PROJ_TN), lambda m, n: (m, n)),
            pl.BlockSpec((PROJ_TM, NG), lambda m, n: (m, 0)),
        ],
        out_shape=[
            jax.ShapeDtypeStruct((T, N), BF16),
            jax.ShapeDtypeStruct((T, NG), F32),
        ],
        scratch_shapes=[pltpu.VMEM((PROJ_TM, D), BF16)],
        compiler_params=_cparams(("parallel", "arbitrary")),
        name="rms_in_proj",
    )(x2, g, w_main, col_scale, w_gates)


def _t5_bucket(rel):
    half = N_BUCKETS // 2
    max_exact = half // 2
    ret = jnp.where(rel > 0, half, 0)
    n = jnp.abs(rel)
    nf = jnp.maximum(n, 1).astype(F32)
    large = max_exact + (jnp.log(nf / max_exact) / math.log(MAX_DISTANCE / max_exact)
                         * (half - max_exact)).astype(jnp.int32)
    large = jnp.minimum(large, half - 1)
    return ret + jnp.where(n < max_exact, n, large)


def _attn_kernel(scal_ref, q_ref, k_ref, v_ref, bias_ref, g_ref, o_ref, m_ref, l_ref, acc_ref,
                 s0_ref, s1_ref, s2_ref, ml0_ref, ml1_ref, ml2_ref, bt_ref, *, lambda_init):
    h = pl.program_id(1)
    qi = pl.program_id(2)
    nq = pl.num_programs(2)
    tq = ATT_TQ
    lam = scal_ref[0]
    cfar = scal_ref[1 + h]

    hq = tq // 2

    def stacked_queries(tile):
        q = q_ref[0, pl.ds(pl.multiple_of(tile * tq, tq), tq), :]
        lane = lax.broadcasted_iota(jnp.int32, q.shape, 1)
        zero = jnp.zeros_like(q)
        q1 = jnp.where(lane < DIFF_HEAD_DIM, q, zero)
        q2 = jnp.where(lane >= DIFF_HEAD_DIM, q, zero)
        return jnp.concatenate([q1[:hq], q2[:hq], q1[hq:], q2[hq:]], axis=0)

    def stacked_bias(bias):
        return jnp.concatenate([bias[:, :hq], bias[:, :hq], bias[:, hq:], bias[:, hq:]], axis=1)

    qs = stacked_queries(qi)
    nxt = jnp.minimum(qi + 1, nq - 1)
    qs_next = stacked_queries(nxt)

    m_ref[...] = jnp.full(m_ref.shape, NEG_BIG, F32)
    l_ref[...] = jnp.zeros(l_ref.shape, F32)
    acc_ref[...] = jnp.zeros(acc_ref.shape, F32)

    @pl.when(qi == 0)
    def _():
        kj = lax.broadcasted_iota(jnp.int32, (tq, tq), 0)
        qq = lax.broadcasted_iota(jnp.int32, (tq, tq), 1)
        allowed = (kj // CHUNK) <= (qq // CHUNK)
        for d in range(2):
            rows = jnp.broadcast_to(bias_ref[0, d], (tq, 2 * tq))
            tile = pltpu.roll(rows, tq + 1, 1, stride=1, stride_axis=0)[:, :tq]
            if d == 1:
                tile = jnp.where(allowed, tile, NEG_BIG)
            bt_ref[d] = tile

    bufs = ((s0_ref, ml0_ref), (s1_ref, ml1_ref), (s2_ref, ml2_ref))

    def score(ki, bias, slot, queries=None):
        s_ref, ml_ref = bufs[slot]
        start = pl.multiple_of(ki * tq, tq)
        kt = k_ref[0, pl.ds(start, tq), :]
        s = lax.dot_general(kt, qs if queries is None else queries, (((1,), (1,)), ((), ())),
                            preferred_element_type=F32)
        if bias is not None:
            s = s + stacked_bias(bias)
        s_ref[...] = s
        ml_ref[...] = jnp.max(s, axis=0, keepdims=True)

    def score_diagonal(ki, queries):
        start = pl.multiple_of(ki * tq, tq)
        bias = stacked_bias(bt_ref[1])
        s_a = lax.dot_general(k_ref[0, pl.ds(start, hq), :], queries, (((1,), (1,)), ((), ())),
                              preferred_element_type=F32) + bias[:hq, :]
        s_b = lax.dot_general(k_ref[0, pl.ds(start + hq, hq), :], queries[tq:, :], (((1,), (1,)), ((), ())),
                              preferred_element_type=F32) + bias[hq:, tq:]
        s2_ref[0:hq, :] = s_a
        s2_ref[hq:tq, tq:2 * tq] = s_b
        ml_a = jnp.max(s_a, axis=0, keepdims=True)
        ml2_ref[:, 0:tq] = ml_a[:, :tq]
        ml2_ref[:, tq:2 * tq] = jnp.maximum(ml_a[:, tq:], jnp.max(s_b, axis=0, keepdims=True))

    def accumulate_diagonal(ki):
        start = pl.multiple_of(ki * tq, tq)
        m_old = m_ref[...]
        m_new = jnp.maximum(m_old, ml2_ref[...])
        alpha = jnp.exp2(m_old - m_new)
        p_a = jnp.exp2(s2_ref[0:hq, :] - m_new)
        p_b = jnp.exp2(s2_ref[hq:tq, tq:2 * tq] - m_new[:, tq:])
        l_ref[...] = alpha * l_ref[...] + jnp.sum(p_a, axis=0, keepdims=True)
        l_ref[:, tq:2 * tq] = l_ref[:, tq:2 * tq] + jnp.sum(p_b, axis=0, keepdims=True)
        dims = (((0,), (0,)), ((), ()))
        pv_a = lax.dot_general(v_ref[0, pl.ds(start, hq), :], p_a.astype(BF16), dims, preferred_element_type=F32)
        pv_b = lax.dot_general(v_ref[0, pl.ds(start + hq, hq), :], p_b.astype(BF16), dims,
                               preferred_element_type=F32)
        acc_ref[...] = alpha * acc_ref[...] + pv_a
        acc_ref[:, tq:2 * tq] = acc_ref[:, tq:2 * tq] + pv_b
        m_ref[...] = m_new

    def accumulate(ki, shift, slot):
        s_ref, ml_ref = bufs[slot]
        start = pl.multiple_of(ki * tq, tq)
        vt = v_ref[0, pl.ds(start, tq), :]
        m_old = m_ref[...]
        m_new = jnp.maximum(m_old, ml_ref[...] + shift)
        alpha = jnp.exp2(m_old - m_new)
        p = jnp.exp2(s_ref[...] - (m_new - shift))
        l_ref[...] = alpha * l_ref[...] + jnp.sum(p, axis=0, keepdims=True)
        pv = lax.dot_general(vt, p.astype(BF16), (((0,), (0,)), ((), ())), preferred_element_type=F32)
        acc_ref[...] = alpha * acc_ref[...] + pv
        m_ref[...] = m_new

    n_far = qi - 1

    def score_next_diagonal():
        score_diagonal(nxt, qs_next)

    @pl.when(qi == 0)
    def _():
        score_diagonal(qi, qs)
        accumulate_diagonal(qi)
        score_next_diagonal()

    @pl.when(qi >= 1)
    def _():
        accumulate_diagonal(qi)
        score(qi - 1, bt_ref[0], 1)

    @pl.when(qi == 1)
    def _():
        accumulate(qi - 1, 0.0, 1)
        score_next_diagonal()

    @pl.when(qi >= 2)
    def _():
        accumulate(qi - 1, 0.0, 1)
        score(0, None, 0)
        trips = (n_far - 1) // 2

        def pair(j, c):
            accumulate(2 * j, cfar, 0)
            score(2 * j + 1, None, 1)
            accumulate(2 * j + 1, cfar, 1)
            score(2 * j + 2, None, 0)
            return c

        lax.fori_loop(0, trips, pair, 0)
        last = 2 * trips

        @pl.when(n_far - last == 2)
        def _():
            accumulate(last, cfar, 0)
            score(last + 1, None, 1)
            accumulate(last + 1, cfar, 1)
            score_next_diagonal()

        @pl.when(n_far - last == 1)
        def _():
            accumulate(last, cfar, 0)
            score_next_diagonal()

    acc = acc_ref[...] * (1.0 / l_ref[...])
    o_t = jnp.concatenate([acc[:, 0:hq] - lam * acc[:, hq:tq],
                           acc[:, tq:tq + hq] - lam * acc[:, tq + hq:2 * tq]], axis=1)
    ms = jnp.mean(o_t * o_t, axis=0, keepdims=True)
    y = (o_t * lax.rsqrt(ms + EPS)).T * (g_ref[...] * (1.0 - lambda_init))
    o_ref[0] = y.astype(o_ref.dtype)


def _diff_attention(proj3, scal, bias_vecs, gnorm, lambda_init):
    B, S, _ = proj3.shape
    H = N_DIFF_HEADS
    tq = ATT_TQ
    kern = functools.partial(_attn_kernel, lambda_init=lambda_init)
    return pl.pallas_call(
        kern,
        grid=(B, H, S // tq),
        in_specs=[
            pl.BlockSpec(memory_space=pltpu.SMEM),
            pl.BlockSpec((1, S, LANES), lambda b, h, i: (b, 0, h)),
            pl.BlockSpec((1, S, LANES), lambda b, h, i: (b, 0, H + h)),
            pl.BlockSpec((1, S, LANES), lambda b, h, i: (b, 0, 2 * H + h)),
            pl.BlockSpec((1, 2, 1, 2 * tq), lambda b, h, i: (h, 0, 0, 0)),
            pl.BlockSpec((1, LANES), lambda b, h, i: (0, h)),
        ],
        out_specs=pl.BlockSpec((1, tq, LANES), lambda b, h, i: (b, i, h)),
        out_shape=jax.ShapeDtypeStruct((B, S, H * LANES), BF16),
        scratch_shapes=[
            pltpu.VMEM((1, 2 * tq), F32),
            pltpu.VMEM((1, 2 * tq), F32),
            pltpu.VMEM((LANES, 2 * tq), F32),
            pltpu.VMEM((tq, 2 * tq), F32),
            pltpu.VMEM((tq, 2 * tq), F32),
            pltpu.VMEM((tq, 2 * tq), F32),
            pltpu.VMEM((1, 2 * tq), F32),
            pltpu.VMEM((1, 2 * tq), F32),
            pltpu.VMEM((1, 2 * tq), F32),
            pltpu.VMEM((2, tq, tq), F32),
        ],
        compiler_params=_cparams(("parallel", "parallel", "arbitrary")),
        name="diff_attention",
    )(scal, proj3, proj3, proj3, bias_vecs, gnorm)


def _log_sigmoid(x):
    return jnp.minimum(x, 0.0) - jnp.log(1.0 + jnp.exp(-jnp.abs(x)))


def _sigmoid(x):
    return 1.0 / (1.0 + jnp.exp(-x))


def _mlstm_kernel(q_ref, k_ref, v_ref, o_ref, gi_ref, gf_ref, cw_ref, cb_ref, bi_ref, bf_ref, gn_ref,
                  out_ref, qext_ref, kext_ref, ct_ref, n_ref, m_ref):
    sb = pl.program_id(1)
    L = CHUNK
    dh = MLSTM_HEAD_DIM
    H = N_MLSTM_HEADS
    ts = MLSTM_TS
    pad = 8

    @pl.when(sb == 0)
    def _():
        qext_ref[0:pad, :] = jnp.zeros((pad, H * dh), F32)
        kext_ref[0:pad, :] = jnp.zeros((pad, H * dh), F32)
        ct_ref[...] = jnp.zeros(ct_ref.shape, F32)
        n_ref[...] = jnp.zeros(n_ref.shape, F32)
        m_ref[...] = jnp.zeros(m_ref.shape, F32)

    qext_ref[pad:pad + ts, :] = q_ref[0].astype(F32)
    kext_ref[pad:pad + ts, :] = k_ref[0].astype(F32)

    row = lax.broadcasted_iota(jnp.int32, (L, L), 0)
    col = lax.broadcasted_iota(jnp.int32, (L, L), 1)
    tril = col <= row
    ltri = tril.astype(F32)

    def conv_silu(ext_ref, base, h, off):
        win = ext_ref[pl.ds(base, L + pad), h * dh:(h + 1) * dh]
        w = cw_ref[:, off + h * dh:off + (h + 1) * dh]
        y = cb_ref[:, off + h * dh:off + (h + 1) * dh]
        for j in range(CONV_WIDTH):
            lo = pad - (CONV_WIDTH - 1) + j
            y = y + w[j:j + 1, :] * win[lo:lo + L, :]
        return y * _sigmoid(y)

    def chunk_body(c, carry):
        base = pl.multiple_of(c * L, L)
        li = gi_ref[0, pl.ds(base, L), :] + bi_ref[...]
        logf = _log_sigmoid(gf_ref[0, pl.ds(base, L), :] + bf_ref[...])
        b = jnp.dot(ltri, logf, preferred_element_type=F32, precision=lax.Precision.HIGHEST)
        a = li - b
        g_row = b[L - 1:L, :]
        m_row = m_ref[...]
        m_new_row = g_row + jnp.maximum(m_row, jnp.max(a, axis=0, keepdims=True))
        a_t = a.T

        lane_h = lax.broadcasted_iota(jnp.int32, (L, LANES), 1)
        cmax_all = jnp.full((L, LANES), NEG_BIG, F32)
        for h in range(H):
            cmax_h = jnp.max(jnp.where(tril, a_t[h:h + 1, :], NEG_BIG), axis=-1, keepdims=True)
            cmax_all = jnp.where(lane_h == h, cmax_h, cmax_all)
        mcol_all = jnp.maximum(cmax_all, m_row)
        inter_all = jnp.exp(m_row - mcol_all)
        floor_all = jnp.exp(-(b + mcol_all))
        wt_all = jnp.exp(g_row + a - m_new_row)
        decay_row = jnp.exp(g_row + m_row - m_new_row)

        for h in range(H):
            qc = conv_silu(qext_ref, base, h, 0)
            kc = conv_silu(kext_ref, base, h, H * dh) * (dh ** -0.5)
            qb = qc.astype(BF16)
            kb = kc.astype(BF16)
            vb = v_ref[0, pl.ds(base, L), h * dh:(h + 1) * dh]

            a_row = a_t[h:h + 1, :]
            amat = jnp.where(tril, a_row, NEG_BIG)
            wts = jnp.exp(amat - mcol_all[:, h:h + 1])
            inter = inter_all[:, h:h + 1]

            s = lax.dot_general(qb, kb, (((1,), (1,)), ((), ())), preferred_element_type=F32)
            sqk = s * wts
            ct = ct_ref[h]
            nrow = n_ref[h:h + 1, :]
            num = (jnp.dot(sqk.astype(BF16), vb, preferred_element_type=F32)
                   + inter * jnp.dot(qb, ct.astype(BF16), preferred_element_type=F32))
            den = (jnp.sum(sqk, axis=-1, keepdims=True)
                   + inter * jnp.sum(qc * nrow, axis=-1, keepdims=True))
            hv = num / jnp.maximum(jnp.abs(den), floor_all[:, h:h + 1])

            wt = wt_all[:, h:h + 1]
            decay = decay_row[:, h:h + 1]
            wv = (wt * vb.astype(F32)).astype(BF16)
            ct_ref[h] = decay * ct + lax.dot_general(kb, wv, (((0,), (0,)), ((), ())),
                                                     preferred_element_type=F32)
            n_ref[h:h + 1, :] = decay * nrow + jnp.sum(wt * kc, axis=0, keepdims=True)

            ms = jnp.mean(hv * hv, axis=-1, keepdims=True)
            y = hv * lax.rsqrt(ms + EPS) * gn_ref[:, h * dh:(h + 1) * dh]
            og = o_ref[0, pl.ds(base, L), h * dh:(h + 1) * dh].astype(F32)
            out_ref[0, pl.ds(base, L), h * dh:(h + 1) * dh] = (y * _sigmoid(og)).astype(out_ref.dtype)

        m_ref[...] = m_new_row
        return carry

    lax.fori_loop(0, ts // L, chunk_body, 0, unroll=4)

    qext_ref[0:pad, :] = qext_ref[ts:ts + pad, :]
    kext_ref[0:pad, :] = kext_ref[ts:ts + pad, :]


def _mlstm(proj3, gates3, conv_w, conv_b, bi_row, bf_row, gnorm):
    B, S, _ = proj3.shape
    W = N_MLSTM_HEADS * MLSTM_HEAD_DIM
    ts = MLSTM_TS
    first = 3
    blk = lambda j: pl.BlockSpec((1, ts, W), lambda b, s: (b, s, j))
    full = lambda shape: pl.BlockSpec(shape, lambda b, s: (0,) * len(shape))
    return pl.pallas_call(
        _mlstm_kernel,
        grid=(B, S // ts),
        in_specs=[
            blk(first), blk(first + 1), blk(first + 2), blk(first + 3),
            pl.BlockSpec((1, ts, LANES), lambda b, s: (b, s, 0)),
            pl.BlockSpec((1, ts, LANES), lambda b, s: (b, s, 1)),
            full((CONV_WIDTH, 2 * W)), full((1, 2 * W)),
            full((1, LANES)), full((1, LANES)), full((1, W)),
        ],
        out_specs=pl.BlockSpec((1, ts, W), lambda b, s: (b, s, 0)),
        out_shape=jax.ShapeDtypeStruct((B, S, W), BF16),
        scratch_shapes=[
            pltpu.VMEM((ts + 8, W), F32),
            pltpu.VMEM((ts + 8, W), F32),
            pltpu.VMEM((N_MLSTM_HEADS, MLSTM_HEAD_DIM, MLSTM_HEAD_DIM), F32),
            pltpu.VMEM((N_MLSTM_HEADS, MLSTM_HEAD_DIM), F32),
            pltpu.VMEM((1, LANES), F32),
        ],
        compiler_params=_cparams(("parallel", "arbitrary")),
        name="mlstm",
    )(proj3, proj3, proj3, proj3, gates3, gates3, conv_w, conv_b, bi_row, bf_row, gnorm)


def _out_kernel(x_ref, a_ref, hm_ref, wa_ref, wm_ref, g_ref, wr_ref, x1_ref, h2_ref, lg_ref):
    y = (jnp.dot(a_ref[...], wa_ref[...].astype(BF16), preferred_element_type=F32)
         + jnp.dot(hm_ref[...], wm_ref[...].astype(BF16), preferred_element_type=F32))
    x1 = x_ref[...] + y
    x1_ref[...] = x1
    ms = jnp.mean(x1 * x1, axis=-1, keepdims=True)
    h2 = x1 * lax.rsqrt(ms + EPS) * g_ref[...]
    h2_ref[...] = _pack_bf16_pairs(h2)
    lg_ref[...] = jnp.dot(h2.astype(BF16), wr_ref[...], preferred_element_type=F32)


def _out_proj(x2, a2, hm2, wa, wm, g, wr):
    T, D = x2.shape
    W = a2.shape[1]
    tm = OUT_TM
    const = lambda shape: pl.BlockSpec(shape, lambda m: (0, 0), pipeline_mode=pl.Buffered(1))
    return pl.pallas_call(
        _out_kernel,
        grid=(T // tm,),
        in_specs=[
            pl.BlockSpec((tm, D), lambda m: (m, 0)),
            pl.BlockSpec((tm, W), lambda m: (m, 0)),
            pl.BlockSpec((tm, W), lambda m: (m, 0)),
            const((W, D)),
            pl.BlockSpec((W, D), lambda m: (1, 0), pipeline_mode=pl.Buffered(1)),
            const((1, D)), const((D, LANES)),
        ],
        out_specs=[
            pl.BlockSpec((tm, D), lambda m: (m, 0)),
            pl.BlockSpec((tm, D // 2), lambda m: (m, 0)),
            pl.BlockSpec((tm, LANES), lambda m: (m, 0)),
        ],
        out_shape=[
            jax.ShapeDtypeStruct((T, D), F32),
            jax.ShapeDtypeStruct((T, D // 2), jnp.uint32),
            jax.ShapeDtypeStruct((T, LANES), F32),
        ],
        compiler_params=_cparams(("parallel",)),
        name="out_proj_router",
    )(x2, a2, hm2, wa, wm, g, wr)


def _moe_kernel(te_ref, nv_ref, first_ref, slot_ref, nxt_ref, nxt2_ref, rows_ref, xs_ref, wg_hbm, wu_hbm,
                wd_hbm, ys_ref, wg_buf, wu_buf, wd_buf, sem):
    j = pl.program_id(0)
    valid = j < nv_ref[0]
    nbuf = wg_buf.shape[0]

    half_f = wd_buf.shape[1] // 2

    def weight_copies(e, s):
        lo, hi = pl.ds(0, half_f), pl.ds(half_f, half_f)
        return ((pltpu.make_async_copy(wg_hbm.at[e], wg_buf.at[s], sem.at[s, 0]), 0),
                (pltpu.make_async_copy(wu_hbm.at[e], wu_buf.at[s], sem.at[s, 1]), 1),
                (pltpu.make_async_copy(wd_hbm.at[e, lo], wd_buf.at[s, lo], sem.at[s, 2]), 0),
                (pltpu.make_async_copy(wd_hbm.at[e, hi], wd_buf.at[s, hi], sem.at[s, 3]), 1))

    @pl.when(j == 0)
    def _():
        for c, prio in weight_copies(te_ref[0], 0):
            c.start(priority=prio)

        @pl.when(nxt_ref[0] >= 0)
        def _():
            for c, prio in weight_copies(nxt_ref[0], 1):
                c.start(priority=prio)

    @pl.when(jnp.logical_and(valid, first_ref[j] == 1))
    def _():
        for c, _ in weight_copies(te_ref[j], slot_ref[j]):
            c.wait()

        @pl.when(nxt2_ref[j] >= 0)
        def _():
            for c, prio in weight_copies(nxt2_ref[j], (slot_ref[j] + 2) % nbuf):
                c.start(priority=prio)

    @pl.when(valid)
    def _():
        s = slot_ref[j]
        row = lax.broadcasted_iota(jnp.int32, xs_ref.shape, 0)
        lo, hi = _unpack_bf16_pairs(jnp.where(row < rows_ref[j], xs_ref[...], jnp.uint32(0)))
        xs = jnp.concatenate([lo.astype(BF16), hi.astype(BF16)], axis=1)
        gt = jnp.dot(xs, wg_buf[s].astype(BF16), preferred_element_type=F32)
        up = jnp.dot(xs, wu_buf[s].astype(BF16), preferred_element_type=F32)
        hid = (gt * _sigmoid(gt) * up).astype(BF16)
        ys_ref[...] = _pack_bf16_pairs(jnp.dot(hid, wd_buf[s].astype(BF16), preferred_element_type=F32))

    @pl.when(jnp.logical_not(valid))
    def _():
        ys_ref[...] = jnp.zeros(ys_ref.shape, ys_ref.dtype)


def _moe(tile_expert, n_valid, tile_first, tile_slot, tile_next, tile_next2, tile_rows, xs, wg, wu, wd):
    R, Dw = xs.shape
    D, F = wg.shape[1], wg.shape[2]
    tm = MOE_TM
    hbm = pl.BlockSpec(memory_space=pl.ANY)
    grid_spec = pltpu.PrefetchScalarGridSpec(
        num_scalar_prefetch=7,
        grid=(R // tm,),
        in_specs=[pl.BlockSpec((tm, Dw), lambda j, te, nv, *_: (jnp.minimum(j, nv[0] - 1), 0)), hbm, hbm, hbm],
        out_specs=pl.BlockSpec((tm, Dw), lambda j, *_: (j, 0)),
        scratch_shapes=[
            pltpu.VMEM((MOE_WBUF, D, F), wg.dtype),
            pltpu.VMEM((MOE_WBUF, D, F), wu.dtype),
            pltpu.VMEM((MOE_WBUF, F, D), wd.dtype),
            pltpu.SemaphoreType.DMA((MOE_WBUF, 4)),
        ],
    )
    return pl.pallas_call(
        _moe_kernel,
        grid_spec=grid_spec,
        out_shape=jax.ShapeDtypeStruct((R, Dw), jnp.uint32),
        compiler_params=_cparams(("arbitrary",)),
        name="moe_experts",
    )(tile_expert, n_valid, tile_first, tile_slot, tile_next, tile_next2, tile_rows, xs, wg, wu, wd)


def _final_kernel(x1_ref, y0_ref, y1_ref, cw_ref, g_ref, o_ref):
    cw = cw_ref[...]
    lo0, hi0 = _unpack_bf16_pairs(y0_ref[...])
    lo1, hi1 = _unpack_bf16_pairs(y1_ref[...])
    w0, w1 = cw[:, 0:1], cw[:, 1:2]
    y = jnp.concatenate([w0 * lo0 + w1 * lo1, w0 * hi0 + w1 * hi1], axis=1)
    x = x1_ref[...] + y
    ms = jnp.mean(x * x, axis=-1, keepdims=True)
    o_ref[...] = x * lax.rsqrt(ms + EPS) * g_ref[...]


def _final(x1, yw, cw, g):
    T, D = x1.shape
    tm = FIN_TM
    row = lambda w: pl.BlockSpec((tm, w), lambda m: (m, 0))
    slot1 = pl.BlockSpec((tm, D // 2), lambda m: (m + T // tm, 0))
    return pl.pallas_call(
        _final_kernel,
        grid=(T // tm,),
        in_specs=[row(D), row(D // 2), slot1, row(LANES), pl.BlockSpec((1, D), lambda m: (0, 0))],
        out_specs=row(D),
        out_shape=jax.ShapeDtypeStruct((T, D), F32),
        compiler_params=_cparams(("parallel",)),
        name="combine_final_norm",
    )(x1, yw, yw, cw, g)


SC_CORES, SC_SUBCORES = 2, 16
SC_CHUNK = 32


def _sc_gather_rows(table, idx):
    V, Dw = table.shape
    R = idx.shape[0]
    n_workers = SC_CORES * SC_SUBCORES
    ch = SC_CHUNK
    per_w = R // n_workers
    n_chunks = per_w // ch
    assert per_w * n_workers == R and n_chunks * ch == per_w and n_chunks % 2 == 0
    idx3 = idx.reshape(n_workers, n_chunks, ch)
    mesh = plsc.VectorSubcoreMesh(core_axis_name="c", subcore_axis_name="s")

    def body(table_hbm, idx_hbm, out_hbm, idx_v, rows_v, gsem, osem):
        wid = lax.axis_index("s") * SC_CORES + lax.axis_index("c")
        base = wid * per_w
        pltpu.sync_copy(idx_hbm.at[wid], idx_v)

        def gather(c, slot):
            return pltpu.make_async_copy(table_hbm.at[idx_v.at[c]], rows_v.at[slot], gsem.at[slot])

        def put(c, slot):
            return pltpu.make_async_copy(rows_v.at[slot], out_hbm.at[pl.ds(base + c * ch, ch)],
                                         osem.at[slot])

        gather(0, 0).start()

        @pl.loop(0, n_chunks, step=2)
        def _(c):
            @pl.when(c > 0)
            def _():
                put(c - 1, 1).wait()

            gather(c + 1, 1).start()
            gather(c, 0).wait()
            put(c, 0).start()
            put(c, 0).wait()

            @pl.when(c + 2 < n_chunks)
            def _():
                gather(c + 2, 0).start()

            gather(c + 1, 1).wait()
            put(c + 1, 1).start()

        put(n_chunks - 1, 1).wait()

    return pl.kernel(
        body,
        out_type=jax.ShapeDtypeStruct((R, Dw), table.dtype),
        mesh=mesh,
        scratch_types=[
            pltpu.VMEM((n_chunks, ch), jnp.int32),
            pltpu.VMEM((2, ch, Dw), table.dtype),
            pltpu.SemaphoreType.DMA((2,)),
            pltpu.SemaphoreType.DMA((2,)),
        ],
        name="sc_gather_rows",
    )(table, idx3)


def _sc_scatter_rows(table, idx, n_rows_out):
    V, Dw = table.shape
    K = idx.shape[0]
    n_workers = SC_CORES * SC_SUBCORES
    ch = SC_CHUNK
    per_w = V // n_workers
    n_chunks = per_w // ch
    assert K == 2 and per_w * n_workers == V and n_chunks * ch == per_w and n_chunks % 2 == 0
    idx4 = jnp.transpose(idx.reshape(K, n_workers, n_chunks, ch), (1, 0, 2, 3))
    mesh = plsc.VectorSubcoreMesh(core_axis_name="c", subcore_axis_name="s")

    def body(table_hbm, idx_hbm, out_hbm, idx_v, rows_v, lsem, ssem):
        wid = lax.axis_index("s") * SC_CORES + lax.axis_index("c")
        base = wid * per_w
        pltpu.sync_copy(idx_hbm.at[wid], idx_v)

        def load(c, slot):
            return pltpu.make_async_copy(table_hbm.at[pl.ds(base + c * ch, ch)], rows_v.at[slot],
                                         lsem.at[slot])

        def scatter(c, slot, k):
            return pltpu.make_async_copy(rows_v.at[slot], out_hbm.at[idx_v.at[k, c]], ssem.at[slot, k])

        load(0, 0).start()

        @pl.loop(0, n_chunks, step=2)
        def _(c):
            @pl.when(c > 0)
            def _():
                scatter(c - 1, 1, 0).wait()
                scatter(c - 1, 1, 1).wait()

            load(c + 1, 1).start()
            load(c, 0).wait()
            scatter(c, 0, 0).start()
            scatter(c, 0, 1).start()
            scatter(c, 0, 0).wait()
            scatter(c, 0, 1).wait()

            @pl.when(c + 2 < n_chunks)
            def _():
                load(c + 2, 0).start()

            load(c + 1, 1).wait()
            scatter(c + 1, 1, 0).start()
            scatter(c + 1, 1, 1).start()

        scatter(n_chunks - 1, 1, 0).wait()
        scatter(n_chunks - 1, 1, 1).wait()

    return pl.kernel(
        body,
        out_type=jax.ShapeDtypeStruct((n_rows_out, Dw), table.dtype),
        mesh=mesh,
        scratch_types=[
            pltpu.VMEM((K, n_chunks, ch), jnp.int32),
            pltpu.VMEM((2, ch, Dw), table.dtype),
            pltpu.SemaphoreType.DMA((2,)),
            pltpu.SemaphoreType.DMA((2, K)),
        ],
        name="sc_scatter_rows",
    )(table, idx4)


ROUTE_BLK = 1024


def _route_kernel(lg_ref, bias_ref, pos_ref, cw_ref, cnt_ref, oh_ref, pre_ref, *, tm):
    T = lg_ref.shape[0]
    G, E, NE = N_GROUPS, EXPERTS_PER_GROUP, N_EXPERTS
    blk = ROUTE_BLK
    sub = lax.broadcasted_iota(jnp.int32, (LANES, blk), 0).astype(F32)
    r_i = lax.broadcasted_iota(jnp.int32, (blk, blk), 0)
    c_i = lax.broadcasted_iota(jnp.int32, (blk, blk), 1)
    before = (r_i < c_i).astype(BF16)

    def first_argmax(v):
        mx = jnp.max(v, axis=0, keepdims=True)
        return mx, jnp.min(jnp.where(v == mx, sub, float(LANES)), axis=0, keepdims=True)

    def phase1(b, carry):
        cols = pl.ds(pl.multiple_of(b * blk, blk), blk)
        x = (lg_ref[cols, :] + bias_ref[...]).T
        gl = jnp.where(sub < G, x, NEG_BIG)
        gmax, gsel = first_argmax(gl)
        gw = 1.0 / jnp.sum(jnp.exp(gl - gmax), axis=0, keepdims=True)
        lo = G + E * gsel
        el = jnp.where(jnp.logical_and(sub >= lo, sub < lo + E), x, NEG_BIG)
        v1, i1 = first_argmax(el)
        v2, i2 = first_argmax(jnp.where(sub == i1, NEG_BIG, el))
        e21 = jnp.exp(v2 - v1)
        w1 = 1.0 / (1.0 + e21)
        cw_t = jnp.where(sub == 0, gw * w1, jnp.where(sub == 1, gw * (e21 * w1), 0.0))
        cw_ref[cols, :] = cw_t.T
        oh = jnp.where(jnp.logical_or(sub == i1 - G, sub == i2 - G + NE), 1.0, 0.0)
        oh_ref[:, cols] = oh
        pre_ref[:, cols] = jnp.dot(oh.astype(BF16), before, preferred_element_type=F32) + carry
        return carry + jnp.sum(oh, axis=1, keepdims=True)

    counts = lax.fori_loop(0, T // blk, phase1, jnp.zeros((LANES, 1), F32))

    c0 = counts[0:NE, :]
    ctot = c0 + counts[NE:2 * NE, :]
    tiles = jnp.floor((ctot + (tm - 1)) * (1.0 / tm))
    e_r = lax.broadcasted_iota(jnp.int32, (NE, NE), 0)
    e_c = lax.broadcasted_iota(jnp.int32, (NE, NE), 1)
    upto = (e_c <= e_r).astype(BF16)
    tile_end = jnp.dot(upto, jnp.broadcast_to(tiles, (NE, LANES)).astype(BF16),
                       preferred_element_type=F32)[:, 0:1]
    row_start = (tile_end - tiles) * tm
    base = jnp.concatenate([row_start, row_start + c0, jnp.zeros((LANES - 2 * NE, 1), F32)], axis=0)
    lane_e = lax.broadcasted_iota(jnp.int32, (NE, LANES), 1)
    cnt_ref[...] = jnp.concatenate(
        [jnp.where(lane_e == 0, ctot, jnp.where(lane_e == 1, tiles, jnp.where(lane_e == 2, tile_end, 0.0))),
         jnp.zeros((LANES - NE, LANES), F32)], axis=0)

    def phase2(b, carry):
        cols = pl.ds(pl.multiple_of(b * blk, blk), blk)
        t = oh_ref[:, cols] * (pre_ref[:, cols] + base)
        p0 = jnp.sum(t[0:NE, :], axis=0, keepdims=True)
        p1 = jnp.sum(t[NE:2 * NE, :], axis=0, keepdims=True)
        pos_ref[:, cols] = jnp.concatenate([p0, p1, jnp.zeros((6, blk), F32)], axis=0).astype(jnp.int32)
        return carry

    lax.fori_loop(0, T // blk, phase2, 0)


def _route(logits, bias_row, tm):
    T = logits.shape[0]
    whole = lambda shape: pl.BlockSpec(shape, lambda: (0,) * len(shape))
    return pl.pallas_call(
        functools.partial(_route_kernel, tm=tm),
        in_specs=[whole((T, LANES)), whole((1, LANES))],
        out_specs=[whole((8, T)), whole((T, LANES)), whole((LANES, LANES))],
        out_shape=[
            jax.ShapeDtypeStruct((8, T), jnp.int32),
            jax.ShapeDtypeStruct((T, LANES), F32),
            jax.ShapeDtypeStruct((LANES, LANES), F32),
        ],
        scratch_shapes=[pltpu.VMEM((LANES, T), F32), pltpu.VMEM((LANES, T), F32)],
        compiler_params=pltpu.CompilerParams(vmem_limit_bytes=VMEM_LIMIT),
        name="route_positions",
    )(logits, bias_row)


def _tile_plan(cnt, tm, n_tiles):
    counts = cnt[:N_EXPERTS, 0].astype(jnp.int32)
    tiles_per_e = cnt[:N_EXPERTS, 1].astype(jnp.int32)
    tile_end = cnt[:N_EXPERTS, 2].astype(jnp.int32)
    n_valid = tile_end[-1]
    tile_ids = jnp.arange(n_tiles, dtype=jnp.int32)
    experts = jnp.arange(N_EXPERTS, dtype=jnp.int32)
    valid = tile_ids < n_valid
    jc = jnp.minimum(tile_ids, n_valid - 1)
    tile_expert = jnp.minimum(jnp.sum((tile_end[None, :] <= jc[:, None]).astype(jnp.int32), axis=1),
                              N_EXPERTS - 1)
    of_tile = (tile_expert[:, None] == experts[None, :]).astype(jnp.int32)
    pick = lambda per_expert: jnp.sum(of_tile * per_expert[None, :], axis=1)
    tile_start = tile_end - tiles_per_e
    nonempty = tiles_per_e > 0
    group_idx = jnp.sum(jnp.logical_and(nonempty[None, :], experts[None, :] <= experts[:, None]).astype(jnp.int32),
                        axis=1) - 1
    later = jnp.logical_and(nonempty[None, :], experts[None, :] > experts[:, None])
    next_e = jnp.min(jnp.where(later, experts[None, :], N_EXPERTS), axis=1)
    next_e = jnp.where(next_e == N_EXPERTS, -1, next_e)
    is_next = (next_e[:, None] == experts[None, :]).astype(jnp.int32)
    next2_e = jnp.where(next_e >= 0, jnp.sum(is_next * next_e[None, :], axis=1), -1)
    tile_first = jnp.logical_and(valid, tile_ids == pick(tile_start)).astype(jnp.int32)
    tile_slot = (pick(group_idx) % MOE_WBUF).astype(jnp.int32)
    tile_next = pick(next_e).astype(jnp.int32)
    tile_next2 = pick(next2_e).astype(jnp.int32)
    tile_rows = jnp.clip(pick(counts) - (tile_ids - pick(tile_start)) * tm, 0, tm)
    tile_rows = jnp.where(valid, tile_rows, 0).astype(jnp.int32)
    return (tile_expert.astype(jnp.int32), n_valid.reshape(1), tile_first, tile_slot, tile_next, tile_next2,
            tile_rows)


def kernel(x, rel_bias, ln_mix_g, w_in, conv_w, conv_b, b_i, b_f, lam_q1, lam_k1, lam_q2, lam_k2,
           diff_norm_g, mlstm_norm_g, w_out, ln_ffn_g, w_group, b_group, w_router, b_router,
           w_gate, w_up, w_down, ln_f_g):
    B, S, D = x.shape
    T = B * S
    depth = w_in.shape[0]
    assert depth == 1, "the final rmsnorm is fused into the single layer's combine kernel"
    Hm = N_MLSTM_HEADS
    n_main = w_in.shape[2] - 2 * Hm
    n_diff = N_DIFF_HEADS * 2 * DIFF_HEAD_DIM
    xf = x.reshape(T, D)

    for l in range(depth):
        lambda_init = 0.8 - 0.6 * math.exp(-0.3 * l)
        w_main = w_in[l, :, :n_main].astype(BF16)
        wgt = w_in[l, :, n_main:]
        w_gates = jnp.zeros((D, 2 * LANES), F32).at[:, :Hm].set(wgt[:, :Hm]).at[:, LANES:LANES + Hm].set(
            wgt[:, Hm:]).astype(BF16)
        bi_row = jnp.zeros((1, LANES), F32).at[0, :Hm].set(b_i[l].astype(F32))
        bf_row = jnp.zeros((1, LANES), F32).at[0, :Hm].set(b_f[l].astype(F32))
        lam = (jnp.exp(jnp.sum(lam_q1[l].astype(F32) * lam_k1[l].astype(F32)))
               - jnp.exp(jnp.sum(lam_q2[l].astype(F32) * lam_k2[l].astype(F32))) + lambda_init)
        tq = ATT_TQ
        assert tq >= MAX_DISTANCE and tq % CHUNK == 0
        rb = rel_bias.astype(F32)
        log2e = math.log2(math.e)
        xx = jnp.arange(2 * tq, dtype=jnp.int32)
        rel_vec = jnp.stack([-tq + tq - 1 - xx, tq - 1 - xx], axis=0)
        bias_vecs = jnp.take(rb, _t5_bucket(rel_vec), axis=0) * log2e
        bias_vecs = jnp.transpose(bias_vecs, (2, 0, 1))[:, :, None, :]
        cfar = rb[N_BUCKETS // 2 - 1] * log2e
        scal = jnp.concatenate([lam.reshape(1), cfar]).astype(F32)
        col_scale = jnp.ones((1, n_main), F32).at[:, :n_diff].set(DIFF_HEAD_DIM ** -0.5 * log2e)

        proj, gates = _proj(xf, ln_mix_g[l].reshape(1, D).astype(F32), w_main, col_scale, w_gates)
        proj3 = proj.reshape(B, S, n_main)
        a = _diff_attention(proj3, scal, bias_vecs, diff_norm_g[l].reshape(1, n_diff).astype(F32),
                            lambda_init)
        hm = _mlstm(proj3, gates.reshape(B, S, 2 * LANES), conv_w[l].astype(F32),
                    conv_b[l].reshape(1, -1).astype(F32), bi_row, bf_row,
                    mlstm_norm_g[l].reshape(1, -1).astype(F32))

        wo = w_out[l]
        G, E = N_GROUPS, EXPERTS_PER_GROUP
        wr = jnp.zeros((D, LANES), F32).at[:, :G].set(w_group[l].astype(F32)).at[:, G:G + G * E].set(
            jnp.transpose(w_router[l].astype(F32), (1, 0, 2)).reshape(D, G * E)).astype(BF16)
        x1, h2, logits = _out_proj(xf, a.reshape(T, n_diff), hm.reshape(T, -1), wo, wo,
                                   ln_ffn_g[l].reshape(1, D).astype(F32), wr)

        route_bias = jnp.concatenate([b_group[l].astype(F32).reshape(-1), b_router[l].astype(F32).reshape(-1),
                                      jnp.zeros((LANES - G - G * E,), F32)]).reshape(1, LANES)
        n_tiles = (T * TOP_K_INNER) // MOE_TM + N_EXPERTS
        pos8, cwp, cnt = _route(logits, route_bias, MOE_TM)
        pos_t = pos8[:TOP_K_INNER]
        tiles = _tile_plan(cnt, MOE_TM, n_tiles)
        xs = _sc_scatter_rows(h2, pos_t, n_tiles * MOE_TM)
        Fe = w_gate.shape[-1]
        ys = _moe(*tiles, xs, w_gate[l].reshape(N_EXPERTS, D, Fe),
                  w_up[l].reshape(N_EXPERTS, D, Fe), w_down[l].reshape(N_EXPERTS, Fe, D))
        yw = _sc_gather_rows(ys, pos_t.reshape(-1))
        xf = _final(x1, yw, cwp, ln_f_g.reshape(1, D).astype(F32))
    return xf.reshape(B, S, D)
```
